```python
import math
import jax, jax.numpy as jnp
from jax import lax
import numpy as np

D_MODEL = 1024
BATCH = 4
SEQ = 4096
DEPTH = 1
DEC_BATCH = 32
DEC_SEQ = 8
PAST_LEN = 8192
PAGE_SIZE = 128

SSD_INNER = D_MODEL
SSD_HEADDIM = 64
SSD_HEADS = SSD_INNER // SSD_HEADDIM
SSD_GROUPS = 2
D_STATE = 128
CONV_W = 4
CONV_DIM = SSD_INNER + 2 * SSD_GROUPS * D_STATE
SSD_CHUNK = 128
DT_MIN = 0.001
DT_MAX = 0.1
ATTN_HEADS = 8
ATTN_HEAD_DIM = 64
ATTN_V_DIM = 2 * ATTN_HEAD_DIM
ATTN_QK = ATTN_HEADS * 2 * ATTN_HEAD_DIM
ATTN_V = ATTN_HEADS * ATTN_V_DIM
Q_BLOCK = 128
N_BUCKETS = 32
MAX_DISTANCE = 128
N_EXPERTS = 64
N_EXPERT_GROUPS = 8
TOPK_GROUPS = 4
TOP_K = 8
D_EXPERT = 256
D_SHARED = 256
ROUTED_SCALE = 2.5
MOE_BLOCK = 128
ALPHA = (2 * DEPTH) ** 0.25
BETA = (8 * DEPTH) ** -0.25
LN_EPS = 1e-5
RMS_EPS = 1e-5
IN_COLS = SSD_INNER + CONV_DIM + SSD_HEADS + 2 * ATTN_QK + ATTN_V + 2 * D_MODEL

kernel_name = 'hybrid_ssd_diffattn_moe_deepnorm_adaln_step'


def layer_norm(x, g, b):
    xf = x.astype(jnp.float32)
    mu = jnp.mean(xf, axis=-1, keepdims=True)
    var = jnp.mean(jnp.square(xf - mu), axis=-1, keepdims=True)
    return ((xf - mu) * lax.rsqrt(var + LN_EPS) * g + b).astype(x.dtype)


def rms_norm(x, w, groups):
    xf = x.astype(jnp.float32)
    xg = xf.reshape(x.shape[:-1] + (groups, x.shape[-1] // groups))
    xg = xg * lax.rsqrt(jnp.mean(jnp.square(xg), axis=-1, keepdims=True) + RMS_EPS)
    return (xg.reshape(x.shape) * w).astype(x.dtype)


def _split_in(proj):
    sizes = (SSD_INNER, CONV_DIM, SSD_HEADS, ATTN_QK, ATTN_QK, ATTN_V, D_MODEL, D_MODEL)
    idx, acc = [], 0
    for s in sizes[:-1]:
        acc += s
        idx.append(acc)
    return jnp.split(proj, idx, axis=-1)


def causal_dwconv(xin, buf, w, b):
    xp = jnp.concatenate([buf.astype(xin.dtype), xin], axis=1)
    y = lax.conv_general_dilated(xp, w[:, None, :].astype(xin.dtype), window_strides=(1,), padding='VALID',
                                 dimension_numbers=('NWC', 'WIO', 'NWC'), feature_group_count=xin.shape[-1])
    return y + b, xp[:, xp.shape[1] - (CONV_W - 1):]


def ssd_scan(xs, dt, a, bm, cm, h0):
    b, L, H, P = xs.shape
    G, N = bm.shape[2], bm.shape[3]
    R = H // G
    cl = SSD_CHUNK if L % SSD_CHUNK == 0 else L
    nc = L // cl
    x = xs.astype(jnp.float32).reshape(b, nc, cl, G, R, P)
    dtc = dt.reshape(b, nc, cl, G, R)
    bc = bm.astype(jnp.float32).reshape(b, nc, cl, G, N)
    cc = cm.astype(jnp.float32).reshape(b, nc, cl, G, N)
    a_cum = jnp.cumsum(dtc * a.reshape(G, R), axis=2)
    seg = a_cum[:, :, :, None] - a_cum[:, :, None]
    causal = jnp.tril(jnp.ones((cl, cl), dtype=bool))[:, :, None, None]
    decay = jnp.exp(jnp.where(causal, seg, -jnp.inf))
    cb = jnp.einsum('bclgn,bcsgn->bclsg', cc, bc)
    y_diag = jnp.einsum('bclsgr,bcsgrp->bclgrp', cb[..., None] * decay * dtc[:, :, None], x)
    dx = (jnp.exp(a_cum[:, :, -1:] - a_cum) * dtc)[..., None] * x
    states = jnp.einsum('bclgn,bclgrp->bcgrpn', bc, dx)
    chunk_decay = jnp.exp(a_cum[:, :, -1])

    def step(h, inp):
        st, dec = inp
        return h * dec[..., None, None] + st, h

    h_last, h_prev = lax.scan(step, h0.astype(jnp.float32).reshape(b, G, R, P, N),
                              (jnp.moveaxis(states, 1, 0), jnp.moveaxis(chunk_decay, 1, 0)))
    h_prev = jnp.moveaxis(h_prev, 0, 1)
    y_off = jnp.einsum('bclgn,bcgrpn->bclgrp', cc, h_prev) * jnp.exp(a_cum)[..., None]
    return (y_diag + y_off).reshape(b, L, H, P), h_last.reshape(b, H, P, N)


def t5_bucket(rel):
    n = jnp.maximum(rel, 0)
    max_exact = N_BUCKETS // 2
    large = max_exact + (jnp.log(jnp.maximum(n, 1).astype(jnp.float32) / max_exact)
                         / math.log(MAX_DISTANCE / max_exact) * (N_BUCKETS - max_exact)).astype(jnp.int32)
    large = jnp.minimum(large, N_BUCKETS - 1)
    return jnp.where(n < max_exact, n, large)


def _diff_scores(qi, k, qpos, kpos, rel_bias):
    rel = qpos[:, None] - kpos[None, :]
    bias = jnp.transpose(rel_bias[t5_bucket(rel)], (2, 0, 1)).astype(jnp.float32)
    s = jnp.einsum('bqhmd,bkhmd->bhmqk', qi, k, preferred_element_type=jnp.float32)
    s = s + bias[None, :, None]
    return jnp.where(rel >= 0, s, -jnp.inf)


def diff_attention(q, k_new, v_new, k_past, v_past, lam, rel_bias):
    bsz, L, H = q.shape[0], q.shape[1], q.shape[2]
    P = 0 if k_past is None else k_past.shape[1]
    blk = Q_BLOCK if L % Q_BLOCK == 0 else L
    nb = L // blk
    q_blocks = jnp.moveaxis(q.reshape(bsz, nb, blk, H, 2, ATTN_HEAD_DIM), 1, 0)
    pos_new = P + jnp.arange(L)
    pos_past = jnp.arange(P)

    def block(args):
        qi, i = args
        qpos = P + i * blk + jnp.arange(blk)
        s = _diff_scores(qi, k_new, qpos, pos_new, rel_bias)
        if k_past is not None:
            s = jnp.concatenate([_diff_scores(qi, k_past, qpos, pos_past, rel_bias), s], axis=-1)
        p = jax.nn.softmax(s, axis=-1)
        a = (p[:, :, 0] - lam * p[:, :, 1]).astype(v_new.dtype)
        o = jnp.einsum('bhqk,bkhd->bqhd', a[..., P:], v_new)
        if k_past is not None:
            o = o + jnp.einsum('bhqk,bkhd->bqhd', a[..., :P], v_past)
        return o

    o = lax.map(block, (q_blocks, jnp.arange(nb)))
    return jnp.moveaxis(o, 0, 1).reshape(bsz, L, H, ATTN_V_DIM)


def moe_ffn(u, w_router, b_router, w_gate, w_up, w_down, ws_gate, ws_up, ws_down):
    t = u.shape[0]
    scores = jax.nn.sigmoid(jnp.matmul(u, w_router, preferred_element_type=jnp.float32))
    biased = scores + b_router.astype(jnp.float32)
    per_group = N_EXPERTS // N_EXPERT_GROUPS
    grp_score = lax.top_k(biased.reshape(t, N_EXPERT_GROUPS, per_group), 2)[0].sum(-1)
    _, g_idx = lax.top_k(grp_score, TOPK_GROUPS)
    g_mask = jax.nn.one_hot(g_idx, N_EXPERT_GROUPS, dtype=jnp.float32).sum(1) > 0
    masked = jnp.where(jnp.repeat(g_mask, per_group, axis=1), biased, -jnp.inf)
    _, e_idx = lax.top_k(masked, TOP_K)
    w_sel = jnp.take_along_axis(scores, e_idx, axis=1)
    w_sel = w_sel / jnp.sum(w_sel, axis=-1, keepdims=True) * ROUTED_SCALE
    n_assign = t * TOP_K
    flat_e = e_idx.reshape(n_assign)
    order = jnp.argsort(flat_e)
    se = flat_e[order]
    st = (order // TOP_K).astype(jnp.int32)
    sw = w_sel.reshape(n_assign)[order]
    counts = jnp.bincount(flat_e, length=N_EXPERTS)
    starts = jnp.cumsum(counts) - counts
    padded = (counts + MOE_BLOCK - 1) // MOE_BLOCK * MOE_BLOCK
    pends = jnp.cumsum(padded)
    dest = pends[se] - padded[se] + jnp.arange(n_assign) - starts[se]
    n_rows = (n_assign + N_EXPERTS * (MOE_BLOCK - 1) + MOE_BLOCK - 1) // MOE_BLOCK * MOE_BLOCK
    n_blocks = n_rows // MOE_BLOCK
    row_tok = jnp.full((n_rows,), t, jnp.int32).at[dest].set(st)
    row_w = jnp.zeros((n_rows,), jnp.float32).at[dest].set(sw)
    blk_e = jnp.minimum(jnp.searchsorted(pends, jnp.arange(n_blocks) * MOE_BLOCK, side='right'), N_EXPERTS - 1)
    u_pad = jnp.concatenate([u, jnp.zeros((1, u.shape[1]), u.dtype)], axis=0)

    def expert_block(args):
        tok, e = args
        xb = u_pad[tok]
        return (jax.nn.silu(xb @ w_gate[e]) * (xb @ w_up[e])) @ w_down[e]

    rows = lax.map(expert_block, (row_tok.reshape(n_blocks, MOE_BLOCK), blk_e))
    routed = jax.ops.segment_sum(rows.reshape(n_rows, -1).astype(jnp.float32) * row_w[:, None],
                                 row_tok, num_segments=t + 1)[:t]
    shared = (jax.nn.silu(u @ ws_gate) * (u @ ws_up)) @ ws_down
    return (routed + shared.astype(jnp.float32)).astype(u.dtype)


def decoder_layer(x, c, k_past, v_past, conv_buf, h0, lam_init, p, rel_bias):
    bsz, L, _ = x.shape
    mod = jax.nn.silu(c) @ p['w_ada'] + p['b_ada']
    sh1, sc1, g1, sh2, sc2, g2 = jnp.split(mod[:, None, :], 6, axis=-1)
    u = x * (1.0 + sc1) + sh1
    z, xbc, dt_raw, q, k, v, gl_a, gl_b = _split_in(u @ p['w_in'])
    xbc, conv_new = causal_dwconv(xbc, conv_buf, p['conv_w'], p['conv_b'])
    xbc = jax.nn.silu(xbc)
    xs, bm, cm = jnp.split(xbc, [SSD_INNER, SSD_INNER + SSD_GROUPS * D_STATE], axis=-1)
    xs = xs.reshape(bsz, L, SSD_HEADS, SSD_HEADDIM)
    dt = jax.nn.softplus(dt_raw.astype(jnp.float32) + p['dt_bias'].astype(jnp.float32))
    a = -jnp.exp(p['a_log'].astype(jnp.float32))
    y, h_new = ssd_scan(xs, dt, a, bm.reshape(bsz, L, SSD_GROUPS, D_STATE),
                        cm.reshape(bsz, L, SSD_GROUPS, D_STATE), h0)
    y = y + p['d_skip'].astype(jnp.float32)[:, None] * xs.astype(jnp.float32)
    y = y.reshape(bsz, L, SSD_INNER).astype(x.dtype) * jax.nn.silu(z)
    branch_a = rms_norm(y, p['ssd_norm_w'], SSD_GROUPS) @ p['w_br_ssd']
    q = q.reshape(bsz, L, ATTN_HEADS, 2, ATTN_HEAD_DIM) * (ATTN_HEAD_DIM ** -0.5)
    k = k.reshape(bsz, L, ATTN_HEADS, 2, ATTN_HEAD_DIM)
    v = v.reshape(bsz, L, ATTN_HEADS, ATTN_V_DIM)
    lam = (jnp.exp(jnp.sum(p['lam_q1'].astype(jnp.float32) * p['lam_k1'].astype(jnp.float32)))
           - jnp.exp(jnp.sum(p['lam_q2'].astype(jnp.float32) * p['lam_k2'].astype(jnp.float32))) + lam_init)
    o = diff_attention(q, k, v, k_past, v_past, lam, rel_bias)
    o = rms_norm(o, p['subln_w'], 1) * (1.0 - lam_init)
    branch_b = o.reshape(bsz, L, ATTN_V) @ p['w_br_attn']
    merged = jax.nn.sigmoid(gl_a) * branch_a + jax.nn.sigmoid(gl_b) * branch_b
    x = layer_norm(ALPHA * x + g1 * (merged @ p['w_out']), p['ln1_g'], p['ln1_b'])
    u2 = x * (1.0 + sc2) + sh2
    f = moe_ffn(u2.reshape(bsz * L, D_MODEL), p['w_router'], p['b_router'], p['w_gate'], p['w_up'],
                p['w_down'], p['ws_gate'], p['ws_up'], p['ws_down']).reshape(bsz, L, D_MODEL)
    x = layer_norm(ALPHA * x + g2 * f, p['ln2_g'], p['ln2_b'])
    return x, k.reshape(bsz, L, ATTN_HEADS, 2 * ATTN_HEAD_DIM), v, conv_new, h_new.astype(x.dtype)


def setup_inputs(seed: int = 0) -> dict:
    key = jax.random.key(seed)
    ks = iter(jax.random.split(key, 64))

    def nrm(shape, scale):
        return jax.random.normal(next(ks), shape, jnp.float32) * scale

    n_pages = PAST_LEN // PAGE_SIZE
    n_pool = (DEC_BATCH * n_pages * 5) // 4
    page_table = jax.random.permutation(next(ks), n_pool)[:DEC_BATCH * n_pages].reshape(DEC_BATCH, n_pages).astype(jnp.int32)
    dsc = D_MODEL ** -0.5
    w_in = jnp.concatenate([
        nrm((DEPTH, D_MODEL, SSD_INNER + CONV_DIM + SSD_HEADS), dsc),
        nrm((DEPTH, D_MODEL, 2 * ATTN_QK), dsc),
        nrm((DEPTH, D_MODEL, ATTN_V), dsc * BETA),
        nrm((DEPTH, D_MODEL, 2 * D_MODEL), dsc)], axis=-1)
    u_dt = jax.random.uniform(next(ks), (DEPTH, SSD_HEADS), jnp.float32)
    dt0 = jnp.exp(u_dt * (math.log(DT_MAX) - math.log(DT_MIN)) + math.log(DT_MIN))
    dt_bias = dt0 + jnp.log(-jnp.expm1(-dt0))
    a_log = jnp.log(jax.random.uniform(next(ks), (DEPTH, SSD_HEADS), jnp.float32, minval=1.0, maxval=16.0))
    return {
        'x_prompt': nrm((BATCH, SEQ, D_MODEL), 1.0),
        'x_sample': nrm((DEC_BATCH, DEC_SEQ, D_MODEL), 1.0),
        'c_prompt': nrm((BATCH, D_MODEL), 1.0),
        'c_sample': nrm((DEC_BATCH, D_MODEL), 1.0),
        'cache_k': nrm((DEPTH, n_pool, PAGE_SIZE, ATTN_HEADS, 2 * ATTN_HEAD_DIM), 1.0),
        'cache_v': nrm((DEPTH, n_pool, PAGE_SIZE, ATTN_HEADS, ATTN_V_DIM), 1.0),
        'page_table': page_table,
        'state_conv': nrm((DEPTH, DEC_BATCH, CONV_W - 1, CONV_DIM), 1.0),
        'state_ssm': nrm((DEPTH, DEC_BATCH, SSD_HEADS, SSD_HEADDIM, D_STATE), 1.0),
        'rel_bias': nrm((N_BUCKETS, ATTN_HEADS), 0.5),
        'w_ada': nrm((DEPTH, D_MODEL, 6 * D_MODEL), 0.5 * dsc),
        'b_ada': nrm((DEPTH, 6 * D_MODEL), 0.01),
        'w_in': w_in,
        'conv_w': nrm((DEPTH, CONV_W, CONV_DIM), CONV_W ** -0.5),
        'conv_b': nrm((DEPTH, CONV_DIM), 0.01),
        'dt_bias': dt_bias,
        'a_log': a_log,
        'd_skip': 1.0 + nrm((DEPTH, SSD_HEADS), 0.01),
        'ssd_norm_w': 1.0 + nrm((DEPTH, SSD_INNER), 0.01),
        'lam_q1': nrm((DEPTH, ATTN_HEAD_DIM), 0.1),
        'lam_k1': nrm((DEPTH, ATTN_HEAD_DIM), 0.1),
        'lam_q2': nrm((DEPTH, ATTN_HEAD_DIM), 0.1),
        'lam_k2': nrm((DEPTH, ATTN_HEAD_DIM), 0.1),
        'subln_w': 1.0 + nrm((DEPTH, ATTN_V_DIM), 0.01),
        'w_br_ssd': nrm((DEPTH, SSD_INNER, D_MODEL), SSD_INNER ** -0.5 * BETA),
        'w_br_attn': nrm((DEPTH, ATTN_V, D_MODEL), ATTN_V ** -0.5 * BETA),
        'w_out': nrm((DEPTH, D_MODEL, D_MODEL), dsc * BETA),
        'ln1_g': 1.0 + nrm((DEPTH, D_MODEL), 0.01),
        'ln1_b': nrm((DEPTH, D_MODEL), 0.01),
        'w_router': nrm((DEPTH, D_MODEL, N_EXPERTS), dsc),
        'b_router': nrm((DEPTH, N_EXPERTS), 0.01),
        'w_gate': nrm((DEPTH, N_EXPERTS, D_MODEL, D_EXPERT), dsc),
        'w_up': nrm((DEPTH, N_EXPERTS, D_MODEL, D_EXPERT), dsc),
        'w_down': nrm((DEPTH, N_EXPERTS, D_EXPERT, D_MODEL), D_EXPERT ** -0.5 * BETA),
        'ws_gate': nrm((DEPTH, D_MODEL, D_SHARED), dsc),
        'ws_up': nrm((DEPTH, D_MODEL, D_SHARED), dsc),
        'ws_down': nrm((DEPTH, D_SHARED, D_MODEL), D_SHARED ** -0.5 * BETA),
        'ln2_g': 1.0 + nrm((DEPTH, D_MODEL), 0.01),
        'ln2_b': nrm((DEPTH, D_MODEL), 0.01),
    }


def reference(x_prompt, x_sample, c_prompt, c_sample, cache_k, cache_v, page_table, state_conv, state_ssm,
              rel_bias, w_ada, b_ada, w_in, conv_w, conv_b, dt_bias, a_log, d_skip, ssd_norm_w,
              lam_q1, lam_k1, lam_q2, lam_k2, subln_w, w_br_ssd, w_br_attn, w_out, ln1_g, ln1_b,
              w_router, b_router, w_gate, w_up, w_down, ws_gate, ws_up, ws_down, ln2_g, ln2_b):
    yp, ys = x_prompt, x_sample
    kp_l, vp_l, cp_l, hp_l = [], [], [], []
    ks_l, vs_l, cs_l, hs_l = [], [], [], []
    n_dec = page_table.shape[0]
    for l in range(DEPTH):
        p = dict(w_ada=w_ada[l], b_ada=b_ada[l], w_in=w_in[l], conv_w=conv_w[l], conv_b=conv_b[l],
                 dt_bias=dt_bias[l], a_log=a_log[l], d_skip=d_skip[l], ssd_norm_w=ssd_norm_w[l],
                 lam_q1=lam_q1[l], lam_k1=lam_k1[l], lam_q2=lam_q2[l], lam_k2=lam_k2[l], subln_w=subln_w[l],
                 w_br_ssd=w_br_ssd[l], w_br_attn=w_br_attn[l], w_out=w_out[l], ln1_g=ln1_g[l], ln1_b=ln1_b[l],
                 w_router=w_router[l], b_router=b_router[l], w_gate=w_gate[l], w_up=w_up[l], w_down=w_down[l],
                 ws_gate=ws_gate[l], ws_up=ws_up[l], ws_down=ws_down[l], ln2_g=ln2_g[l], ln2_b=ln2_b[l])
        lam_init = 0.8 - 0.6 * math.exp(-0.3 * l)
        conv0 = jnp.zeros((yp.shape[0], CONV_W - 1, CONV_DIM), yp.dtype)
        h0 = jnp.zeros((yp.shape[0], SSD_HEADS, SSD_HEADDIM, D_STATE), jnp.float32)
        yp, k_r, v_r, c_r, h_r = decoder_layer(yp, c_prompt, None, None, conv0, h0, lam_init, p, rel_bias)
        kp_l.append(k_r); vp_l.append(v_r); cp_l.append(c_r); hp_l.append(h_r)
        k_past = cache_k[l, page_table].reshape(n_dec, -1, ATTN_HEADS, 2, ATTN_HEAD_DIM)
        v_past = cache_v[l, page_table].reshape(n_dec, -1, ATTN_HEADS, ATTN_V_DIM)
        ys, k_r, v_r, c_r, h_r = decoder_layer(ys, c_sample, k_past, v_past, state_conv[l],
                                               state_ssm[l].astype(jnp.float32), lam_init, p, rel_bias)
        ks_l.append(k_r); vs_l.append(v_r); cs_l.append(c_r); hs_l.append(h_r)
    return (yp, ys, jnp.stack(kp_l), jnp.stack(vp_l), jnp.stack(cp_l), jnp.stack(hp_l),
            jnp.stack(ks_l), jnp.stack(vs_l), jnp.stack(cs_l), jnp.stack(hs_l))
```

```python
import functools
import math

import jax
import jax.numpy as jnp
from jax import lax
from jax.experimental import pallas as pl
from jax.experimental.pallas import tpu as pltpu

F32 = jnp.float32
BF16 = jnp.bfloat16
HIGHEST = lax.Precision.HIGHEST

D_MODEL = 1024
SSD_INNER = 1024
SSD_HEADDIM = 64
SSD_HEADS = 16
SSD_GROUPS = 2
D_STATE = 128
CONV_W = 4
CONV_DIM = SSD_INNER + 2 * SSD_GROUPS * D_STATE
SSD_CHUNK = 128
ATTN_HEADS = 8
ATTN_HEAD_DIM = 64
ATTN_V_DIM = 128
N_BUCKETS = 32
MAX_DISTANCE = 128
N_EXPERTS = 64
N_EXPERT_GROUPS = 8
GROUP_SIZE = N_EXPERTS // N_EXPERT_GROUPS
TOPK_GROUPS = 4
TOP_K = 8
D_EXPERT = 256
D_SHARED = 256
ROUTED_SCALE = 2.5
PAGE_SIZE = 128
DEPTH = 1
ALPHA = (2 * DEPTH) ** 0.25
LN_EPS = 1e-5
RMS_EPS = 1e-5
LAM_INIT = 0.8 - 0.6 * math.exp(-0.3 * 0)
LANE = 128
VMEM_LIMIT = 56 * 1024 * 1024

SEG_Z, SEG_XBC, SEG_Q, SEG_K, SEG_V, SEG_GA, SEG_GB, SEG_DT, SEG_END = (
    0, 1024, 2560, 3584, 4608, 5632, 6656, 7680, 7808)


def _silu(x):
    return x * jax.nn.sigmoid(x)


def _dot(a, b):
    return jnp.dot(a, b, preferred_element_type=F32)


def _dot_nt(a, b):
    return lax.dot_general(a, b, (((1,), (1,)), ((), ())), preferred_element_type=F32)


def _dot_tn(a, b):
    return lax.dot_general(a, b, (((0,), (0,)), ((), ())), preferred_element_type=F32)


def _params(*sem):
    return pltpu.CompilerParams(dimension_semantics=sem, vmem_limit_bytes=VMEM_LIMIT)


def _adaln_kernel(c_ref, w_ref, b_ref, o_ref):
    s = _silu(c_ref[...]).astype(BF16)
    o_ref[...] = _dot(s, w_ref[...].astype(BF16)) + b_ref[...]


def _adaln(c, w_ada, b_ada):
    r = c.shape[0]
    n = w_ada.shape[1]
    tn = 1024
    return pl.pallas_call(
        _adaln_kernel,
        grid=(n // tn,),
        in_specs=[pl.BlockSpec((r, D_MODEL), lambda j: (0, 0)),
                  pl.BlockSpec((D_MODEL, tn), lambda j: (0, j)),
                  pl.BlockSpec((1, tn), lambda j: (0, j))],
        out_specs=pl.BlockSpec((r, tn), lambda j: (0, j)),
        out_shape=jax.ShapeDtypeStruct((r, n), F32),
        compiler_params=_params("arbitrary"),
        name="adaln",
    )(c, w_ada, b_ada.reshape(1, n))


def _inproj_kernel(x_ref, sc_ref, sh_ref, w_ref, z_ref, xbc_ref, dt_ref, q_ref, k_ref, v_ref,
                   kb_ref, vb_ref, ga_ref, gb_ref):
    u = (x_ref[...] * (1.0 + sc_ref[...]) + sh_ref[...]).astype(BF16)

    def seg(a, b):
        return _dot(u, w_ref[:, a:b])

    z_ref[...] = seg(SEG_Z, SEG_XBC).astype(BF16)
    xbc_ref[...] = seg(SEG_XBC, SEG_Q)
    q_ref[...] = (seg(SEG_Q, SEG_K) * (ATTN_HEAD_DIM ** -0.5)).astype(BF16)
    kk = seg(SEG_K, SEG_V)
    k_ref[...] = kk
    kb_ref[...] = kk.astype(BF16)
    vv = seg(SEG_V, SEG_GA)
    v_ref[...] = vv
    vb_ref[...] = vv.astype(BF16)
    ga_ref[...] = seg(SEG_GA, SEG_GB).astype(BF16)
    gb_ref[...] = seg(SEG_GB, SEG_DT).astype(BF16)
    dt_ref[...] = seg(SEG_DT, SEG_END)


def _inproj(x, sc, sh, w_cat, rows_per_mod):
    t = x.shape[0]
    tm = min(256, t)
    per_row = sc.shape[1] != 1
    if per_row:
        sc2 = sc.reshape(t, D_MODEL)
        sh2 = sh.reshape(t, D_MODEL)
        mod_spec = pl.BlockSpec((tm, D_MODEL), lambda i: (i, 0))
    else:
        assert rows_per_mod % tm == 0
        sc2, sh2 = sc, sh
        mod_spec = pl.BlockSpec((None, 1, D_MODEL), lambda i: (i // (rows_per_mod // tm), 0, 0))

    def rows(width):
        return pl.BlockSpec((tm, width), lambda i: (i, 0))

    widths = (1024, CONV_DIM, LANE, 1024, 1024, 1024, 1024, 1024, 1024, 1024)
    dtypes = (BF16, F32, F32, BF16, F32, F32, BF16, BF16, BF16, BF16)
    return pl.pallas_call(
        _inproj_kernel,
        grid=(t // tm,),
        in_specs=[rows(D_MODEL), mod_spec, mod_spec,
                  pl.BlockSpec((D_MODEL, SEG_END), lambda i: (0, 0), pipeline_mode=pl.Buffered(1))],
        out_specs=[rows(w) for w in widths],
        out_shape=[jax.ShapeDtypeStruct((t, w), d) for w, d in zip(widths, dtypes)],
        compiler_params=_params("arbitrary"),
        name="inproj",
    )(x, sc2, sh2, w_cat)


def _ssd_kernel(xbc_ref, dt_ref, z_ref, cw_ref, cb_ref, dtb_ref, alog_ref, dsk_ref, nw_ref,
                cbuf_ref, h0_ref, yn_ref, cnew_ref, hnew_ref, xp_scr, h_scr, y_scr, *, valid_len):
    c = pl.program_id(1)
    nc = pl.num_programs(1)
    cl = SSD_CHUNK
    head = 8

    @pl.when(c == 0)
    def _():
        xp_scr[head - (CONV_W - 1):head, :] = cbuf_ref[...]
        h_scr[...] = h0_ref[...]

    @pl.when(c > 0)
    def _():
        xp_scr[head - (CONV_W - 1):head, :] = xp_scr[head + cl - (CONV_W - 1):head + cl, :]

    xp_scr[head:head + cl, :] = xbc_ref[...]

    acc = xp_scr[head:head + cl, :] * cw_ref[CONV_W - 1:CONV_W, :]
    for j in range(CONV_W - 1):
        lo = head - (CONV_W - 1) + j
        acc = acc + xp_scr[lo:lo + cl, :] * cw_ref[j:j + 1, :]
    xc = _silu(acc + cb_ref[...])
    xs = xc[:, :SSD_INNER]
    bmat = [xc[:, SSD_INNER + g * D_STATE:SSD_INNER + (g + 1) * D_STATE].astype(BF16)
            for g in range(SSD_GROUPS)]
    coff = SSD_INNER + SSD_GROUPS * D_STATE
    cmat = [xc[:, coff + g * D_STATE:coff + (g + 1) * D_STATE].astype(BF16) for g in range(SSD_GROUPS)]

    li = lax.broadcasted_iota(jnp.int32, (cl, cl), 0)
    si = lax.broadcasted_iota(jnp.int32, (cl, cl), 1)
    causal = li >= si
    lane_lo = si < SSD_HEADDIM
    sub_lo = li < SSD_HEADDIM

    dpre = dt_ref[...] + dtb_ref[...]
    dtv = jnp.maximum(dpre, 0.0) + jnp.log(1.0 + jnp.exp(-jnp.abs(dpre)))
    if valid_len < cl:
        dtv = jnp.where(li < valid_len, dtv, 0.0)
    da = dtv * (-jnp.exp(alog_ref[...]))
    tril = causal.astype(F32)
    acum = jnp.dot(tril, da, preferred_element_type=F32, precision=HIGHEST)
    acum_t = acum.T
    dt_t = dtv.T
    last = acum[cl - 1:cl, :]

    cb = [_dot_nt(cmat[g], bmat[g]) for g in range(SSD_GROUPS)]

    def colb(a, h):
        return jnp.broadcast_to(a[:, h:h + 1], (cl, cl))

    pairs = SSD_HEADS // 2
    for p in range(pairs):
        h0, h1 = 2 * p, 2 * p + 1
        g = h0 // (SSD_HEADS // SSD_GROUPS)
        xpair = xs[:, p * LANE:(p + 1) * LANE]
        x_lo = jnp.where(lane_lo, xpair, 0.0).astype(BF16)
        x_hi = jnp.where(lane_lo, 0.0, xpair).astype(BF16)
        ydiag = None
        for hh, xm in ((h0, x_lo), (h1, x_hi)):
            seg = colb(acum, hh) - acum_t[hh:hh + 1, :]
            dec = jnp.exp(jnp.where(causal, seg, -jnp.inf))
            m = (cb[g] * dec * dt_t[hh:hh + 1, :]).astype(BF16)
            part = _dot(m, xm)
            ydiag = part if ydiag is None else ydiag + part
        col0, col1 = colb(acum, h0), colb(acum, h1)
        ecol = jnp.where(lane_lo, jnp.exp(col0), jnp.exp(col1))
        hp = h_scr[p]
        yoff = _dot_nt(cmat[g], hp.astype(BF16)) * ecol
        l0, l1 = last[:, h0:h0 + 1], last[:, h1:h1 + 1]
        wcol = jnp.where(lane_lo, jnp.exp(l0 - col0) * colb(dtv, h0), jnp.exp(l1 - col1) * colb(dtv, h1))
        dx = (xpair * wcol).astype(BF16)
        st = _dot_tn(dx, bmat[g])
        hdec = jnp.where(sub_lo, jnp.exp(l0), jnp.exp(l1))
        h_scr[p] = hp * hdec + st
        y_scr[:, p * LANE:(p + 1) * LANE] = ydiag + yoff + dsk_ref[:, p * LANE:(p + 1) * LANE] * xpair

    zf = z_ref[...].astype(F32)
    gated = y_scr[...] * _silu(zf)
    gw = SSD_INNER // SSD_GROUPS
    for g in range(SSD_GROUPS):
        sg = gated[:, g * gw:(g + 1) * gw]
        ms = jnp.mean(sg * sg, axis=-1, keepdims=True)
        yn_ref[:, g * gw:(g + 1) * gw] = (sg * lax.rsqrt(ms + RMS_EPS) * nw_ref[:, g * gw:(g + 1) * gw]).astype(BF16)

    @pl.when(c == nc - 1)
    def _():
        cnew_ref[...] = xp_scr[head + valid_len - (CONV_W - 1):head + valid_len, :]
        hnew_ref[...] = h_scr[...]


def _ssd(xbc, dt, z, conv_w, conv_b, dt_bias, a_log, d_skip, norm_w, cbuf, h0, nb, nc, valid_len):
    cl = SSD_CHUNK
    pairs = SSD_HEADS // 2
    pad = LANE - SSD_HEADS
    row = lambda b, c: (b * nc + c, 0)
    const = lambda b, c: (0, 0)
    return pl.pallas_call(
        functools.partial(_ssd_kernel, valid_len=valid_len),
        grid=(nb, nc),
        in_specs=[pl.BlockSpec((cl, CONV_DIM), row), pl.BlockSpec((cl, LANE), row),
                  pl.BlockSpec((cl, SSD_INNER), row),
                  pl.BlockSpec((CONV_W, CONV_DIM), const), pl.BlockSpec((1, CONV_DIM), const),
                  pl.BlockSpec((1, LANE), const), pl.BlockSpec((1, LANE), const),
                  pl.BlockSpec((1, SSD_INNER), const), pl.BlockSpec((1, SSD_INNER), const),
                  pl.BlockSpec((None, CONV_W - 1, CONV_DIM), lambda b, c: (b, 0, 0)),
                  pl.BlockSpec((None, pairs, LANE, D_STATE), lambda b, c: (b, 0, 0, 0))],
        out_specs=[pl.BlockSpec((cl, SSD_INNER), row),
                   pl.BlockSpec((None, CONV_W - 1, CONV_DIM), lambda b, c: (b, 0, 0)),
                   pl.BlockSpec((None, pairs, LANE, D_STATE), lambda b, c: (b, 0, 0, 0))],
        out_shape=[jax.ShapeDtypeStruct((nb * nc * cl, SSD_INNER), BF16),
                   jax.ShapeDtypeStruct((nb, CONV_W - 1, CONV_DIM), F32),
                   jax.ShapeDtypeStruct((nb, pairs, LANE, D_STATE), F32)],
        scratch_shapes=[pltpu.VMEM((8 + cl, CONV_DIM), F32), pltpu.VMEM((pairs, LANE, D_STATE), F32),
                        pltpu.VMEM((cl, SSD_INNER), F32)],
        compiler_params=_params("arbitrary", "arbitrary"),
        name="ssd",
    )(xbc, dt, z, conv_w, conv_b.reshape(1, CONV_DIM),
      jnp.pad(dt_bias, (0, pad)).reshape(1, LANE), jnp.pad(a_log, (0, pad)).reshape(1, LANE),
      jnp.repeat(d_skip, SSD_HEADDIM).reshape(1, SSD_INNER), norm_w.reshape(1, SSD_INNER),
      cbuf, h0.reshape(nb, pairs, LANE, D_STATE))


def _bias_kernel(rel_ref, tab_ref, o_ref):
    rel = rel_ref[...]
    n = jnp.maximum(rel, 0)
    max_exact = N_BUCKETS // 2
    large = max_exact + (jnp.log(jnp.maximum(n, 1).astype(F32) / max_exact)
                         / math.log(MAX_DISTANCE / max_exact) * (N_BUCKETS - max_exact)).astype(jnp.int32)
    bucket = jnp.where(n < max_exact, n, jnp.minimum(large, N_BUCKETS - 1))
    outs = [jnp.zeros(rel.shape, F32) for _ in range(ATTN_HEADS)]
    for kb in range(N_BUCKETS):
        hit = bucket == kb
        for h in range(ATTN_HEADS):
            outs[h] = jnp.where(hit, tab_ref[kb, h], outs[h])
    for h in range(ATTN_HEADS):
        o_ref[h] = jnp.where(rel >= 0, outs[h], -jnp.inf)


def _bias_tiles(rel, rel_bias):
    r, c = rel.shape
    tr = min(r, 64)
    return pl.pallas_call(
        _bias_kernel,
        grid=(r // tr,),
        in_specs=[pl.BlockSpec((tr, c), lambda i: (i, 0)),
                  pl.BlockSpec(memory_space=pltpu.SMEM)],
        out_specs=pl.BlockSpec((ATTN_HEADS, tr, c), lambda i: (0, i, 0)),
        out_shape=jax.ShapeDtypeStruct((ATTN_HEADS, r, c), F32),
        compiler_params=_params("arbitrary"),
        name="t5bias",
    )(rel, rel_bias)


def _lam(lamp_ref):
    lp = lamp_ref[...]
    e1 = jnp.exp(jnp.sum(lp[0:1, :] * lp[1:2, :], axis=-1, keepdims=True))
    e2 = jnp.exp(jnp.sum(lp[2:3, :] * lp[3:4, :], axis=-1, keepdims=True))
    return e1 - e2 + LAM_INIT


def _subln(o, sw):
    ms = jnp.mean(o * o, axis=-1, keepdims=True)
    return o * lax.rsqrt(ms + RMS_EPS) * sw * (1.0 - LAM_INIT)


def _attn_kernel(q_ref, k_ref, v_ref, bias_ref, lamp_ref, sw_ref, o_ref, *, tq):
    qi = pl.program_id(2)
    q = q_ref[...]
    lane_lo = lax.broadcasted_iota(jnp.int32, q.shape, 1) < ATTN_HEAD_DIM
    zero = jnp.zeros_like(q)
    qm = (jnp.where(lane_lo, q, zero), jnp.where(lane_lo, zero, q))

    def body(j, carry):
        off = pl.multiple_of(j * tq, tq)
        kt = k_ref[pl.ds(off, tq), :]
        vt = v_ref[pl.ds(off, tq), :]
        bias = bias_ref[jnp.minimum(qi - j, 2)]
        new = []
        for mi in range(2):
            m, l, a = carry[3 * mi:3 * mi + 3]
            s = _dot_nt(qm[mi], kt) + bias
            mn = jnp.maximum(m, jnp.max(s, axis=-1, keepdims=True))
            al = jnp.exp(m - mn)
            p = jnp.exp(s - mn)
            new += [mn, al * l + jnp.sum(p, axis=-1, keepdims=True), al * a + _dot(p.astype(BF16), vt)]
        return tuple(new)

    init = (jnp.full((tq, 1), -jnp.inf, F32), jnp.zeros((tq, 1), F32), jnp.zeros((tq, ATTN_V_DIM), F32)) * 2
    m0, l0, a0, m1, l1, a1 = lax.fori_loop(0, qi + 1, body, init)
    o = a0 / l0 - _lam(lamp_ref) * (a1 / l1)
    o_ref[...] = _subln(o, sw_ref[...]).astype(BF16)


def _attn_prompt(q, kb, vb, bias, lamp, subln_w, nb, seq):
    tq = min(256, seq)
    nq = seq // tq
    return pl.pallas_call(
        functools.partial(_attn_kernel, tq=tq),
        grid=(nb, ATTN_HEADS, nq),
        in_specs=[pl.BlockSpec((tq, LANE), lambda b, h, i: (b * nq + i, h)),
                  pl.BlockSpec((seq, LANE), lambda b, h, i: (b, h)),
                  pl.BlockSpec((seq, LANE), lambda b, h, i: (b, h)),
                  pl.BlockSpec((None, 3, tq, tq), lambda b, h, i: (h, 0, 0, 0)),
                  pl.BlockSpec((4, ATTN_HEAD_DIM), lambda b, h, i: (0, 0)),
                  pl.BlockSpec((1, ATTN_V_DIM), lambda b, h, i: (0, 0))],
        out_specs=pl.BlockSpec((tq, LANE), lambda b, h, i: (b * nq + i, h)),
        out_shape=jax.ShapeDtypeStruct((nb * seq, ATTN_HEADS * ATTN_V_DIM), BF16),
        compiler_params=_params("arbitrary", "arbitrary", "arbitrary"),
        name="attn_prompt",
    )(q, kb, vb, bias, lamp, subln_w.reshape(1, ATTN_V_DIM))


def _decode_kernel(pt_ref, q_ref, kn_ref, vn_ref, ck_ref, cv_ref, bias_ref, tab_ref, lamp_ref, sw_ref,
                   o_ref, qh_scr, m_scr, l_scr, a_scr, *, nq):
    j = pl.program_id(1)
    npages = pl.num_programs(1)
    rows = 2 * nq
    lane_lo = lax.broadcasted_iota(jnp.int32, (nq, LANE), 1) < ATTN_HEAD_DIM

    def update(h, s, vmat, first):
        if first:
            mn = jnp.max(s, axis=-1, keepdims=True)
            p = jnp.exp(s - mn)
            l_new = jnp.sum(p, axis=-1, keepdims=True)
            a_new = _dot(p.astype(BF16), vmat)
        else:
            m = m_scr[h]
            mn = jnp.maximum(m, jnp.max(s, axis=-1, keepdims=True))
            al = jnp.exp(m - mn)
            p = jnp.exp(s - mn)
            l_new = al * l_scr[h] + jnp.sum(p, axis=-1, keepdims=True)
            a_new = al * a_scr[h] + _dot(p.astype(BF16), vmat)
        m_scr[h] = mn
        l_scr[h] = l_new
        a_scr[h] = a_new

    @pl.when(j == 0)
    def _():
        zpad = jnp.zeros((PAGE_SIZE - nq, LANE), F32)
        for h in range(ATTN_HEADS):
            qh = q_ref[:, h * LANE:(h + 1) * LANE].astype(F32)
            qh_scr[h] = jnp.concatenate([jnp.where(lane_lo, qh, 0.0), jnp.where(lane_lo, 0.0, qh)], axis=0).astype(BF16)
            kh = jnp.concatenate([kn_ref[:, h * LANE:(h + 1) * LANE], zpad], axis=0).astype(BF16)
            vh = jnp.concatenate([vn_ref[:, h * LANE:(h + 1) * LANE], zpad], axis=0).astype(BF16)
            s = _dot_nt(qh_scr[h], kh) + bias_ref[h, 1]
            update(h, s, vh, True)

    is_last = j == npages - 1
    for h in range(ATTN_HEADS):
        kh = ck_ref[:, h, :].astype(BF16)
        vh = cv_ref[:, h, :].astype(BF16)
        far = tab_ref[N_BUCKETS - 1, h]
        bias = jnp.where(is_last, bias_ref[h, 0], far)
        s = _dot_nt(qh_scr[h], kh) + bias
        update(h, s, vh, False)

    @pl.when(is_last)
    def _():
        lam = _lam(lamp_ref)
        for h in range(ATTN_HEADS):
            on = a_scr[h] / l_scr[h]
            o = on[0:nq, :] - lam * on[nq:rows, :]
            o_ref[:, h * LANE:(h + 1) * LANE] = _subln(o, sw_ref[...])


def _attn_decode(q, k_new, v_new, cache_k, cache_v, page_table, bias, rel_bias, lamp, subln_w):
    nb, nq, _ = q.shape
    npages = page_table.shape[1]
    rows = 2 * nq
    new = pl.BlockSpec((None, nq, ATTN_HEADS * LANE), lambda b, j, pt: (b, 0, 0))
    page = pl.BlockSpec((None, PAGE_SIZE, ATTN_HEADS, LANE), lambda b, j, pt: (pt[b, j], 0, 0, 0))
    grid_spec = pltpu.PrefetchScalarGridSpec(
        num_scalar_prefetch=1,
        grid=(nb, npages),
        in_specs=[new, new, new, page, page,
                  pl.BlockSpec((ATTN_HEADS, 2, rows, PAGE_SIZE), lambda b, j, pt: (0, 0, 0, 0)),
                  pl.BlockSpec(memory_space=pltpu.SMEM),
                  pl.BlockSpec((4, ATTN_HEAD_DIM), lambda b, j, pt: (0, 0)),
                  pl.BlockSpec((1, ATTN_V_DIM), lambda b, j, pt: (0, 0))],
        out_specs=new,
        scratch_shapes=[pltpu.VMEM((ATTN_HEADS, rows, LANE), BF16), pltpu.VMEM((ATTN_HEADS, rows, 1), F32),
                        pltpu.VMEM((ATTN_HEADS, rows, 1), F32), pltpu.VMEM((ATTN_HEADS, rows, LANE), F32)])
    return pl.pallas_call(
        functools.partial(_decode_kernel, nq=nq),
        grid_spec=grid_spec,
        out_shape=jax.ShapeDtypeStruct((nb, nq, ATTN_HEADS * ATTN_V_DIM), F32),
        compiler_params=_params("arbitrary", "arbitrary"),
        name="attn_decode",
    )(page_table, q, k_new, v_new, cache_k, cache_v, bias, rel_bias, lamp, subln_w.reshape(1, ATTN_V_DIM))


def _layer_norm(r, g, b):
    mu = jnp.mean(r, axis=-1, keepdims=True)
    d = r - mu
    var = jnp.mean(d * d, axis=-1, keepdims=True)
    return d * lax.rsqrt(var + LN_EPS) * g + b


def _merge_kernel(yn_ref, on_ref, ga_ref, gb_ref, x_ref, g1_ref, sc2_ref, sh2_ref, wa_ref, wb_ref, wo_ref,
                  lg_ref, lb_ref, x1_ref, u2_ref):
    ba = _dot(yn_ref[...].astype(BF16), wa_ref[...])
    bb = _dot(on_ref[...].astype(BF16), wb_ref[...])
    merged = jax.nn.sigmoid(ga_ref[...].astype(F32)) * ba + jax.nn.sigmoid(gb_ref[...].astype(F32)) * bb
    t = _dot(merged.astype(BF16), wo_ref[...])
    x1 = _layer_norm(ALPHA * x_ref[...] + g1_ref[...] * t, lg_ref[...], lb_ref[...])
    x1_ref[...] = x1
    u2_ref[...] = x1 * (1.0 + sc2_ref[...]) + sh2_ref[...]


def _mod_spec(mod, t, tm, rows_per_mod):
    if mod.shape[1] != 1:
        return mod.reshape(t, D_MODEL), pl.BlockSpec((tm, D_MODEL), lambda i: (i, 0))
    assert rows_per_mod % tm == 0
    return mod, pl.BlockSpec((None, 1, D_MODEL), lambda i: (i // (rows_per_mod // tm), 0, 0))


def _merge(yn, on, ga, gb, x, g1, sc2, sh2, wa, wb, wo, ln_g, ln_b, rows_per_mod):
    t = x.shape[0]
    tm = min(256, t)
    rows = pl.BlockSpec((tm, D_MODEL), lambda i: (i, 0))
    wspec = pl.BlockSpec((D_MODEL, D_MODEL), lambda i: (0, 0))
    vec = pl.BlockSpec((1, D_MODEL), lambda i: (0, 0))
    g1a, mspec = _mod_spec(g1, t, tm, rows_per_mod)
    sc2a, _ = _mod_spec(sc2, t, tm, rows_per_mod)
    sh2a, _ = _mod_spec(sh2, t, tm, rows_per_mod)
    return pl.pallas_call(
        _merge_kernel,
        grid=(t // tm,),
        in_specs=[rows, rows, rows, rows, rows, mspec, mspec, mspec, wspec, wspec, wspec, vec, vec],
        out_specs=[rows, rows],
        out_shape=[jax.ShapeDtypeStruct((t, D_MODEL), F32)] * 2,
        compiler_params=_params("arbitrary"),
        name="merge",
    )(yn, on, ga, gb, x, g1a, sc2a, sh2a, wa, wb, wo, ln_g.reshape(1, D_MODEL), ln_b.reshape(1, D_MODEL))


def _router_kernel(u_ref, wr_ref, br_ref, e_ref, w_ref, r_ref, cnt_ref, carry_scr, tri_scr, *, tr):
    i = pl.program_id(0)

    @pl.when(i == 0)
    def _():
        carry_scr[...] = jnp.zeros_like(carry_scr)
        a = lax.broadcasted_iota(jnp.int32, (tr, tr), 0)
        b = lax.broadcasted_iota(jnp.int32, (tr, tr), 1)
        tri_scr[...] = (a < b).astype(BF16)

    logits = lax.dot_general(wr_ref[...], u_ref[...], (((1,), (1,)), ((), ())),
                             preferred_element_type=F32, precision=HIGHEST)
    scores = jax.nn.sigmoid(logits)
    biased = scores + br_ref[...]
    ninf = -jnp.inf

    b3 = biased.reshape(N_EXPERT_GROUPS, GROUP_SIZE, tr)
    j3 = lax.broadcasted_iota(jnp.int32, b3.shape, 1).astype(F32)
    top1 = jnp.max(b3, axis=1, keepdims=True)
    first = jnp.min(jnp.where(b3 == top1, j3, float(GROUP_SIZE)), axis=1, keepdims=True)
    top2 = jnp.max(jnp.where(j3 == first, ninf, b3), axis=1, keepdims=True)
    gscore = (top1 + top2).reshape(N_EXPERT_GROUPS, tr)
    gi = lax.broadcasted_iota(jnp.int32, gscore.shape, 0).astype(F32)
    gsel = jnp.zeros(gscore.shape, F32)
    for _ in range(TOPK_GROUPS):
        mx = jnp.max(gscore, axis=0, keepdims=True)
        pick = gi == jnp.min(jnp.where(gscore == mx, gi, float(N_EXPERT_GROUPS)), axis=0, keepdims=True)
        gsel = jnp.where(pick, 1.0, gsel)
        gscore = jnp.where(pick, ninf, gscore)
    emask = jnp.broadcast_to(gsel.reshape(N_EXPERT_GROUPS, 1, tr), b3.shape).reshape(N_EXPERTS, tr)
    masked = jnp.where(emask > 0.5, biased, ninf)

    ei = lax.broadcasted_iota(jnp.int32, masked.shape, 0).astype(F32)
    picked = jnp.zeros(masked.shape, F32)
    idxs, wsel = [], []
    for _ in range(TOP_K):
        mx = jnp.max(masked, axis=0, keepdims=True)
        idx = jnp.min(jnp.where(masked == mx, ei, float(N_EXPERTS)), axis=0, keepdims=True)
        pick = ei == idx
        idxs.append(idx)
        wsel.append(jnp.sum(jnp.where(pick, scores, 0.0), axis=0, keepdims=True))
        picked = jnp.where(pick, 1.0, picked)
        masked = jnp.where(pick, ninf, masked)
    wall = jnp.concatenate(wsel, axis=0)
    w_ref[...] = wall / jnp.sum(wall, axis=0, keepdims=True) * ROUTED_SCALE
    e_ref[...] = jnp.concatenate(idxs, axis=0).astype(jnp.int32)

    rank = carry_scr[:, 0:1] + _dot(picked.astype(BF16), tri_scr[...])
    r_ref[...] = jnp.concatenate(
        [jnp.sum(jnp.where(ei == idx, rank, 0.0), axis=0, keepdims=True) for idx in idxs], axis=0).astype(jnp.int32)
    total = carry_scr[...] + jnp.sum(picked, axis=1, keepdims=True)
    carry_scr[...] = total
    cnt_ref[...] = total.astype(jnp.int32)


def _router(u2, w_router, b_router):
    t = u2.shape[0]
    tr = min(512, t)
    tok = pl.BlockSpec((TOP_K, tr), lambda i: (0, i))
    e_t, w_t, r_t, cnt = pl.pallas_call(
        functools.partial(_router_kernel, tr=tr),
        grid=(t // tr,),
        in_specs=[pl.BlockSpec((tr, D_MODEL), lambda i: (i, 0)),
                  pl.BlockSpec((N_EXPERTS, D_MODEL), lambda i: (0, 0)),
                  pl.BlockSpec((N_EXPERTS, 1), lambda i: (0, 0))],
        out_specs=[tok, tok, tok, pl.BlockSpec((N_EXPERTS, LANE), lambda i: (0, 0))],
        out_shape=[jax.ShapeDtypeStruct((TOP_K, t), jnp.int32), jax.ShapeDtypeStruct((TOP_K, t), F32),
                   jax.ShapeDtypeStruct((TOP_K, t), jnp.int32), jax.ShapeDtypeStruct((N_EXPERTS, LANE), jnp.int32)],
        scratch_shapes=[pltpu.VMEM((N_EXPERTS, LANE), F32), pltpu.VMEM((tr, tr), BF16)],
        compiler_params=_params("arbitrary"),
        name="router",
    )(u2, w_router.T, b_router.reshape(N_EXPERTS, 1))
    return e_t, w_t, r_t, cnt[:, 0]


def _row_copy(src, s, dst, d, sem):
    return pltpu.make_async_copy(src.at[pl.ds(s, 1), :], dst.at[pl.ds(d, 1), :], sem)


def _dispatch_kernel(dest_ref, u_ref, xs_ref, sem, *, tm):
    def start(r, carry):
        for k in range(TOP_K):
            _row_copy(u_ref, r, xs_ref, dest_ref[k, r], sem).start()
        return carry

    def wait(r, carry):
        for k in range(TOP_K):
            _row_copy(u_ref, r, xs_ref, dest_ref[k, r], sem).wait()
        return carry

    lax.fori_loop(0, tm, start, 0)
    lax.fori_loop(0, tm, wait, 0)


def _dispatch(u2, dest_t, n_rows):
    t = u2.shape[0]
    tm = min(256, t)
    nt = t // tm
    dest_blocks = dest_t.reshape(TOP_K, nt, tm).transpose(1, 0, 2)
    return pl.pallas_call(
        functools.partial(_dispatch_kernel, tm=tm),
        grid=(nt,),
        in_specs=[pl.BlockSpec((None, TOP_K, tm), lambda i: (i, 0, 0), memory_space=pltpu.SMEM),
                  pl.BlockSpec((tm, D_MODEL), lambda i: (i, 0))],
        out_specs=pl.BlockSpec(memory_space=pl.ANY),
        out_shape=jax.ShapeDtypeStruct((n_rows, D_MODEL), F32),
        scratch_shapes=[pltpu.SemaphoreType.DMA(())],
        compiler_params=_params("arbitrary"),
        name="moe_dispatch",
    )(dest_blocks, u2)


def _expert_kernel(be_ref, bv_ref, nu_ref, x_ref, wg_ref, wu_ref, wd_ref, y_ref, wg_scr, wu_scr, wd_scr, *, br):
    i = pl.program_id(0)
    prev = be_ref[jnp.maximum(i - 1, 0)]

    @pl.when(jnp.logical_and(i < nu_ref[0], jnp.logical_or(i == 0, be_ref[i] != prev)))
    def _():
        wg_scr[...] = wg_ref[...].astype(BF16)
        wu_scr[...] = wu_ref[...].astype(BF16)
        wd_scr[...] = wd_ref[...].astype(BF16)

    @pl.when(i < nu_ref[0])
    def _():
        live = lax.broadcasted_iota(jnp.int32, (br, D_MODEL), 0) < bv_ref[i]
        x = jnp.where(live, x_ref[...], 0.0).astype(BF16)
        hcat = _silu(_dot(x, wg_scr[...])) * _dot(x, wu_scr[...])
        y_ref[...] = _dot(hcat.astype(BF16), wd_scr[...])


def _experts(x_sorted, blk_e, blk_valid, n_used, w_gate, w_up, w_down, br):
    n_rows = x_sorted.shape[0]
    nblk = n_rows // br

    def blk(i, be, bv, nu):
        return (jnp.minimum(i, nu[0] - 1), 0)

    grid_spec = pltpu.PrefetchScalarGridSpec(
        num_scalar_prefetch=3,
        grid=(nblk,),
        in_specs=[pl.BlockSpec((br, D_MODEL), blk),
                  pl.BlockSpec((None, D_MODEL, D_EXPERT), lambda i, be, bv, nu: (be[i], 0, 0)),
                  pl.BlockSpec((None, D_MODEL, D_EXPERT), lambda i, be, bv, nu: (be[i], 0, 0)),
                  pl.BlockSpec((None, D_EXPERT, D_MODEL), lambda i, be, bv, nu: (be[i], 0, 0))],
        out_specs=pl.BlockSpec((br, D_MODEL), blk),
        scratch_shapes=[pltpu.VMEM((D_MODEL, D_EXPERT), BF16), pltpu.VMEM((D_MODEL, D_EXPERT), BF16),
                        pltpu.VMEM((D_EXPERT, D_MODEL), BF16)])
    return pl.pallas_call(
        functools.partial(_expert_kernel, br=br),
        grid_spec=grid_spec,
        out_shape=jax.ShapeDtypeStruct((n_rows, D_MODEL), F32),
        compiler_params=_params("arbitrary"),
        name="moe_experts",
    )(blk_e, blk_valid, n_used, x_sorted, w_gate, w_up, w_down)


def _combine_kernel(dest_ref, ys_ref, w_ref, u_ref, x1_ref, g2_ref, sg_ref, su_ref, sd_ref, lg_ref, lb_ref,
                    o_ref, g_scr, sem, *, tm):
    def start(r, carry):
        for k in range(TOP_K):
            _row_copy(ys_ref, dest_ref[k, r], g_scr.at[k], r, sem).start()
        return carry

    def wait(r, carry):
        for k in range(TOP_K):
            _row_copy(ys_ref, dest_ref[k, r], g_scr.at[k], r, sem).wait()
        return carry

    lax.fori_loop(0, tm, start, 0)
    ub = u_ref[...].astype(BF16)
    hs = _silu(_dot(ub, sg_ref[...])) * _dot(ub, su_ref[...])
    f = _dot(hs.astype(BF16), sd_ref[...])
    lax.fori_loop(0, tm, wait, 0)
    w = w_ref[...]
    for k in range(TOP_K):
        f = f + g_scr[k] * w[:, k:k + 1]
    o_ref[...] = _layer_norm(ALPHA * x1_ref[...] + g2_ref[...] * f, lg_ref[...], lb_ref[...])


def _combine(y_sorted, dest_t, w_t, u2, x1, g2, ws_gate, ws_up, ws_down, ln_g, ln_b, rows_per_mod):
    t = u2.shape[0]
    tm = min(128, t)
    nt = t // tm
    dest_blocks = dest_t.reshape(TOP_K, nt, tm).transpose(1, 0, 2)
    rows = pl.BlockSpec((tm, D_MODEL), lambda i: (i, 0))
    vec = pl.BlockSpec((1, D_MODEL), lambda i: (0, 0))
    g2a, mspec = _mod_spec(g2, t, tm, rows_per_mod)
    return pl.pallas_call(
        functools.partial(_combine_kernel, tm=tm),
        grid=(nt,),
        in_specs=[pl.BlockSpec((None, TOP_K, tm), lambda i: (i, 0, 0), memory_space=pltpu.SMEM),
                  pl.BlockSpec(memory_space=pl.ANY),
                  pl.BlockSpec((tm, TOP_K), lambda i: (i, 0)),
                  rows, rows, mspec,
                  pl.BlockSpec((D_MODEL, D_SHARED), lambda i: (0, 0)),
                  pl.BlockSpec((D_MODEL, D_SHARED), lambda i: (0, 0)),
                  pl.BlockSpec((D_SHARED, D_MODEL), lambda i: (0, 0)),
                  vec, vec],
        out_specs=rows,
        out_shape=jax.ShapeDtypeStruct((t, D_MODEL), F32),
        scratch_shapes=[pltpu.VMEM((TOP_K, tm, D_MODEL), F32), pltpu.SemaphoreType.DMA(())],
        compiler_params=_params("arbitrary"),
        name="moe_combine",
    )(dest_blocks, y_sorted, w_t.T, u2, x1, g2a, ws_gate.astype(BF16), ws_up.astype(BF16), ws_down.astype(BF16),
      ln_g.reshape(1, D_MODEL), ln_b.reshape(1, D_MODEL))


def _moe(u2, x1, g2, p, rows_per_mod, br):
    t = u2.shape[0]
    e_t, w_t, r_t, counts = _router(u2, p["w_router"], p["b_router"])
    padded = (counts + br - 1) // br * br
    pends = jnp.cumsum(padded)
    pstart = pends - padded
    dest_t = pstart[e_t] + r_t
    n_rows = (t * TOP_K + N_EXPERTS * (br - 1) + br - 1) // br * br
    nblk = n_rows // br
    blk_start = jnp.arange(nblk, dtype=jnp.int32) * br
    blk_e = jnp.minimum(jnp.searchsorted(pends, blk_start, side="right"), N_EXPERTS - 1).astype(jnp.int32)
    blk_valid = jnp.clip(pstart[blk_e] + counts[blk_e] - blk_start, 0, br).astype(jnp.int32)
    n_used = (pends[-1:] // br).astype(jnp.int32)
    x_sorted = _dispatch(u2, dest_t, n_rows)
    y_sorted = _experts(x_sorted, blk_e, blk_valid, n_used, p["w_gate"], p["w_up"], p["w_down"], br)
    return _combine(y_sorted, dest_t, w_t, u2, x1, g2, p["ws_gate"], p["ws_up"], p["ws_down"],
                    p["ln2_g"], p["ln2_b"], rows_per_mod)


def _cat_w_in(w_in):
    sizes = (SSD_INNER, CONV_DIM, SSD_HEADS, 1024, 1024, 1024, D_MODEL, D_MODEL)
    offs = [0]
    for s in sizes:
        offs.append(offs[-1] + s)
    z, xbc, dt, q, k, v, ga, gb = [w_in[:, offs[i]:offs[i + 1]] for i in range(8)]
    dt = jnp.pad(dt, ((0, 0), (0, SEG_END - SEG_DT - SSD_HEADS)))
    return jnp.concatenate([z, xbc, q, k, v, ga, gb, dt], axis=1).astype(BF16)


def kernel(x_prompt, x_sample, c_prompt, c_sample, cache_k, cache_v, page_table, state_conv, state_ssm, rel_bias, w_ada, b_ada, w_in, conv_w, conv_b, dt_bias, a_log, d_skip, ssd_norm_w, lam_q1, lam_k1, lam_q2, lam_k2, subln_w, w_br_ssd, w_br_attn, w_out, ln1_g, ln1_b, w_router, b_router, w_gate, w_up, w_down, ws_gate, ws_up, ws_down, ln2_g, ln2_b):
    assert w_in.shape[0] == DEPTH
    nbp, seq, _ = x_prompt.shape
    nbs, dseq, _ = x_sample.shape
    tp, ts = nbp * seq, nbs * dseq
    past = page_table.shape[1] * PAGE_SIZE
    cl = SSD_CHUNK

    pad_p = -nbp % 8
    c_all = jnp.concatenate([c_prompt, jnp.zeros((pad_p, D_MODEL), F32), jnp.repeat(c_sample, dseq, axis=0)], axis=0)
    mod = _adaln(c_all, w_ada[0], b_ada[0])
    mod_p = mod[:nbp].reshape(nbp, 1, 6, D_MODEL)
    mod_s = mod[nbp + pad_p:].reshape(nbs, dseq, 6, D_MODEL)
    mp = [mod_p[:, :, i] for i in range(6)]
    ms = [mod_s[:, :, i] for i in range(6)]

    w_cat = _cat_w_in(w_in[0])
    wa, wb, wo = w_br_ssd[0].astype(BF16), w_br_attn[0].astype(BF16), w_out[0].astype(BF16)
    lamp = jnp.stack([lam_q1[0], lam_k1[0], lam_q2[0], lam_k2[0]])
    moe_p = dict(w_router=w_router[0], b_router=b_router[0], w_gate=w_gate[0], w_up=w_up[0], w_down=w_down[0],
                 ws_gate=ws_gate[0], ws_up=ws_up[0], ws_down=ws_down[0], ln2_g=ln2_g[0], ln2_b=ln2_b[0])
    ssd_w = (conv_w[0], conv_b[0], dt_bias[0], a_log[0], d_skip[0], ssd_norm_w[0])

    xp = x_prompt.reshape(tp, D_MODEL)
    z, xbc, dt, q, k, v, kb, vb, ga, gb = _inproj(xp, mp[1], mp[0], w_cat, seq)
    yn, conv_p, h_p = _ssd(xbc, dt, z, *ssd_w, jnp.zeros((nbp, CONV_W - 1, CONV_DIM), F32),
                           jnp.zeros((nbp, SSD_HEADS, SSD_HEADDIM, D_STATE), F32), nbp, seq // cl, cl)
    tq = min(256, seq)
    ar = jnp.arange(tq, dtype=jnp.int32)
    rel_p = (ar[:, None] - ar[None, :])[None] + (jnp.arange(3, dtype=jnp.int32) * tq)[:, None, None]
    bias_p = _bias_tiles(rel_p.reshape(3 * tq, tq), rel_bias).reshape(ATTN_HEADS, 3, tq, tq)
    on = _attn_prompt(q, kb, vb, bias_p, lamp, subln_w[0], nbp, seq)
    x1, u2 = _merge(yn, on, ga, gb, xp, mp[2], mp[4], mp[3], wa, wb, wo, ln1_g[0], ln1_b[0], seq)
    y_prompt = _moe(u2, x1, mp[5], moe_p, seq, 256).reshape(nbp, seq, D_MODEL)
    k_prompt = k.reshape(1, nbp, seq, ATTN_HEADS, 2 * ATTN_HEAD_DIM)
    v_prompt = v.reshape(1, nbp, seq, ATTN_HEADS, ATTN_V_DIM)

    xs_ = x_sample.reshape(ts, D_MODEL)
    z, xbc, dt, q, k, v, kb, vb, ga, gb = _inproj(xs_, ms[1], ms[0], w_cat, dseq)

    def chunk_pad(a):
        a = a.reshape(nbs, dseq, a.shape[-1])
        return jnp.pad(a, ((0, 0), (0, cl - dseq), (0, 0))).reshape(nbs * cl, a.shape[-1])

    yn, conv_s, h_s = _ssd(chunk_pad(xbc), chunk_pad(dt), chunk_pad(z), *ssd_w, state_conv[0], state_ssm[0],
                           nbs, 1, dseq)
    yn = yn.reshape(nbs, cl, SSD_INNER)[:, :dseq].reshape(ts, SSD_INNER)
    qpos = past + jnp.arange(dseq, dtype=jnp.int32)
    kpos = jnp.arange(past - PAGE_SIZE, past + PAGE_SIZE, dtype=jnp.int32)
    rel_s = qpos[:, None] - kpos[None, :]
    rel_s = jnp.where(kpos[None, :] < past + dseq, rel_s, -1)
    bias_s = _bias_tiles(jnp.pad(rel_s, ((0, -dseq % 8), (0, 0))), rel_bias)[:, :dseq]
    bias_s = bias_s.reshape(ATTN_HEADS, dseq, 2, PAGE_SIZE).transpose(0, 2, 1, 3)
    bias_s = jnp.concatenate([bias_s, bias_s], axis=2)
    on = _attn_decode(q.reshape(nbs, dseq, -1), k.reshape(nbs, dseq, -1), v.reshape(nbs, dseq, -1),
                      cache_k[0], cache_v[0], page_table, bias_s, rel_bias, lamp, subln_w[0])
    x1, u2 = _merge(yn, on.reshape(ts, -1), ga, gb, xs_, ms[2], ms[4], ms[3], wa, wb, wo, ln1_g[0], ln1_b[0], dseq)
    y_sample = _moe(u2, x1, ms[5], moe_p, dseq, 64).reshape(nbs, dseq, D_MODEL)
    k_sample = k.reshape(1, nbs, dseq, ATTN_HEADS, 2 * ATTN_HEAD_DIM)
    v_sample = v.reshape(1, nbs, dseq, ATTN_HEADS, ATTN_V_DIM)

    return (y_prompt, y_sample, k_prompt, v_prompt, conv_p[None], h_p.reshape(1, nbp, SSD_HEADS, SSD_HEADDIM, D_STATE),
            k_sample, v_sample, conv_s[None], h_s.reshape(1, nbs, SSD_HEADS, SSD_HEADDIM, D_STATE))
```

```python
import functools
import math

import jax
import jax.numpy as jnp
from jax import lax
from jax.experimental import pallas as pl
from jax.experimental.pallas import tpu as pltpu

F32 = jnp.float32
BF16 = jnp.bfloat16
HIGHEST = lax.Precision.HIGHEST

D_MODEL = 1024
SSD_INNER = 1024
SSD_HEADDIM = 64
SSD_HEADS = 16
SSD_GROUPS = 2
D_STATE = 128
CONV_W = 4
CONV_DIM = SSD_INNER + 2 * SSD_GROUPS * D_STATE
SSD_CHUNK = 128
ATTN_HEADS = 8
ATTN_HEAD_DIM = 64
ATTN_V_DIM = 128
N_BUCKETS = 32
MAX_DISTANCE = 128
N_EXPERTS = 64
N_EXPERT_GROUPS = 8
GROUP_SIZE = N_EXPERTS // N_EXPERT_GROUPS
TOPK_GROUPS = 4
TOP_K = 8
D_EXPERT = 256
D_SHARED = 256
ROUTED_SCALE = 2.5
PAGE_SIZE = 128
DEPTH = 1
ALPHA = (2 * DEPTH) ** 0.25
LN_EPS = 1e-5
RMS_EPS = 1e-5
LAM_INIT = 0.8 - 0.6 * math.exp(-0.3 * 0)
LANE = 128
VMEM_LIMIT = 56 * 1024 * 1024

SEG_Z, SEG_XBC, SEG_Q, SEG_K, SEG_V, SEG_GA, SEG_GB, SEG_DT, SEG_END = (
    0, 1024, 2560, 3584, 4608, 5632, 6656, 7680, 7808)


def _silu(x):
    return x * jax.nn.sigmoid(x)


def _dot(a, b):
    return jnp.dot(a, b, preferred_element_type=F32)


def _dot_nt(a, b):
    return lax.dot_general(a, b, (((1,), (1,)), ((), ())), preferred_element_type=F32)


def _dot_tn(a, b):
    return lax.dot_general(a, b, (((0,), (0,)), ((), ())), preferred_element_type=F32)


def _params(*sem):
    return pltpu.CompilerParams(dimension_semantics=sem, vmem_limit_bytes=VMEM_LIMIT)


def _adaln_kernel(c_ref, w_ref, b_ref, o_ref):
    s = _silu(c_ref[...]).astype(BF16)
    o_ref[...] = _dot(s, w_ref[...].astype(BF16)) + b_ref[...]


def _adaln(c, w_ada, b_ada):
    r = c.shape[0]
    n = w_ada.shape[1]
    tn = 1024
    return pl.pallas_call(
        _adaln_kernel,
        grid=(n // tn,),
        in_specs=[pl.BlockSpec((r, D_MODEL), lambda j: (0, 0)),
                  pl.BlockSpec((D_MODEL, tn), lambda j: (0, j)),
                  pl.BlockSpec((1, tn), lambda j: (0, j))],
        out_specs=pl.BlockSpec((r, tn), lambda j: (0, j)),
        out_shape=jax.ShapeDtypeStruct((r, n), F32),
        compiler_params=_params("arbitrary"),
        name="adaln",
    )(c, w_ada, b_ada.reshape(1, n))


def _inproj_kernel(x_ref, sc_ref, sh_ref, w_ref, z_ref, xbc_ref, dt_ref, q_ref, k_ref, v_ref,
                   kb_ref, vb_ref, ga_ref, gb_ref):
    u = (x_ref[...] * (1.0 + sc_ref[...]) + sh_ref[...]).astype(BF16)

    def seg(a, b):
        return _dot(u, w_ref[:, a:b])

    z_ref[...] = seg(SEG_Z, SEG_XBC).astype(BF16)
    xbc_ref[...] = seg(SEG_XBC, SEG_Q)
    q_ref[...] = (seg(SEG_Q, SEG_K) * (ATTN_HEAD_DIM ** -0.5)).astype(BF16)
    kk = seg(SEG_K, SEG_V)
    k_ref[...] = kk
    kb_ref[...] = kk.astype(BF16)
    vv = seg(SEG_V, SEG_GA)
    v_ref[...] = vv
    vb_ref[...] = vv.astype(BF16)
    ga_ref[...] = seg(SEG_GA, SEG_GB).astype(BF16)
    gb_ref[...] = seg(SEG_GB, SEG_DT).astype(BF16)
    dt_ref[...] = seg(SEG_DT, SEG_END)


def _inproj(x, sc, sh, w_cat, rows_per_mod):
    t = x.shape[0]
    tm = min(256, t)
    per_row = sc.shape[1] != 1
    if per_row:
        sc2 = sc.reshape(t, D_MODEL)
        sh2 = sh.reshape(t, D_MODEL)
        mod_spec = pl.BlockSpec((tm, D_MODEL), lambda i: (i, 0))
    else:
        assert rows_per_mod % tm == 0
        sc2, sh2 = sc, sh
        mod_spec = pl.BlockSpec((None, 1, D_MODEL), lambda i: (i // (rows_per_mod // tm), 0, 0))

    def rows(width):
        return pl.BlockSpec((tm, width), lambda i: (i, 0))

    widths = (1024, CONV_DIM, LANE, 1024, 1024, 1024, 1024, 1024, 1024, 1024)
    dtypes = (BF16, F32, F32, BF16, F32, F32, BF16, BF16, BF16, BF16)
    return pl.pallas_call(
        _inproj_kernel,
        grid=(t // tm,),
        in_specs=[rows(D_MODEL), mod_spec, mod_spec,
                  pl.BlockSpec((D_MODEL, SEG_END), lambda i: (0, 0), pipeline_mode=pl.Buffered(1))],
        out_specs=[rows(w) for w in widths],
        out_shape=[jax.ShapeDtypeStruct((t, w), d) for w, d in zip(widths, dtypes)],
        compiler_params=_params("arbitrary"),
        name="inproj",
    )(x, sc2, sh2, w_cat)


def _ssd_kernel(xbc_ref, dt_ref, z_ref, cw_ref, cb_ref, dtb_ref, alog_ref, dsk_ref, nw_ref,
                cbuf_ref, h0_ref, yn_ref, cnew_ref, hnew_ref, xp_scr, h_scr, y_scr, *, valid_len):
    c = pl.program_id(1)
    nc = pl.num_programs(1)
    cl = SSD_CHUNK
    head = 8

    @pl.when(c == 0)
    def _():
        xp_scr[head - (CONV_W - 1):head, :] = cbuf_ref[...]
        h_scr[...] = h0_ref[...]

    @pl.when(c > 0)
    def _():
        xp_scr[head - (CONV_W - 1):head, :] = xp_scr[head + cl - (CONV_W - 1):head + cl, :]

    xp_scr[head:head + cl, :] = xbc_ref[...]

    acc = xp_scr[head:head + cl, :] * cw_ref[CONV_W - 1:CONV_W, :]
    for j in range(CONV_W - 1):
        lo = head - (CONV_W - 1) + j
        acc = acc + xp_scr[lo:lo + cl, :] * cw_ref[j:j + 1, :]
    xc = _silu(acc + cb_ref[...])
    xs = xc[:, :SSD_INNER]
    bmat = [xc[:, SSD_INNER + g * D_STATE:SSD_INNER + (g + 1) * D_STATE].astype(BF16)
            for g in range(SSD_GROUPS)]
    coff = SSD_INNER + SSD_GROUPS * D_STATE
    cmat = [xc[:, coff + g * D_STATE:coff + (g + 1) * D_STATE].astype(BF16) for g in range(SSD_GROUPS)]

    li = lax.broadcasted_iota(jnp.int32, (cl, cl), 0)
    si = lax.broadcasted_iota(jnp.int32, (cl, cl), 1)
    causal = li >= si
    lane_lo = si < SSD_HEADDIM
    sub_lo = li < SSD_HEADDIM

    dpre = dt_ref[...] + dtb_ref[...]
    dtv = jnp.maximum(dpre, 0.0) + jnp.log(1.0 + jnp.exp(-jnp.abs(dpre)))
    if valid_len < cl:
        dtv = jnp.where(li < valid_len, dtv, 0.0)
    da = dtv * (-jnp.exp(alog_ref[...]))
    tril = causal.astype(F32)
    acum = jnp.dot(tril, da, preferred_element_type=F32, precision=HIGHEST)
    acum_t = acum.T
    dt_t = dtv.T
    last = acum[cl - 1:cl, :]

    cb = [_dot_nt(cmat[g], bmat[g]) for g in range(SSD_GROUPS)]

    def colb(a, h):
        return jnp.broadcast_to(a[:, h:h + 1], (cl, cl))

    pairs = SSD_HEADS // 2
    for p in range(pairs):
        h0, h1 = 2 * p, 2 * p + 1
        g = h0 // (SSD_HEADS // SSD_GROUPS)
        xpair = xs[:, p * LANE:(p + 1) * LANE]
        x_lo = jnp.where(lane_lo, xpair, 0.0).astype(BF16)
        x_hi = jnp.where(lane_lo, 0.0, xpair).astype(BF16)
        ydiag = None
        for hh, xm in ((h0, x_lo), (h1, x_hi)):
            seg = colb(acum, hh) - acum_t[hh:hh + 1, :]
            dec = jnp.exp(jnp.where(causal, seg, -jnp.inf))
            m = (cb[g] * dec * dt_t[hh:hh + 1, :]).astype(BF16)
            part = _dot(m, xm)
            ydiag = part if ydiag is None else ydiag + part
        col0, col1 = colb(acum, h0), colb(acum, h1)
        ecol = jnp.where(lane_lo, jnp.exp(col0), jnp.exp(col1))
        hp = h_scr[p]
        yoff = _dot_nt(cmat[g], hp.astype(BF16)) * ecol
        l0, l1 = last[:, h0:h0 + 1], last[:, h1:h1 + 1]
        wcol = jnp.where(lane_lo, jnp.exp(l0 - col0) * colb(dtv, h0), jnp.exp(l1 - col1) * colb(dtv, h1))
        dx = (xpair * wcol).astype(BF16)
        st = _dot_tn(dx, bmat[g])
        hdec = jnp.where(sub_lo, jnp.exp(l0), jnp.exp(l1))
        h_scr[p] = hp * hdec + st
        y_scr[:, p * LANE:(p + 1) * LANE] = ydiag + yoff + dsk_ref[:, p * LANE:(p + 1) * LANE] * xpair

    zf = z_ref[...].astype(F32)
    gated = y_scr[...] * _silu(zf)
    gw = SSD_INNER // SSD_GROUPS
    for g in range(SSD_GROUPS):
        sg = gated[:, g * gw:(g + 1) * gw]
        ms = jnp.mean(sg * sg, axis=-1, keepdims=True)
        yn_ref[:, g * gw:(g + 1) * gw] = (sg * lax.rsqrt(ms + RMS_EPS) * nw_ref[:, g * gw:(g + 1) * gw]).astype(BF16)

    @pl.when(c == nc - 1)
    def _():
        cnew_ref[...] = xp_scr[head + valid_len - (CONV_W - 1):head + valid_len, :]
        hnew_ref[...] = h_scr[...]


def _ssd(xbc, dt, z, conv_w, conv_b, dt_bias, a_log, d_skip, norm_w, cbuf, h0, nb, nc, valid_len):
    cl = SSD_CHUNK
    pairs = SSD_HEADS // 2
    pad = LANE - SSD_HEADS
    row = lambda b, c: (b * nc + c, 0)
    const = lambda b, c: (0, 0)
    return pl.pallas_call(
        functools.partial(_ssd_kernel, valid_len=valid_len),
        grid=(nb, nc),
        in_specs=[pl.BlockSpec((cl, CONV_DIM), row), pl.BlockSpec((cl, LANE), row),
                  pl.BlockSpec((cl, SSD_INNER), row),
                  pl.BlockSpec((CONV_W, CONV_DIM), const), pl.BlockSpec((1, CONV_DIM), const),
                  pl.BlockSpec((1, LANE), const), pl.BlockSpec((1, LANE), const),
                  pl.BlockSpec((1, SSD_INNER), const), pl.BlockSpec((1, SSD_INNER), const),
                  pl.BlockSpec((None, CONV_W - 1, CONV_DIM), lambda b, c: (b, 0, 0)),
                  pl.BlockSpec((None, pairs, LANE, D_STATE), lambda b, c: (b, 0, 0, 0))],
        out_specs=[pl.BlockSpec((cl, SSD_INNER), row),
                   pl.BlockSpec((None, CONV_W - 1, CONV_DIM), lambda b, c: (b, 0, 0)),
                   pl.BlockSpec((None, pairs, LANE, D_STATE), lambda b, c: (b, 0, 0, 0))],
        out_shape=[jax.ShapeDtypeStruct((nb * nc * cl, SSD_INNER), BF16),
                   jax.ShapeDtypeStruct((nb, CONV_W - 1, CONV_DIM), F32),
                   jax.ShapeDtypeStruct((nb, pairs, LANE, D_STATE), F32)],
        scratch_shapes=[pltpu.VMEM((8 + cl, CONV_DIM), F32), pltpu.VMEM((pairs, LANE, D_STATE), F32),
                        pltpu.VMEM((cl, SSD_INNER), F32)],
        compiler_params=_params("arbitrary", "arbitrary"),
        name="ssd",
    )(xbc, dt, z, conv_w, conv_b.reshape(1, CONV_DIM),
      jnp.pad(dt_bias, (0, pad)).reshape(1, LANE), jnp.pad(a_log, (0, pad)).reshape(1, LANE),
      jnp.repeat(d_skip, SSD_HEADDIM).reshape(1, SSD_INNER), norm_w.reshape(1, SSD_INNER),
      cbuf, h0.reshape(nb, pairs, LANE, D_STATE))


def _bias_kernel(rel_ref, tab_ref, o_ref):
    rel = rel_ref[...]
    n = jnp.maximum(rel, 0)
    max_exact = N_BUCKETS // 2
    large = max_exact + (jnp.log(jnp.maximum(n, 1).astype(F32) / max_exact)
                         / math.log(MAX_DISTANCE / max_exact) * (N_BUCKETS - max_exact)).astype(jnp.int32)
    bucket = jnp.where(n < max_exact, n, jnp.minimum(large, N_BUCKETS - 1))
    outs = [jnp.zeros(rel.shape, F32) for _ in range(ATTN_HEADS)]
    for kb in range(N_BUCKETS):
        hit = bucket == kb
        for h in range(ATTN_HEADS):
            outs[h] = jnp.where(hit, tab_ref[kb, h], outs[h])
    for h in range(ATTN_HEADS):
        o_ref[h] = jnp.where(rel >= 0, outs[h], -jnp.inf)


def _bias_tiles(rel, rel_bias):
    r, c = rel.shape
    tr = min(r, 64)
    return pl.pallas_call(
        _bias_kernel,
        grid=(r // tr,),
        in_specs=[pl.BlockSpec((tr, c), lambda i: (i, 0)),
                  pl.BlockSpec(memory_space=pltpu.SMEM)],
        out_specs=pl.BlockSpec((ATTN_HEADS, tr, c), lambda i: (0, i, 0)),
        out_shape=jax.ShapeDtypeStruct((ATTN_HEADS, r, c), F32),
        compiler_params=_params("arbitrary"),
        name="t5bias",
    )(rel, rel_bias)


def _lam(lamp_ref):
    lp = lamp_ref[...]
    e1 = jnp.exp(jnp.sum(lp[0:1, :] * lp[1:2, :], axis=-1, keepdims=True))
    e2 = jnp.exp(jnp.sum(lp[2:3, :] * lp[3:4, :], axis=-1, keepdims=True))
    return e1 - e2 + LAM_INIT


def _subln(o, sw):
    ms = jnp.mean(o * o, axis=-1, keepdims=True)
    return o * lax.rsqrt(ms + RMS_EPS) * sw * (1.0 - LAM_INIT)


ATTN_TQ = 512
ATTN_STRIP = 512
ATTN_TK = 512
ATTN_D_MIN = 1 - ATTN_STRIP // ATTN_TK
ATTN_D_FAR = -(-(MAX_DISTANCE + ATTN_TK - 1) // ATTN_TK)


def _attn_kernel(q_ref, k_ref, v_ref, bias_ref, lamp_ref, sw_ref, o_ref, m_scr, l_scr, a_scr, *, tq, strip, tk):
    qi = pl.program_id(2)
    lane_lo = lax.broadcasted_iota(jnp.int32, (strip, LANE), 1) < ATTN_HEAD_DIM
    reps = tk // LANE
    lam = _lam(lamp_ref)
    for st in range(tq // strip):
        q = q_ref[st * strip:(st + 1) * strip, :]
        zero = jnp.zeros_like(q)
        qm = (jnp.where(lane_lo, q, zero), jnp.where(lane_lo, zero, q))
        m_scr[...] = jnp.full(m_scr.shape, -jnp.inf, F32)
        l_scr[...] = jnp.zeros(l_scr.shape, F32)
        a_scr[...] = jnp.zeros(a_scr.shape, F32)
        row0 = qi * (tq // strip) + st
        ntiles = (row0 * strip + strip - 1) // tk + 1

        def body(j, carry):
            off = pl.multiple_of(j * tk, tk)
            kt = k_ref[pl.ds(off, tk), :]
            vt = v_ref[pl.ds(off, tk), :]
            bias = bias_ref[jnp.minimum(row0 * (strip // tk) - j, ATTN_D_FAR) - ATTN_D_MIN]
            for mi in range(2):
                sc = _dot_nt(qm[mi], kt) + bias
                m_prev = m_scr[mi]
                m_new = jnp.maximum(m_prev, jnp.max(sc, axis=-1, keepdims=True))
                p = jnp.exp(sc - jnp.concatenate([m_new] * reps, axis=1))
                alpha = jnp.exp(m_prev - m_new)
                l_scr[mi] = alpha * l_scr[mi] + jnp.sum(p, axis=-1, keepdims=True)
                a_scr[mi] = alpha * a_scr[mi] + _dot(p.astype(BF16), vt)
                m_scr[mi] = m_new
            return carry

        lax.fori_loop(0, ntiles, body, 0)
        o = a_scr[0] / l_scr[0] - lam * (a_scr[1] / l_scr[1])
        o_ref[st * strip:(st + 1) * strip, :] = _subln(o, sw_ref[...]).astype(BF16)


def _attn_prompt(q, kb, vb, rel_bias, lamp, subln_w, nb, seq):
    tq, strip, tk = ATTN_TQ, ATTN_STRIP, ATTN_TK
    assert seq % tq == 0 and tq % strip == 0 and strip % tk == 0
    nq = seq // tq
    nd = ATTN_D_FAR - ATTN_D_MIN + 1
    rel = (jnp.arange(ATTN_D_MIN, ATTN_D_FAR + 1, dtype=jnp.int32)[:, None, None] * tk
           + jnp.arange(strip, dtype=jnp.int32)[None, :, None] - jnp.arange(tk, dtype=jnp.int32)[None, None, :])
    bias = _bias_tiles(rel.reshape(-1, tk), rel_bias).reshape(ATTN_HEADS, nd, strip, tk)
    return pl.pallas_call(
        functools.partial(_attn_kernel, tq=tq, strip=strip, tk=tk),
        grid=(nb, ATTN_HEADS, nq),
        in_specs=[pl.BlockSpec((tq, LANE), lambda b, h, i: (b * nq + i, h)),
                  pl.BlockSpec((seq, LANE), lambda b, h, i: (b, h)),
                  pl.BlockSpec((seq, LANE), lambda b, h, i: (b, h)),
                  pl.BlockSpec((None, nd, strip, tk), lambda b, h, i: (h, 0, 0, 0)),
                  pl.BlockSpec((4, ATTN_HEAD_DIM), lambda b, h, i: (0, 0)),
                  pl.BlockSpec((1, ATTN_V_DIM), lambda b, h, i: (0, 0))],
        out_specs=pl.BlockSpec((tq, LANE), lambda b, h, i: (b * nq + i, h)),
        out_shape=jax.ShapeDtypeStruct((nb * seq, ATTN_HEADS * ATTN_V_DIM), BF16),
        scratch_shapes=[pltpu.VMEM((2, strip, LANE), F32)] * 3,
        compiler_params=_params("arbitrary", "arbitrary", "arbitrary"),
        name="attn_prompt",
    )(q, kb, vb, bias, lamp, subln_w.reshape(1, ATTN_V_DIM))


DECODE_PAGES = 4


def _decode_kernel(pt_ref, q_ref, kn_ref, vn_ref, *rest, nq, pps):
    ck_refs, cv_refs = rest[:pps], rest[pps:2 * pps]
    bpage_ref, bnew_ref, lamp_ref, sw_ref, o_ref, qa_scr, m_scr, l_scr, a_scr = rest[2 * pps:]
    j = pl.program_id(1)
    nsteps = pl.num_programs(1)
    cols = PAGE_SIZE * ATTN_HEADS

    @pl.when(j == 0)
    def _():
        lane_lo = lax.broadcasted_iota(jnp.int32, (nq, LANE), 1) < ATTN_HEAD_DIM
        for h in range(ATTN_HEADS):
            qh = q_ref[:, h * LANE:(h + 1) * LANE].astype(F32)
            qa_scr[(2 * h) * nq:(2 * h + 1) * nq, :] = jnp.where(lane_lo, qh, 0.0)
            qa_scr[(2 * h + 1) * nq:(2 * h + 2) * nq, :] = jnp.where(lane_lo, 0.0, qh)
        nnew = nq * ATTN_HEADS
        zpad = jnp.zeros((LANE - nnew, LANE), F32)
        kn = jnp.concatenate([kn_ref[...], zpad], axis=0).astype(BF16)
        vn = jnp.concatenate([vn_ref[...], zpad], axis=0).astype(BF16)
        s = _dot_nt(qa_scr[...].astype(BF16), kn) + bnew_ref[...]
        m = jnp.max(s, axis=-1, keepdims=True)
        p = jnp.exp(s - m)
        m_scr[...] = m
        l_scr[...] = jnp.sum(p, axis=-1, keepdims=True)
        a_scr[...] = _dot(p.astype(BF16), vn)

    qa = qa_scr[...].astype(BF16)
    scores = []
    for i in range(pps):
        bias = bpage_ref[jnp.where(j == nsteps - 1, 1, 0)] if i == pps - 1 else bpage_ref[0]
        scores.append(_dot_nt(qa, ck_refs[i][...].reshape(cols, LANE).astype(BF16)) + bias)
    m_prev = m_scr[...]
    m_new = m_prev
    for s in scores:
        m_new = jnp.maximum(m_new, jnp.max(s, axis=-1, keepdims=True))
    alpha = jnp.exp(m_prev - m_new)
    l_new = alpha * l_scr[...]
    a_new = alpha * a_scr[...]
    for i, s in enumerate(scores):
        p = jnp.exp(s - m_new)
        l_new = l_new + jnp.sum(p, axis=-1, keepdims=True)
        a_new = a_new + _dot(p.astype(BF16), cv_refs[i][...].reshape(cols, LANE).astype(BF16))
    l_scr[...] = l_new
    a_scr[...] = a_new
    m_scr[...] = m_new

    @pl.when(j == nsteps - 1)
    def _():
        lam = _lam(lamp_ref)
        on = a_scr[...] / l_scr[...]
        for h in range(ATTN_HEADS):
            o = on[(2 * h) * nq:(2 * h + 1) * nq, :] - lam * on[(2 * h + 1) * nq:(2 * h + 2) * nq, :]
            o_ref[:, h * LANE:(h + 1) * LANE] = _subln(o, sw_ref[...])


def _attn_decode(q, k_new, v_new, cache_k, cache_v, page_table, rel_bias, lamp, subln_w):
    nb, nq, _ = q.shape
    npages = page_table.shape[1]
    past = npages * PAGE_SIZE
    rows = ATTN_HEADS * 2 * nq
    cols = PAGE_SIZE * ATTN_HEADS
    nnew = nq * ATTN_HEADS
    assert nnew <= LANE and npages >= 2
    qpos = past + jnp.arange(nq, dtype=jnp.int32)
    kpos = jnp.arange(past - 2 * PAGE_SIZE, past + PAGE_SIZE, dtype=jnp.int32)
    rel = jnp.where(kpos[None, :] < past + nq, qpos[:, None] - kpos[None, :], -1)
    b3 = _bias_tiles(jnp.pad(rel, ((0, -nq % 8), (0, 0))), rel_bias)[:, :nq].reshape(ATTN_HEADS, nq, 3, PAGE_SIZE)
    own = jnp.arange(ATTN_HEADS)[:, None] == jnp.arange(ATTN_HEADS)[None, :]
    full = jnp.where(own[:, None, None, None, :], b3[..., None], -jnp.inf)
    full = jnp.broadcast_to(full[:, None], (ATTN_HEADS, 2, nq, 3, PAGE_SIZE, ATTN_HEADS))
    full = full.transpose(3, 0, 1, 2, 4, 5).reshape(3, rows, cols)
    bias_pages = full[:2]
    bias_new = jnp.pad(full[2][:, :nnew], ((0, 0), (0, LANE - nnew)), constant_values=-jnp.inf)
    new_q = pl.BlockSpec((None, nq, ATTN_HEADS * LANE), lambda b, j, pt: (b, 0, 0))
    new_kv = pl.BlockSpec((None, nnew, LANE), lambda b, j, pt: (b, 0, 0))
    pps = math.gcd(DECODE_PAGES, npages)

    def page(i):
        return pl.BlockSpec((None, PAGE_SIZE, ATTN_HEADS, LANE), lambda b, j, pt: (pt[b, j * pps + i], 0, 0, 0))

    pages = [page(i) for i in range(pps)]
    grid_spec = pltpu.PrefetchScalarGridSpec(
        num_scalar_prefetch=1,
        grid=(nb, npages // pps),
        in_specs=[new_q, new_kv, new_kv, *pages, *pages,
                  pl.BlockSpec((2, rows, cols), lambda b, j, pt: (0, 0, 0)),
                  pl.BlockSpec((rows, LANE), lambda b, j, pt: (0, 0)),
                  pl.BlockSpec((4, ATTN_HEAD_DIM), lambda b, j, pt: (0, 0)),
                  pl.BlockSpec((1, ATTN_V_DIM), lambda b, j, pt: (0, 0))],
        out_specs=new_q,
        scratch_shapes=[pltpu.VMEM((rows, LANE), F32), pltpu.VMEM((rows, 1), F32),
                        pltpu.VMEM((rows, 1), F32), pltpu.VMEM((rows, LANE), F32)])
    return pl.pallas_call(
        functools.partial(_decode_kernel, nq=nq, pps=pps),
        grid_spec=grid_spec,
        out_shape=jax.ShapeDtypeStruct((nb, nq, ATTN_HEADS * ATTN_V_DIM), F32),
        compiler_params=_params("arbitrary", "arbitrary"),
        name="attn_decode",
    )(page_table, q, k_new, v_new, *([cache_k] * pps), *([cache_v] * pps), bias_pages, bias_new, lamp,
      subln_w.reshape(1, ATTN_V_DIM))


def _layer_norm(r, g, b):
    mu = jnp.mean(r, axis=-1, keepdims=True)
    d = r - mu
    var = jnp.mean(d * d, axis=-1, keepdims=True)
    return d * lax.rsqrt(var + LN_EPS) * g + b


def _merge_kernel(yn_ref, on_ref, ga_ref, gb_ref, x_ref, g1_ref, sc2_ref, sh2_ref, wa_ref, wb_ref, wo_ref,
                  lg_ref, lb_ref, x1_ref, u2_ref):
    ba = _dot(yn_ref[...].astype(BF16), wa_ref[...])
    bb = _dot(on_ref[...].astype(BF16), wb_ref[...])
    merged = jax.nn.sigmoid(ga_ref[...].astype(F32)) * ba + jax.nn.sigmoid(gb_ref[...].astype(F32)) * bb
    t = _dot(merged.astype(BF16), wo_ref[...])
    x1 = _layer_norm(ALPHA * x_ref[...] + g1_ref[...] * t, lg_ref[...], lb_ref[...])
    x1_ref[...] = x1
    u2_ref[...] = x1 * (1.0 + sc2_ref[...]) + sh2_ref[...]


def _mod_spec(mod, t, tm, rows_per_mod):
    if mod.shape[1] != 1:
        return mod.reshape(t, D_MODEL), pl.BlockSpec((tm, D_MODEL), lambda i: (i, 0))
    assert rows_per_mod % tm == 0
    return mod, pl.BlockSpec((None, 1, D_MODEL), lambda i: (i // (rows_per_mod // tm), 0, 0))


def _merge(yn, on, ga, gb, x, g1, sc2, sh2, wa, wb, wo, ln_g, ln_b, rows_per_mod):
    t = x.shape[0]
    tm = min(256, t)
    rows = pl.BlockSpec((tm, D_MODEL), lambda i: (i, 0))
    wspec = pl.BlockSpec((D_MODEL, D_MODEL), lambda i: (0, 0))
    vec = pl.BlockSpec((1, D_MODEL), lambda i: (0, 0))
    g1a, mspec = _mod_spec(g1, t, tm, rows_per_mod)
    sc2a, _ = _mod_spec(sc2, t, tm, rows_per_mod)
    sh2a, _ = _mod_spec(sh2, t, tm, rows_per_mod)
    return pl.pallas_call(
        _merge_kernel,
        grid=(t // tm,),
        in_specs=[rows, rows, rows, rows, rows, mspec, mspec, mspec, wspec, wspec, wspec, vec, vec],
        out_specs=[rows, rows],
        out_shape=[jax.ShapeDtypeStruct((t, D_MODEL), F32)] * 2,
        compiler_params=_params("arbitrary"),
        name="merge",
    )(yn, on, ga, gb, x, g1a, sc2a, sh2a, wa, wb, wo, ln_g.reshape(1, D_MODEL), ln_b.reshape(1, D_MODEL))


def _router_kernel(u_ref, wr_ref, br_ref, e_ref, w_ref, r_ref, cnt_ref, carry_scr, tri_scr, *, tr):
    i = pl.program_id(0)

    @pl.when(i == 0)
    def _():
        carry_scr[...] = jnp.zeros_like(carry_scr)
        a = lax.broadcasted_iota(jnp.int32, (tr, tr), 0)
        b = lax.broadcasted_iota(jnp.int32, (tr, tr), 1)
        tri_scr[...] = (a < b).astype(BF16)

    logits = lax.dot_general(wr_ref[...], u_ref[...], (((1,), (1,)), ((), ())),
                             preferred_element_type=F32, precision=HIGHEST)
    scores = jax.nn.sigmoid(logits)
    biased = scores + br_ref[...]
    ninf = -jnp.inf

    b3 = biased.reshape(N_EXPERT_GROUPS, GROUP_SIZE, tr)
    j3 = lax.broadcasted_iota(jnp.int32, b3.shape, 1).astype(F32)
    top1 = jnp.max(b3, axis=1, keepdims=True)
    first = jnp.min(jnp.where(b3 == top1, j3, float(GROUP_SIZE)), axis=1, keepdims=True)
    top2 = jnp.max(jnp.where(j3 == first, ninf, b3), axis=1, keepdims=True)
    gscore = (top1 + top2).reshape(N_EXPERT_GROUPS, tr)
    gi = lax.broadcasted_iota(jnp.int32, gscore.shape, 0).astype(F32)
    gsel = jnp.zeros(gscore.shape, F32)
    for _ in range(TOPK_GROUPS):
        mx = jnp.max(gscore, axis=0, keepdims=True)
        pick = gi == jnp.min(jnp.where(gscore == mx, gi, float(N_EXPERT_GROUPS)), axis=0, keepdims=True)
        gsel = jnp.where(pick, 1.0, gsel)
        gscore = jnp.where(pick, ninf, gscore)
    emask = jnp.broadcast_to(gsel.reshape(N_EXPERT_GROUPS, 1, tr), b3.shape).reshape(N_EXPERTS, tr)
    masked = jnp.where(emask > 0.5, biased, ninf)

    ei = lax.broadcasted_iota(jnp.int32, masked.shape, 0).astype(F32)
    picked = jnp.zeros(masked.shape, F32)
    idxs, wsel = [], []
    for _ in range(TOP_K):
        mx = jnp.max(masked, axis=0, keepdims=True)
        idx = jnp.min(jnp.where(masked == mx, ei, float(N_EXPERTS)), axis=0, keepdims=True)
        pick = ei == idx
        idxs.append(idx)
        wsel.append(jnp.sum(jnp.where(pick, scores, 0.0), axis=0, keepdims=True))
        picked = jnp.where(pick, 1.0, picked)
        masked = jnp.where(pick, ninf, masked)
    wall = jnp.concatenate(wsel, axis=0)
    w_ref[...] = wall / jnp.sum(wall, axis=0, keepdims=True) * ROUTED_SCALE
    e_ref[...] = jnp.concatenate(idxs, axis=0).astype(jnp.int32)

    rank = carry_scr[:, 0:1] + _dot(picked.astype(BF16), tri_scr[...])
    r_ref[...] = jnp.concatenate(
        [jnp.sum(jnp.where(ei == idx, rank, 0.0), axis=0, keepdims=True) for idx in idxs], axis=0).astype(jnp.int32)
    total = carry_scr[...] + jnp.sum(picked, axis=1, keepdims=True)
    carry_scr[...] = total
    cnt_ref[...] = total.astype(jnp.int32)


def _router(u2, w_router, b_router):
    t = u2.shape[0]
    tr = min(512, t)
    tok = pl.BlockSpec((TOP_K, tr), lambda i: (0, i))
    e_t, w_t, r_t, cnt = pl.pallas_call(
        functools.partial(_router_kernel, tr=tr),
        grid=(t // tr,),
        in_specs=[pl.BlockSpec((tr, D_MODEL), lambda i: (i, 0)),
                  pl.BlockSpec((N_EXPERTS, D_MODEL), lambda i: (0, 0)),
                  pl.BlockSpec((N_EXPERTS, 1), lambda i: (0, 0))],
        out_specs=[tok, tok, tok, pl.BlockSpec((N_EXPERTS, LANE), lambda i: (0, 0))],
        out_shape=[jax.ShapeDtypeStruct((TOP_K, t), jnp.int32), jax.ShapeDtypeStruct((TOP_K, t), F32),
                   jax.ShapeDtypeStruct((TOP_K, t), jnp.int32), jax.ShapeDtypeStruct((N_EXPERTS, LANE), jnp.int32)],
        scratch_shapes=[pltpu.VMEM((N_EXPERTS, LANE), F32), pltpu.VMEM((tr, tr), BF16)],
        compiler_params=_params("arbitrary"),
        name="router",
    )(u2, w_router.T, b_router.reshape(N_EXPERTS, 1))
    return e_t, w_t, r_t, cnt


def _plan_kernel(e_ref, r_ref, cnt_ref, dest_ref, be_ref, bv_ref, nu_ref, *, br, nblk_pad):
    cnt = cnt_ref[...]
    shift = br.bit_length() - 1
    padded = lax.shift_left(lax.shift_right_logical(cnt + (br - 1), shift), shift)
    ea = lax.broadcasted_iota(jnp.int32, (N_EXPERTS, N_EXPERTS), 0)
    eb = lax.broadcasted_iota(jnp.int32, (N_EXPERTS, N_EXPERTS), 1)
    pends = jnp.dot((eb <= ea).astype(F32), padded.astype(F32), preferred_element_type=F32,
                    precision=HIGHEST).astype(jnp.int32)
    pstart = pends - padded

    e = e_ref[...]
    first_row = jnp.zeros(e.shape, jnp.int32)
    for x in range(N_EXPERTS):
        first_row = jnp.where(e == x, pstart[x:x + 1, 0:1], first_row)
    dest_ref[...] = first_row + r_ref[...]

    @pl.when(pl.program_id(0) == 0)
    def _():
        esub = lax.broadcasted_iota(jnp.int32, (N_EXPERTS, LANE), 0)
        real_end = (pstart + cnt).astype(F32)
        for c in range(nblk_pad // LANE):
            bstart = (lax.broadcasted_iota(jnp.int32, (1, LANE), 1) + c * LANE) * br
            be = jnp.minimum(jnp.sum((pends <= bstart).astype(F32), axis=0, keepdims=True),
                             float(N_EXPERTS - 1)).astype(jnp.int32)
            end = jnp.sum(jnp.where(esub == be, real_end, 0.0), axis=0, keepdims=True).astype(jnp.int32)
            be_ref[:, c * LANE:(c + 1) * LANE] = be
            bv_ref[:, c * LANE:(c + 1) * LANE] = jnp.clip(end - bstart, 0, br)
        nu_ref[...] = lax.shift_right_logical(pends[N_EXPERTS - 1:N_EXPERTS, :], shift)


def _plan(e_t, r_t, cnt, br, nblk):
    t = e_t.shape[1]
    tc = min(2048, t)
    nblk_pad = -(-nblk // LANE) * LANE
    tok = pl.BlockSpec((TOP_K, tc), lambda i: (0, i))
    blk = pl.BlockSpec((1, nblk_pad), lambda i: (0, 0))
    dest, be, bv, nu = pl.pallas_call(
        functools.partial(_plan_kernel, br=br, nblk_pad=nblk_pad),
        grid=(t // tc,),
        in_specs=[tok, tok, pl.BlockSpec((N_EXPERTS, LANE), lambda i: (0, 0))],
        out_specs=[tok, blk, blk, pl.BlockSpec((1, LANE), lambda i: (0, 0))],
        out_shape=[jax.ShapeDtypeStruct((TOP_K, t), jnp.int32), jax.ShapeDtypeStruct((1, nblk_pad), jnp.int32),
                   jax.ShapeDtypeStruct((1, nblk_pad), jnp.int32), jax.ShapeDtypeStruct((1, LANE), jnp.int32)],
        compiler_params=_params("arbitrary"),
        name="moe_plan",
    )(e_t, r_t, cnt)
    return dest, be[0, :nblk], bv[0, :nblk], nu[0, :1]


def _row_copy(src, s, dst, d, sem):
    return pltpu.make_async_copy(src.at[pl.ds(s, 1), :], dst.at[pl.ds(d, 1), :], sem)


def _dispatch_kernel(dest_ref, u_ref, xs_ref, sem, *, tm):
    def start(r, carry):
        for k in range(TOP_K):
            _row_copy(u_ref, r, xs_ref, dest_ref[k, r], sem).start()
        return carry

    def wait(r, carry):
        for k in range(TOP_K):
            _row_copy(u_ref, r, xs_ref, dest_ref[k, r], sem).wait()
        return carry

    lax.fori_loop(0, tm, start, 0)
    lax.fori_loop(0, tm, wait, 0)


def _dispatch(u2, dest_t, n_rows):
    t = u2.shape[0]
    tm = min(256, t)
    nt = t // tm
    dest_blocks = dest_t.reshape(TOP_K, nt, tm).transpose(1, 0, 2)
    return pl.pallas_call(
        functools.partial(_dispatch_kernel, tm=tm),
        grid=(nt,),
        in_specs=[pl.BlockSpec((None, TOP_K, tm), lambda i: (i, 0, 0), memory_space=pltpu.SMEM),
                  pl.BlockSpec((tm, D_MODEL), lambda i: (i, 0))],
        out_specs=pl.BlockSpec(memory_space=pl.ANY),
        out_shape=jax.ShapeDtypeStruct((n_rows, D_MODEL), F32),
        scratch_shapes=[pltpu.SemaphoreType.DMA(())],
        compiler_params=_params("arbitrary"),
        name="moe_dispatch",
    )(dest_blocks, u2)


def _expert_kernel(be_ref, bv_ref, nu_ref, x_ref, wg_ref, wu_ref, wd_ref, y_ref, wg_scr, wu_scr, wd_scr, *, br):
    i = pl.program_id(0)
    prev = be_ref[jnp.maximum(i - 1, 0)]

    @pl.when(jnp.logical_and(i < nu_ref[0], jnp.logical_or(i == 0, be_ref[i] != prev)))
    def _():
        wg_scr[...] = wg_ref[...].astype(BF16)
        wu_scr[...] = wu_ref[...].astype(BF16)
        wd_scr[...] = wd_ref[...].astype(BF16)

    @pl.when(i < nu_ref[0])
    def _():
        live = lax.broadcasted_iota(jnp.int32, (br, D_MODEL), 0) < bv_ref[i]
        x = jnp.where(live, x_ref[...], 0.0).astype(BF16)
        hcat = _silu(_dot(x, wg_scr[...])) * _dot(x, wu_scr[...])
        y_ref[...] = _dot(hcat.astype(BF16), wd_scr[...])


def _experts(x_sorted, blk_e, blk_valid, n_used, w_gate, w_up, w_down, br):
    n_rows = x_sorted.shape[0]
    nblk = n_rows // br

    def blk(i, be, bv, nu):
        return (jnp.minimum(i, nu[0] - 1), 0)

    grid_spec = pltpu.PrefetchScalarGridSpec(
        num_scalar_prefetch=3,
        grid=(nblk,),
        in_specs=[pl.BlockSpec((br, D_MODEL), blk),
                  pl.BlockSpec((None, D_MODEL, D_EXPERT), lambda i, be, bv, nu: (be[i], 0, 0)),
                  pl.BlockSpec((None, D_MODEL, D_EXPERT), lambda i, be, bv, nu: (be[i], 0, 0)),
                  pl.BlockSpec((None, D_EXPERT, D_MODEL), lambda i, be, bv, nu: (be[i], 0, 0))],
        out_specs=pl.BlockSpec((br, D_MODEL), blk),
        scratch_shapes=[pltpu.VMEM((D_MODEL, D_EXPERT), BF16), pltpu.VMEM((D_MODEL, D_EXPERT), BF16),
                        pltpu.VMEM((D_EXPERT, D_MODEL), BF16)])
    return pl.pallas_call(
        functools.partial(_expert_kernel, br=br),
        grid_spec=grid_spec,
        out_shape=jax.ShapeDtypeStruct((n_rows, D_MODEL), F32),
        compiler_params=_params("arbitrary"),
        name="moe_experts",
    )(blk_e, blk_valid, n_used, x_sorted, w_gate, w_up, w_down)


def _combine_kernel(dest_ref, ys_ref, w_ref, u_ref, x1_ref, g2_ref, sg_ref, su_ref, sd_ref, lg_ref, lb_ref,
                    o_ref, g_scr, sem, *, tm):
    def start(r, carry):
        for k in range(TOP_K):
            _row_copy(ys_ref, dest_ref[k, r], g_scr.at[k], r, sem).start()
        return carry

    def wait(r, carry):
        for k in range(TOP_K):
            _row_copy(ys_ref, dest_ref[k, r], g_scr.at[k], r, sem).wait()
        return carry

    lax.fori_loop(0, tm, start, 0)
    ub = u_ref[...].astype(BF16)
    hs = _silu(_dot(ub, sg_ref[...])) * _dot(ub, su_ref[...])
    f = _dot(hs.astype(BF16), sd_ref[...])
    lax.fori_loop(0, tm, wait, 0)
    w = w_ref[...]
    for k in range(TOP_K):
        f = f + g_scr[k] * w[:, k:k + 1]
    o_ref[...] = _layer_norm(ALPHA * x1_ref[...] + g2_ref[...] * f, lg_ref[...], lb_ref[...])


def _combine(y_sorted, dest_t, w_t, u2, x1, g2, ws_gate, ws_up, ws_down, ln_g, ln_b, rows_per_mod):
    t = u2.shape[0]
    tm = min(128, t)
    nt = t // tm
    dest_blocks = dest_t.reshape(TOP_K, nt, tm).transpose(1, 0, 2)
    rows = pl.BlockSpec((tm, D_MODEL), lambda i: (i, 0))
    vec = pl.BlockSpec((1, D_MODEL), lambda i: (0, 0))
    g2a, mspec = _mod_spec(g2, t, tm, rows_per_mod)
    return pl.pallas_call(
        functools.partial(_combine_kernel, tm=tm),
        grid=(nt,),
        in_specs=[pl.BlockSpec((None, TOP_K, tm), lambda i: (i, 0, 0), memory_space=pltpu.SMEM),
                  pl.BlockSpec(memory_space=pl.ANY),
                  pl.BlockSpec((tm, TOP_K), lambda i: (i, 0)),
                  rows, rows, mspec,
                  pl.BlockSpec((D_MODEL, D_SHARED), lambda i: (0, 0)),
                  pl.BlockSpec((D_MODEL, D_SHARED), lambda i: (0, 0)),
                  pl.BlockSpec((D_SHARED, D_MODEL), lambda i: (0, 0)),
                  vec, vec],
        out_specs=rows,
        out_shape=jax.ShapeDtypeStruct((t, D_MODEL), F32),
        scratch_shapes=[pltpu.VMEM((TOP_K, tm, D_MODEL), F32), pltpu.SemaphoreType.DMA(())],
        compiler_params=_params("arbitrary"),
        name="moe_combine",
    )(dest_blocks, y_sorted, w_t.T, u2, x1, g2a, ws_gate.astype(BF16), ws_up.astype(BF16), ws_down.astype(BF16),
      ln_g.reshape(1, D_MODEL), ln_b.reshape(1, D_MODEL))


def _moe(u2, x1, g2, p, rows_per_mod, br):
    t = u2.shape[0]
    e_t, w_t, r_t, counts = _router(u2, p["w_router"], p["b_router"])
    n_rows = (t * TOP_K + N_EXPERTS * (br - 1) + br - 1) // br * br
    dest_t, blk_e, blk_valid, n_used = _plan(e_t, r_t, counts, br, n_rows // br)
    x_sorted = _dispatch(u2, dest_t, n_rows)
    y_sorted = _experts(x_sorted, blk_e, blk_valid, n_used, p["w_gate"], p["w_up"], p["w_down"], br)
    return _combine(y_sorted, dest_t, w_t, u2, x1, g2, p["ws_gate"], p["ws_up"], p["ws_down"],
                    p["ln2_g"], p["ln2_b"], rows_per_mod)


def _cat_w_in(w_in):
    sizes = (SSD_INNER, CONV_DIM, SSD_HEADS, 1024, 1024, 1024, D_MODEL, D_MODEL)
    offs = [0]
    for s in sizes:
        offs.append(offs[-1] + s)
    z, xbc, dt, q, k, v, ga, gb = [w_in[:, offs[i]:offs[i + 1]] for i in range(8)]
    dt = jnp.pad(dt, ((0, 0), (0, SEG_END - SEG_DT - SSD_HEADS)))
    return jnp.concatenate([z, xbc, q, k, v, ga, gb, dt], axis=1).astype(BF16)


def kernel(x_prompt, x_sample, c_prompt, c_sample, cache_k, cache_v, page_table, state_conv, state_ssm, rel_bias, w_ada, b_ada, w_in, conv_w, conv_b, dt_bias, a_log, d_skip, ssd_norm_w, lam_q1, lam_k1, lam_q2, lam_k2, subln_w, w_br_ssd, w_br_attn, w_out, ln1_g, ln1_b, w_router, b_router, w_gate, w_up, w_down, ws_gate, ws_up, ws_down, ln2_g, ln2_b):
    assert w_in.shape[0] == DEPTH
    nbp, seq, _ = x_prompt.shape
    nbs, dseq, _ = x_sample.shape
    tp, ts = nbp * seq, nbs * dseq
    past = page_table.shape[1] * PAGE_SIZE
    cl = SSD_CHUNK

    pad_p = -nbp % 8
    c_all = jnp.concatenate([c_prompt, jnp.zeros((pad_p, D_MODEL), F32), jnp.repeat(c_sample, dseq, axis=0)], axis=0)
    mod = _adaln(c_all, w_ada[0], b_ada[0])
    mod_p = mod[:nbp].reshape(nbp, 1, 6, D_MODEL)
    mod_s = mod[nbp + pad_p:].reshape(nbs, dseq, 6, D_MODEL)
    mp = [mod_p[:, :, i] for i in range(6)]
    ms = [mod_s[:, :, i] for i in range(6)]

    w_cat = _cat_w_in(w_in[0])
    wa, wb, wo = w_br_ssd[0].astype(BF16), w_br_attn[0].astype(BF16), w_out[0].astype(BF16)
    lamp = jnp.stack([lam_q1[0], lam_k1[0], lam_q2[0], lam_k2[0]])
    moe_p = dict(w_router=w_router[0], b_router=b_router[0], w_gate=w_gate[0], w_up=w_up[0], w_down=w_down[0],
                 ws_gate=ws_gate[0], ws_up=ws_up[0], ws_down=ws_down[0], ln2_g=ln2_g[0], ln2_b=ln2_b[0])
    ssd_w = (conv_w[0], conv_b[0], dt_bias[0], a_log[0], d_skip[0], ssd_norm_w[0])

    xp = x_prompt.reshape(tp, D_MODEL)
    z, xbc, dt, q, k, v, kb, vb, ga, gb = _inproj(xp, mp[1], mp[0], w_cat, seq)
    yn, conv_p, h_p = _ssd(xbc, dt, z, *ssd_w, jnp.zeros((nbp, CONV_W - 1, CONV_DIM), F32),
                           jnp.zeros((nbp, SSD_HEADS, SSD_HEADDIM, D_STATE), F32), nbp, seq // cl, cl)
    on = _attn_prompt(q, kb, vb, rel_bias, lamp, subln_w[0], nbp, seq)
    x1, u2 = _merge(yn, on, ga, gb, xp, mp[2], mp[4], mp[3], wa, wb, wo, ln1_g[0], ln1_b[0], seq)
    y_prompt = _moe(u2, x1, mp[5], moe_p, seq, 256).reshape(nbp, seq, D_MODEL)
    k_prompt = k.reshape(1, nbp, seq, ATTN_HEADS, 2 * ATTN_HEAD_DIM)
    v_prompt = v.reshape(1, nbp, seq, ATTN_HEADS, ATTN_V_DIM)

    xs_ = x_sample.reshape(ts, D_MODEL)
    z, xbc, dt, q, k, v, kb, vb, ga, gb = _inproj(xs_, ms[1], ms[0], w_cat, dseq)

    def chunk_pad(a):
        a = a.reshape(nbs, dseq, a.shape[-1])
        return jnp.pad(a, ((0, 0), (0, cl - dseq), (0, 0))).reshape(nbs * cl, a.shape[-1])

    yn, conv_s, h_s = _ssd(chunk_pad(xbc), chunk_pad(dt), chunk_pad(z), *ssd_w, state_conv[0], state_ssm[0],
                           nbs, 1, dseq)
    yn = yn.reshape(nbs, cl, SSD_INNER)[:, :dseq].reshape(ts, SSD_INNER)
    on = _attn_decode(q.reshape(nbs, dseq, -1), k.reshape(nbs, dseq * ATTN_HEADS, LANE),
                      v.reshape(nbs, dseq * ATTN_HEADS, LANE), cache_k[0], cache_v[0], page_table,
                      rel_bias, lamp, subln_w[0])
    x1, u2 = _merge(yn, on.reshape(ts, -1), ga, gb, xs_, ms[2], ms[4], ms[3], wa, wb, wo, ln1_g[0], ln1_b[0], dseq)
    y_sample = _moe(u2, x1, ms[5], moe_p, dseq, 64).reshape(nbs, dseq, D_MODEL)
    k_sample = k.reshape(1, nbs, dseq, ATTN_HEADS, 2 * ATTN_HEAD_DIM)
    v_sample = v.reshape(1, nbs, dseq, ATTN_HEADS, ATTN_V_DIM)

    return (y_prompt, y_sample, k_prompt, v_prompt, conv_p[None], h_p.reshape(1, nbp, SSD_HEADS, SSD_HEADDIM, D_STATE),
            k_sample, v_sample, conv_s[None], h_s.reshape(1, nbs, SSD_HEADS, SSD_HEADDIM, D_STATE))
```

```python
import functools
import math

import jax
import jax.numpy as jnp
from jax import lax
from jax.experimental import pallas as pl
from jax.experimental.pallas import tpu as pltpu

F32 = jnp.float32
BF16 = jnp.bfloat16
HIGHEST = lax.Precision.HIGHEST

D_MODEL = 1024
SSD_INNER = 1024
SSD_HEADDIM = 64
SSD_HEADS = 16
SSD_GROUPS = 2
D_STATE = 128
CONV_W = 4
CONV_DIM = SSD_INNER + 2 * SSD_GROUPS * D_STATE
SSD_CHUNK = 128
ATTN_HEADS = 8
ATTN_HEAD_DIM = 64
ATTN_V_DIM = 128
N_BUCKETS = 32
MAX_DISTANCE = 128
N_EXPERTS = 64
N_EXPERT_GROUPS = 8
GROUP_SIZE = N_EXPERTS // N_EXPERT_GROUPS
TOPK_GROUPS = 4
TOP_K = 8
D_EXPERT = 256
D_SHARED = 256
ROUTED_SCALE = 2.5
PAGE_SIZE = 128
DEPTH = 1
ALPHA = (2 * DEPTH) ** 0.25
LN_EPS = 1e-5
RMS_EPS = 1e-5
LAM_INIT = 0.8 - 0.6 * math.exp(-0.3 * 0)
LANE = 128
VMEM_LIMIT = 56 * 1024 * 1024

SEG_Z, SEG_XBC, SEG_Q, SEG_K, SEG_V, SEG_GA, SEG_GB, SEG_DT, SEG_END = (
    0, 1024, 2560, 3584, 4608, 5632, 6656, 7680, 7808)


def _silu(x):
    return x * jax.nn.sigmoid(x)


def _dot(a, b):
    return jnp.dot(a, b, preferred_element_type=F32)


def _dot_nt(a, b):
    return lax.dot_general(a, b, (((1,), (1,)), ((), ())), preferred_element_type=F32)


def _dot_tn(a, b):
    return lax.dot_general(a, b, (((0,), (0,)), ((), ())), preferred_element_type=F32)


def _params(*sem):
    return pltpu.CompilerParams(dimension_semantics=sem, vmem_limit_bytes=VMEM_LIMIT)


def _adaln_kernel(c_ref, w_ref, b_ref, o_ref):
    s = _silu(c_ref[...]).astype(BF16)
    o_ref[...] = _dot(s, w_ref[...].astype(BF16)) + b_ref[...]


def _adaln(c, w_ada, b_ada):
    r = c.shape[0]
    n = w_ada.shape[1]
    tn = 1024
    return pl.pallas_call(
        _adaln_kernel,
        grid=(n // tn,),
        in_specs=[pl.BlockSpec((r, D_MODEL), lambda j: (0, 0)),
                  pl.BlockSpec((D_MODEL, tn), lambda j: (0, j)),
                  pl.BlockSpec((1, tn), lambda j: (0, j))],
        out_specs=pl.BlockSpec((r, tn), lambda j: (0, j)),
        out_shape=jax.ShapeDtypeStruct((r, n), F32),
        compiler_params=_params("arbitrary"),
        name="adaln",
    )(c, w_ada, b_ada.reshape(1, n))


def _inproj_kernel(x_ref, sc_ref, sh_ref, w_ref, z_ref, xbc_ref, dt_ref, q_ref, k_ref, v_ref,
                   kb_ref, vb_ref, ga_ref, gb_ref):
    u = (x_ref[...] * (1.0 + sc_ref[...]) + sh_ref[...]).astype(BF16)

    def seg(a, b):
        return _dot(u, w_ref[:, a:b])

    z_ref[...] = seg(SEG_Z, SEG_XBC).astype(BF16)
    xbc_ref[...] = seg(SEG_XBC, SEG_Q)
    q_ref[...] = (seg(SEG_Q, SEG_K) * (ATTN_HEAD_DIM ** -0.5)).astype(BF16)
    kk = seg(SEG_K, SEG_V)
    k_ref[...] = kk
    kb_ref[...] = kk.astype(BF16)
    vv = seg(SEG_V, SEG_GA)
    v_ref[...] = vv
    vb_ref[...] = vv.astype(BF16)
    ga_ref[...] = seg(SEG_GA, SEG_GB).astype(BF16)
    gb_ref[...] = seg(SEG_GB, SEG_DT).astype(BF16)
    dt_ref[...] = seg(SEG_DT, SEG_END)


def _inproj(x, sc, sh, w_cat, rows_per_mod):
    t = x.shape[0]
    tm = min(256, t)
    per_row = sc.shape[1] != 1
    if per_row:
        sc2 = sc.reshape(t, D_MODEL)
        sh2 = sh.reshape(t, D_MODEL)
        mod_spec = pl.BlockSpec((tm, D_MODEL), lambda i: (i, 0))
    else:
        assert rows_per_mod % tm == 0
        sc2, sh2 = sc, sh
        mod_spec = pl.BlockSpec((None, 1, D_MODEL), lambda i: (i // (rows_per_mod // tm), 0, 0))

    def rows(width):
        return pl.BlockSpec((tm, width), lambda i: (i, 0))

    widths = (1024, CONV_DIM, LANE, 1024, 1024, 1024, 1024, 1024, 1024, 1024)
    dtypes = (BF16, F32, F32, BF16, F32, F32, BF16, BF16, BF16, BF16)
    return pl.pallas_call(
        _inproj_kernel,
        grid=(t // tm,),
        in_specs=[rows(D_MODEL), mod_spec, mod_spec,
                  pl.BlockSpec((D_MODEL, SEG_END), lambda i: (0, 0), pipeline_mode=pl.Buffered(1))],
        out_specs=[rows(w) for w in widths],
        out_shape=[jax.ShapeDtypeStruct((t, w), d) for w, d in zip(widths, dtypes)],
        compiler_params=_params("arbitrary"),
        name="inproj",
    )(x, sc2, sh2, w_cat)


def _ssd_kernel(xbc_ref, dt_ref, z_ref, cw_ref, cb_ref, dtb_ref, alog_ref, dsk_ref, nw_ref,
                cbuf_ref, h0_ref, yn_ref, cnew_ref, hnew_ref, xp_scr, h_scr, y_scr, *, valid_len):
    c = pl.program_id(1)
    nc = pl.num_programs(1)
    cl = SSD_CHUNK
    head = 8

    @pl.when(c == 0)
    def _():
        xp_scr[head - (CONV_W - 1):head, :] = cbuf_ref[...]
        h_scr[...] = h0_ref[...]

    @pl.when(c > 0)
    def _():
        xp_scr[head - (CONV_W - 1):head, :] = xp_scr[head + cl - (CONV_W - 1):head + cl, :]

    xp_scr[head:head + cl, :] = xbc_ref[...]

    acc = xp_scr[head:head + cl, :] * cw_ref[CONV_W - 1:CONV_W, :]
    for j in range(CONV_W - 1):
        lo = head - (CONV_W - 1) + j
        acc = acc + xp_scr[lo:lo + cl, :] * cw_ref[j:j + 1, :]
    xc = _silu(acc + cb_ref[...])
    xs = xc[:, :SSD_INNER]
    bmat = [xc[:, SSD_INNER + g * D_STATE:SSD_INNER + (g + 1) * D_STATE].astype(BF16)
            for g in range(SSD_GROUPS)]
    coff = SSD_INNER + SSD_GROUPS * D_STATE
    cmat = [xc[:, coff + g * D_STATE:coff + (g + 1) * D_STATE].astype(BF16) for g in range(SSD_GROUPS)]

    li = lax.broadcasted_iota(jnp.int32, (cl, cl), 0)
    si = lax.broadcasted_iota(jnp.int32, (cl, cl), 1)
    causal = li >= si
    lane_lo = si < SSD_HEADDIM
    sub_lo = li < SSD_HEADDIM

    dpre = dt_ref[...] + dtb_ref[...]
    dtv = jnp.maximum(dpre, 0.0) + jnp.log(1.0 + jnp.exp(-jnp.abs(dpre)))
    if valid_len < cl:
        dtv = jnp.where(li < valid_len, dtv, 0.0)
    da = dtv * (-jnp.exp(alog_ref[...]))
    tril = causal.astype(F32)
    acum = jnp.dot(tril, da, preferred_element_type=F32, precision=HIGHEST)
    acum_t = acum.T
    dt_t = dtv.T
    last = acum[cl - 1:cl, :]

    cb = [_dot_nt(cmat[g], bmat[g]) for g in range(SSD_GROUPS)]

    def colb(a, h):
        return jnp.broadcast_to(a[:, h:h + 1], (cl, cl))

    pairs = SSD_HEADS // 2
    for p in range(pairs):
        h0, h1 = 2 * p, 2 * p + 1
        g = h0 // (SSD_HEADS // SSD_GROUPS)
        xpair = xs[:, p * LANE:(p + 1) * LANE]
        x_lo = jnp.where(lane_lo, xpair, 0.0).astype(BF16)
        x_hi = jnp.where(lane_lo, 0.0, xpair).astype(BF16)
        ydiag = None
        for hh, xm in ((h0, x_lo), (h1, x_hi)):
            seg = colb(acum, hh) - acum_t[hh:hh + 1, :]
            dec = jnp.exp(jnp.where(causal, seg, -jnp.inf))
            m = (cb[g] * dec * dt_t[hh:hh + 1, :]).astype(BF16)
            part = _dot(m, xm)
            ydiag = part if ydiag is None else ydiag + part
        col0, col1 = colb(acum, h0), colb(acum, h1)
        ecol = jnp.where(lane_lo, jnp.exp(col0), jnp.exp(col1))
        hp = h_scr[p]
        yoff = _dot_nt(cmat[g], hp.astype(BF16)) * ecol
        l0, l1 = last[:, h0:h0 + 1], last[:, h1:h1 + 1]
        wcol = jnp.where(lane_lo, jnp.exp(l0 - col0) * colb(dtv, h0), jnp.exp(l1 - col1) * colb(dtv, h1))
        dx = (xpair * wcol).astype(BF16)
        st = _dot_tn(dx, bmat[g])
        hdec = jnp.where(sub_lo, jnp.exp(l0), jnp.exp(l1))
        h_scr[p] = hp * hdec + st
        y_scr[:, p * LANE:(p + 1) * LANE] = ydiag + yoff + dsk_ref[:, p * LANE:(p + 1) * LANE] * xpair

    zf = z_ref[...].astype(F32)
    gated = y_scr[...] * _silu(zf)
    gw = SSD_INNER // SSD_GROUPS
    for g in range(SSD_GROUPS):
        sg = gated[:, g * gw:(g + 1) * gw]
        ms = jnp.mean(sg * sg, axis=-1, keepdims=True)
        yn_ref[:, g * gw:(g + 1) * gw] = (sg * lax.rsqrt(ms + RMS_EPS) * nw_ref[:, g * gw:(g + 1) * gw]).astype(BF16)

    @pl.when(c == nc - 1)
    def _():
        cnew_ref[...] = xp_scr[head + valid_len - (CONV_W - 1):head + valid_len, :]
        hnew_ref[...] = h_scr[...]


def _ssd(xbc, dt, z, conv_w, conv_b, dt_bias, a_log, d_skip, norm_w, cbuf, h0, nb, nc, valid_len):
    cl = SSD_CHUNK
    pairs = SSD_HEADS // 2
    pad = LANE - SSD_HEADS
    row = lambda b, c: (b * nc + c, 0)
    const = lambda b, c: (0, 0)
    return pl.pallas_call(
        functools.partial(_ssd_kernel, valid_len=valid_len),
        grid=(nb, nc),
        in_specs=[pl.BlockSpec((cl, CONV_DIM), row), pl.BlockSpec((cl, LANE), row),
                  pl.BlockSpec((cl, SSD_INNER), row),
                  pl.BlockSpec((CONV_W, CONV_DIM), const), pl.BlockSpec((1, CONV_DIM), const),
                  pl.BlockSpec((1, LANE), const), pl.BlockSpec((1, LANE), const),
                  pl.BlockSpec((1, SSD_INNER), const), pl.BlockSpec((1, SSD_INNER), const),
                  pl.BlockSpec((None, CONV_W - 1, CONV_DIM), lambda b, c: (b, 0, 0)),
                  pl.BlockSpec((None, pairs, LANE, D_STATE), lambda b, c: (b, 0, 0, 0))],
        out_specs=[pl.BlockSpec((cl, SSD_INNER), row),
                   pl.BlockSpec((None, CONV_W - 1, CONV_DIM), lambda b, c: (b, 0, 0)),
                   pl.BlockSpec((None, pairs, LANE, D_STATE), lambda b, c: (b, 0, 0, 0))],
        out_shape=[jax.ShapeDtypeStruct((nb * nc * cl, SSD_INNER), BF16),
                   jax.ShapeDtypeStruct((nb, CONV_W - 1, CONV_DIM), F32),
                   jax.ShapeDtypeStruct((nb, pairs, LANE, D_STATE), F32)],
        scratch_shapes=[pltpu.VMEM((8 + cl, CONV_DIM), F32), pltpu.VMEM((pairs, LANE, D_STATE), F32),
                        pltpu.VMEM((cl, SSD_INNER), F32)],
        compiler_params=_params("arbitrary", "arbitrary"),
        name="ssd",
    )(xbc, dt, z, conv_w, conv_b.reshape(1, CONV_DIM),
      jnp.pad(dt_bias, (0, pad)).reshape(1, LANE), jnp.pad(a_log, (0, pad)).reshape(1, LANE),
      jnp.repeat(d_skip, SSD_HEADDIM).reshape(1, SSD_INNER), norm_w.reshape(1, SSD_INNER),
      cbuf, h0.reshape(nb, pairs, LANE, D_STATE))


def _bias_kernel(rel_ref, tab_ref, o_ref):
    rel = rel_ref[...]
    n = jnp.maximum(rel, 0)
    max_exact = N_BUCKETS // 2
    large = max_exact + (jnp.log(jnp.maximum(n, 1).astype(F32) / max_exact)
                         / math.log(MAX_DISTANCE / max_exact) * (N_BUCKETS - max_exact)).astype(jnp.int32)
    bucket = jnp.where(n < max_exact, n, jnp.minimum(large, N_BUCKETS - 1))
    outs = [jnp.zeros(rel.shape, F32) for _ in range(ATTN_HEADS)]
    for kb in range(N_BUCKETS):
        hit = bucket == kb
        for h in range(ATTN_HEADS):
            outs[h] = jnp.where(hit, tab_ref[kb, h], outs[h])
    for h in range(ATTN_HEADS):
        o_ref[h] = jnp.where(rel >= 0, outs[h], -jnp.inf)


def _bias_tiles(rel, rel_bias):
    r, c = rel.shape
    tr = min(r, 64)
    return pl.pallas_call(
        _bias_kernel,
        grid=(r // tr,),
        in_specs=[pl.BlockSpec((tr, c), lambda i: (i, 0)),
                  pl.BlockSpec(memory_space=pltpu.SMEM)],
        out_specs=pl.BlockSpec((ATTN_HEADS, tr, c), lambda i: (0, i, 0)),
        out_shape=jax.ShapeDtypeStruct((ATTN_HEADS, r, c), F32),
        compiler_params=_params("arbitrary"),
        name="t5bias",
    )(rel, rel_bias)


def _lam(lamp_ref):
    lp = lamp_ref[...]
    e1 = jnp.exp(jnp.sum(lp[0:1, :] * lp[1:2, :], axis=-1, keepdims=True))
    e2 = jnp.exp(jnp.sum(lp[2:3, :] * lp[3:4, :], axis=-1, keepdims=True))
    return e1 - e2 + LAM_INIT


def _subln(o, sw):
    ms = jnp.mean(o * o, axis=-1, keepdims=True)
    return o * lax.rsqrt(ms + RMS_EPS) * sw * (1.0 - LAM_INIT)


ATTN_TQ = 512
ATTN_STRIP = 512
ATTN_TK = 512
ATTN_D_MIN = 1 - ATTN_STRIP // ATTN_TK
ATTN_D_FAR = -(-(MAX_DISTANCE + ATTN_TK - 1) // ATTN_TK)


def _attn_kernel(q_ref, k_ref, v_ref, bias_ref, lamp_ref, sw_ref, o_ref, m_scr, l_scr, a_scr, *, tq, strip, tk):
    qi = pl.program_id(2)
    lane_lo = lax.broadcasted_iota(jnp.int32, (strip, LANE), 1) < ATTN_HEAD_DIM
    reps = tk // LANE
    lam = _lam(lamp_ref)
    for st in range(tq // strip):
        q = q_ref[st * strip:(st + 1) * strip, :]
        zero = jnp.zeros_like(q)
        qm = (jnp.where(lane_lo, q, zero), jnp.where(lane_lo, zero, q))
        m_scr[...] = jnp.full(m_scr.shape, -jnp.inf, F32)
        l_scr[...] = jnp.zeros(l_scr.shape, F32)
        a_scr[...] = jnp.zeros(a_scr.shape, F32)
        row0 = qi * (tq // strip) + st
        ntiles = (row0 * strip + strip - 1) // tk + 1

        def body(j, carry):
            off = pl.multiple_of(j * tk, tk)
            kt = k_ref[pl.ds(off, tk), :]
            vt = v_ref[pl.ds(off, tk), :]
            bias = bias_ref[jnp.minimum(row0 * (strip // tk) - j, ATTN_D_FAR) - ATTN_D_MIN]
            for mi in range(2):
                sc = _dot_nt(qm[mi], kt) + bias
                m_prev = m_scr[mi]
                m_new = jnp.maximum(m_prev, jnp.max(sc, axis=-1, keepdims=True))
                p = jnp.exp(sc - jnp.concatenate([m_new] * reps, axis=1))
                alpha = jnp.exp(m_prev - m_new)
                l_scr[mi] = alpha * l_scr[mi] + jnp.sum(p, axis=-1, keepdims=True)
                a_scr[mi] = alpha * a_scr[mi] + _dot(p.astype(BF16), vt)
                m_scr[mi] = m_new
            return carry

        lax.fori_loop(0, ntiles, body, 0)
        o = a_scr[0] / l_scr[0] - lam * (a_scr[1] / l_scr[1])
        o_ref[st * strip:(st + 1) * strip, :] = _subln(o, sw_ref[...]).astype(BF16)


def _attn_prompt(q, kb, vb, rel_bias, lamp, subln_w, nb, seq):
    tq, strip, tk = ATTN_TQ, ATTN_STRIP, ATTN_TK
    assert seq % tq == 0 and tq % strip == 0 and strip % tk == 0
    nq = seq // tq
    nd = ATTN_D_FAR - ATTN_D_MIN + 1
    rel = (jnp.arange(ATTN_D_MIN, ATTN_D_FAR + 1, dtype=jnp.int32)[:, None, None] * tk
           + jnp.arange(strip, dtype=jnp.int32)[None, :, None] - jnp.arange(tk, dtype=jnp.int32)[None, None, :])
    bias = _bias_tiles(rel.reshape(-1, tk), rel_bias).reshape(ATTN_HEADS, nd, strip, tk)
    return pl.pallas_call(
        functools.partial(_attn_kernel, tq=tq, strip=strip, tk=tk),
        grid=(nb, ATTN_HEADS, nq),
        in_specs=[pl.BlockSpec((tq, LANE), lambda b, h, i: (b * nq + i, h)),
                  pl.BlockSpec((seq, LANE), lambda b, h, i: (b, h)),
                  pl.BlockSpec((seq, LANE), lambda b, h, i: (b, h)),
                  pl.BlockSpec((None, nd, strip, tk), lambda b, h, i: (h, 0, 0, 0)),
                  pl.BlockSpec((4, ATTN_HEAD_DIM), lambda b, h, i: (0, 0)),
                  pl.BlockSpec((1, ATTN_V_DIM), lambda b, h, i: (0, 0))],
        out_specs=pl.BlockSpec((tq, LANE), lambda b, h, i: (b * nq + i, h)),
        out_shape=jax.ShapeDtypeStruct((nb * seq, ATTN_HEADS * ATTN_V_DIM), BF16),
        scratch_shapes=[pltpu.VMEM((2, strip, LANE), F32)] * 3,
        compiler_params=_params("arbitrary", "arbitrary", "arbitrary"),
        name="attn_prompt",
    )(q, kb, vb, bias, lamp, subln_w.reshape(1, ATTN_V_DIM))


DECODE_PAGES = 8


def _decode_kernel(pt_ref, q_ref, kn_ref, vn_ref, *rest, nq, pps):
    ck_refs, cv_refs = rest[:pps], rest[pps:2 * pps]
    bpage_ref, bnew_ref, lamp_ref, sw_ref, o_ref, qa_scr, m_scr, l_scr, a_scr = rest[2 * pps:]
    j = pl.program_id(1)
    nsteps = pl.num_programs(1)
    cols = PAGE_SIZE * ATTN_HEADS

    @pl.when(j == 0)
    def _():
        lane_lo = lax.broadcasted_iota(jnp.int32, (nq, LANE), 1) < ATTN_HEAD_DIM
        for h in range(ATTN_HEADS):
            qh = q_ref[:, h * LANE:(h + 1) * LANE].astype(F32)
            qa_scr[(2 * h) * nq:(2 * h + 1) * nq, :] = jnp.where(lane_lo, qh, 0.0)
            qa_scr[(2 * h + 1) * nq:(2 * h + 2) * nq, :] = jnp.where(lane_lo, 0.0, qh)
        nnew = nq * ATTN_HEADS
        zpad = jnp.zeros((LANE - nnew, LANE), F32)
        kn = jnp.concatenate([kn_ref[...], zpad], axis=0).astype(BF16)
        vn = jnp.concatenate([vn_ref[...], zpad], axis=0).astype(BF16)
        s = _dot_nt(qa_scr[...].astype(BF16), kn) + bnew_ref[...]
        m = jnp.max(s, axis=-1, keepdims=True)
        p = jnp.exp(s - m)
        m_scr[...] = m
        l_scr[...] = jnp.sum(p, axis=-1, keepdims=True)
        a_scr[...] = _dot(p.astype(BF16), vn)

    qa = qa_scr[...].astype(BF16)
    scores = []
    for i in range(pps):
        bias = bpage_ref[jnp.where(j == nsteps - 1, 1, 0)] if i == pps - 1 else bpage_ref[0]
        scores.append(_dot_nt(qa, ck_refs[i][...].reshape(cols, LANE).astype(BF16)) + bias)
    m_prev = m_scr[...]
    m_new = m_prev
    for s in scores:
        m_new = jnp.maximum(m_new, jnp.max(s, axis=-1, keepdims=True))
    alpha = jnp.exp(m_prev - m_new)
    l_new = alpha * l_scr[...]
    a_new = alpha * a_scr[...]
    for i, s in enumerate(scores):
        p = jnp.exp(s - m_new)
        l_new = l_new + jnp.sum(p, axis=-1, keepdims=True)
        a_new = a_new + _dot(p.astype(BF16), cv_refs[i][...].reshape(cols, LANE).astype(BF16))
    l_scr[...] = l_new
    a_scr[...] = a_new
    m_scr[...] = m_new

    @pl.when(j == nsteps - 1)
    def _():
        lam = _lam(lamp_ref)
        on = a_scr[...] / l_scr[...]
        for h in range(ATTN_HEADS):
            o = on[(2 * h) * nq:(2 * h + 1) * nq, :] - lam * on[(2 * h + 1) * nq:(2 * h + 2) * nq, :]
            o_ref[:, h * LANE:(h + 1) * LANE] = _subln(o, sw_ref[...])


def _attn_decode(q, k_new, v_new, cache_k, cache_v, page_table, rel_bias, lamp, subln_w):
    nb, nq, _ = q.shape
    npages = page_table.shape[1]
    past = npages * PAGE_SIZE
    rows = ATTN_HEADS * 2 * nq
    cols = PAGE_SIZE * ATTN_HEADS
    nnew = nq * ATTN_HEADS
    assert nnew <= LANE and npages >= 2
    qpos = past + jnp.arange(nq, dtype=jnp.int32)
    kpos = jnp.arange(past - 2 * PAGE_SIZE, past + PAGE_SIZE, dtype=jnp.int32)
    rel = jnp.where(kpos[None, :] < past + nq, qpos[:, None] - kpos[None, :], -1)
    b3 = _bias_tiles(jnp.pad(rel, ((0, -nq % 8), (0, 0))), rel_bias)[:, :nq].reshape(ATTN_HEADS, nq, 3, PAGE_SIZE)
    own = jnp.arange(ATTN_HEADS)[:, None] == jnp.arange(ATTN_HEADS)[None, :]
    full = jnp.where(own[:, None, None, None, :], b3[..., None], -jnp.inf)
    full = jnp.broadcast_to(full[:, None], (ATTN_HEADS, 2, nq, 3, PAGE_SIZE, ATTN_HEADS))
    full = full.transpose(3, 0, 1, 2, 4, 5).reshape(3, rows, cols)
    bias_pages = full[:2]
    bias_new = jnp.pad(full[2][:, :nnew], ((0, 0), (0, LANE - nnew)), constant_values=-jnp.inf)
    new_q = pl.BlockSpec((None, nq, ATTN_HEADS * LANE), lambda b, j, pt: (b, 0, 0))
    new_kv = pl.BlockSpec((None, nnew, LANE), lambda b, j, pt: (b, 0, 0))
    pps = math.gcd(DECODE_PAGES, npages)

    def page(i):
        return pl.BlockSpec((None, PAGE_SIZE, ATTN_HEADS, LANE), lambda b, j, pt: (pt[b, j * pps + i], 0, 0, 0))

    pages = [page(i) for i in range(pps)]
    grid_spec = pltpu.PrefetchScalarGridSpec(
        num_scalar_prefetch=1,
        grid=(nb, npages // pps),
        in_specs=[new_q, new_kv, new_kv, *pages, *pages,
                  pl.BlockSpec((2, rows, cols), lambda b, j, pt: (0, 0, 0)),
                  pl.BlockSpec((rows, LANE), lambda b, j, pt: (0, 0)),
                  pl.BlockSpec((4, ATTN_HEAD_DIM), lambda b, j, pt: (0, 0)),
                  pl.BlockSpec((1, ATTN_V_DIM), lambda b, j, pt: (0, 0))],
        out_specs=new_q,
        scratch_shapes=[pltpu.VMEM((rows, LANE), F32), pltpu.VMEM((rows, 1), F32),
                        pltpu.VMEM((rows, 1), F32), pltpu.VMEM((rows, LANE), F32)])
    return pl.pallas_call(
        functools.partial(_decode_kernel, nq=nq, pps=pps),
        grid_spec=grid_spec,
        out_shape=jax.ShapeDtypeStruct((nb, nq, ATTN_HEADS * ATTN_V_DIM), F32),
        compiler_params=_params("arbitrary", "arbitrary"),
        name="attn_decode",
    )(page_table, q, k_new, v_new, *([cache_k] * pps), *([cache_v] * pps), bias_pages, bias_new, lamp,
      subln_w.reshape(1, ATTN_V_DIM))


HALF = D_MODEL // 2


def _pack_rows(x):
    def rounded(v):
        bits = lax.bitcast_convert_type(v, jnp.uint32)
        return bits + (jnp.uint32(0x7FFF) + ((bits >> 16) & jnp.uint32(1)))
    return (rounded(x[:, :HALF]) >> 16) | (rounded(x[:, HALF:]) & jnp.uint32(0xFFFF0000))


def _unpack_rows(u):
    lo = lax.bitcast_convert_type(u << 16, F32)
    hi = lax.bitcast_convert_type(u & jnp.uint32(0xFFFF0000), F32)
    return jnp.concatenate([lo, hi], axis=1)


def _layer_norm(r, g, b):
    mu = jnp.mean(r, axis=-1, keepdims=True)
    d = r - mu
    var = jnp.mean(d * d, axis=-1, keepdims=True)
    return d * lax.rsqrt(var + LN_EPS) * g + b


def _merge_kernel(yn_ref, on_ref, ga_ref, gb_ref, x_ref, g1_ref, sc2_ref, sh2_ref, wa_ref, wb_ref, wo_ref,
                  lg_ref, lb_ref, wr_ref, x1_ref, u2p_ref, logit_ref):
    ba = _dot(yn_ref[...].astype(BF16), wa_ref[...])
    bb = _dot(on_ref[...].astype(BF16), wb_ref[...])
    merged = jax.nn.sigmoid(ga_ref[...].astype(F32)) * ba + jax.nn.sigmoid(gb_ref[...].astype(F32)) * bb
    t = _dot(merged.astype(BF16), wo_ref[...])
    x1 = _layer_norm(ALPHA * x_ref[...] + g1_ref[...] * t, lg_ref[...], lb_ref[...])
    x1_ref[...] = x1
    u2 = x1 * (1.0 + sc2_ref[...]) + sh2_ref[...]
    u2p_ref[...] = _pack_rows(u2)
    u_hi = u2.astype(BF16)
    u_lo = (u2 - u_hi.astype(F32)).astype(BF16)
    by_hi = _dot_nt(wr_ref[...], u_hi)
    logit_ref[...] = by_hi[:N_EXPERTS] + by_hi[N_EXPERTS:] + _dot_nt(wr_ref[:N_EXPERTS, :], u_lo)


def _mod_spec(mod, t, tm, rows_per_mod):
    if mod.shape[1] != 1:
        return mod.reshape(t, D_MODEL), pl.BlockSpec((tm, D_MODEL), lambda i: (i, 0))
    assert rows_per_mod % tm == 0
    return mod, pl.BlockSpec((None, 1, D_MODEL), lambda i: (i // (rows_per_mod // tm), 0, 0))


def _merge(yn, on, ga, gb, x, g1, sc2, sh2, wa, wb, wo, ln_g, ln_b, w_router, rows_per_mod):
    t = x.shape[0]
    tm = min(256, t)
    rows = pl.BlockSpec((tm, D_MODEL), lambda i: (i, 0))
    wspec = pl.BlockSpec((D_MODEL, D_MODEL), lambda i: (0, 0))
    vec = pl.BlockSpec((1, D_MODEL), lambda i: (0, 0))
    g1a, mspec = _mod_spec(g1, t, tm, rows_per_mod)
    sc2a, _ = _mod_spec(sc2, t, tm, rows_per_mod)
    sh2a, _ = _mod_spec(sh2, t, tm, rows_per_mod)
    wr_t = w_router.T
    wr_hi = wr_t.astype(BF16)
    return pl.pallas_call(
        _merge_kernel,
        grid=(t // tm,),
        in_specs=[rows, rows, rows, rows, rows, mspec, mspec, mspec, wspec, wspec, wspec, vec, vec,
                  pl.BlockSpec((2 * N_EXPERTS, D_MODEL), lambda i: (0, 0))],
        out_specs=[rows, pl.BlockSpec((tm, HALF), lambda i: (i, 0)), pl.BlockSpec((N_EXPERTS, tm), lambda i: (0, i))],
        out_shape=[jax.ShapeDtypeStruct((t, D_MODEL), F32), jax.ShapeDtypeStruct((t, HALF), jnp.uint32),
                   jax.ShapeDtypeStruct((N_EXPERTS, t), F32)],
        compiler_params=_params("arbitrary"),
        name="merge",
    )(yn, on, ga, gb, x, g1a, sc2a, sh2a, wa, wb, wo, ln_g.reshape(1, D_MODEL), ln_b.reshape(1, D_MODEL),
      jnp.concatenate([wr_hi, (wr_t - wr_hi.astype(F32)).astype(BF16)], axis=0))


def _router_kernel(logit_ref, br_ref, e_ref, w_ref, r_ref, cnt_ref, carry_scr, tri_scr, *, tr):
    i = pl.program_id(0)

    @pl.when(i == 0)
    def _():
        carry_scr[...] = jnp.zeros_like(carry_scr)
        a = lax.broadcasted_iota(jnp.int32, (tr, tr), 0)
        b = lax.broadcasted_iota(jnp.int32, (tr, tr), 1)
        tri_scr[...] = (a < b).astype(BF16)

    scores = jax.nn.sigmoid(logit_ref[...])
    biased = scores + br_ref[...]
    ninf = -jnp.inf

    b3 = biased.reshape(N_EXPERT_GROUPS, GROUP_SIZE, tr)
    j3 = lax.broadcasted_iota(jnp.int32, b3.shape, 1).astype(F32)
    top1 = jnp.max(b3, axis=1, keepdims=True)
    first = jnp.min(jnp.where(b3 == top1, j3, float(GROUP_SIZE)), axis=1, keepdims=True)
    top2 = jnp.max(jnp.where(j3 == first, ninf, b3), axis=1, keepdims=True)
    gscore = (top1 + top2).reshape(N_EXPERT_GROUPS, tr)
    gi = lax.broadcasted_iota(jnp.int32, gscore.shape, 0).astype(F32)
    gsel = jnp.zeros(gscore.shape, F32)
    for _ in range(TOPK_GROUPS):
        mx = jnp.max(gscore, axis=0, keepdims=True)
        pick = gi == jnp.min(jnp.where(gscore == mx, gi, float(N_EXPERT_GROUPS)), axis=0, keepdims=True)
        gsel = jnp.where(pick, 1.0, gsel)
        gscore = jnp.where(pick, ninf, gscore)
    emask = jnp.broadcast_to(gsel.reshape(N_EXPERT_GROUPS, 1, tr), b3.shape).reshape(N_EXPERTS, tr)
    masked = jnp.where(emask > 0.5, biased, ninf)

    ei = lax.broadcasted_iota(jnp.int32, masked.shape, 0).astype(F32)
    picked = jnp.zeros(masked.shape, F32)
    idxs, wsel = [], []
    for _ in range(TOP_K):
        mx = jnp.max(masked, axis=0, keepdims=True)
        idx = jnp.min(jnp.where(masked == mx, ei, float(N_EXPERTS)), axis=0, keepdims=True)
        pick = ei == idx
        idxs.append(idx)
        wsel.append(jnp.sum(jnp.where(pick, scores, 0.0), axis=0, keepdims=True))
        picked = jnp.where(pick, 1.0, picked)
        masked = jnp.where(pick, ninf, masked)
    wall = jnp.concatenate(wsel, axis=0)
    w_ref[...] = wall / jnp.sum(wall, axis=0, keepdims=True) * ROUTED_SCALE
    e_ref[...] = jnp.concatenate(idxs, axis=0).astype(jnp.int32)

    rank = carry_scr[:, 0:1] + _dot(picked.astype(BF16), tri_scr[...])
    r_ref[...] = jnp.concatenate(
        [jnp.sum(jnp.where(ei == idx, rank, 0.0), axis=0, keepdims=True) for idx in idxs], axis=0).astype(jnp.int32)
    total = carry_scr[...] + jnp.sum(picked, axis=1, keepdims=True)
    carry_scr[...] = total
    cnt_ref[...] = total.astype(jnp.int32)


def _router(logits, b_router):
    t = logits.shape[1]
    tr = min(512, t)
    tok = pl.BlockSpec((TOP_K, tr), lambda i: (0, i))
    e_t, w_t, r_t, cnt = pl.pallas_call(
        functools.partial(_router_kernel, tr=tr),
        grid=(t // tr,),
        in_specs=[pl.BlockSpec((N_EXPERTS, tr), lambda i: (0, i)),
                  pl.BlockSpec((N_EXPERTS, 1), lambda i: (0, 0))],
        out_specs=[tok, tok, tok, pl.BlockSpec((N_EXPERTS, LANE), lambda i: (0, 0))],
        out_shape=[jax.ShapeDtypeStruct((TOP_K, t), jnp.int32), jax.ShapeDtypeStruct((TOP_K, t), F32),
                   jax.ShapeDtypeStruct((TOP_K, t), jnp.int32), jax.ShapeDtypeStruct((N_EXPERTS, LANE), jnp.int32)],
        scratch_shapes=[pltpu.VMEM((N_EXPERTS, LANE), F32), pltpu.VMEM((tr, tr), BF16)],
        compiler_params=_params("arbitrary"),
        name="router",
    )(logits, b_router.reshape(N_EXPERTS, 1))
    return e_t, w_t, r_t, cnt


def _plan_kernel(e_ref, r_ref, cnt_ref, dest_ref, be_ref, bv_ref, nu_ref, *, br, nblk_pad):
    cnt = cnt_ref[...]
    shift = br.bit_length() - 1
    padded = lax.shift_left(lax.shift_right_logical(cnt + (br - 1), shift), shift)
    ea = lax.broadcasted_iota(jnp.int32, (N_EXPERTS, N_EXPERTS), 0)
    eb = lax.broadcasted_iota(jnp.int32, (N_EXPERTS, N_EXPERTS), 1)
    pends = jnp.dot((eb <= ea).astype(F32), padded.astype(F32), preferred_element_type=F32,
                    precision=HIGHEST).astype(jnp.int32)
    pstart = pends - padded

    e = e_ref[...]
    first_row = jnp.zeros(e.shape, jnp.int32)
    for x in range(N_EXPERTS):
        first_row = jnp.where(e == x, pstart[x:x + 1, 0:1], first_row)
    dest_ref[...] = first_row + r_ref[...]

    @pl.when(pl.program_id(0) == 0)
    def _():
        esub = lax.broadcasted_iota(jnp.int32, (N_EXPERTS, LANE), 0)
        real_end = (pstart + cnt).astype(F32)
        for c in range(nblk_pad // LANE):
            bstart = (lax.broadcasted_iota(jnp.int32, (1, LANE), 1) + c * LANE) * br
            be = jnp.minimum(jnp.sum((pends <= bstart).astype(F32), axis=0, keepdims=True),
                             float(N_EXPERTS - 1)).astype(jnp.int32)
            end = jnp.sum(jnp.where(esub == be, real_end, 0.0), axis=0, keepdims=True).astype(jnp.int32)
            be_ref[:, c * LANE:(c + 1) * LANE] = be
            bv_ref[:, c * LANE:(c + 1) * LANE] = jnp.clip(end - bstart, 0, br)
        nu_ref[...] = lax.shift_right_logical(pends[N_EXPERTS - 1:N_EXPERTS, :], shift)


def _plan(e_t, r_t, cnt, br, nblk):
    t = e_t.shape[1]
    tc = min(2048, t)
    nblk_pad = -(-nblk // LANE) * LANE
    tok = pl.BlockSpec((TOP_K, tc), lambda i: (0, i))
    blk = pl.BlockSpec((1, nblk_pad), lambda i: (0, 0))
    dest, be, bv, nu = pl.pallas_call(
        functools.partial(_plan_kernel, br=br, nblk_pad=nblk_pad),
        grid=(t // tc,),
        in_specs=[tok, tok, pl.BlockSpec((N_EXPERTS, LANE), lambda i: (0, 0))],
        out_specs=[tok, blk, blk, pl.BlockSpec((1, LANE), lambda i: (0, 0))],
        out_shape=[jax.ShapeDtypeStruct((TOP_K, t), jnp.int32), jax.ShapeDtypeStruct((1, nblk_pad), jnp.int32),
                   jax.ShapeDtypeStruct((1, nblk_pad), jnp.int32), jax.ShapeDtypeStruct((1, LANE), jnp.int32)],
        compiler_params=_params("arbitrary"),
        name="moe_plan",
    )(e_t, r_t, cnt)
    return dest, be[0, :nblk], bv[0, :nblk], nu[0, :1]


def _row_copy(src, s, dst, d, sem):
    return pltpu.make_async_copy(src.at[pl.ds(s, 1), :], dst.at[pl.ds(d, 1), :], sem)


def _dispatch_kernel(dest_ref, u_ref, xs_ref, sem, *, tm):
    def start(r, carry):
        for k in range(TOP_K):
            _row_copy(u_ref, r, xs_ref, dest_ref[k, r], sem).start()
        return carry

    def wait(r, carry):
        for k in range(TOP_K):
            _row_copy(u_ref, r, xs_ref, dest_ref[k, r], sem).wait()
        return carry

    lax.fori_loop(0, tm, start, 0)
    lax.fori_loop(0, tm, wait, 0)


def _dispatch(u2p, dest_t, n_rows):
    t = u2p.shape[0]
    tm = min(256, t)
    nt = t // tm
    dest_blocks = dest_t.reshape(TOP_K, nt, tm).transpose(1, 0, 2)
    return pl.pallas_call(
        functools.partial(_dispatch_kernel, tm=tm),
        grid=(nt,),
        in_specs=[pl.BlockSpec((None, TOP_K, tm), lambda i: (i, 0, 0), memory_space=pltpu.SMEM),
                  pl.BlockSpec((tm, HALF), lambda i: (i, 0))],
        out_specs=pl.BlockSpec(memory_space=pl.ANY),
        out_shape=jax.ShapeDtypeStruct((n_rows, HALF), jnp.uint32),
        scratch_shapes=[pltpu.SemaphoreType.DMA(())],
        compiler_params=_params("arbitrary"),
        name="moe_dispatch",
    )(dest_blocks, u2p)


def _expert_kernel(be_ref, bv_ref, nu_ref, x_ref, wg_ref, wu_ref, wd_ref, y_ref, wg_scr, wu_scr, wd_scr, *, br):
    i = pl.program_id(0)
    prev = be_ref[jnp.maximum(i - 1, 0)]

    @pl.when(jnp.logical_and(i < nu_ref[0], jnp.logical_or(i == 0, be_ref[i] != prev)))
    def _():
        wg_scr[...] = wg_ref[...].astype(BF16)
        wu_scr[...] = wu_ref[...].astype(BF16)
        wd_scr[...] = wd_ref[...].astype(BF16)

    @pl.when(i < nu_ref[0])
    def _():
        live = lax.broadcasted_iota(jnp.int32, (br, HALF), 0) < bv_ref[i]
        x = _unpack_rows(jnp.where(live, x_ref[...], jnp.uint32(0))).astype(BF16)
        hcat = _silu(_dot(x, wg_scr[...])) * _dot(x, wu_scr[...])
        y_ref[...] = _pack_rows(_dot(hcat.astype(BF16), wd_scr[...]))


def _experts(x_sorted, blk_e, blk_valid, n_used, w_gate, w_up, w_down, br):
    n_rows = x_sorted.shape[0]
    nblk = n_rows // br

    def blk(i, be, bv, nu):
        return (jnp.minimum(i, nu[0] - 1), 0)

    grid_spec = pltpu.PrefetchScalarGridSpec(
        num_scalar_prefetch=3,
        grid=(nblk,),
        in_specs=[pl.BlockSpec((br, HALF), blk),
                  pl.BlockSpec((None, D_MODEL, D_EXPERT), lambda i, be, bv, nu: (be[i], 0, 0)),
                  pl.BlockSpec((None, D_MODEL, D_EXPERT), lambda i, be, bv, nu: (be[i], 0, 0)),
                  pl.BlockSpec((None, D_EXPERT, D_MODEL), lambda i, be, bv, nu: (be[i], 0, 0))],
        out_specs=pl.BlockSpec((br, HALF), blk),
        scratch_shapes=[pltpu.VMEM((D_MODEL, D_EXPERT), BF16), pltpu.VMEM((D_MODEL, D_EXPERT), BF16),
                        pltpu.VMEM((D_EXPERT, D_MODEL), BF16)])
    return pl.pallas_call(
        functools.partial(_expert_kernel, br=br),
        grid_spec=grid_spec,
        out_shape=jax.ShapeDtypeStruct((n_rows, HALF), jnp.uint32),
        compiler_params=_params("arbitrary"),
        name="moe_experts",
    )(blk_e, blk_valid, n_used, x_sorted, w_gate, w_up, w_down)


def _combine_kernel(dest_ref, ys_ref, w_ref, u_ref, x1_ref, g2_ref, sg_ref, su_ref, sd_ref, lg_ref, lb_ref,
                    o_ref, g_scr, sem, *, tm):
    def start(r, carry):
        for k in range(TOP_K):
            _row_copy(ys_ref, dest_ref[k, r], g_scr.at[k], r, sem).start()
        return carry

    def wait(r, carry):
        for k in range(TOP_K):
            _row_copy(ys_ref, dest_ref[k, r], g_scr.at[k], r, sem).wait()
        return carry

    lax.fori_loop(0, tm, start, 0)
    ub = _unpack_rows(u_ref[...]).astype(BF16)
    hs = _silu(_dot(ub, sg_ref[...])) * _dot(ub, su_ref[...])
    f = _dot(hs.astype(BF16), sd_ref[...])
    lax.fori_loop(0, tm, wait, 0)
    w = w_ref[...]
    for k in range(TOP_K):
        f = f + _unpack_rows(g_scr[k]) * w[:, k:k + 1]
    o_ref[...] = _layer_norm(ALPHA * x1_ref[...] + g2_ref[...] * f, lg_ref[...], lb_ref[...])


def _combine(y_sorted, dest_t, w_t, u2p, x1, g2, ws_gate, ws_up, ws_down, ln_g, ln_b, rows_per_mod):
    t = u2p.shape[0]
    tm = min(128, t)
    nt = t // tm
    dest_blocks = dest_t.reshape(TOP_K, nt, tm).transpose(1, 0, 2)
    rows = pl.BlockSpec((tm, D_MODEL), lambda i: (i, 0))
    vec = pl.BlockSpec((1, D_MODEL), lambda i: (0, 0))
    g2a, mspec = _mod_spec(g2, t, tm, rows_per_mod)
    return pl.pallas_call(
        functools.partial(_combine_kernel, tm=tm),
        grid=(nt,),
        in_specs=[pl.BlockSpec((None, TOP_K, tm), lambda i: (i, 0, 0), memory_space=pltpu.SMEM),
                  pl.BlockSpec(memory_space=pl.ANY),
                  pl.BlockSpec((tm, TOP_K), lambda i: (i, 0)),
                  pl.BlockSpec((tm, HALF), lambda i: (i, 0)), rows, mspec,
                  pl.BlockSpec((D_MODEL, D_SHARED), lambda i: (0, 0)),
                  pl.BlockSpec((D_MODEL, D_SHARED), lambda i: (0, 0)),
                  pl.BlockSpec((D_SHARED, D_MODEL), lambda i: (0, 0)),
                  vec, vec],
        out_specs=rows,
        out_shape=jax.ShapeDtypeStruct((t, D_MODEL), F32),
        scratch_shapes=[pltpu.VMEM((TOP_K, tm, HALF), jnp.uint32), pltpu.SemaphoreType.DMA(())],
        compiler_params=_params("arbitrary"),
        name="moe_combine",
    )(dest_blocks, y_sorted, w_t.T, u2p, x1, g2a, ws_gate.astype(BF16), ws_up.astype(BF16), ws_down.astype(BF16),
      ln_g.reshape(1, D_MODEL), ln_b.reshape(1, D_MODEL))


def _moe(u2p, logits, x1, g2, p, rows_per_mod, br):
    t = u2p.shape[0]
    e_t, w_t, r_t, counts = _router(logits, p["b_router"])
    n_rows = (t * TOP_K + N_EXPERTS * (br - 1) + br - 1) // br * br
    dest_t, blk_e, blk_valid, n_used = _plan(e_t, r_t, counts, br, n_rows // br)
    x_sorted = _dispatch(u2p, dest_t, n_rows)
    y_sorted = _experts(x_sorted, blk_e, blk_valid, n_used, p["w_gate"], p["w_up"], p["w_down"], br)
    return _combine(y_sorted, dest_t, w_t, u2p, x1, g2, p["ws_gate"], p["ws_up"], p["ws_down"],
                    p["ln2_g"], p["ln2_b"], rows_per_mod)


def _cat_w_in(w_in):
    sizes = (SSD_INNER, CONV_DIM, SSD_HEADS, 1024, 1024, 1024, D_MODEL, D_MODEL)
    offs = [0]
    for s in sizes:
        offs.append(offs[-1] + s)
    z, xbc, dt, q, k, v, ga, gb = [w_in[:, offs[i]:offs[i + 1]] for i in range(8)]
    dt = jnp.pad(dt, ((0, 0), (0, SEG_END - SEG_DT - SSD_HEADS)))
    return jnp.concatenate([z, xbc, q, k, v, ga, gb, dt], axis=1).astype(BF16)


def kernel(x_prompt, x_sample, c_prompt, c_sample, cache_k, cache_v, page_table, state_conv, state_ssm, rel_bias, w_ada, b_ada, w_in, conv_w, conv_b, dt_bias, a_log, d_skip, ssd_norm_w, lam_q1, lam_k1, lam_q2, lam_k2, subln_w, w_br_ssd, w_br_attn, w_out, ln1_g, ln1_b, w_router, b_router, w_gate, w_up, w_down, ws_gate, ws_up, ws_down, ln2_g, ln2_b):
    assert w_in.shape[0] == DEPTH
    nbp, seq, _ = x_prompt.shape
    nbs, dseq, _ = x_sample.shape
    tp, ts = nbp * seq, nbs * dseq
    past = page_table.shape[1] * PAGE_SIZE
    cl = SSD_CHUNK

    pad_p = -nbp % 8
    c_all = jnp.concatenate([c_prompt, jnp.zeros((pad_p, D_MODEL), F32), jnp.repeat(c_sample, dseq, axis=0)], axis=0)
    mod = _adaln(c_all, w_ada[0], b_ada[0])
    mod_p = mod[:nbp].reshape(nbp, 1, 6, D_MODEL)
    mod_s = mod[nbp + pad_p:].reshape(nbs, dseq, 6, D_MODEL)
    mp = [mod_p[:, :, i] for i in range(6)]
    ms = [mod_s[:, :, i] for i in range(6)]

    w_cat = _cat_w_in(w_in[0])
    wa, wb, wo = w_br_ssd[0].astype(BF16), w_br_attn[0].astype(BF16), w_out[0].astype(BF16)
    lamp = jnp.stack([lam_q1[0], lam_k1[0], lam_q2[0], lam_k2[0]])
    moe_p = dict(b_router=b_router[0], w_gate=w_gate[0], w_up=w_up[0], w_down=w_down[0],
                 ws_gate=ws_gate[0], ws_up=ws_up[0], ws_down=ws_down[0], ln2_g=ln2_g[0], ln2_b=ln2_b[0])
    ssd_w = (conv_w[0], conv_b[0], dt_bias[0], a_log[0], d_skip[0], ssd_norm_w[0])

    xp = x_prompt.reshape(tp, D_MODEL)
    z, xbc, dt, q, k, v, kb, vb, ga, gb = _inproj(xp, mp[1], mp[0], w_cat, seq)
    yn, conv_p, h_p = _ssd(xbc, dt, z, *ssd_w, jnp.zeros((nbp, CONV_W - 1, CONV_DIM), F32),
                           jnp.zeros((nbp, SSD_HEADS, SSD_HEADDIM, D_STATE), F32), nbp, seq // cl, cl)
    on = _attn_prompt(q, kb, vb, rel_bias, lamp, subln_w[0], nbp, seq)
    x1, u2p, logits = _merge(yn, on, ga, gb, xp, mp[2], mp[4], mp[3], wa, wb, wo, ln1_g[0], ln1_b[0], w_router[0], seq)
    y_prompt = _moe(u2p, logits, x1, mp[5], moe_p, seq, 256).reshape(nbp, seq, D_MODEL)
    k_prompt = k.reshape(1, nbp, seq, ATTN_HEADS, 2 * ATTN_HEAD_DIM)
    v_prompt = v.reshape(1, nbp, seq, ATTN_HEADS, ATTN_V_DIM)

    xs_ = x_sample.reshape(ts, D_MODEL)
    z, xbc, dt, q, k, v, kb, vb, ga, gb = _inproj(xs_, ms[1], ms[0], w_cat, dseq)

    def chunk_pad(a):
        a = a.reshape(nbs, dseq, a.shape[-1])
        return jnp.pad(a, ((0, 0), (0, cl - dseq), (0, 0))).reshape(nbs * cl, a.shape[-1])

    yn, conv_s, h_s = _ssd(chunk_pad(xbc), chunk_pad(dt), chunk_pad(z), *ssd_w, state_conv[0], state_ssm[0],
                           nbs, 1, dseq)
    yn = yn.reshape(nbs, cl, SSD_INNER)[:, :dseq].reshape(ts, SSD_INNER)
    on = _attn_decode(q.reshape(nbs, dseq, -1), k.reshape(nbs, dseq * ATTN_HEADS, LANE),
                      v.reshape(nbs, dseq * ATTN_HEADS, LANE), cache_k[0], cache_v[0], page_table,
                      rel_bias, lamp, subln_w[0])
    x1, u2p, logits = _merge(yn, on.reshape(ts, -1), ga, gb, xs_, ms[2], ms[4], ms[3], wa, wb, wo, ln1_g[0], ln1_b[0],
                             w_router[0], dseq)
    y_sample = _moe(u2p, logits, x1, ms[5], moe_p, dseq, 64).reshape(nbs, dseq, D_MODEL)
    k_sample = k.reshape(1, nbs, dseq, ATTN_HEADS, 2 * ATTN_HEAD_DIM)
    v_sample = v.reshape(1, nbs, dseq, ATTN_HEADS, ATTN_V_DIM)

    return (y_prompt, y_sample, k_prompt, v_prompt, conv_p[None], h_p.reshape(1, nbp, SSD_HEADS, SSD_HEADDIM, D_STATE),
            k_sample, v_sample, conv_s[None], h_s.reshape(1, nbs, SSD_HEADS, SSD_HEADDIM, D_STATE))
```

```python
import functools
import math

import jax
import jax.numpy as jnp
from jax import lax
from jax.experimental import pallas as pl
from jax.experimental.pallas import tpu as pltpu
from jax.experimental.pallas import tpu_sc as plsc

F32 = jnp.float32
BF16 = jnp.bfloat16
HIGHEST = lax.Precision.HIGHEST

D_MODEL = 1024
SSD_INNER = 1024
SSD_HEADDIM = 64
SSD_HEADS = 16
SSD_GROUPS = 2
D_STATE = 128
CONV_W = 4
CONV_DIM = SSD_INNER + 2 * SSD_GROUPS * D_STATE
SSD_CHUNK = 128
ATTN_HEADS = 8
ATTN_HEAD_DIM = 64
ATTN_V_DIM = 128
N_BUCKETS = 32
MAX_DISTANCE = 128
N_EXPERTS = 64
N_EXPERT_GROUPS = 8
GROUP_SIZE = N_EXPERTS // N_EXPERT_GROUPS
TOPK_GROUPS = 4
TOP_K = 8
D_EXPERT = 256
D_SHARED = 256
ROUTED_SCALE = 2.5
PAGE_SIZE = 128
DEPTH = 1
ALPHA = (2 * DEPTH) ** 0.25
LN_EPS = 1e-5
RMS_EPS = 1e-5
LAM_INIT = 0.8 - 0.6 * math.exp(-0.3 * 0)
LANE = 128
VMEM_LIMIT = 56 * 1024 * 1024

SEG_Z, SEG_XBC, SEG_Q, SEG_K, SEG_V, SEG_GA, SEG_GB, SEG_DT, SEG_END = (
    0, 1024, 2560, 3584, 4608, 5632, 6656, 7680, 7808)


def _silu(x):
    return x * jax.nn.sigmoid(x)


def _dot(a, b):
    return jnp.dot(a, b, preferred_element_type=F32)


def _dot_nt(a, b):
    return lax.dot_general(a, b, (((1,), (1,)), ((), ())), preferred_element_type=F32)


def _dot_tn(a, b):
    return lax.dot_general(a, b, (((0,), (0,)), ((), ())), preferred_element_type=F32)


def _params(*sem):
    return pltpu.CompilerParams(dimension_semantics=sem, vmem_limit_bytes=VMEM_LIMIT)


def _adaln_kernel(c_ref, w_ref, b_ref, o_ref):
    s = _silu(c_ref[...]).astype(BF16)
    o_ref[...] = _dot(s, w_ref[...].astype(BF16)) + b_ref[...]


def _adaln(c, w_ada, b_ada):
    r = c.shape[0]
    n = w_ada.shape[1]
    tn = 1024
    return pl.pallas_call(
        _adaln_kernel,
        grid=(n // tn,),
        in_specs=[pl.BlockSpec((r, D_MODEL), lambda j: (0, 0)),
                  pl.BlockSpec((D_MODEL, tn), lambda j: (0, j)),
                  pl.BlockSpec((1, tn), lambda j: (0, j))],
        out_specs=pl.BlockSpec((r, tn), lambda j: (0, j)),
        out_shape=jax.ShapeDtypeStruct((r, n), F32),
        compiler_params=_params("arbitrary"),
        name="adaln",
    )(c, w_ada, b_ada.reshape(1, n))


def _inproj_kernel(x_ref, sc_ref, sh_ref, w_ref, z_ref, xbc_ref, dt_ref, q_ref, k_ref, v_ref,
                   kb_ref, vb_ref, ga_ref, gb_ref):
    u = (x_ref[...] * (1.0 + sc_ref[...]) + sh_ref[...]).astype(BF16)

    def seg(a, b):
        return _dot(u, w_ref[:, a:b])

    z_ref[...] = seg(SEG_Z, SEG_XBC).astype(BF16)
    xbc_ref[...] = seg(SEG_XBC, SEG_Q)
    q_ref[...] = (seg(SEG_Q, SEG_K) * (ATTN_HEAD_DIM ** -0.5)).astype(BF16)
    kk = seg(SEG_K, SEG_V)
    k_ref[...] = kk
    kb_ref[...] = kk.astype(BF16)
    vv = seg(SEG_V, SEG_GA)
    v_ref[...] = vv
    vb_ref[...] = vv.astype(BF16)
    ga_ref[...] = seg(SEG_GA, SEG_GB).astype(BF16)
    gb_ref[...] = seg(SEG_GB, SEG_DT).astype(BF16)
    dt_ref[...] = seg(SEG_DT, SEG_END)


def _inproj(x, sc, sh, w_cat, rows_per_mod):
    t = x.shape[0]
    tm = min(256, t)
    per_row = sc.shape[1] != 1
    if per_row:
        sc2 = sc.reshape(t, D_MODEL)
        sh2 = sh.reshape(t, D_MODEL)
        mod_spec = pl.BlockSpec((tm, D_MODEL), lambda i: (i, 0))
    else:
        assert rows_per_mod % tm == 0
        sc2, sh2 = sc, sh
        mod_spec = pl.BlockSpec((None, 1, D_MODEL), lambda i: (i // (rows_per_mod // tm), 0, 0))

    def rows(width):
        return pl.BlockSpec((tm, width), lambda i: (i, 0))

    widths = (1024, CONV_DIM, LANE, 1024, 1024, 1024, 1024, 1024, 1024, 1024)
    dtypes = (BF16, F32, F32, BF16, F32, F32, BF16, BF16, BF16, BF16)
    return pl.pallas_call(
        _inproj_kernel,
        grid=(t // tm,),
        in_specs=[rows(D_MODEL), mod_spec, mod_spec,
                  pl.BlockSpec((D_MODEL, SEG_END), lambda i: (0, 0), pipeline_mode=pl.Buffered(1))],
        out_specs=[rows(w) for w in widths],
        out_shape=[jax.ShapeDtypeStruct((t, w), d) for w, d in zip(widths, dtypes)],
        compiler_params=_params("arbitrary"),
        name="inproj",
    )(x, sc2, sh2, w_cat)


def _ssd_kernel(xbc_ref, dt_ref, z_ref, cw_ref, cb_ref, dtb_ref, alog_ref, dsk_ref, nw_ref,
                cbuf_ref, h0_ref, yn_ref, cnew_ref, hnew_ref, xp_scr, h_scr, y_scr, *, valid_len):
    c = pl.program_id(1)
    nc = pl.num_programs(1)
    cl = SSD_CHUNK
    head = 8

    @pl.when(c == 0)
    def _():
        xp_scr[head - (CONV_W - 1):head, :] = cbuf_ref[...]
        h_scr[...] = h0_ref[...]

    @pl.when(c > 0)
    def _():
        xp_scr[head - (CONV_W - 1):head, :] = xp_scr[head + cl - (CONV_W - 1):head + cl, :]

    xp_scr[head:head + cl, :] = xbc_ref[...]

    acc = xp_scr[head:head + cl, :] * cw_ref[CONV_W - 1:CONV_W, :]
    for j in range(CONV_W - 1):
        lo = head - (CONV_W - 1) + j
        acc = acc + xp_scr[lo:lo + cl, :] * cw_ref[j:j + 1, :]
    xc = _silu(acc + cb_ref[...])
    xs = xc[:, :SSD_INNER]
    bmat = [xc[:, SSD_INNER + g * D_STATE:SSD_INNER + (g + 1) * D_STATE].astype(BF16)
            for g in range(SSD_GROUPS)]
    coff = SSD_INNER + SSD_GROUPS * D_STATE
    cmat = [xc[:, coff + g * D_STATE:coff + (g + 1) * D_STATE].astype(BF16) for g in range(SSD_GROUPS)]

    li = lax.broadcasted_iota(jnp.int32, (cl, cl), 0)
    si = lax.broadcasted_iota(jnp.int32, (cl, cl), 1)
    causal = li >= si
    lane_lo = si < SSD_HEADDIM
    sub_lo = li < SSD_HEADDIM

    dpre = dt_ref[...] + dtb_ref[...]
    dtv = jnp.maximum(dpre, 0.0) + jnp.log(1.0 + jnp.exp(-jnp.abs(dpre)))
    if valid_len < cl:
        dtv = jnp.where(li < valid_len, dtv, 0.0)
    da = dtv * (-jnp.exp(alog_ref[...]))
    tril = causal.astype(F32)
    acum = jnp.dot(tril, da, preferred_element_type=F32, precision=HIGHEST)
    acum_t = acum.T
    dt_t = dtv.T
    last = acum[cl - 1:cl, :]

    cb = [_dot_nt(cmat[g], bmat[g]) for g in range(SSD_GROUPS)]

    def colb(a, h):
        return jnp.broadcast_to(a[:, h:h + 1], (cl, cl))

    pairs = SSD_HEADS // 2
    for p in range(pairs):
        h0, h1 = 2 * p, 2 * p + 1
        g = h0 // (SSD_HEADS // SSD_GROUPS)
        xpair = xs[:, p * LANE:(p + 1) * LANE]
        x_lo = jnp.where(lane_lo, xpair, 0.0).astype(BF16)
        x_hi = jnp.where(lane_lo, 0.0, xpair).astype(BF16)
        ydiag = None
        for hh, xm in ((h0, x_lo), (h1, x_hi)):
            seg = colb(acum, hh) - acum_t[hh:hh + 1, :]
            dec = jnp.exp(jnp.where(causal, seg, -jnp.inf))
            m = (cb[g] * dec * dt_t[hh:hh + 1, :]).astype(BF16)
            part = _dot(m, xm)
            ydiag = part if ydiag is None else ydiag + part
        col0, col1 = colb(acum, h0), colb(acum, h1)
        ecol = jnp.where(lane_lo, jnp.exp(col0), jnp.exp(col1))
        hp = h_scr[p]
        yoff = _dot_nt(cmat[g], hp.astype(BF16)) * ecol
        l0, l1 = last[:, h0:h0 + 1], last[:, h1:h1 + 1]
        wcol = jnp.where(lane_lo, jnp.exp(l0 - col0) * colb(dtv, h0), jnp.exp(l1 - col1) * colb(dtv, h1))
        dx = (xpair * wcol).astype(BF16)
        st = _dot_tn(dx, bmat[g])
        hdec = jnp.where(sub_lo, jnp.exp(l0), jnp.exp(l1))
        h_scr[p] = hp * hdec + st
        y_scr[:, p * LANE:(p + 1) * LANE] = ydiag + yoff + dsk_ref[:, p * LANE:(p + 1) * LANE] * xpair

    zf = z_ref[...].astype(F32)
    gated = y_scr[...] * _silu(zf)
    gw = SSD_INNER // SSD_GROUPS
    for g in range(SSD_GROUPS):
        sg = gated[:, g * gw:(g + 1) * gw]
        ms = jnp.mean(sg * sg, axis=-1, keepdims=True)
        yn_ref[:, g * gw:(g + 1) * gw] = (sg * lax.rsqrt(ms + RMS_EPS) * nw_ref[:, g * gw:(g + 1) * gw]).astype(BF16)

    @pl.when(c == nc - 1)
    def _():
        cnew_ref[...] = xp_scr[head + valid_len - (CONV_W - 1):head + valid_len, :]
        hnew_ref[...] = h_scr[...]


def _ssd(xbc, dt, z, conv_w, conv_b, dt_bias, a_log, d_skip, norm_w, cbuf, h0, nb, nc, valid_len):
    cl = SSD_CHUNK
    pairs = SSD_HEADS // 2
    pad = LANE - SSD_HEADS
    row = lambda b, c: (b * nc + c, 0)
    const = lambda b, c: (0, 0)
    return pl.pallas_call(
        functools.partial(_ssd_kernel, valid_len=valid_len),
        grid=(nb, nc),
        in_specs=[pl.BlockSpec((cl, CONV_DIM), row), pl.BlockSpec((cl, LANE), row),
                  pl.BlockSpec((cl, SSD_INNER), row),
                  pl.BlockSpec((CONV_W, CONV_DIM), const), pl.BlockSpec((1, CONV_DIM), const),
                  pl.BlockSpec((1, LANE), const), pl.BlockSpec((1, LANE), const),
                  pl.BlockSpec((1, SSD_INNER), const), pl.BlockSpec((1, SSD_INNER), const),
                  pl.BlockSpec((None, CONV_W - 1, CONV_DIM), lambda b, c: (b, 0, 0)),
                  pl.BlockSpec((None, pairs, LANE, D_STATE), lambda b, c: (b, 0, 0, 0))],
        out_specs=[pl.BlockSpec((cl, SSD_INNER), row),
                   pl.BlockSpec((None, CONV_W - 1, CONV_DIM), lambda b, c: (b, 0, 0)),
                   pl.BlockSpec((None, pairs, LANE, D_STATE), lambda b, c: (b, 0, 0, 0))],
        out_shape=[jax.ShapeDtypeStruct((nb * nc * cl, SSD_INNER), BF16),
                   jax.ShapeDtypeStruct((nb, CONV_W - 1, CONV_DIM), F32),
                   jax.ShapeDtypeStruct((nb, pairs, LANE, D_STATE), F32)],
        scratch_shapes=[pltpu.VMEM((8 + cl, CONV_DIM), F32), pltpu.VMEM((pairs, LANE, D_STATE), F32),
                        pltpu.VMEM((cl, SSD_INNER), F32)],
        compiler_params=_params("arbitrary", "arbitrary"),
        name="ssd",
    )(xbc, dt, z, conv_w, conv_b.reshape(1, CONV_DIM),
      jnp.pad(dt_bias, (0, pad)).reshape(1, LANE), jnp.pad(a_log, (0, pad)).reshape(1, LANE),
      jnp.repeat(d_skip, SSD_HEADDIM).reshape(1, SSD_INNER), norm_w.reshape(1, SSD_INNER),
      cbuf, h0.reshape(nb, pairs, LANE, D_STATE))


def _bias_kernel(rel_ref, tab_ref, o_ref):
    rel = rel_ref[...]
    n = jnp.maximum(rel, 0)
    max_exact = N_BUCKETS // 2
    large = max_exact + (jnp.log(jnp.maximum(n, 1).astype(F32) / max_exact)
                         / math.log(MAX_DISTANCE / max_exact) * (N_BUCKETS - max_exact)).astype(jnp.int32)
    bucket = jnp.where(n < max_exact, n, jnp.minimum(large, N_BUCKETS - 1))
    outs = [jnp.zeros(rel.shape, F32) for _ in range(ATTN_HEADS)]
    for kb in range(N_BUCKETS):
        hit = bucket == kb
        for h in range(ATTN_HEADS):
            outs[h] = jnp.where(hit, tab_ref[kb, h], outs[h])
    for h in range(ATTN_HEADS):
        o_ref[h] = jnp.where(rel >= 0, outs[h], -jnp.inf)


def _bias_tiles(rel, rel_bias):
    r, c = rel.shape
    tr = min(r, 64)
    return pl.pallas_call(
        _bias_kernel,
        grid=(r // tr,),
        in_specs=[pl.BlockSpec((tr, c), lambda i: (i, 0)),
                  pl.BlockSpec(memory_space=pltpu.SMEM)],
        out_specs=pl.BlockSpec((ATTN_HEADS, tr, c), lambda i: (0, i, 0)),
        out_shape=jax.ShapeDtypeStruct((ATTN_HEADS, r, c), F32),
        compiler_params=_params("arbitrary"),
        name="t5bias",
    )(rel, rel_bias)


def _lam(lamp_ref):
    lp = lamp_ref[...]
    e1 = jnp.exp(jnp.sum(lp[0:1, :] * lp[1:2, :], axis=-1, keepdims=True))
    e2 = jnp.exp(jnp.sum(lp[2:3, :] * lp[3:4, :], axis=-1, keepdims=True))
    return e1 - e2 + LAM_INIT


def _subln(o, sw):
    ms = jnp.mean(o * o, axis=-1, keepdims=True)
    return o * lax.rsqrt(ms + RMS_EPS) * sw * (1.0 - LAM_INIT)


ATTN_TQ = 512
ATTN_STRIP = 512
ATTN_TK = 512
ATTN_D_MIN = 1 - ATTN_STRIP // ATTN_TK
ATTN_D_FAR = -(-(MAX_DISTANCE + ATTN_TK - 1) // ATTN_TK)


def _attn_kernel(q_ref, k_ref, v_ref, bias_ref, lamp_ref, sw_ref, o_ref, m_scr, l_scr, a_scr, *, tq, strip, tk):
    qi = pl.program_id(2)
    lane_lo = lax.broadcasted_iota(jnp.int32, (strip, LANE), 1) < ATTN_HEAD_DIM
    reps = tk // LANE
    lam = _lam(lamp_ref)
    for st in range(tq // strip):
        q = q_ref[st * strip:(st + 1) * strip, :]
        zero = jnp.zeros_like(q)
        qm = (jnp.where(lane_lo, q, zero), jnp.where(lane_lo, zero, q))
        m_scr[...] = jnp.full(m_scr.shape, -jnp.inf, F32)
        l_scr[...] = jnp.zeros(l_scr.shape, F32)
        a_scr[...] = jnp.zeros(a_scr.shape, F32)
        row0 = qi * (tq // strip) + st
        ntiles = (row0 * strip + strip - 1) // tk + 1

        def body(j, carry):
            off = pl.multiple_of(j * tk, tk)
            kt = k_ref[pl.ds(off, tk), :]
            vt = v_ref[pl.ds(off, tk), :]
            bias = bias_ref[jnp.minimum(row0 * (strip // tk) - j, ATTN_D_FAR) - ATTN_D_MIN]
            for mi in range(2):
                sc = _dot_nt(qm[mi], kt) + bias
                m_prev = m_scr[mi]
                m_new = jnp.maximum(m_prev, jnp.max(sc, axis=-1, keepdims=True))
                p = jnp.exp(sc - jnp.concatenate([m_new] * reps, axis=1))
                alpha = jnp.exp(m_prev - m_new)
                l_scr[mi] = alpha * l_scr[mi] + jnp.sum(p, axis=-1, keepdims=True)
                a_scr[mi] = alpha * a_scr[mi] + _dot(p.astype(BF16), vt)
                m_scr[mi] = m_new
            return carry

        lax.fori_loop(0, ntiles, body, 0)
        o = a_scr[0] / l_scr[0] - lam * (a_scr[1] / l_scr[1])
        o_ref[st * strip:(st + 1) * strip, :] = _subln(o, sw_ref[...]).astype(BF16)


def _attn_prompt(q, kb, vb, rel_bias, lamp, subln_w, nb, seq):
    tq, strip, tk = ATTN_TQ, ATTN_STRIP, ATTN_TK
    assert seq % tq == 0 and tq % strip == 0 and strip % tk == 0
    nq = seq // tq
    nd = ATTN_D_FAR - ATTN_D_MIN + 1
    rel = (jnp.arange(ATTN_D_MIN, ATTN_D_FAR + 1, dtype=jnp.int32)[:, None, None] * tk
           + jnp.arange(strip, dtype=jnp.int32)[None, :, None] - jnp.arange(tk, dtype=jnp.int32)[None, None, :])
    bias = _bias_tiles(rel.reshape(-1, tk), rel_bias).reshape(ATTN_HEADS, nd, strip, tk)
    return pl.pallas_call(
        functools.partial(_attn_kernel, tq=tq, strip=strip, tk=tk),
        grid=(nb, ATTN_HEADS, nq),
        in_specs=[pl.BlockSpec((tq, LANE), lambda b, h, i: (b * nq + i, h)),
                  pl.BlockSpec((seq, LANE), lambda b, h, i: (b, h)),
                  pl.BlockSpec((seq, LANE), lambda b, h, i: (b, h)),
                  pl.BlockSpec((None, nd, strip, tk), lambda b, h, i: (h, 0, 0, 0)),
                  pl.BlockSpec((4, ATTN_HEAD_DIM), lambda b, h, i: (0, 0)),
                  pl.BlockSpec((1, ATTN_V_DIM), lambda b, h, i: (0, 0))],
        out_specs=pl.BlockSpec((tq, LANE), lambda b, h, i: (b * nq + i, h)),
        out_shape=jax.ShapeDtypeStruct((nb * seq, ATTN_HEADS * ATTN_V_DIM), BF16),
        scratch_shapes=[pltpu.VMEM((2, strip, LANE), F32)] * 3,
        compiler_params=_params("arbitrary", "arbitrary", "arbitrary"),
        name="attn_prompt",
    )(q, kb, vb, bias, lamp, subln_w.reshape(1, ATTN_V_DIM))


DECODE_PAGES = 8


def _decode_kernel(pt_ref, q_ref, kn_ref, vn_ref, *rest, nq, pps):
    ck_refs, cv_refs = rest[:pps], rest[pps:2 * pps]
    bpage_ref, bnew_ref, lamp_ref, sw_ref, o_ref, qa_scr, m_scr, l_scr, a_scr = rest[2 * pps:]
    j = pl.program_id(1)
    nsteps = pl.num_programs(1)
    cols = PAGE_SIZE * ATTN_HEADS

    @pl.when(j == 0)
    def _():
        lane_lo = lax.broadcasted_iota(jnp.int32, (nq, LANE), 1) < ATTN_HEAD_DIM
        for h in range(ATTN_HEADS):
            qh = q_ref[:, h * LANE:(h + 1) * LANE].astype(F32)
            qa_scr[(2 * h) * nq:(2 * h + 1) * nq, :] = jnp.where(lane_lo, qh, 0.0)
            qa_scr[(2 * h + 1) * nq:(2 * h + 2) * nq, :] = jnp.where(lane_lo, 0.0, qh)
        nnew = nq * ATTN_HEADS
        zpad = jnp.zeros((LANE - nnew, LANE), F32)
        kn = jnp.concatenate([kn_ref[...], zpad], axis=0).astype(BF16)
        vn = jnp.concatenate([vn_ref[...], zpad], axis=0).astype(BF16)
        s = _dot_nt(qa_scr[...].astype(BF16), kn) + bnew_ref[...]
        m = jnp.max(s, axis=-1, keepdims=True)
        p = jnp.exp(s - m)
        m_scr[...] = m
        l_scr[...] = jnp.sum(p, axis=-1, keepdims=True)
        a_scr[...] = _dot(p.astype(BF16), vn)

    qa = qa_scr[...].astype(BF16)
    scores = []
    for i in range(pps):
        bias = bpage_ref[jnp.where(j == nsteps - 1, 1, 0)] if i == pps - 1 else bpage_ref[0]
        scores.append(_dot_nt(qa, ck_refs[i][...].reshape(cols, LANE).astype(BF16)) + bias)
    m_prev = m_scr[...]
    m_new = m_prev
    for s in scores:
        m_new = jnp.maximum(m_new, jnp.max(s, axis=-1, keepdims=True))
    alpha = jnp.exp(m_prev - m_new)
    l_new = alpha * l_scr[...]
    a_new = alpha * a_scr[...]
    for i, s in enumerate(scores):
        p = jnp.exp(s - m_new)
        l_new = l_new + jnp.sum(p, axis=-1, keepdims=True)
        a_new = a_new + _dot(p.astype(BF16), cv_refs[i][...].reshape(cols, LANE).astype(BF16))
    l_scr[...] = l_new
    a_scr[...] = a_new
    m_scr[...] = m_new

    @pl.when(j == nsteps - 1)
    def _():
        lam = _lam(lamp_ref)
        on = a_scr[...] / l_scr[...]
        for h in range(ATTN_HEADS):
            o = on[(2 * h) * nq:(2 * h + 1) * nq, :] - lam * on[(2 * h + 1) * nq:(2 * h + 2) * nq, :]
            o_ref[:, h * LANE:(h + 1) * LANE] = _subln(o, sw_ref[...])


def _attn_decode(q, k_new, v_new, cache_k, cache_v, page_table, rel_bias, lamp, subln_w):
    nb, nq, _ = q.shape
    npages = page_table.shape[1]
    past = npages * PAGE_SIZE
    rows = ATTN_HEADS * 2 * nq
    cols = PAGE_SIZE * ATTN_HEADS
    nnew = nq * ATTN_HEADS
    assert nnew <= LANE and npages >= 2
    qpos = past + jnp.arange(nq, dtype=jnp.int32)
    kpos = jnp.arange(past - 2 * PAGE_SIZE, past + PAGE_SIZE, dtype=jnp.int32)
    rel = jnp.where(kpos[None, :] < past + nq, qpos[:, None] - kpos[None, :], -1)
    b3 = _bias_tiles(jnp.pad(rel, ((0, -nq % 8), (0, 0))), rel_bias)[:, :nq].reshape(ATTN_HEADS, nq, 3, PAGE_SIZE)
    own = jnp.arange(ATTN_HEADS)[:, None] == jnp.arange(ATTN_HEADS)[None, :]
    full = jnp.where(own[:, None, None, None, :], b3[..., None], -jnp.inf)
    full = jnp.broadcast_to(full[:, None], (ATTN_HEADS, 2, nq, 3, PAGE_SIZE, ATTN_HEADS))
    full = full.transpose(3, 0, 1, 2, 4, 5).reshape(3, rows, cols)
    bias_pages = full[:2]
    bias_new = jnp.pad(full[2][:, :nnew], ((0, 0), (0, LANE - nnew)), constant_values=-jnp.inf)
    new_q = pl.BlockSpec((None, nq, ATTN_HEADS * LANE), lambda b, j, pt: (b, 0, 0))
    new_kv = pl.BlockSpec((None, nnew, LANE), lambda b, j, pt: (b, 0, 0))
    pps = math.gcd(DECODE_PAGES, npages)

    def page(i):
        return pl.BlockSpec((None, PAGE_SIZE, ATTN_HEADS, LANE), lambda b, j, pt: (pt[b, j * pps + i], 0, 0, 0))

    pages = [page(i) for i in range(pps)]
    grid_spec = pltpu.PrefetchScalarGridSpec(
        num_scalar_prefetch=1,
        grid=(nb, npages // pps),
        in_specs=[new_q, new_kv, new_kv, *pages, *pages,
                  pl.BlockSpec((2, rows, cols), lambda b, j, pt: (0, 0, 0)),
                  pl.BlockSpec((rows, LANE), lambda b, j, pt: (0, 0)),
                  pl.BlockSpec((4, ATTN_HEAD_DIM), lambda b, j, pt: (0, 0)),
                  pl.BlockSpec((1, ATTN_V_DIM), lambda b, j, pt: (0, 0))],
        out_specs=new_q,
        scratch_shapes=[pltpu.VMEM((rows, LANE), F32), pltpu.VMEM((rows, 1), F32),
                        pltpu.VMEM((rows, 1), F32), pltpu.VMEM((rows, LANE), F32)])
    return pl.pallas_call(
        functools.partial(_decode_kernel, nq=nq, pps=pps),
        grid_spec=grid_spec,
        out_shape=jax.ShapeDtypeStruct((nb, nq, ATTN_HEADS * ATTN_V_DIM), F32),
        compiler_params=_params("arbitrary", "arbitrary"),
        name="attn_decode",
    )(page_table, q, k_new, v_new, *([cache_k] * pps), *([cache_v] * pps), bias_pages, bias_new, lamp,
      subln_w.reshape(1, ATTN_V_DIM))


HALF = D_MODEL // 2
PACKED = jnp.int32


def _pack_rows(x):
    def rounded(v):
        bits = lax.bitcast_convert_type(v, jnp.uint32)
        return bits + (jnp.uint32(0x7FFF) + ((bits >> 16) & jnp.uint32(1)))
    word = (rounded(x[:, :HALF]) >> 16) | (rounded(x[:, HALF:]) & jnp.uint32(0xFFFF0000))
    return lax.bitcast_convert_type(word, PACKED)


def _unpack_rows(p):
    u = lax.bitcast_convert_type(p, jnp.uint32)
    lo = lax.bitcast_convert_type(u << 16, F32)
    hi = lax.bitcast_convert_type(u & jnp.uint32(0xFFFF0000), F32)
    return jnp.concatenate([lo, hi], axis=1)


def _layer_norm(r, g, b):
    mu = jnp.mean(r, axis=-1, keepdims=True)
    d = r - mu
    var = jnp.mean(d * d, axis=-1, keepdims=True)
    return d * lax.rsqrt(var + LN_EPS) * g + b


def _merge_kernel(yn_ref, on_ref, ga_ref, gb_ref, x_ref, g1_ref, sc2_ref, sh2_ref, wa_ref, wb_ref, wo_ref,
                  lg_ref, lb_ref, wr_ref, x1_ref, u2p_ref, logit_ref):
    ba = _dot(yn_ref[...].astype(BF16), wa_ref[...])
    bb = _dot(on_ref[...].astype(BF16), wb_ref[...])
    merged = jax.nn.sigmoid(ga_ref[...].astype(F32)) * ba + jax.nn.sigmoid(gb_ref[...].astype(F32)) * bb
    t = _dot(merged.astype(BF16), wo_ref[...])
    x1 = _layer_norm(ALPHA * x_ref[...] + g1_ref[...] * t, lg_ref[...], lb_ref[...])
    x1_ref[...] = x1
    u2 = x1 * (1.0 + sc2_ref[...]) + sh2_ref[...]
    u2p_ref[...] = _pack_rows(u2)
    u_hi = u2.astype(BF16)
    u_lo = (u2 - u_hi.astype(F32)).astype(BF16)
    by_hi = _dot_nt(wr_ref[...], u_hi)
    logit_ref[...] = by_hi[:N_EXPERTS] + by_hi[N_EXPERTS:] + _dot_nt(wr_ref[:N_EXPERTS, :], u_lo)


def _mod_spec(mod, t, tm, rows_per_mod):
    if mod.shape[1] != 1:
        return mod.reshape(t, D_MODEL), pl.BlockSpec((tm, D_MODEL), lambda i: (i, 0))
    assert rows_per_mod % tm == 0
    return mod, pl.BlockSpec((None, 1, D_MODEL), lambda i: (i // (rows_per_mod // tm), 0, 0))


def _merge(yn, on, ga, gb, x, g1, sc2, sh2, wa, wb, wo, ln_g, ln_b, w_router, rows_per_mod):
    t = x.shape[0]
    tm = min(256, t)
    rows = pl.BlockSpec((tm, D_MODEL), lambda i: (i, 0))
    wspec = pl.BlockSpec((D_MODEL, D_MODEL), lambda i: (0, 0))
    vec = pl.BlockSpec((1, D_MODEL), lambda i: (0, 0))
    g1a, mspec = _mod_spec(g1, t, tm, rows_per_mod)
    sc2a, _ = _mod_spec(sc2, t, tm, rows_per_mod)
    sh2a, _ = _mod_spec(sh2, t, tm, rows_per_mod)
    wr_t = w_router.T
    wr_hi = wr_t.astype(BF16)
    return pl.pallas_call(
        _merge_kernel,
        grid=(t // tm,),
        in_specs=[rows, rows, rows, rows, rows, mspec, mspec, mspec, wspec, wspec, wspec, vec, vec,
                  pl.BlockSpec((2 * N_EXPERTS, D_MODEL), lambda i: (0, 0))],
        out_specs=[rows, pl.BlockSpec((tm, HALF), lambda i: (i, 0)), pl.BlockSpec((N_EXPERTS, tm), lambda i: (0, i))],
        out_shape=[jax.ShapeDtypeStruct((t, D_MODEL), F32), jax.ShapeDtypeStruct((t, HALF), PACKED),
                   jax.ShapeDtypeStruct((N_EXPERTS, t), F32)],
        compiler_params=_params("arbitrary"),
        name="merge",
    )(yn, on, ga, gb, x, g1a, sc2a, sh2a, wa, wb, wo, ln_g.reshape(1, D_MODEL), ln_b.reshape(1, D_MODEL),
      jnp.concatenate([wr_hi, (wr_t - wr_hi.astype(F32)).astype(BF16)], axis=0))


def _router_kernel(logit_ref, br_ref, e_ref, w_ref, r_ref, cnt_ref, carry_scr, tri_scr, *, tr):
    i = pl.program_id(0)

    @pl.when(i == 0)
    def _():
        carry_scr[...] = jnp.zeros_like(carry_scr)
        a = lax.broadcasted_iota(jnp.int32, (tr, tr), 0)
        b = lax.broadcasted_iota(jnp.int32, (tr, tr), 1)
        tri_scr[...] = (a < b).astype(BF16)

    scores = jax.nn.sigmoid(logit_ref[...])
    biased = scores + br_ref[...]
    ninf = -jnp.inf

    b3 = biased.reshape(N_EXPERT_GROUPS, GROUP_SIZE, tr)
    j3 = lax.broadcasted_iota(jnp.int32, b3.shape, 1).astype(F32)
    top1 = jnp.max(b3, axis=1, keepdims=True)
    first = jnp.min(jnp.where(b3 == top1, j3, float(GROUP_SIZE)), axis=1, keepdims=True)
    top2 = jnp.max(jnp.where(j3 == first, ninf, b3), axis=1, keepdims=True)
    gscore = (top1 + top2).reshape(N_EXPERT_GROUPS, tr)
    gi = lax.broadcasted_iota(jnp.int32, gscore.shape, 0).astype(F32)
    gsel = jnp.zeros(gscore.shape, F32)
    for _ in range(TOPK_GROUPS):
        mx = jnp.max(gscore, axis=0, keepdims=True)
        pick = gi == jnp.min(jnp.where(gscore == mx, gi, float(N_EXPERT_GROUPS)), axis=0, keepdims=True)
        gsel = jnp.where(pick, 1.0, gsel)
        gscore = jnp.where(pick, ninf, gscore)
    emask = jnp.broadcast_to(gsel.reshape(N_EXPERT_GROUPS, 1, tr), b3.shape).reshape(N_EXPERTS, tr)
    masked = jnp.where(emask > 0.5, biased, ninf)

    ei = lax.broadcasted_iota(jnp.int32, masked.shape, 0).astype(F32)
    picked = jnp.zeros(masked.shape, F32)
    idxs, wsel = [], []
    for _ in range(TOP_K):
        mx = jnp.max(masked, axis=0, keepdims=True)
        idx = jnp.min(jnp.where(masked == mx, ei, float(N_EXPERTS)), axis=0, keepdims=True)
        pick = ei == idx
        idxs.append(idx)
        wsel.append(jnp.sum(jnp.where(pick, scores, 0.0), axis=0, keepdims=True))
        picked = jnp.where(pick, 1.0, picked)
        masked = jnp.where(pick, ninf, masked)
    wall = jnp.concatenate(wsel, axis=0)
    w_ref[...] = wall / jnp.sum(wall, axis=0, keepdims=True) * ROUTED_SCALE
    e_ref[...] = jnp.concatenate(idxs, axis=0).astype(jnp.int32)

    rank = carry_scr[:, 0:1] + _dot(picked.astype(BF16), tri_scr[...])
    r_ref[...] = jnp.concatenate(
        [jnp.sum(jnp.where(ei == idx, rank, 0.0), axis=0, keepdims=True) for idx in idxs], axis=0).astype(jnp.int32)
    total = carry_scr[...] + jnp.sum(picked, axis=1, keepdims=True)
    carry_scr[...] = total
    cnt_ref[...] = total.astype(jnp.int32)


def _router(logits, b_router):
    t = logits.shape[1]
    tr = min(512, t)
    tok = pl.BlockSpec((TOP_K, tr), lambda i: (0, i))
    e_t, w_t, r_t, cnt = pl.pallas_call(
        functools.partial(_router_kernel, tr=tr),
        grid=(t // tr,),
        in_specs=[pl.BlockSpec((N_EXPERTS, tr), lambda i: (0, i)),
                  pl.BlockSpec((N_EXPERTS, 1), lambda i: (0, 0))],
        out_specs=[tok, tok, tok, pl.BlockSpec((N_EXPERTS, LANE), lambda i: (0, 0))],
        out_shape=[jax.ShapeDtypeStruct((TOP_K, t), jnp.int32), jax.ShapeDtypeStruct((TOP_K, t), F32),
                   jax.ShapeDtypeStruct((TOP_K, t), jnp.int32), jax.ShapeDtypeStruct((N_EXPERTS, LANE), jnp.int32)],
        scratch_shapes=[pltpu.VMEM((N_EXPERTS, LANE), F32), pltpu.VMEM((tr, tr), BF16)],
        compiler_params=_params("arbitrary"),
        name="router",
    )(logits, b_router.reshape(N_EXPERTS, 1))
    return e_t, w_t, r_t, cnt


def _plan_kernel(e_ref, r_ref, cnt_ref, dest_ref, be_ref, bv_ref, nu_ref, *, br, nblk_pad):
    cnt = cnt_ref[...]
    shift = br.bit_length() - 1
    padded = lax.shift_left(lax.shift_right_logical(cnt + (br - 1), shift), shift)
    ea = lax.broadcasted_iota(jnp.int32, (N_EXPERTS, N_EXPERTS), 0)
    eb = lax.broadcasted_iota(jnp.int32, (N_EXPERTS, N_EXPERTS), 1)
    pends = jnp.dot((eb <= ea).astype(F32), padded.astype(F32), preferred_element_type=F32,
                    precision=HIGHEST).astype(jnp.int32)
    pstart = pends - padded

    e = e_ref[...]
    first_row = jnp.zeros(e.shape, jnp.int32)
    for x in range(N_EXPERTS):
        first_row = jnp.where(e == x, pstart[x:x + 1, 0:1], first_row)
    dest_ref[...] = first_row + r_ref[...]

    @pl.when(pl.program_id(0) == 0)
    def _():
        esub = lax.broadcasted_iota(jnp.int32, (N_EXPERTS, LANE), 0)
        real_end = (pstart + cnt).astype(F32)
        for c in range(nblk_pad // LANE):
            bstart = (lax.broadcasted_iota(jnp.int32, (1, LANE), 1) + c * LANE) * br
            be = jnp.minimum(jnp.sum((pends <= bstart).astype(F32), axis=0, keepdims=True),
                             float(N_EXPERTS - 1)).astype(jnp.int32)
            end = jnp.sum(jnp.where(esub == be, real_end, 0.0), axis=0, keepdims=True).astype(jnp.int32)
            be_ref[:, c * LANE:(c + 1) * LANE] = be
            bv_ref[:, c * LANE:(c + 1) * LANE] = jnp.clip(end - bstart, 0, br)
        nu_ref[...] = lax.shift_right_logical(pends[N_EXPERTS - 1:N_EXPERTS, :], shift)


def _plan(e_t, r_t, cnt, br, nblk):
    t = e_t.shape[1]
    tc = min(2048, t)
    nblk_pad = -(-nblk // LANE) * LANE
    tok = pl.BlockSpec((TOP_K, tc), lambda i: (0, i))
    blk = pl.BlockSpec((1, nblk_pad), lambda i: (0, 0))
    dest, be, bv, nu = pl.pallas_call(
        functools.partial(_plan_kernel, br=br, nblk_pad=nblk_pad),
        grid=(t // tc,),
        in_specs=[tok, tok, pl.BlockSpec((N_EXPERTS, LANE), lambda i: (0, 0))],
        out_specs=[tok, blk, blk, pl.BlockSpec((1, LANE), lambda i: (0, 0))],
        out_shape=[jax.ShapeDtypeStruct((TOP_K, t), jnp.int32), jax.ShapeDtypeStruct((1, nblk_pad), jnp.int32),
                   jax.ShapeDtypeStruct((1, nblk_pad), jnp.int32), jax.ShapeDtypeStruct((1, LANE), jnp.int32)],
        compiler_params=_params("arbitrary"),
        name="moe_plan",
    )(e_t, r_t, cnt)
    return dest, be[0, :nblk], bv[0, :nblk], nu[0, :1]


def _row_copy(src, s, dst, d, sem):
    return pltpu.make_async_copy(src.at[pl.ds(s, 1), :], dst.at[pl.ds(d, 1), :], sem)


def _dispatch_kernel(dest_ref, u_ref, xs_ref, sem, *, tm):
    def start(r, carry):
        for k in range(TOP_K):
            _row_copy(u_ref, r, xs_ref, dest_ref[k, r], sem).start()
        return carry

    def wait(r, carry):
        for k in range(TOP_K):
            _row_copy(u_ref, r, xs_ref, dest_ref[k, r], sem).wait()
        return carry

    lax.fori_loop(0, tm, start, 0)
    lax.fori_loop(0, tm, wait, 0)


def _dispatch(u2p, dest_t, n_rows):
    t = u2p.shape[0]
    tm = min(256, t)
    nt = t // tm
    dest_blocks = dest_t.reshape(TOP_K, nt, tm).transpose(1, 0, 2)
    return pl.pallas_call(
        functools.partial(_dispatch_kernel, tm=tm),
        grid=(nt,),
        in_specs=[pl.BlockSpec((None, TOP_K, tm), lambda i: (i, 0, 0), memory_space=pltpu.SMEM),
                  pl.BlockSpec((tm, HALF), lambda i: (i, 0))],
        out_specs=pl.BlockSpec(memory_space=pl.ANY),
        out_shape=jax.ShapeDtypeStruct((n_rows, HALF), PACKED),
        scratch_shapes=[pltpu.SemaphoreType.DMA(())],
        compiler_params=_params("arbitrary"),
        name="moe_dispatch",
    )(dest_blocks, u2p)


SC_CORES = 2
SC_SUBCORES = 16
SC_WINDOW = 128
SC_WORKERS = SC_CORES * SC_SUBCORES


def _sc_mesh():
    return plsc.VectorSubcoreMesh(core_axis_name="c", subcore_axis_name="s")


def _sc_worker():
    return lax.axis_index("s") * SC_CORES + lax.axis_index("c")


def _sc_scatter_rows(rows, dest_blocks, n_rows):
    t, width = rows.shape
    nchunks = t // SC_WINDOW // SC_WORKERS

    def body(rows_hbm, dest_hbm, out_hbm, idx_v, rows_v, sem):
        wid = _sc_worker()

        @pl.loop(0, nchunks)
        def _(c):
            chunk = wid * nchunks + c
            pltpu.sync_copy(dest_hbm.at[chunk], idx_v)
            pltpu.sync_copy(rows_hbm.at[pl.ds(pl.multiple_of(chunk * SC_WINDOW, SC_WINDOW), SC_WINDOW)], rows_v)
            copies = [pltpu.async_copy(rows_v, out_hbm.at[idx_v.at[k]], sem) for k in range(TOP_K)]
            for cp in copies:
                cp.wait()

    return pl.kernel(
        body, out_type=jax.ShapeDtypeStruct((n_rows, width), rows.dtype), mesh=_sc_mesh(),
        scratch_types=[pltpu.VMEM((TOP_K, SC_WINDOW), jnp.int32), pltpu.VMEM((SC_WINDOW, width), rows.dtype),
                       pltpu.SemaphoreType.DMA],
        name="moe_sc_dispatch")(rows, dest_blocks)


def _sc_gather_rows(table, idx):
    n = idx.shape[0]
    width = table.shape[1]
    nchunks = n // SC_WINDOW // SC_WORKERS

    def body(table_hbm, idx_hbm, out_hbm, idx_v, rows_v, sem):
        wid = _sc_worker()

        @pl.loop(0, nchunks)
        def _(c):
            base = pl.multiple_of((wid * nchunks + c) * SC_WINDOW, SC_WINDOW)
            pltpu.sync_copy(idx_hbm.at[pl.ds(base, SC_WINDOW)], idx_v)
            pltpu.async_copy(table_hbm.at[idx_v], rows_v, sem).wait()
            pltpu.sync_copy(rows_v, out_hbm.at[pl.ds(base, SC_WINDOW)])

    return pl.kernel(
        body, out_type=jax.ShapeDtypeStruct((n, width), table.dtype), mesh=_sc_mesh(),
        scratch_types=[pltpu.VMEM((SC_WINDOW,), jnp.int32), pltpu.VMEM((SC_WINDOW, width), table.dtype),
                       pltpu.SemaphoreType.DMA],
        name="moe_sc_gather")(table, idx)


def _expert_kernel(be_ref, bv_ref, nu_ref, x_ref, wg_ref, wu_ref, wd_ref, y_ref, wg_scr, wu_scr, wd_scr, *, br):
    i = pl.program_id(0)
    prev = be_ref[jnp.maximum(i - 1, 0)]

    @pl.when(jnp.logical_and(i < nu_ref[0], jnp.logical_or(i == 0, be_ref[i] != prev)))
    def _():
        wg_scr[...] = wg_ref[...].astype(BF16)
        wu_scr[...] = wu_ref[...].astype(BF16)
        wd_scr[...] = wd_ref[...].astype(BF16)

    @pl.when(i < nu_ref[0])
    def _():
        live = lax.broadcasted_iota(jnp.int32, (br, HALF), 0) < bv_ref[i]
        x = _unpack_rows(jnp.where(live, x_ref[...], 0)).astype(BF16)
        hcat = _silu(_dot(x, wg_scr[...])) * _dot(x, wu_scr[...])
        y_ref[...] = _pack_rows(_dot(hcat.astype(BF16), wd_scr[...]))


def _experts(x_sorted, blk_e, blk_valid, n_used, w_gate, w_up, w_down, br):
    n_rows = x_sorted.shape[0]
    nblk = n_rows // br

    def blk(i, be, bv, nu):
        return (jnp.minimum(i, nu[0] - 1), 0)

    grid_spec = pltpu.PrefetchScalarGridSpec(
        num_scalar_prefetch=3,
        grid=(nblk,),
        in_specs=[pl.BlockSpec((br, HALF), blk),
                  pl.BlockSpec((None, D_MODEL, D_EXPERT), lambda i, be, bv, nu: (be[i], 0, 0)),
                  pl.BlockSpec((None, D_MODEL, D_EXPERT), lambda i, be, bv, nu: (be[i], 0, 0)),
                  pl.BlockSpec((None, D_EXPERT, D_MODEL), lambda i, be, bv, nu: (be[i], 0, 0))],
        out_specs=pl.BlockSpec((br, HALF), blk),
        scratch_shapes=[pltpu.VMEM((D_MODEL, D_EXPERT), BF16), pltpu.VMEM((D_MODEL, D_EXPERT), BF16),
                        pltpu.VMEM((D_EXPERT, D_MODEL), BF16)])
    return pl.pallas_call(
        functools.partial(_expert_kernel, br=br),
        grid_spec=grid_spec,
        out_shape=jax.ShapeDtypeStruct((n_rows, HALF), PACKED),
        compiler_params=_params("arbitrary"),
        name="moe_experts",
    )(blk_e, blk_valid, n_used, x_sorted, w_gate, w_up, w_down)


def _combine_kernel(dest_ref, ys_ref, w_ref, u_ref, x1_ref, g2_ref, sg_ref, su_ref, sd_ref, lg_ref, lb_ref,
                    o_ref, g_scr, sem, *, tm):
    def start(r, carry):
        for k in range(TOP_K):
            _row_copy(ys_ref, dest_ref[k, r], g_scr.at[k], r, sem).start()
        return carry

    def wait(r, carry):
        for k in range(TOP_K):
            _row_copy(ys_ref, dest_ref[k, r], g_scr.at[k], r, sem).wait()
        return carry

    lax.fori_loop(0, tm, start, 0)
    f = _shared_ffn(u_ref, sg_ref, su_ref, sd_ref)
    lax.fori_loop(0, tm, wait, 0)
    f = _add_routed(f, g_scr, w_ref[...])
    o_ref[...] = _layer_norm(ALPHA * x1_ref[...] + g2_ref[...] * f, lg_ref[...], lb_ref[...])


def _shared_ffn(u_ref, sg_ref, su_ref, sd_ref):
    ub = _unpack_rows(u_ref[...]).astype(BF16)
    hs = _silu(_dot(ub, sg_ref[...])) * _dot(ub, su_ref[...])
    return _dot(hs.astype(BF16), sd_ref[...])


def _add_routed(f, slots_ref, w):
    for k in range(TOP_K):
        f = f + _unpack_rows(slots_ref[k]) * w[:, k:k + 1]
    return f


def _combine_gathered_kernel(g_ref, w_ref, u_ref, x1_ref, g2_ref, sg_ref, su_ref, sd_ref, lg_ref, lb_ref, o_ref):
    f = _add_routed(_shared_ffn(u_ref, sg_ref, su_ref, sd_ref), g_ref, w_ref[...])
    o_ref[...] = _layer_norm(ALPHA * x1_ref[...] + g2_ref[...] * f, lg_ref[...], lb_ref[...])


def _combine(y_rows, dest_t, w_t, u2p, x1, g2, ws_gate, ws_up, ws_down, ln_g, ln_b, rows_per_mod, gathered):
    t = u2p.shape[0]
    tm = min(256 if gathered else 128, t)
    nt = t // tm
    rows = pl.BlockSpec((tm, D_MODEL), lambda i: (i, 0))
    vec = pl.BlockSpec((1, D_MODEL), lambda i: (0, 0))
    g2a, mspec = _mod_spec(g2, t, tm, rows_per_mod)
    common_specs = [pl.BlockSpec((tm, TOP_K), lambda i: (i, 0)),
                    pl.BlockSpec((tm, HALF), lambda i: (i, 0)), rows, mspec,
                    pl.BlockSpec((D_MODEL, D_SHARED), lambda i: (0, 0)),
                    pl.BlockSpec((D_MODEL, D_SHARED), lambda i: (0, 0)),
                    pl.BlockSpec((D_SHARED, D_MODEL), lambda i: (0, 0)),
                    vec, vec]
    common_args = (w_t.T, u2p, x1, g2a, ws_gate.astype(BF16), ws_up.astype(BF16), ws_down.astype(BF16),
                   ln_g.reshape(1, D_MODEL), ln_b.reshape(1, D_MODEL))
    if gathered:
        return pl.pallas_call(
            _combine_gathered_kernel,
            grid=(nt,),
            in_specs=[pl.BlockSpec((TOP_K, tm, HALF), lambda i: (0, i, 0))] + common_specs,
            out_specs=rows,
            out_shape=jax.ShapeDtypeStruct((t, D_MODEL), F32),
            compiler_params=_params("arbitrary"),
            name="moe_combine_gathered",
        )(y_rows.reshape(TOP_K, t, HALF), *common_args)
    dest_blocks = dest_t.reshape(TOP_K, nt, tm).transpose(1, 0, 2)
    return pl.pallas_call(
        functools.partial(_combine_kernel, tm=tm),
        grid=(nt,),
        in_specs=[pl.BlockSpec((None, TOP_K, tm), lambda i: (i, 0, 0), memory_space=pltpu.SMEM),
                  pl.BlockSpec(memory_space=pl.ANY)] + common_specs,
        out_specs=rows,
        out_shape=jax.ShapeDtypeStruct((t, D_MODEL), F32),
        scratch_shapes=[pltpu.VMEM((TOP_K, tm, HALF), PACKED), pltpu.SemaphoreType.DMA(())],
        compiler_params=_params("arbitrary"),
        name="moe_combine",
    )(dest_blocks, y_rows, *common_args)


def _moe(u2p, logits, x1, g2, p, rows_per_mod, br):
    t = u2p.shape[0]
    e_t, w_t, r_t, counts = _router(logits, p["b_router"])
    n_rows = (t * TOP_K + N_EXPERTS * (br - 1) + br - 1) // br * br
    dest_t, blk_e, blk_valid, n_used = _plan(e_t, r_t, counts, br, n_rows // br)
    on_sc = t % (SC_WORKERS * SC_WINDOW) == 0
    if on_sc:
        dest_blocks = dest_t.reshape(TOP_K, t // SC_WINDOW, SC_WINDOW).transpose(1, 0, 2)
        x_sorted = _sc_scatter_rows(u2p, dest_blocks, n_rows)
    else:
        x_sorted = _dispatch(u2p, dest_t, n_rows)
    y_rows = _experts(x_sorted, blk_e, blk_valid, n_used, p["w_gate"], p["w_up"], p["w_down"], br)
    if on_sc:
        y_rows = _sc_gather_rows(y_rows, dest_t.reshape(TOP_K * t))
    return _combine(y_rows, dest_t, w_t, u2p, x1, g2, p["ws_gate"], p["ws_up"], p["ws_down"],
                    p["ln2_g"], p["ln2_b"], rows_per_mod, on_sc)


def _cat_w_in(w_in):
    sizes = (SSD_INNER, CONV_DIM, SSD_HEADS, 1024, 1024, 1024, D_MODEL, D_MODEL)
    offs = [0]
    for s in sizes:
        offs.append(offs[-1] + s)
    z, xbc, dt, q, k, v, ga, gb = [w_in[:, offs[i]:offs[i + 1]] for i in range(8)]
    dt = jnp.pad(dt, ((0, 0), (0, SEG_END - SEG_DT - SSD_HEADS)))
    return jnp.concatenate([z, xbc, q, k, v, ga, gb, dt], axis=1).astype(BF16)


def kernel(x_prompt, x_sample, c_prompt, c_sample, cache_k, cache_v, page_table, state_conv, state_ssm, rel_bias, w_ada, b_ada, w_in, conv_w, conv_b, dt_bias, a_log, d_skip, ssd_norm_w, lam_q1, lam_k1, lam_q2, lam_k2, subln_w, w_br_ssd, w_br_attn, w_out, ln1_g, ln1_b, w_router, b_router, w_gate, w_up, w_down, ws_gate, ws_up, ws_down, ln2_g, ln2_b):
    assert w_in.shape[0] == DEPTH
    nbp, seq, _ = x_prompt.shape
    nbs, dseq, _ = x_sample.shape
    tp, ts = nbp * seq, nbs * dseq
    past = page_table.shape[1] * PAGE_SIZE
    cl = SSD_CHUNK

    pad_p = -nbp % 8
    c_all = jnp.concatenate([c_prompt, jnp.zeros((pad_p, D_MODEL), F32), jnp.repeat(c_sample, dseq, axis=0)], axis=0)
    mod = _adaln(c_all, w_ada[0], b_ada[0])
    mod_p = mod[:nbp].reshape(nbp, 1, 6, D_MODEL)
    mod_s = mod[nbp + pad_p:].reshape(nbs, dseq, 6, D_MODEL)
    mp = [mod_p[:, :, i] for i in range(6)]
    ms = [mod_s[:, :, i] for i in range(6)]

    w_cat = _cat_w_in(w_in[0])
    wa, wb, wo = w_br_ssd[0].astype(BF16), w_br_attn[0].astype(BF16), w_out[0].astype(BF16)
    lamp = jnp.stack([lam_q1[0], lam_k1[0], lam_q2[0], lam_k2[0]])
    moe_p = dict(b_router=b_router[0], w_gate=w_gate[0], w_up=w_up[0], w_down=w_down[0],
                 ws_gate=ws_gate[0], ws_up=ws_up[0], ws_down=ws_down[0], ln2_g=ln2_g[0], ln2_b=ln2_b[0])
    ssd_w = (conv_w[0], conv_b[0], dt_bias[0], a_log[0], d_skip[0], ssd_norm_w[0])

    xp = x_prompt.reshape(tp, D_MODEL)
    z, xbc, dt, q, k, v, kb, vb, ga, gb = _inproj(xp, mp[1], mp[0], w_cat, seq)
    yn, conv_p, h_p = _ssd(xbc, dt, z, *ssd_w, jnp.zeros((nbp, CONV_W - 1, CONV_DIM), F32),
                           jnp.zeros((nbp, SSD_HEADS, SSD_HEADDIM, D_STATE), F32), nbp, seq // cl, cl)
    on = _attn_prompt(q, kb, vb, rel_bias, lamp, subln_w[0], nbp, seq)
    x1, u2p, logits = _merge(yn, on, ga, gb, xp, mp[2], mp[4], mp[3], wa, wb, wo, ln1_g[0], ln1_b[0], w_router[0], seq)
    y_prompt = _moe(u2p, logits, x1, mp[5], moe_p, seq, 256).reshape(nbp, seq, D_MODEL)
    k_prompt = k.reshape(1, nbp, seq, ATTN_HEADS, 2 * ATTN_HEAD_DIM)
    v_prompt = v.reshape(1, nbp, seq, ATTN_HEADS, ATTN_V_DIM)

    xs_ = x_sample.reshape(ts, D_MODEL)
    z, xbc, dt, q, k, v, kb, vb, ga, gb = _inproj(xs_, ms[1], ms[0], w_cat, dseq)

    def chunk_pad(a):
        a = a.reshape(nbs, dseq, a.shape[-1])
        return jnp.pad(a, ((0, 0), (0, cl - dseq), (0, 0))).reshape(nbs * cl, a.shape[-1])

    yn, conv_s, h_s = _ssd(chunk_pad(xbc), chunk_pad(dt), chunk_pad(z), *ssd_w, state_conv[0], state_ssm[0],
                           nbs, 1, dseq)
    yn = yn.reshape(nbs, cl, SSD_INNER)[:, :dseq].reshape(ts, SSD_INNER)
    on = _attn_decode(q.reshape(nbs, dseq, -1), k.reshape(nbs, dseq * ATTN_HEADS, LANE),
                      v.reshape(nbs, dseq * ATTN_HEADS, LANE), cache_k[0], cache_v[0], page_table,
                      rel_bias, lamp, subln_w[0])
    x1, u2p, logits = _merge(yn, on.reshape(ts, -1), ga, gb, xs_, ms[2], ms[4], ms[3], wa, wb, wo, ln1_g[0], ln1_b[0],
                             w_router[0], dseq)
    y_sample = _moe(u2p, logits, x1, ms[5], moe_p, dseq, 64).reshape(nbs, dseq, D_MODEL)
    k_sample = k.reshape(1, nbs, dseq, ATTN_HEADS, 2 * ATTN_HEAD_DIM)
    v_sample = v.reshape(1, nbs, dseq, ATTN_HEADS, ATTN_V_DIM)

    return (y_prompt, y_sample, k_prompt, v_prompt, conv_p[None], h_p.reshape(1, nbp, SSD_HEADS, SSD_HEADDIM, D_STATE),
            k_sample, v_sample, conv_s[None], h_s.reshape(1, nbs, SSD_HEADS, SSD_HEADDIM, D_STATE))
```

```python
import functools
import math

import jax
import jax.numpy as jnp
from jax import lax
from jax.experimental import pallas as pl
from jax.experimental.pallas import tpu as pltpu
from jax.experimental.pallas import tpu_sc as plsc

F32 = jnp.float32
BF16 = jnp.bfloat16
HIGHEST = lax.Precision.HIGHEST

D_MODEL = 1024
SSD_INNER = 1024
SSD_HEADDIM = 64
SSD_HEADS = 16
SSD_GROUPS = 2
D_STATE = 128
CONV_W = 4
CONV_DIM = SSD_INNER + 2 * SSD_GROUPS * D_STATE
SSD_CHUNK = 128
ATTN_HEADS = 8
ATTN_HEAD_DIM = 64
ATTN_V_DIM = 128
N_BUCKETS = 32
MAX_DISTANCE = 128
N_EXPERTS = 64
N_EXPERT_GROUPS = 8
GROUP_SIZE = N_EXPERTS // N_EXPERT_GROUPS
TOPK_GROUPS = 4
TOP_K = 8
D_EXPERT = 256
D_SHARED = 256
ROUTED_SCALE = 2.5
PAGE_SIZE = 128
DEPTH = 1
ALPHA = (2 * DEPTH) ** 0.25
LN_EPS = 1e-5
RMS_EPS = 1e-5
LAM_INIT = 0.8 - 0.6 * math.exp(-0.3 * 0)
LOG2E = math.log2(math.e)
LANE = 128
VMEM_LIMIT = 56 * 1024 * 1024

SEG_Z, SEG_XBC, SEG_Q, SEG_K, SEG_V, SEG_GA, SEG_GB, SEG_DT, SEG_END = (
    0, 1024, 2560, 3584, 4608, 5632, 6656, 7680, 7808)


def _silu(x):
    return x * jax.nn.sigmoid(x)


def _dot(a, b):
    return jnp.dot(a, b, preferred_element_type=F32)


def _dot_nt(a, b):
    return lax.dot_general(a, b, (((1,), (1,)), ((), ())), preferred_element_type=F32)


def _dot_tn(a, b):
    return lax.dot_general(a, b, (((0,), (0,)), ((), ())), preferred_element_type=F32)


def _params(*sem):
    return pltpu.CompilerParams(dimension_semantics=sem, vmem_limit_bytes=VMEM_LIMIT)


def _adaln_kernel(c_ref, w_ref, b_ref, o_ref):
    s = _silu(c_ref[...]).astype(BF16)
    o_ref[...] = _dot(s, w_ref[...].astype(BF16)) + b_ref[...]


def _adaln(c, w_ada, b_ada):
    r = c.shape[0]
    n = w_ada.shape[1]
    tn = 1024
    return pl.pallas_call(
        _adaln_kernel,
        grid=(n // tn,),
        in_specs=[pl.BlockSpec((r, D_MODEL), lambda j: (0, 0)),
                  pl.BlockSpec((D_MODEL, tn), lambda j: (0, j)),
                  pl.BlockSpec((1, tn), lambda j: (0, j))],
        out_specs=pl.BlockSpec((r, tn), lambda j: (0, j)),
        out_shape=jax.ShapeDtypeStruct((r, n), F32),
        compiler_params=_params("arbitrary"),
        name="adaln",
    )(c, w_ada, b_ada.reshape(1, n))


def _inproj_kernel(x_ref, sc_ref, sh_ref, w_ref, z_ref, xbc_ref, dt_ref, q_ref, k_ref, v_ref,
                   kb_ref, vb_ref, ga_ref, gb_ref):
    u = (x_ref[...] * (1.0 + sc_ref[...]) + sh_ref[...]).astype(BF16)

    def seg(a, b):
        return _dot(u, w_ref[:, a:b])

    z_ref[...] = seg(SEG_Z, SEG_XBC).astype(BF16)
    xbc_ref[...] = seg(SEG_XBC, SEG_Q)
    q_ref[...] = (seg(SEG_Q, SEG_K) * (ATTN_HEAD_DIM ** -0.5 * LOG2E)).astype(BF16)
    kk = seg(SEG_K, SEG_V)
    k_ref[...] = kk
    kb_ref[...] = kk.astype(BF16)
    vv = seg(SEG_V, SEG_GA)
    v_ref[...] = vv
    vb_ref[...] = vv.astype(BF16)
    ga_ref[...] = seg(SEG_GA, SEG_GB).astype(BF16)
    gb_ref[...] = seg(SEG_GB, SEG_DT).astype(BF16)
    dt_ref[...] = seg(SEG_DT, SEG_END)


def _inproj(x, sc, sh, w_cat, rows_per_mod):
    t = x.shape[0]
    tm = min(256, t)
    per_row = sc.shape[1] != 1
    if per_row:
        sc2 = sc.reshape(t, D_MODEL)
        sh2 = sh.reshape(t, D_MODEL)
        mod_spec = pl.BlockSpec((tm, D_MODEL), lambda i: (i, 0))
    else:
        assert rows_per_mod % tm == 0
        sc2, sh2 = sc, sh
        mod_spec = pl.BlockSpec((None, 1, D_MODEL), lambda i: (i // (rows_per_mod // tm), 0, 0))

    def rows(width):
        return pl.BlockSpec((tm, width), lambda i: (i, 0))

    widths = (1024, CONV_DIM, LANE, 1024, 1024, 1024, 1024, 1024, 1024, 1024)
    dtypes = (BF16, F32, F32, BF16, F32, F32, BF16, BF16, BF16, BF16)
    return pl.pallas_call(
        _inproj_kernel,
        grid=(t // tm,),
        in_specs=[rows(D_MODEL), mod_spec, mod_spec,
                  pl.BlockSpec((D_MODEL, SEG_END), lambda i: (0, 0), pipeline_mode=pl.Buffered(1))],
        out_specs=[rows(w) for w in widths],
        out_shape=[jax.ShapeDtypeStruct((t, w), d) for w, d in zip(widths, dtypes)],
        compiler_params=_params("arbitrary"),
        name="inproj",
    )(x, sc2, sh2, w_cat)


def _ssd_kernel(xbc_ref, dt_ref, z_ref, cw_ref, cb_ref, dtb_ref, alog_ref, dsk_ref, nw_ref,
                cbuf_ref, h0_ref, yn_ref, cnew_ref, hnew_ref, xp_scr, h_scr, y_scr, *, valid_len):
    c = pl.program_id(1)
    nc = pl.num_programs(1)
    cl = SSD_CHUNK
    head = 8

    @pl.when(c == 0)
    def _():
        xp_scr[head - (CONV_W - 1):head, :] = cbuf_ref[...]
        h_scr[...] = h0_ref[...]

    @pl.when(c > 0)
    def _():
        xp_scr[head - (CONV_W - 1):head, :] = xp_scr[head + cl - (CONV_W - 1):head + cl, :]

    xp_scr[head:head + cl, :] = xbc_ref[...]

    acc = xp_scr[head:head + cl, :] * cw_ref[CONV_W - 1:CONV_W, :]
    for j in range(CONV_W - 1):
        lo = head - (CONV_W - 1) + j
        acc = acc + xp_scr[lo:lo + cl, :] * cw_ref[j:j + 1, :]
    xc = _silu(acc + cb_ref[...])
    xs = xc[:, :SSD_INNER]
    bmat = [xc[:, SSD_INNER + g * D_STATE:SSD_INNER + (g + 1) * D_STATE].astype(BF16)
            for g in range(SSD_GROUPS)]
    coff = SSD_INNER + SSD_GROUPS * D_STATE
    cmat = [xc[:, coff + g * D_STATE:coff + (g + 1) * D_STATE].astype(BF16) for g in range(SSD_GROUPS)]

    li = lax.broadcasted_iota(jnp.int32, (cl, cl), 0)
    si = lax.broadcasted_iota(jnp.int32, (cl, cl), 1)
    causal = li >= si
    lane_lo = si < SSD_HEADDIM
    sub_lo = li < SSD_HEADDIM

    dpre = dt_ref[...] + dtb_ref[...]
    dtv = jnp.maximum(dpre, 0.0) + jnp.log(1.0 + jnp.exp(-jnp.abs(dpre)))
    if valid_len < cl:
        dtv = jnp.where(li < valid_len, dtv, 0.0)
    da = dtv * (-jnp.exp(alog_ref[...]))
    tril = causal.astype(F32)
    acum = jnp.dot(tril, da, preferred_element_type=F32, precision=HIGHEST)
    acum_t = acum.T
    dt_t = dtv.T
    last = acum[cl - 1:cl, :]

    cb = [_dot_nt(cmat[g], bmat[g]) for g in range(SSD_GROUPS)]

    def colb(a, h):
        return jnp.broadcast_to(a[:, h:h + 1], (cl, cl))

    pairs = SSD_HEADS // 2
    for p in range(pairs):
        h0, h1 = 2 * p, 2 * p + 1
        g = h0 // (SSD_HEADS // SSD_GROUPS)
        xpair = xs[:, p * LANE:(p + 1) * LANE]
        x_lo = jnp.where(lane_lo, xpair, 0.0).astype(BF16)
        x_hi = jnp.where(lane_lo, 0.0, xpair).astype(BF16)
        ydiag = None
        for hh, xm in ((h0, x_lo), (h1, x_hi)):
            seg = colb(acum, hh) - acum_t[hh:hh + 1, :]
            dec = jnp.exp(jnp.where(causal, seg, -jnp.inf))
            m = (cb[g] * dec * dt_t[hh:hh + 1, :]).astype(BF16)
            part = _dot(m, xm)
            ydiag = part if ydiag is None else ydiag + part
        col0, col1 = colb(acum, h0), colb(acum, h1)
        ecol = jnp.where(lane_lo, jnp.exp(col0), jnp.exp(col1))
        hp = h_scr[p]
        yoff = _dot_nt(cmat[g], hp.astype(BF16)) * ecol
        l0, l1 = last[:, h0:h0 + 1], last[:, h1:h1 + 1]
        wcol = jnp.where(lane_lo, jnp.exp(l0 - col0) * colb(dtv, h0), jnp.exp(l1 - col1) * colb(dtv, h1))
        dx = (xpair * wcol).astype(BF16)
        st = _dot_tn(dx, bmat[g])
        hdec = jnp.where(sub_lo, jnp.exp(l0), jnp.exp(l1))
        h_scr[p] = hp * hdec + st
        y_scr[:, p * LANE:(p + 1) * LANE] = ydiag + yoff + dsk_ref[:, p * LANE:(p + 1) * LANE] * xpair

    zf = z_ref[...].astype(F32)
    gated = y_scr[...] * _silu(zf)
    gw = SSD_INNER // SSD_GROUPS
    for g in range(SSD_GROUPS):
        sg = gated[:, g * gw:(g + 1) * gw]
        ms = jnp.mean(sg * sg, axis=-1, keepdims=True)
        yn_ref[:, g * gw:(g + 1) * gw] = (sg * lax.rsqrt(ms + RMS_EPS) * nw_ref[:, g * gw:(g + 1) * gw]).astype(BF16)

    @pl.when(c == nc - 1)
    def _():
        cnew_ref[...] = xp_scr[head + valid_len - (CONV_W - 1):head + valid_len, :]
        hnew_ref[...] = h_scr[...]


def _ssd(xbc, dt, z, conv_w, conv_b, dt_bias, a_log, d_skip, norm_w, cbuf, h0, nb, nc, valid_len):
    cl = SSD_CHUNK
    pairs = SSD_HEADS // 2
    pad = LANE - SSD_HEADS
    row = lambda b, c: (b * nc + c, 0)
    const = lambda b, c: (0, 0)
    return pl.pallas_call(
        functools.partial(_ssd_kernel, valid_len=valid_len),
        grid=(nb, nc),
        in_specs=[pl.BlockSpec((cl, CONV_DIM), row), pl.BlockSpec((cl, LANE), row),
                  pl.BlockSpec((cl, SSD_INNER), row),
                  pl.BlockSpec((CONV_W, CONV_DIM), const), pl.BlockSpec((1, CONV_DIM), const),
                  pl.BlockSpec((1, LANE), const), pl.BlockSpec((1, LANE), const),
                  pl.BlockSpec((1, SSD_INNER), const), pl.BlockSpec((1, SSD_INNER), const),
                  pl.BlockSpec((None, CONV_W - 1, CONV_DIM), lambda b, c: (b, 0, 0)),
                  pl.BlockSpec((None, pairs, LANE, D_STATE), lambda b, c: (b, 0, 0, 0))],
        out_specs=[pl.BlockSpec((cl, SSD_INNER), row),
                   pl.BlockSpec((None, CONV_W - 1, CONV_DIM), lambda b, c: (b, 0, 0)),
                   pl.BlockSpec((None, pairs, LANE, D_STATE), lambda b, c: (b, 0, 0, 0))],
        out_shape=[jax.ShapeDtypeStruct((nb * nc * cl, SSD_INNER), BF16),
                   jax.ShapeDtypeStruct((nb, CONV_W - 1, CONV_DIM), F32),
                   jax.ShapeDtypeStruct((nb, pairs, LANE, D_STATE), F32)],
        scratch_shapes=[pltpu.VMEM((8 + cl, CONV_DIM), F32), pltpu.VMEM((pairs, LANE, D_STATE), F32),
                        pltpu.VMEM((cl, SSD_INNER), F32)],
        compiler_params=_params("arbitrary", "arbitrary"),
        name="ssd",
    )(xbc, dt, z, conv_w, conv_b.reshape(1, CONV_DIM),
      jnp.pad(dt_bias, (0, pad)).reshape(1, LANE), jnp.pad(a_log, (0, pad)).reshape(1, LANE),
      jnp.repeat(d_skip, SSD_HEADDIM).reshape(1, SSD_INNER), norm_w.reshape(1, SSD_INNER),
      cbuf, h0.reshape(nb, pairs, LANE, D_STATE))


def _bias_kernel(rel_ref, tab_ref, o_ref, *, shift_far):
    rel = rel_ref[...]
    n = jnp.maximum(rel, 0)
    max_exact = N_BUCKETS // 2
    large = max_exact + (jnp.log(jnp.maximum(n, 1).astype(F32) / max_exact)
                         / math.log(MAX_DISTANCE / max_exact) * (N_BUCKETS - max_exact)).astype(jnp.int32)
    bucket = jnp.where(n < max_exact, n, jnp.minimum(large, N_BUCKETS - 1))
    outs = [jnp.zeros(rel.shape, F32) for _ in range(ATTN_HEADS)]
    for kb in range(N_BUCKETS):
        hit = bucket == kb
        for h in range(ATTN_HEADS):
            far = tab_ref[N_BUCKETS - 1, h] if shift_far else 0.0
            outs[h] = jnp.where(hit, (tab_ref[kb, h] - far) * LOG2E, outs[h])
    for h in range(ATTN_HEADS):
        o_ref[h] = jnp.where(rel >= 0, outs[h], -jnp.inf)


def _bias_tiles(rel, rel_bias, shift_far):
    r, c = rel.shape
    tr = min(r, 64)
    return pl.pallas_call(
        functools.partial(_bias_kernel, shift_far=shift_far),
        grid=(r // tr,),
        in_specs=[pl.BlockSpec((tr, c), lambda i: (i, 0)),
                  pl.BlockSpec(memory_space=pltpu.SMEM)],
        out_specs=pl.BlockSpec((ATTN_HEADS, tr, c), lambda i: (0, i, 0)),
        out_shape=jax.ShapeDtypeStruct((ATTN_HEADS, r, c), F32),
        compiler_params=_params("arbitrary"),
        name="t5bias",
    )(rel, rel_bias)


def _lam(lamp_ref):
    lp = lamp_ref[...]
    e1 = jnp.exp(jnp.sum(lp[0:1, :] * lp[1:2, :], axis=-1, keepdims=True))
    e2 = jnp.exp(jnp.sum(lp[2:3, :] * lp[3:4, :], axis=-1, keepdims=True))
    return e1 - e2 + LAM_INIT


def _subln(o, sw):
    ms = jnp.mean(o * o, axis=-1, keepdims=True)
    return o * lax.rsqrt(ms + RMS_EPS) * sw * (1.0 - LAM_INIT)


ATTN_TQ = 512
ATTN_STRIP = 512
ATTN_TK = 512
ATTN_D_MIN = 1 - ATTN_STRIP // ATTN_TK
ATTN_D_FAR = -(-(MAX_DISTANCE + ATTN_TK - 1) // ATTN_TK)


def _attn_kernel(q_ref, k_ref, v_ref, bias_ref, lamp_ref, sw_ref, o_ref, m_scr, l_scr, a_scr, *, tq, strip, tk):
    qi = pl.program_id(2)
    lane_lo = lax.broadcasted_iota(jnp.int32, (strip, LANE), 1) < ATTN_HEAD_DIM
    reps = tk // LANE
    lam = _lam(lamp_ref)
    for st in range(tq // strip):
        q = q_ref[st * strip:(st + 1) * strip, :]
        zero = jnp.zeros_like(q)
        qm = (jnp.where(lane_lo, q, zero), jnp.where(lane_lo, zero, q))
        m_scr[...] = jnp.full(m_scr.shape, -jnp.inf, F32)
        l_scr[...] = jnp.zeros(l_scr.shape, F32)
        a_scr[...] = jnp.zeros(a_scr.shape, F32)
        row0 = qi * (tq // strip) + st
        ntiles = (row0 * strip + strip - 1) // tk + 1

        def tile(j, near):
            off = pl.multiple_of(j * tk, tk)
            kt = k_ref[pl.ds(off, tk), :]
            vt = v_ref[pl.ds(off, tk), :]
            for mi in range(2):
                sc = _dot_nt(qm[mi], kt)
                if near:
                    sc = sc + bias_ref[row0 * (strip // tk) - j - ATTN_D_MIN]
                m_prev = m_scr[mi]
                m_new = jnp.maximum(m_prev, jnp.max(sc, axis=-1, keepdims=True))
                p = jnp.exp2(sc - jnp.concatenate([m_new] * reps, axis=1))
                alpha = jnp.exp2(m_prev - m_new)
                l_scr[mi] = alpha * l_scr[mi] + jnp.sum(p, axis=-1, keepdims=True)
                a_scr[mi] = alpha * a_scr[mi] + _dot(p.astype(BF16), vt)
                m_scr[mi] = m_new

        nfar = jnp.maximum(row0 * (strip // tk) - ATTN_D_FAR + 1, 0)
        lax.fori_loop(0, nfar, lambda j, c: (tile(j, False), c)[1], 0)
        lax.fori_loop(nfar, ntiles, lambda j, c: (tile(j, True), c)[1], 0)
        o = a_scr[0] / l_scr[0] - lam * (a_scr[1] / l_scr[1])
        o_ref[st * strip:(st + 1) * strip, :] = _subln(o, sw_ref[...]).astype(BF16)


def _attn_prompt(q, kb, vb, rel_bias, lamp, subln_w, nb, seq):
    tq, strip, tk = ATTN_TQ, ATTN_STRIP, ATTN_TK
    assert seq % tq == 0 and tq % strip == 0 and strip % tk == 0
    nq = seq // tq
    nd = ATTN_D_FAR - ATTN_D_MIN
    rel = (jnp.arange(ATTN_D_MIN, ATTN_D_FAR, dtype=jnp.int32)[:, None, None] * tk
           + jnp.arange(strip, dtype=jnp.int32)[None, :, None] - jnp.arange(tk, dtype=jnp.int32)[None, None, :])
    bias = _bias_tiles(rel.reshape(-1, tk), rel_bias, True).reshape(ATTN_HEADS, nd, strip, tk)
    return pl.pallas_call(
        functools.partial(_attn_kernel, tq=tq, strip=strip, tk=tk),
        grid=(nb, ATTN_HEADS, nq),
        in_specs=[pl.BlockSpec((tq, LANE), lambda b, h, i: (b * nq + i, h)),
                  pl.BlockSpec((seq, LANE), lambda b, h, i: (b, h)),
                  pl.BlockSpec((seq, LANE), lambda b, h, i: (b, h)),
                  pl.BlockSpec((None, nd, strip, tk), lambda b, h, i: (h, 0, 0, 0)),
                  pl.BlockSpec((4, ATTN_HEAD_DIM), lambda b, h, i: (0, 0)),
                  pl.BlockSpec((1, ATTN_V_DIM), lambda b, h, i: (0, 0))],
        out_specs=pl.BlockSpec((tq, LANE), lambda b, h, i: (b * nq + i, h)),
        out_shape=jax.ShapeDtypeStruct((nb * seq, ATTN_HEADS * ATTN_V_DIM), BF16),
        scratch_shapes=[pltpu.VMEM((2, strip, LANE), F32)] * 3,
        compiler_params=_params("arbitrary", "arbitrary", "arbitrary"),
        name="attn_prompt",
    )(q, kb, vb, bias, lamp, subln_w.reshape(1, ATTN_V_DIM))


DECODE_PAGES = 8
DECODE_SEQS = 1


def _decode_kernel(pt_ref, q_ref, kn_ref, vn_ref, *rest, nq, pps, nbq):
    npg = nbq * pps
    ck_refs, cv_refs = rest[:npg], rest[npg:2 * npg]
    bpage_ref, bnew_ref, lamp_ref, sw_ref, o_ref, qa_scr, m_scr, l_scr, a_scr = rest[2 * npg:]
    j = pl.program_id(1)
    nsteps = pl.num_programs(1)
    cols = PAGE_SIZE * ATTN_HEADS

    @pl.when(j == 0)
    def _():
        lane_lo = lax.broadcasted_iota(jnp.int32, (nq, LANE), 1) < ATTN_HEAD_DIM
        nnew = nq * ATTN_HEADS
        zpad = jnp.zeros((LANE - nnew, LANE), F32)
        for bi in range(nbq):
            for h in range(ATTN_HEADS):
                qh = q_ref[bi, :, h * LANE:(h + 1) * LANE].astype(F32)
                qa_scr[bi, (2 * h) * nq:(2 * h + 1) * nq, :] = jnp.where(lane_lo, qh, 0.0)
                qa_scr[bi, (2 * h + 1) * nq:(2 * h + 2) * nq, :] = jnp.where(lane_lo, 0.0, qh)
            kn = jnp.concatenate([kn_ref[bi], zpad], axis=0).astype(BF16)
            vn = jnp.concatenate([vn_ref[bi], zpad], axis=0).astype(BF16)
            s = _dot_nt(qa_scr[bi].astype(BF16), kn) + bnew_ref[...]
            m = jnp.max(s, axis=-1, keepdims=True)
            p = jnp.exp2(s - m)
            m_scr[bi] = m
            l_scr[bi] = jnp.sum(p, axis=-1, keepdims=True)
            a_scr[bi] = _dot(p.astype(BF16), vn)

    for bi in range(nbq):
        qa = qa_scr[bi].astype(BF16)
        scores = []
        for i in range(pps):
            bias = bpage_ref[jnp.where(j == nsteps - 1, 1, 0)] if i == pps - 1 else bpage_ref[0]
            scores.append(_dot_nt(qa, ck_refs[bi * pps + i][...].reshape(cols, LANE).astype(BF16)) + bias)
        m_prev = m_scr[bi]
        m_new = m_prev
        for s in scores:
            m_new = jnp.maximum(m_new, jnp.max(s, axis=-1, keepdims=True))
        alpha = jnp.exp2(m_prev - m_new)
        l_new = alpha * l_scr[bi]
        a_new = alpha * a_scr[bi]
        for i, s in enumerate(scores):
            p = jnp.exp2(s - m_new)
            l_new = l_new + jnp.sum(p, axis=-1, keepdims=True)
            a_new = a_new + _dot(p.astype(BF16), cv_refs[bi * pps + i][...].reshape(cols, LANE).astype(BF16))
        l_scr[bi] = l_new
        a_scr[bi] = a_new
        m_scr[bi] = m_new

    @pl.when(j == nsteps - 1)
    def _():
        lam = _lam(lamp_ref)
        for bi in range(nbq):
            on = a_scr[bi] / l_scr[bi]
            for h in range(ATTN_HEADS):
                o = on[(2 * h) * nq:(2 * h + 1) * nq, :] - lam * on[(2 * h + 1) * nq:(2 * h + 2) * nq, :]
                o_ref[bi, :, h * LANE:(h + 1) * LANE] = _subln(o, sw_ref[...])


def _attn_decode(q, k_new, v_new, cache_k, cache_v, page_table, rel_bias, lamp, subln_w):
    nb, nq, _ = q.shape
    npages = page_table.shape[1]
    past = npages * PAGE_SIZE
    rows = ATTN_HEADS * 2 * nq
    cols = PAGE_SIZE * ATTN_HEADS
    nnew = nq * ATTN_HEADS
    assert nnew <= LANE and npages >= 2
    qpos = past + jnp.arange(nq, dtype=jnp.int32)
    kpos = jnp.arange(past - 2 * PAGE_SIZE, past + PAGE_SIZE, dtype=jnp.int32)
    rel = jnp.where(kpos[None, :] < past + nq, qpos[:, None] - kpos[None, :], -1)
    b3 = _bias_tiles(jnp.pad(rel, ((0, -nq % 8), (0, 0))), rel_bias, False)[:, :nq].reshape(ATTN_HEADS, nq, 3, PAGE_SIZE)
    own = jnp.arange(ATTN_HEADS)[:, None] == jnp.arange(ATTN_HEADS)[None, :]
    full = jnp.where(own[:, None, None, None, :], b3[..., None], -jnp.inf)
    full = jnp.broadcast_to(full[:, None], (ATTN_HEADS, 2, nq, 3, PAGE_SIZE, ATTN_HEADS))
    full = full.transpose(3, 0, 1, 2, 4, 5).reshape(3, rows, cols)
    bias_pages = full[:2]
    bias_new = jnp.pad(full[2][:, :nnew], ((0, 0), (0, LANE - nnew)), constant_values=-jnp.inf)
    pps = math.gcd(DECODE_PAGES, npages)
    nbq = math.gcd(DECODE_SEQS, nb)
    new_q = pl.BlockSpec((nbq, nq, ATTN_HEADS * LANE), lambda b, j, pt: (b, 0, 0))
    new_kv = pl.BlockSpec((nbq, nnew, LANE), lambda b, j, pt: (b, 0, 0))

    def page(bi, i):
        return pl.BlockSpec((None, PAGE_SIZE, ATTN_HEADS, LANE),
                            lambda b, j, pt: (pt[b * nbq + bi, j * pps + i], 0, 0, 0))

    pages = [page(bi, i) for bi in range(nbq) for i in range(pps)]
    grid_spec = pltpu.PrefetchScalarGridSpec(
        num_scalar_prefetch=1,
        grid=(nb // nbq, npages // pps),
        in_specs=[new_q, new_kv, new_kv, *pages, *pages,
                  pl.BlockSpec((2, rows, cols), lambda b, j, pt: (0, 0, 0)),
                  pl.BlockSpec((rows, LANE), lambda b, j, pt: (0, 0)),
                  pl.BlockSpec((4, ATTN_HEAD_DIM), lambda b, j, pt: (0, 0)),
                  pl.BlockSpec((1, ATTN_V_DIM), lambda b, j, pt: (0, 0))],
        out_specs=new_q,
        scratch_shapes=[pltpu.VMEM((nbq, rows, LANE), F32), pltpu.VMEM((nbq, rows, 1), F32),
                        pltpu.VMEM((nbq, rows, 1), F32), pltpu.VMEM((nbq, rows, LANE), F32)])
    return pl.pallas_call(
        functools.partial(_decode_kernel, nq=nq, pps=pps, nbq=nbq),
        grid_spec=grid_spec,
        out_shape=jax.ShapeDtypeStruct((nb, nq, ATTN_HEADS * ATTN_V_DIM), F32),
        compiler_params=_params("arbitrary", "arbitrary"),
        name="attn_decode",
    )(page_table, q, k_new, v_new, *([cache_k] * len(pages)), *([cache_v] * len(pages)), bias_pages, bias_new, lamp,
      subln_w.reshape(1, ATTN_V_DIM))


HALF = D_MODEL // 2
PACKED = jnp.int32


def _pack_rows(x):
    word = pltpu.pack_elementwise([x[:, :HALF], x[:, HALF:]], packed_dtype=BF16)
    return lax.bitcast_convert_type(word, PACKED)


def _unpack_rows(p):
    halves = [pltpu.unpack_elementwise(p, index=i, packed_dtype=BF16, unpacked_dtype=F32) for i in range(2)]
    return jnp.concatenate(halves, axis=1)


def _layer_norm(r, g, b):
    mu = jnp.mean(r, axis=-1, keepdims=True)
    d = r - mu
    var = jnp.mean(d * d, axis=-1, keepdims=True)
    return d * lax.rsqrt(var + LN_EPS) * g + b


def _merge_kernel(yn_ref, on_ref, ga_ref, gb_ref, x_ref, g1_ref, sc2_ref, sh2_ref, wa_ref, wb_ref, wo_ref,
                  lg_ref, lb_ref, wr_ref, x1_ref, u2p_ref, logit_ref):
    ba = _dot(yn_ref[...].astype(BF16), wa_ref[...])
    bb = _dot(on_ref[...].astype(BF16), wb_ref[...])
    merged = jax.nn.sigmoid(ga_ref[...].astype(F32)) * ba + jax.nn.sigmoid(gb_ref[...].astype(F32)) * bb
    t = _dot(merged.astype(BF16), wo_ref[...])
    x1 = _layer_norm(ALPHA * x_ref[...] + g1_ref[...] * t, lg_ref[...], lb_ref[...])
    x1_ref[...] = x1
    u2 = x1 * (1.0 + sc2_ref[...]) + sh2_ref[...]
    u2p_ref[...] = _pack_rows(u2)
    u_hi = u2.astype(BF16)
    u_lo = (u2 - u_hi.astype(F32)).astype(BF16)
    by_hi = _dot_nt(wr_ref[...], u_hi)
    logit_ref[...] = by_hi[:N_EXPERTS] + by_hi[N_EXPERTS:] + _dot_nt(wr_ref[:N_EXPERTS, :], u_lo)


def _mod_spec(mod, t, tm, rows_per_mod):
    if mod.shape[1] != 1:
        return mod.reshape(t, D_MODEL), pl.BlockSpec((tm, D_MODEL), lambda i: (i, 0))
    assert rows_per_mod % tm == 0
    return mod, pl.BlockSpec((None, 1, D_MODEL), lambda i: (i // (rows_per_mod // tm), 0, 0))


def _merge(yn, on, ga, gb, x, g1, sc2, sh2, wa, wb, wo, ln_g, ln_b, w_router, rows_per_mod):
    t = x.shape[0]
    tm = min(256, t)
    rows = pl.BlockSpec((tm, D_MODEL), lambda i: (i, 0))
    wspec = pl.BlockSpec((D_MODEL, D_MODEL), lambda i: (0, 0))
    vec = pl.BlockSpec((1, D_MODEL), lambda i: (0, 0))
    g1a, mspec = _mod_spec(g1, t, tm, rows_per_mod)
    sc2a, _ = _mod_spec(sc2, t, tm, rows_per_mod)
    sh2a, _ = _mod_spec(sh2, t, tm, rows_per_mod)
    wr_t = w_router.T
    wr_hi = wr_t.astype(BF16)
    return pl.pallas_call(
        _merge_kernel,
        grid=(t // tm,),
        in_specs=[rows, rows, rows, rows, rows, mspec, mspec, mspec, wspec, wspec, wspec, vec, vec,
                  pl.BlockSpec((2 * N_EXPERTS, D_MODEL), lambda i: (0, 0))],
        out_specs=[rows, pl.BlockSpec((tm, HALF), lambda i: (i, 0)), pl.BlockSpec((N_EXPERTS, tm), lambda i: (0, i))],
        out_shape=[jax.ShapeDtypeStruct((t, D_MODEL), F32), jax.ShapeDtypeStruct((t, HALF), PACKED),
                   jax.ShapeDtypeStruct((N_EXPERTS, t), F32)],
        compiler_params=_params("arbitrary"),
        name="merge",
    )(yn, on, ga, gb, x, g1a, sc2a, sh2a, wa, wb, wo, ln_g.reshape(1, D_MODEL), ln_b.reshape(1, D_MODEL),
      jnp.concatenate([wr_hi, (wr_t - wr_hi.astype(F32)).astype(BF16)], axis=0))


def _router_kernel(logit_ref, br_ref, e_ref, w_ref, r_ref, cnt_ref, carry_scr, tri_scr, *, tr):
    i = pl.program_id(0)

    @pl.when(i == 0)
    def _():
        carry_scr[...] = jnp.zeros_like(carry_scr)
        a = lax.broadcasted_iota(jnp.int32, (tr, tr), 0)
        b = lax.broadcasted_iota(jnp.int32, (tr, tr), 1)
        tri_scr[...] = (a < b).astype(BF16)

    scores = jax.nn.sigmoid(logit_ref[...])
    biased = scores + br_ref[...]
    ninf = -jnp.inf

    b3 = biased.reshape(N_EXPERT_GROUPS, GROUP_SIZE, tr)
    j3 = lax.broadcasted_iota(jnp.int32, b3.shape, 1).astype(F32)
    top1 = jnp.max(b3, axis=1, keepdims=True)
    first = jnp.min(jnp.where(b3 == top1, j3, float(GROUP_SIZE)), axis=1, keepdims=True)
    top2 = jnp.max(jnp.where(j3 == first, ninf, b3), axis=1, keepdims=True)
    gscore = (top1 + top2).reshape(N_EXPERT_GROUPS, tr)
    gi = lax.broadcasted_iota(jnp.int32, gscore.shape, 0).astype(F32)
    gsel = jnp.zeros(gscore.shape, F32)
    for _ in range(TOPK_GROUPS):
        mx = jnp.max(gscore, axis=0, keepdims=True)
        pick = gi == jnp.min(jnp.where(gscore == mx, gi, float(N_EXPERT_GROUPS)), axis=0, keepdims=True)
        gsel = jnp.where(pick, 1.0, gsel)
        gscore = jnp.where(pick, ninf, gscore)
    emask = jnp.broadcast_to(gsel.reshape(N_EXPERT_GROUPS, 1, tr), b3.shape).reshape(N_EXPERTS, tr)
    masked = jnp.where(emask > 0.5, biased, ninf)

    ei = lax.broadcasted_iota(jnp.int32, masked.shape, 0).astype(F32)
    picked = jnp.zeros(masked.shape, F32)
    idxs, wsel = [], []
    for _ in range(TOP_K):
        mx = jnp.max(masked, axis=0, keepdims=True)
        idx = jnp.min(jnp.where(masked == mx, ei, float(N_EXPERTS)), axis=0, keepdims=True)
        pick = ei == idx
        idxs.append(idx)
        wsel.append(jnp.sum(jnp.where(pick, scores, 0.0), axis=0, keepdims=True))
        picked = jnp.where(pick, 1.0, picked)
        masked = jnp.where(pick, ninf, masked)
    wall = jnp.concatenate(wsel, axis=0)
    w_ref[...] = wall / jnp.sum(wall, axis=0, keepdims=True) * ROUTED_SCALE
    e_ref[...] = jnp.concatenate(idxs, axis=0).astype(jnp.int32)

    rank = carry_scr[:, 0:1] + _dot(picked.astype(BF16), tri_scr[...])
    r_ref[...] = jnp.concatenate(
        [jnp.sum(jnp.where(ei == idx, rank, 0.0), axis=0, keepdims=True) for idx in idxs], axis=0).astype(jnp.int32)
    total = carry_scr[...] + jnp.sum(picked, axis=1, keepdims=True)
    carry_scr[...] = total
    cnt_ref[...] = total.astype(jnp.int32)


def _router(logits, b_router):
    t = logits.shape[1]
    tr = min(512, t)
    tok = pl.BlockSpec((TOP_K, tr), lambda i: (0, i))
    e_t, w_t, r_t, cnt = pl.pallas_call(
        functools.partial(_router_kernel, tr=tr),
        grid=(t // tr,),
        in_specs=[pl.BlockSpec((N_EXPERTS, tr), lambda i: (0, i)),
                  pl.BlockSpec((N_EXPERTS, 1), lambda i: (0, 0))],
        out_specs=[tok, tok, tok, pl.BlockSpec((N_EXPERTS, LANE), lambda i: (0, 0))],
        out_shape=[jax.ShapeDtypeStruct((TOP_K, t), jnp.int32), jax.ShapeDtypeStruct((TOP_K, t), F32),
                   jax.ShapeDtypeStruct((TOP_K, t), jnp.int32), jax.ShapeDtypeStruct((N_EXPERTS, LANE), jnp.int32)],
        scratch_shapes=[pltpu.VMEM((N_EXPERTS, LANE), F32), pltpu.VMEM((tr, tr), BF16)],
        compiler_params=_params("arbitrary"),
        name="router",
    )(logits, b_router.reshape(N_EXPERTS, 1))
    return e_t, w_t, r_t, cnt


def _plan_kernel(e_ref, r_ref, cnt_ref, dest_ref, be_ref, bv_ref, nu_ref, *, br, nblk_pad):
    cnt = cnt_ref[...]
    shift = br.bit_length() - 1
    padded = lax.shift_left(lax.shift_right_logical(cnt + (br - 1), shift), shift)
    ea = lax.broadcasted_iota(jnp.int32, (N_EXPERTS, N_EXPERTS), 0)
    eb = lax.broadcasted_iota(jnp.int32, (N_EXPERTS, N_EXPERTS), 1)
    pends = jnp.dot((eb <= ea).astype(F32), padded.astype(F32), preferred_element_type=F32,
                    precision=HIGHEST).astype(jnp.int32)
    pstart = pends - padded

    e = e_ref[...]
    first_row = jnp.zeros(e.shape, jnp.int32)
    for x in range(N_EXPERTS):
        first_row = jnp.where(e == x, pstart[x:x + 1, 0:1], first_row)
    dest_ref[...] = first_row + r_ref[...]

    @pl.when(pl.program_id(0) == 0)
    def _():
        esub = lax.broadcasted_iota(jnp.int32, (N_EXPERTS, LANE), 0)
        real_end = (pstart + cnt).astype(F32)
        for c in range(nblk_pad // LANE):
            bstart = (lax.broadcasted_iota(jnp.int32, (1, LANE), 1) + c * LANE) * br
            be = jnp.minimum(jnp.sum((pends <= bstart).astype(F32), axis=0, keepdims=True),
                             float(N_EXPERTS - 1)).astype(jnp.int32)
            end = jnp.sum(jnp.where(esub == be, real_end, 0.0), axis=0, keepdims=True).astype(jnp.int32)
            be_ref[:, c * LANE:(c + 1) * LANE] = be
            bv_ref[:, c * LANE:(c + 1) * LANE] = jnp.clip(end - bstart, 0, br)
        nu_ref[...] = lax.shift_right_logical(pends[N_EXPERTS - 1:N_EXPERTS, :], shift)


def _plan(e_t, r_t, cnt, br, nblk):
    t = e_t.shape[1]
    tc = min(2048, t)
    nblk_pad = -(-nblk // LANE) * LANE
    tok = pl.BlockSpec((TOP_K, tc), lambda i: (0, i))
    blk = pl.BlockSpec((1, nblk_pad), lambda i: (0, 0))
    dest, be, bv, nu = pl.pallas_call(
        functools.partial(_plan_kernel, br=br, nblk_pad=nblk_pad),
        grid=(t // tc,),
        in_specs=[tok, tok, pl.BlockSpec((N_EXPERTS, LANE), lambda i: (0, 0))],
        out_specs=[tok, blk, blk, pl.BlockSpec((1, LANE), lambda i: (0, 0))],
        out_shape=[jax.ShapeDtypeStruct((TOP_K, t), jnp.int32), jax.ShapeDtypeStruct((1, nblk_pad), jnp.int32),
                   jax.ShapeDtypeStruct((1, nblk_pad), jnp.int32), jax.ShapeDtypeStruct((1, LANE), jnp.int32)],
        compiler_params=_params("arbitrary"),
        name="moe_plan",
    )(e_t, r_t, cnt)
    return dest, be[0, :nblk], bv[0, :nblk], nu[0, :1]


def _row_copy(src, s, dst, d, sem):
    return pltpu.make_async_copy(src.at[pl.ds(s, 1), :], dst.at[pl.ds(d, 1), :], sem)


def _dispatch_kernel(dest_ref, u_ref, xs_ref, sem, *, tm):
    def start(r, carry):
        for k in range(TOP_K):
            _row_copy(u_ref, r, xs_ref, dest_ref[k, r], sem).start()
        return carry

    def wait(r, carry):
        for k in range(TOP_K):
            _row_copy(u_ref, r, xs_ref, dest_ref[k, r], sem).wait()
        return carry

    lax.fori_loop(0, tm, start, 0)
    lax.fori_loop(0, tm, wait, 0)


def _dispatch(u2p, dest_t, n_rows):
    t = u2p.shape[0]
    tm = min(256, t)
    nt = t // tm
    dest_blocks = dest_t.reshape(TOP_K, nt, tm).transpose(1, 0, 2)
    return pl.pallas_call(
        functools.partial(_dispatch_kernel, tm=tm),
        grid=(nt,),
        in_specs=[pl.BlockSpec((None, TOP_K, tm), lambda i: (i, 0, 0), memory_space=pltpu.SMEM),
                  pl.BlockSpec((tm, HALF), lambda i: (i, 0))],
        out_specs=pl.BlockSpec(memory_space=pl.ANY),
        out_shape=jax.ShapeDtypeStruct((n_rows, HALF), PACKED),
        scratch_shapes=[pltpu.SemaphoreType.DMA(())],
        compiler_params=_params("arbitrary"),
        name="moe_dispatch",
    )(dest_blocks, u2p)


SC_CORES = 2
SC_SUBCORES = 16
SC_WINDOW = 128
SC_WORKERS = SC_CORES * SC_SUBCORES


def _sc_mesh():
    return plsc.VectorSubcoreMesh(core_axis_name="c", subcore_axis_name="s")


def _sc_worker():
    return lax.axis_index("s") * SC_CORES + lax.axis_index("c")


def _sc_scatter_rows(rows, dest_blocks, n_rows):
    t, width = rows.shape
    nchunks = t // SC_WINDOW // SC_WORKERS

    def body(rows_hbm, dest_hbm, out_hbm, idx_v, rows_v, sem):
        wid = _sc_worker()

        @pl.loop(0, nchunks)
        def _(c):
            chunk = wid * nchunks + c
            pltpu.sync_copy(dest_hbm.at[chunk], idx_v)
            pltpu.sync_copy(rows_hbm.at[pl.ds(pl.multiple_of(chunk * SC_WINDOW, SC_WINDOW), SC_WINDOW)], rows_v)
            copies = [pltpu.async_copy(rows_v, out_hbm.at[idx_v.at[k]], sem) for k in range(TOP_K)]
            for cp in copies:
                cp.wait()

    return pl.kernel(
        body, out_type=jax.ShapeDtypeStruct((n_rows, width), rows.dtype), mesh=_sc_mesh(),
        scratch_types=[pltpu.VMEM((TOP_K, SC_WINDOW), jnp.int32), pltpu.VMEM((SC_WINDOW, width), rows.dtype),
                       pltpu.SemaphoreType.DMA],
        name="moe_sc_dispatch")(rows, dest_blocks)


def _sc_gather_rows(table, idx):
    n = idx.shape[0]
    width = table.shape[1]
    nchunks = n // SC_WINDOW // SC_WORKERS

    def body(table_hbm, idx_hbm, out_hbm, idx_v, rows_v, sem):
        wid = _sc_worker()

        @pl.loop(0, nchunks)
        def _(c):
            base = pl.multiple_of((wid * nchunks + c) * SC_WINDOW, SC_WINDOW)
            pltpu.sync_copy(idx_hbm.at[pl.ds(base, SC_WINDOW)], idx_v)
            pltpu.async_copy(table_hbm.at[idx_v], rows_v, sem).wait()
            pltpu.sync_copy(rows_v, out_hbm.at[pl.ds(base, SC_WINDOW)])

    return pl.kernel(
        body, out_type=jax.ShapeDtypeStruct((n, width), table.dtype), mesh=_sc_mesh(),
        scratch_types=[pltpu.VMEM((SC_WINDOW,), jnp.int32), pltpu.VMEM((SC_WINDOW, width), table.dtype),
                       pltpu.SemaphoreType.DMA],
        name="moe_sc_gather")(table, idx)


def _expert_kernel(be_ref, bv_ref, nu_ref, x_ref, wg_ref, wu_ref, wd_ref, y_ref, wg_scr, wu_scr, wd_scr, *, br):
    i = pl.program_id(0)
    prev = be_ref[jnp.maximum(i - 1, 0)]

    @pl.when(jnp.logical_and(i < nu_ref[0], jnp.logical_or(i == 0, be_ref[i] != prev)))
    def _():
        wg_scr[...] = wg_ref[...].astype(BF16)
        wu_scr[...] = wu_ref[...].astype(BF16)
        wd_scr[...] = wd_ref[...].astype(BF16)

    @pl.when(i < nu_ref[0])
    def _():
        live = lax.broadcasted_iota(jnp.int32, (br, HALF), 0) < bv_ref[i]
        x = _unpack_rows(jnp.where(live, x_ref[...], 0)).astype(BF16)
        hcat = _silu(_dot(x, wg_scr[...])) * _dot(x, wu_scr[...])
        y_ref[...] = _pack_rows(_dot(hcat.astype(BF16), wd_scr[...]))


def _experts(x_sorted, blk_e, blk_valid, n_used, w_gate, w_up, w_down, br):
    n_rows = x_sorted.shape[0]
    nblk = n_rows // br

    def blk(i, be, bv, nu):
        return (jnp.minimum(i, nu[0] - 1), 0)

    grid_spec = pltpu.PrefetchScalarGridSpec(
        num_scalar_prefetch=3,
        grid=(nblk,),
        in_specs=[pl.BlockSpec((br, HALF), blk),
                  pl.BlockSpec((None, D_MODEL, D_EXPERT), lambda i, be, bv, nu: (be[i], 0, 0)),
                  pl.BlockSpec((None, D_MODEL, D_EXPERT), lambda i, be, bv, nu: (be[i], 0, 0)),
                  pl.BlockSpec((None, D_EXPERT, D_MODEL), lambda i, be, bv, nu: (be[i], 0, 0))],
        out_specs=pl.BlockSpec((br, HALF), blk),
        scratch_shapes=[pltpu.VMEM((D_MODEL, D_EXPERT), BF16), pltpu.VMEM((D_MODEL, D_EXPERT), BF16),
                        pltpu.VMEM((D_EXPERT, D_MODEL), BF16)])
    return pl.pallas_call(
        functools.partial(_expert_kernel, br=br),
        grid_spec=grid_spec,
        out_shape=jax.ShapeDtypeStruct((n_rows, HALF), PACKED),
        compiler_params=_params("arbitrary"),
        name="moe_experts",
    )(blk_e, blk_valid, n_used, x_sorted, w_gate, w_up, w_down)


def _combine_kernel(dest_ref, ys_ref, w_ref, u_ref, x1_ref, g2_ref, sg_ref, su_ref, sd_ref, lg_ref, lb_ref,
                    o_ref, g_scr, sem, *, tm):
    def start(r, carry):
        for k in range(TOP_K):
            _row_copy(ys_ref, dest_ref[k, r], g_scr.at[k], r, sem).start()
        return carry

    def wait(r, carry):
        for k in range(TOP_K):
            _row_copy(ys_ref, dest_ref[k, r], g_scr.at[k], r, sem).wait()
        return carry

    lax.fori_loop(0, tm, start, 0)
    f = _shared_ffn(u_ref, sg_ref, su_ref, sd_ref)
    lax.fori_loop(0, tm, wait, 0)
    f = _add_routed(f, g_scr, w_ref[...])
    o_ref[...] = _layer_norm(ALPHA * x1_ref[...] + g2_ref[...] * f, lg_ref[...], lb_ref[...])


def _shared_ffn(u_ref, sg_ref, su_ref, sd_ref):
    ub = _unpack_rows(u_ref[...]).astype(BF16)
    hs = _silu(_dot(ub, sg_ref[...])) * _dot(ub, su_ref[...])
    return _dot(hs.astype(BF16), sd_ref[...])


def _add_routed(f, slots_ref, w):
    for k in range(TOP_K):
        f = f + _unpack_rows(slots_ref[k]) * w[:, k:k + 1]
    return f


def _combine_gathered_kernel(g_ref, w_ref, u_ref, x1_ref, g2_ref, sg_ref, su_ref, sd_ref, lg_ref, lb_ref, o_ref):
    f = _add_routed(_shared_ffn(u_ref, sg_ref, su_ref, sd_ref), g_ref, w_ref[...])
    o_ref[...] = _layer_norm(ALPHA * x1_ref[...] + g2_ref[...] * f, lg_ref[...], lb_ref[...])


def _combine(y_rows, dest_t, w_t, u2p, x1, g2, ws_gate, ws_up, ws_down, ln_g, ln_b, rows_per_mod, gathered):
    t = u2p.shape[0]
    tm = min(256 if gathered else 128, t)
    nt = t // tm
    rows = pl.BlockSpec((tm, D_MODEL), lambda i: (i, 0))
    vec = pl.BlockSpec((1, D_MODEL), lambda i: (0, 0))
    g2a, mspec = _mod_spec(g2, t, tm, rows_per_mod)
    common_specs = [pl.BlockSpec((tm, TOP_K), lambda i: (i, 0)),
                    pl.BlockSpec((tm, HALF), lambda i: (i, 0)), rows, mspec,
                    pl.BlockSpec((D_MODEL, D_SHARED), lambda i: (0, 0)),
                    pl.BlockSpec((D_MODEL, D_SHARED), lambda i: (0, 0)),
                    pl.BlockSpec((D_SHARED, D_MODEL), lambda i: (0, 0)),
                    vec, vec]
    common_args = (w_t.T, u2p, x1, g2a, ws_gate.astype(BF16), ws_up.astype(BF16), ws_down.astype(BF16),
                   ln_g.reshape(1, D_MODEL), ln_b.reshape(1, D_MODEL))
    if gathered:
        return pl.pallas_call(
            _combine_gathered_kernel,
            grid=(nt,),
            in_specs=[pl.BlockSpec((TOP_K, tm, HALF), lambda i: (0, i, 0))] + common_specs,
            out_specs=rows,
            out_shape=jax.ShapeDtypeStruct((t, D_MODEL), F32),
            compiler_params=_params("arbitrary"),
            name="moe_combine_gathered",
        )(y_rows.reshape(TOP_K, t, HALF), *common_args)
    dest_blocks = dest_t.reshape(TOP_K, nt, tm).transpose(1, 0, 2)
    return pl.pallas_call(
        functools.partial(_combine_kernel, tm=tm),
        grid=(nt,),
        in_specs=[pl.BlockSpec((None, TOP_K, tm), lambda i: (i, 0, 0), memory_space=pltpu.SMEM),
                  pl.BlockSpec(memory_space=pl.ANY)] + common_specs,
        out_specs=rows,
        out_shape=jax.ShapeDtypeStruct((t, D_MODEL), F32),
        scratch_shapes=[pltpu.VMEM((TOP_K, tm, HALF), PACKED), pltpu.SemaphoreType.DMA(())],
        compiler_params=_params("arbitrary"),
        name="moe_combine",
    )(dest_blocks, y_rows, *common_args)


def _moe(u2p, logits, x1, g2, p, rows_per_mod, br):
    t = u2p.shape[0]
    e_t, w_t, r_t, counts = _router(logits, p["b_router"])
    n_rows = (t * TOP_K + N_EXPERTS * (br - 1) + br - 1) // br * br
    dest_t, blk_e, blk_valid, n_used = _plan(e_t, r_t, counts, br, n_rows // br)
    on_sc = t % (SC_WORKERS * SC_WINDOW) == 0
    if on_sc:
        dest_blocks = dest_t.reshape(TOP_K, t // SC_WINDOW, SC_WINDOW).transpose(1, 0, 2)
        x_sorted = _sc_scatter_rows(u2p, dest_blocks, n_rows)
    else:
        x_sorted = _dispatch(u2p, dest_t, n_rows)
    y_rows = _experts(x_sorted, blk_e, blk_valid, n_used, p["w_gate"], p["w_up"], p["w_down"], br)
    if on_sc:
        y_rows = _sc_gather_rows(y_rows, dest_t.reshape(TOP_K * t))
    return _combine(y_rows, dest_t, w_t, u2p, x1, g2, p["ws_gate"], p["ws_up"], p["ws_down"],
                    p["ln2_g"], p["ln2_b"], rows_per_mod, on_sc)


def _cat_w_in(w_in):
    sizes = (SSD_INNER, CONV_DIM, SSD_HEADS, 1024, 1024, 1024, D_MODEL, D_MODEL)
    offs = [0]
    for s in sizes:
        offs.append(offs[-1] + s)
    z, xbc, dt, q, k, v, ga, gb = [w_in[:, offs[i]:offs[i + 1]] for i in range(8)]
    dt = jnp.pad(dt, ((0, 0), (0, SEG_END - SEG_DT - SSD_HEADS)))
    return jnp.concatenate([z, xbc, q, k, v, ga, gb, dt], axis=1).astype(BF16)


def kernel(x_prompt, x_sample, c_prompt, c_sample, cache_k, cache_v, page_table, state_conv, state_ssm, rel_bias, w_ada, b_ada, w_in, conv_w, conv_b, dt_bias, a_log, d_skip, ssd_norm_w, lam_q1, lam_k1, lam_q2, lam_k2, subln_w, w_br_ssd, w_br_attn, w_out, ln1_g, ln1_b, w_router, b_router, w_gate, w_up, w_down, ws_gate, ws_up, ws_down, ln2_g, ln2_b):
    assert w_in.shape[0] == DEPTH
    nbp, seq, _ = x_prompt.shape
    nbs, dseq, _ = x_sample.shape
    tp, ts = nbp * seq, nbs * dseq
    past = page_table.shape[1] * PAGE_SIZE
    cl = SSD_CHUNK

    pad_p = -nbp % 8
    c_all = jnp.concatenate([c_prompt, jnp.zeros((pad_p, D_MODEL), F32), jnp.repeat(c_sample, dseq, axis=0)], axis=0)
    mod = _adaln(c_all, w_ada[0], b_ada[0])
    mod_p = mod[:nbp].reshape(nbp, 1, 6, D_MODEL)
    mod_s = mod[nbp + pad_p:].reshape(nbs, dseq, 6, D_MODEL)
    mp = [mod_p[:, :, i] for i in range(6)]
    ms = [mod_s[:, :, i] for i in range(6)]

    w_cat = _cat_w_in(w_in[0])
    wa, wb, wo = w_br_ssd[0].astype(BF16), w_br_attn[0].astype(BF16), w_out[0].astype(BF16)
    lamp = jnp.stack([lam_q1[0], lam_k1[0], lam_q2[0], lam_k2[0]])
    moe_p = dict(b_router=b_router[0], w_gate=w_gate[0], w_up=w_up[0], w_down=w_down[0],
                 ws_gate=ws_gate[0], ws_up=ws_up[0], ws_down=ws_down[0], ln2_g=ln2_g[0], ln2_b=ln2_b[0])
    ssd_w = (conv_w[0], conv_b[0], dt_bias[0], a_log[0], d_skip[0], ssd_norm_w[0])

    xp = x_prompt.reshape(tp, D_MODEL)
    z, xbc, dt, q, k, v, kb, vb, ga, gb = _inproj(xp, mp[1], mp[0], w_cat, seq)
    yn, conv_p, h_p = _ssd(xbc, dt, z, *ssd_w, jnp.zeros((nbp, CONV_W - 1, CONV_DIM), F32),
                           jnp.zeros((nbp, SSD_HEADS, SSD_HEADDIM, D_STATE), F32), nbp, seq // cl, cl)
    on = _attn_prompt(q, kb, vb, rel_bias, lamp, subln_w[0], nbp, seq)
    x1, u2p, logits = _merge(yn, on, ga, gb, xp, mp[2], mp[4], mp[3], wa, wb, wo, ln1_g[0], ln1_b[0], w_router[0], seq)
    y_prompt = _moe(u2p, logits, x1, mp[5], moe_p, seq, 512).reshape(nbp, seq, D_MODEL)
    k_prompt = k.reshape(1, nbp, seq, ATTN_HEADS, 2 * ATTN_HEAD_DIM)
    v_prompt = v.reshape(1, nbp, seq, ATTN_HEADS, ATTN_V_DIM)

    xs_ = x_sample.reshape(ts, D_MODEL)
    z, xbc, dt, q, k, v, kb, vb, ga, gb = _inproj(xs_, ms[1], ms[0], w_cat, dseq)

    def chunk_pad(a):
        a = a.reshape(nbs, dseq, a.shape[-1])
        return jnp.pad(a, ((0, 0), (0, cl - dseq), (0, 0))).reshape(nbs * cl, a.shape[-1])

    yn, conv_s, h_s = _ssd(chunk_pad(xbc), chunk_pad(dt), chunk_pad(z), *ssd_w, state_conv[0], state_ssm[0],
                           nbs, 1, dseq)
    yn = yn.reshape(nbs, cl, SSD_INNER)[:, :dseq].reshape(ts, SSD_INNER)
    on = _attn_decode(q.reshape(nbs, dseq, -1), k.reshape(nbs, dseq * ATTN_HEADS, LANE),
                      v.reshape(nbs, dseq * ATTN_HEADS, LANE), cache_k[0], cache_v[0], page_table,
                      rel_bias, lamp, subln_w[0])
    x1, u2p, logits = _merge(yn, on.reshape(ts, -1), ga, gb, xs_, ms[2], ms[4], ms[3], wa, wb, wo, ln1_g[0], ln1_b[0],
                             w_router[0], dseq)
    y_sample = _moe(u2p, logits, x1, ms[5], moe_p, dseq, 64).reshape(nbs, dseq, D_MODEL)
    k_sample = k.reshape(1, nbs, dseq, ATTN_HEADS, 2 * ATTN_HEAD_DIM)
    v_sample = v.reshape(1, nbs, dseq, ATTN_HEADS, ATTN_V_DIM)

    return (y_prompt, y_sample, k_prompt, v_prompt, conv_p[None], h_p.reshape(1, nbp, SSD_HEADS, SSD_HEADDIM, D_STATE),
            k_sample, v_sample, conv_s[None], h_s.reshape(1, nbs, SSD_HEADS, SSD_HEADDIM, D_STATE))
```

```python
import functools
import math

import jax
import jax.numpy as jnp
from jax import lax
from jax.experimental import pallas as pl
from jax.experimental.pallas import tpu as pltpu
from jax.experimental.pallas import tpu_sc as plsc

F32 = jnp.float32
BF16 = jnp.bfloat16
HIGHEST = lax.Precision.HIGHEST

D_MODEL = 1024
SSD_INNER = 1024
SSD_HEADDIM = 64
SSD_HEADS = 16
SSD_GROUPS = 2
D_STATE = 128
CONV_W = 4
CONV_DIM = SSD_INNER + 2 * SSD_GROUPS * D_STATE
SSD_CHUNK = 128
ATTN_HEADS = 8
ATTN_HEAD_DIM = 64
ATTN_V_DIM = 128
N_BUCKETS = 32
MAX_DISTANCE = 128
N_EXPERTS = 64
N_EXPERT_GROUPS = 8
GROUP_SIZE = N_EXPERTS // N_EXPERT_GROUPS
TOPK_GROUPS = 4
TOP_K = 8
D_EXPERT = 256
D_SHARED = 256
ROUTED_SCALE = 2.5
PAGE_SIZE = 128
DEPTH = 1
ALPHA = (2 * DEPTH) ** 0.25
LN_EPS = 1e-5
RMS_EPS = 1e-5
LAM_INIT = 0.8 - 0.6 * math.exp(-0.3 * 0)
LOG2E = math.log2(math.e)
LANE = 128
VMEM_LIMIT = 56 * 1024 * 1024

SEG_Z, SEG_XBC, SEG_Q, SEG_K, SEG_V, SEG_GA, SEG_GB, SEG_DT, SEG_END = (
    0, 1024, 2560, 3584, 4608, 5632, 6656, 7680, 7808)


def _silu(x):
    return x * jax.nn.sigmoid(x)


def _dot(a, b):
    return jnp.dot(a, b, preferred_element_type=F32)


def _dot_nt(a, b):
    return lax.dot_general(a, b, (((1,), (1,)), ((), ())), preferred_element_type=F32)


def _dot_tn(a, b):
    return lax.dot_general(a, b, (((0,), (0,)), ((), ())), preferred_element_type=F32)


def _params(*sem):
    return pltpu.CompilerParams(dimension_semantics=sem, vmem_limit_bytes=VMEM_LIMIT)


def _adaln_kernel(c_ref, w_ref, b_ref, o_ref):
    s = _silu(c_ref[...]).astype(BF16)
    o_ref[...] = _dot(s, w_ref[...].astype(BF16)) + b_ref[...]


def _adaln(c, w_ada, b_ada):
    r = c.shape[0]
    n = w_ada.shape[1]
    tn = 1024
    return pl.pallas_call(
        _adaln_kernel,
        grid=(n // tn,),
        in_specs=[pl.BlockSpec((r, D_MODEL), lambda j: (0, 0)),
                  pl.BlockSpec((D_MODEL, tn), lambda j: (0, j)),
                  pl.BlockSpec((1, tn), lambda j: (0, j))],
        out_specs=pl.BlockSpec((r, tn), lambda j: (0, j)),
        out_shape=jax.ShapeDtypeStruct((r, n), F32),
        compiler_params=_params("arbitrary"),
        name="adaln",
    )(c, w_ada, b_ada.reshape(1, n))


def _inproj_kernel(x_ref, sc_ref, sh_ref, w_ref, z_ref, xbc_ref, dt_ref, q_ref, k_ref, v_ref,
                   kb_ref, vb_ref, ga_ref, gb_ref):
    u = (x_ref[...] * (1.0 + sc_ref[...]) + sh_ref[...]).astype(BF16)

    def seg(a, b):
        return _dot(u, w_ref[:, a:b])

    z_ref[...] = seg(SEG_Z, SEG_XBC).astype(BF16)
    xbc_ref[...] = seg(SEG_XBC, SEG_Q)
    q_ref[...] = (seg(SEG_Q, SEG_K) * (ATTN_HEAD_DIM ** -0.5 * LOG2E)).astype(BF16)
    kk = seg(SEG_K, SEG_V)
    k_ref[...] = kk
    kb_ref[...] = kk.astype(BF16)
    vv = seg(SEG_V, SEG_GA)
    v_ref[...] = vv
    vb_ref[...] = vv.astype(BF16)
    ga_ref[...] = seg(SEG_GA, SEG_GB).astype(BF16)
    gb_ref[...] = seg(SEG_GB, SEG_DT).astype(BF16)
    dt_ref[...] = seg(SEG_DT, SEG_END)


def _inproj(x, sc, sh, w_cat, rows_per_mod):
    t = x.shape[0]
    tm = min(256, t)
    per_row = sc.shape[1] != 1
    if per_row:
        sc2 = sc.reshape(t, D_MODEL)
        sh2 = sh.reshape(t, D_MODEL)
        mod_spec = pl.BlockSpec((tm, D_MODEL), lambda i: (i, 0))
    else:
        assert rows_per_mod % tm == 0
        sc2, sh2 = sc, sh
        mod_spec = pl.BlockSpec((None, 1, D_MODEL), lambda i: (i // (rows_per_mod // tm), 0, 0))

    def rows(width):
        return pl.BlockSpec((tm, width), lambda i: (i, 0))

    widths = (1024, CONV_DIM, LANE, 1024, 1024, 1024, 1024, 1024, 1024, 1024)
    dtypes = (BF16, F32, F32, BF16, F32, F32, BF16, BF16, BF16, BF16)
    return pl.pallas_call(
        _inproj_kernel,
        grid=(t // tm,),
        in_specs=[rows(D_MODEL), mod_spec, mod_spec,
                  pl.BlockSpec((D_MODEL, SEG_END), lambda i: (0, 0), pipeline_mode=pl.Buffered(1))],
        out_specs=[rows(w) for w in widths],
        out_shape=[jax.ShapeDtypeStruct((t, w), d) for w, d in zip(widths, dtypes)],
        compiler_params=_params("arbitrary"),
        name="inproj",
    )(x, sc2, sh2, w_cat)


def _ssd_kernel(xbc_ref, dt_ref, z_ref, cw_ref, cb_ref, dtb_ref, alog_ref, dsk_ref, nw_ref,
                cbuf_ref, h0_ref, yn_ref, cnew_ref, hnew_ref, xp_scr, h_scr, y_scr, *, valid_len):
    c = pl.program_id(1)
    nc = pl.num_programs(1)
    cl = SSD_CHUNK
    head = 8

    @pl.when(c == 0)
    def _():
        xp_scr[head - (CONV_W - 1):head, :] = cbuf_ref[...]
        h_scr[...] = h0_ref[...]

    @pl.when(c > 0)
    def _():
        xp_scr[head - (CONV_W - 1):head, :] = xp_scr[head + cl - (CONV_W - 1):head + cl, :]

    xp_scr[head:head + valid_len, :] = xbc_ref[...]
    if valid_len < cl:
        xp_scr[head + valid_len:head + cl, :] = jnp.zeros((cl - valid_len, CONV_DIM), F32)

    acc = xp_scr[head:head + cl, :] * cw_ref[CONV_W - 1:CONV_W, :]
    for j in range(CONV_W - 1):
        lo = head - (CONV_W - 1) + j
        acc = acc + xp_scr[lo:lo + cl, :] * cw_ref[j:j + 1, :]
    xc = _silu(acc + cb_ref[...])
    xs = xc[:, :SSD_INNER]
    bmat = [xc[:, SSD_INNER + g * D_STATE:SSD_INNER + (g + 1) * D_STATE].astype(BF16)
            for g in range(SSD_GROUPS)]
    coff = SSD_INNER + SSD_GROUPS * D_STATE
    cmat = [xc[:, coff + g * D_STATE:coff + (g + 1) * D_STATE].astype(BF16) for g in range(SSD_GROUPS)]

    li = lax.broadcasted_iota(jnp.int32, (cl, cl), 0)
    si = lax.broadcasted_iota(jnp.int32, (cl, cl), 1)
    causal = li >= si
    lane_lo = si < SSD_HEADDIM
    sub_lo = li < SSD_HEADDIM

    def rows_padded(v):
        return v if valid_len == cl else jnp.concatenate([v, jnp.zeros((cl - valid_len, v.shape[1]), F32)], axis=0)

    dpre = rows_padded(dt_ref[...]) + dtb_ref[...]
    dtv = jnp.maximum(dpre, 0.0) + jnp.log(1.0 + jnp.exp(-jnp.abs(dpre)))
    if valid_len < cl:
        dtv = jnp.where(li < valid_len, dtv, 0.0)
    da = dtv * (-jnp.exp(alog_ref[...]))
    tril = causal.astype(F32)
    acum = jnp.dot(tril, da, preferred_element_type=F32, precision=HIGHEST)
    acum_t = acum.T
    dt_t = dtv.T
    last = acum[cl - 1:cl, :]
    e_all = jnp.exp(acum)
    w_all = jnp.exp(last - acum) * dtv
    e_last = jnp.exp(last)

    cb = [_dot_nt(cmat[g], bmat[g]) for g in range(SSD_GROUPS)]

    def colb(a, h):
        return jnp.broadcast_to(a[:, h:h + 1], (cl, cl))

    pairs = SSD_HEADS // 2
    for p in range(pairs):
        h0, h1 = 2 * p, 2 * p + 1
        g = h0 // (SSD_HEADS // SSD_GROUPS)
        xpair = xs[:, p * LANE:(p + 1) * LANE]
        x_lo = jnp.where(lane_lo, xpair, 0.0).astype(BF16)
        x_hi = jnp.where(lane_lo, 0.0, xpair).astype(BF16)
        ydiag = None
        for hh, xm in ((h0, x_lo), (h1, x_hi)):
            seg = colb(acum, hh) - acum_t[hh:hh + 1, :]
            dec = jnp.exp(jnp.where(causal, seg, -jnp.inf))
            m = (cb[g] * dec * dt_t[hh:hh + 1, :]).astype(BF16)
            part = _dot(m, xm)
            ydiag = part if ydiag is None else ydiag + part
        ecol = jnp.where(lane_lo, colb(e_all, h0), colb(e_all, h1))
        hp = h_scr[p]
        yoff = _dot_nt(cmat[g], hp.astype(BF16)) * ecol
        wcol = jnp.where(lane_lo, colb(w_all, h0), colb(w_all, h1))
        dx = (xpair * wcol).astype(BF16)
        st = _dot_tn(dx, bmat[g])
        hdec = jnp.where(sub_lo, e_last[:, h0:h0 + 1], e_last[:, h1:h1 + 1])
        h_scr[p] = hp * hdec + st
        y_scr[:, p * LANE:(p + 1) * LANE] = ydiag + yoff + dsk_ref[:, p * LANE:(p + 1) * LANE] * xpair

    zf = z_ref[...].astype(F32)
    gated = y_scr[0:valid_len, :] * _silu(zf)
    gw = SSD_INNER // SSD_GROUPS
    for g in range(SSD_GROUPS):
        sg = gated[:, g * gw:(g + 1) * gw]
        ms = jnp.mean(sg * sg, axis=-1, keepdims=True)
        yn_ref[:, g * gw:(g + 1) * gw] = (sg * lax.rsqrt(ms + RMS_EPS) * nw_ref[:, g * gw:(g + 1) * gw]).astype(yn_ref.dtype)

    @pl.when(c == nc - 1)
    def _():
        cnew_ref[...] = xp_scr[head + valid_len - (CONV_W - 1):head + valid_len, :]
        hnew_ref[...] = h_scr[...]


def _ssd(xbc, dt, z, conv_w, conv_b, dt_bias, a_log, d_skip, norm_w, cbuf, h0, nb, nc, valid_len):
    cl = SSD_CHUNK
    assert valid_len == cl or nc == 1
    pairs = SSD_HEADS // 2
    pad = LANE - SSD_HEADS
    row = lambda b, c: (b * nc + c, 0, 0)
    const = lambda b, c: (0, 0)
    chunks = lambda a: a.reshape(nb * nc, valid_len, a.shape[-1])
    z = z if valid_len % 16 == 0 else z.astype(F32)
    return pl.pallas_call(
        functools.partial(_ssd_kernel, valid_len=valid_len),
        grid=(nb, nc),
        in_specs=[pl.BlockSpec((None, valid_len, CONV_DIM), row), pl.BlockSpec((None, valid_len, LANE), row),
                  pl.BlockSpec((None, valid_len, SSD_INNER), row),
                  pl.BlockSpec((CONV_W, CONV_DIM), const), pl.BlockSpec((1, CONV_DIM), const),
                  pl.BlockSpec((1, LANE), const), pl.BlockSpec((1, LANE), const),
                  pl.BlockSpec((1, SSD_INNER), const), pl.BlockSpec((1, SSD_INNER), const),
                  pl.BlockSpec((None, CONV_W - 1, CONV_DIM), lambda b, c: (b, 0, 0)),
                  pl.BlockSpec((None, pairs, LANE, D_STATE), lambda b, c: (b, 0, 0, 0))],
        out_specs=[pl.BlockSpec((None, valid_len, SSD_INNER), row),
                   pl.BlockSpec((None, CONV_W - 1, CONV_DIM), lambda b, c: (b, 0, 0)),
                   pl.BlockSpec((None, pairs, LANE, D_STATE), lambda b, c: (b, 0, 0, 0))],
        out_shape=[jax.ShapeDtypeStruct((nb * nc, valid_len, SSD_INNER), z.dtype),
                   jax.ShapeDtypeStruct((nb, CONV_W - 1, CONV_DIM), F32),
                   jax.ShapeDtypeStruct((nb, pairs, LANE, D_STATE), F32)],
        scratch_shapes=[pltpu.VMEM((8 + cl, CONV_DIM), F32), pltpu.VMEM((pairs, LANE, D_STATE), F32),
                        pltpu.VMEM((cl, SSD_INNER), F32)],
        compiler_params=_params("arbitrary", "arbitrary"),
        name="ssd",
    )(chunks(xbc), chunks(dt), chunks(z), conv_w, conv_b.reshape(1, CONV_DIM),
      jnp.pad(dt_bias, (0, pad)).reshape(1, LANE), jnp.pad(a_log, (0, pad)).reshape(1, LANE),
      jnp.repeat(d_skip, SSD_HEADDIM).reshape(1, SSD_INNER), norm_w.reshape(1, SSD_INNER),
      cbuf, h0.reshape(nb, pairs, LANE, D_STATE))


def _bias_kernel(rel_ref, tab_ref, o_ref, *, shift_far):
    rel = rel_ref[...]
    n = jnp.maximum(rel, 0)
    max_exact = N_BUCKETS // 2
    large = max_exact + (jnp.log(jnp.maximum(n, 1).astype(F32) / max_exact)
                         / math.log(MAX_DISTANCE / max_exact) * (N_BUCKETS - max_exact)).astype(jnp.int32)
    bucket = jnp.where(n < max_exact, n, jnp.minimum(large, N_BUCKETS - 1))
    outs = [jnp.zeros(rel.shape, F32) for _ in range(ATTN_HEADS)]
    for kb in range(N_BUCKETS):
        hit = bucket == kb
        for h in range(ATTN_HEADS):
            far = tab_ref[N_BUCKETS - 1, h] if shift_far else 0.0
            outs[h] = jnp.where(hit, (tab_ref[kb, h] - far) * LOG2E, outs[h])
    for h in range(ATTN_HEADS):
        o_ref[h] = jnp.where(rel >= 0, outs[h], -jnp.inf)


def _bias_tiles(rel, rel_bias, shift_far):
    r, c = rel.shape
    tr = min(r, 64)
    return pl.pallas_call(
        functools.partial(_bias_kernel, shift_far=shift_far),
        grid=(r // tr,),
        in_specs=[pl.BlockSpec((tr, c), lambda i: (i, 0)),
                  pl.BlockSpec(memory_space=pltpu.SMEM)],
        out_specs=pl.BlockSpec((ATTN_HEADS, tr, c), lambda i: (0, i, 0)),
        out_shape=jax.ShapeDtypeStruct((ATTN_HEADS, r, c), F32),
        compiler_params=_params("arbitrary"),
        name="t5bias",
    )(rel, rel_bias)


def _lam(lamp_ref):
    lp = lamp_ref[...]
    e1 = jnp.exp(jnp.sum(lp[0:1, :] * lp[1:2, :], axis=-1, keepdims=True))
    e2 = jnp.exp(jnp.sum(lp[2:3, :] * lp[3:4, :], axis=-1, keepdims=True))
    return e1 - e2 + LAM_INIT


def _subln(o, sw):
    ms = jnp.mean(o * o, axis=-1, keepdims=True)
    return o * lax.rsqrt(ms + RMS_EPS) * sw * (1.0 - LAM_INIT)


ATTN_TQ = 512
ATTN_STRIP = 512
ATTN_TK = 512
ATTN_D_MIN = 1 - ATTN_STRIP // ATTN_TK
ATTN_D_FAR = -(-(MAX_DISTANCE + ATTN_TK - 1) // ATTN_TK)


def _attn_kernel(q_ref, k_ref, v_ref, bias_ref, lamp_ref, sw_ref, o_ref, m_scr, l_scr, a_scr, *, tq, strip, tk):
    qi = pl.program_id(2)
    lane_lo = lax.broadcasted_iota(jnp.int32, (strip, LANE), 1) < ATTN_HEAD_DIM
    reps = tk // LANE
    lam = _lam(lamp_ref)
    for st in range(tq // strip):
        q = q_ref[st * strip:(st + 1) * strip, :]
        zero = jnp.zeros_like(q)
        qm = (jnp.where(lane_lo, q, zero), jnp.where(lane_lo, zero, q))
        m_scr[...] = jnp.full(m_scr.shape, -jnp.inf, F32)
        l_scr[...] = jnp.zeros(l_scr.shape, F32)
        a_scr[...] = jnp.zeros(a_scr.shape, F32)
        row0 = qi * (tq // strip) + st
        ntiles = (row0 * strip + strip - 1) // tk + 1

        def tile(j, near):
            off = pl.multiple_of(j * tk, tk)
            kt = k_ref[pl.ds(off, tk), :]
            vt = v_ref[pl.ds(off, tk), :]
            for mi in range(2):
                sc = _dot_nt(qm[mi], kt)
                if near:
                    sc = sc + bias_ref[row0 * (strip // tk) - j - ATTN_D_MIN]
                m_prev = m_scr[mi]
                m_new = jnp.maximum(m_prev, jnp.max(sc, axis=-1, keepdims=True))
                p = jnp.exp2(sc - jnp.concatenate([m_new] * reps, axis=1))
                alpha = jnp.exp2(m_prev - m_new)
                l_scr[mi] = alpha * l_scr[mi] + jnp.sum(p, axis=-1, keepdims=True)
                a_scr[mi] = alpha * a_scr[mi] + _dot(p.astype(BF16), vt)
                m_scr[mi] = m_new

        nfar = jnp.maximum(row0 * (strip // tk) - ATTN_D_FAR + 1, 0)
        lax.fori_loop(0, nfar, lambda j, c: (tile(j, False), c)[1], 0)
        lax.fori_loop(nfar, ntiles, lambda j, c: (tile(j, True), c)[1], 0)
        o = a_scr[0] / l_scr[0] - lam * (a_scr[1] / l_scr[1])
        o_ref[st * strip:(st + 1) * strip, :] = _subln(o, sw_ref[...]).astype(BF16)


def _attn_prompt(q, kb, vb, rel_bias, lamp, subln_w, nb, seq):
    tq, strip, tk = ATTN_TQ, ATTN_STRIP, ATTN_TK
    assert seq % tq == 0 and tq % strip == 0 and strip % tk == 0
    nq = seq // tq
    nd = ATTN_D_FAR - ATTN_D_MIN
    rel = (jnp.arange(ATTN_D_MIN, ATTN_D_FAR, dtype=jnp.int32)[:, None, None] * tk
           + jnp.arange(strip, dtype=jnp.int32)[None, :, None] - jnp.arange(tk, dtype=jnp.int32)[None, None, :])
    bias = _bias_tiles(rel.reshape(-1, tk), rel_bias, True).reshape(ATTN_HEADS, nd, strip, tk)
    return pl.pallas_call(
        functools.partial(_attn_kernel, tq=tq, strip=strip, tk=tk),
        grid=(nb, ATTN_HEADS, nq),
        in_specs=[pl.BlockSpec((tq, LANE), lambda b, h, i: (b * nq + i, h)),
                  pl.BlockSpec((seq, LANE), lambda b, h, i: (b, h)),
                  pl.BlockSpec((seq, LANE), lambda b, h, i: (b, h)),
                  pl.BlockSpec((None, nd, strip, tk), lambda b, h, i: (h, 0, 0, 0)),
                  pl.BlockSpec((4, ATTN_HEAD_DIM), lambda b, h, i: (0, 0)),
                  pl.BlockSpec((1, ATTN_V_DIM), lambda b, h, i: (0, 0))],
        out_specs=pl.BlockSpec((tq, LANE), lambda b, h, i: (b * nq + i, h)),
        out_shape=jax.ShapeDtypeStruct((nb * seq, ATTN_HEADS * ATTN_V_DIM), BF16),
        scratch_shapes=[pltpu.VMEM((2, strip, LANE), F32)] * 3,
        compiler_params=_params("arbitrary", "arbitrary", "arbitrary"),
        name="attn_prompt",
    )(q, kb, vb, bias, lamp, subln_w.reshape(1, ATTN_V_DIM))


DECODE_PAGES = 8
DECODE_SEQS = 1


def _decode_kernel(pt_ref, q_ref, kn_ref, vn_ref, *rest, nq, pps, nbq):
    npg = nbq * pps
    ck_refs, cv_refs = rest[:npg], rest[npg:2 * npg]
    bpage_ref, bnew_ref, lamp_ref, sw_ref, o_ref, qa_scr, m_scr, l_scr, a_scr = rest[2 * npg:]
    j = pl.program_id(1)
    nsteps = pl.num_programs(1)
    cols = PAGE_SIZE * ATTN_HEADS

    @pl.when(j == 0)
    def _():
        lane_lo = lax.broadcasted_iota(jnp.int32, (nq, LANE), 1) < ATTN_HEAD_DIM
        nnew = nq * ATTN_HEADS
        zpad = jnp.zeros((LANE - nnew, LANE), F32)
        for bi in range(nbq):
            for h in range(ATTN_HEADS):
                qh = q_ref[bi, :, h * LANE:(h + 1) * LANE].astype(F32)
                qa_scr[bi, (2 * h) * nq:(2 * h + 1) * nq, :] = jnp.where(lane_lo, qh, 0.0)
                qa_scr[bi, (2 * h + 1) * nq:(2 * h + 2) * nq, :] = jnp.where(lane_lo, 0.0, qh)
            kn = jnp.concatenate([kn_ref[bi], zpad], axis=0).astype(BF16)
            vn = jnp.concatenate([vn_ref[bi], zpad], axis=0).astype(BF16)
            s = _dot_nt(qa_scr[bi].astype(BF16), kn) + bnew_ref[...]
            m = jnp.max(s, axis=-1, keepdims=True)
            p = jnp.exp2(s - m)
            m_scr[bi] = m
            l_scr[bi] = jnp.sum(p, axis=-1, keepdims=True)
            a_scr[bi] = _dot(p.astype(BF16), vn)

    for bi in range(nbq):
        qa = qa_scr[bi].astype(BF16)
        scores = []
        for i in range(pps):
            bias = bpage_ref[jnp.where(j == nsteps - 1, 1, 0)] if i == pps - 1 else bpage_ref[0]
            scores.append(_dot_nt(qa, ck_refs[bi * pps + i][...].reshape(cols, LANE).astype(BF16)) + bias)
        m_prev = m_scr[bi]
        m_new = m_prev
        for s in scores:
            m_new = jnp.maximum(m_new, jnp.max(s, axis=-1, keepdims=True))
        alpha = jnp.exp2(m_prev - m_new)
        l_new = alpha * l_scr[bi]
        a_new = alpha * a_scr[bi]
        for i, s in enumerate(scores):
            p = jnp.exp2(s - m_new)
            l_new = l_new + jnp.sum(p, axis=-1, keepdims=True)
            a_new = a_new + _dot(p.astype(BF16), cv_refs[bi * pps + i][...].reshape(cols, LANE).astype(BF16))
        l_scr[bi] = l_new
        a_scr[bi] = a_new
        m_scr[bi] = m_new

    @pl.when(j == nsteps - 1)
    def _():
        lam = _lam(lamp_ref)
        for bi in range(nbq):
            on = a_scr[bi] / l_scr[bi]
            for h in range(ATTN_HEADS):
                o = on[(2 * h) * nq:(2 * h + 1) * nq, :] - lam * on[(2 * h + 1) * nq:(2 * h + 2) * nq, :]
                o_ref[bi, :, h * LANE:(h + 1) * LANE] = _subln(o, sw_ref[...])


def _attn_decode(q, k_new, v_new, cache_k, cache_v, page_table, rel_bias, lamp, subln_w):
    nb, nq, _ = q.shape
    npages = page_table.shape[1]
    past = npages * PAGE_SIZE
    rows = ATTN_HEADS * 2 * nq
    cols = PAGE_SIZE * ATTN_HEADS
    nnew = nq * ATTN_HEADS
    assert nnew <= LANE and npages >= 2
    r_i = jnp.arange(rows, dtype=jnp.int32)
    c_i = jnp.arange(cols, dtype=jnp.int32)
    own = (r_i // (2 * nq))[:, None] == (c_i % ATTN_HEADS)[None, :]
    qpos = (past + r_i % nq)[None, :, None]
    kpos = (past + (jnp.arange(3, dtype=jnp.int32) - 2) * PAGE_SIZE)[:, None, None] + (c_i // ATTN_HEADS)[None, None, :]
    rel = jnp.where(own[None] & (kpos < past + nq), qpos - kpos, -1)
    by_head = _bias_tiles(rel.reshape(3 * rows, cols), rel_bias, False).reshape(ATTN_HEADS, 3, ATTN_HEADS, 2 * nq, cols)
    full = jnp.concatenate([by_head[h, :, h] for h in range(ATTN_HEADS)], axis=1)
    bias_pages = full[:2]
    bias_new = full[2][:, :LANE]
    pps = math.gcd(DECODE_PAGES, npages)
    nbq = math.gcd(DECODE_SEQS, nb)
    new_q = pl.BlockSpec((nbq, nq, ATTN_HEADS * LANE), lambda b, j, pt: (b, 0, 0))
    new_kv = pl.BlockSpec((nbq, nnew, LANE), lambda b, j, pt: (b, 0, 0))

    def page(bi, i):
        return pl.BlockSpec((None, PAGE_SIZE, ATTN_HEADS, LANE),
                            lambda b, j, pt: (pt[b * nbq + bi, j * pps + i], 0, 0, 0))

    pages = [page(bi, i) for bi in range(nbq) for i in range(pps)]
    grid_spec = pltpu.PrefetchScalarGridSpec(
        num_scalar_prefetch=1,
        grid=(nb // nbq, npages // pps),
        in_specs=[new_q, new_kv, new_kv, *pages, *pages,
                  pl.BlockSpec((2, rows, cols), lambda b, j, pt: (0, 0, 0)),
                  pl.BlockSpec((rows, LANE), lambda b, j, pt: (0, 0)),
                  pl.BlockSpec((4, ATTN_HEAD_DIM), lambda b, j, pt: (0, 0)),
                  pl.BlockSpec((1, ATTN_V_DIM), lambda b, j, pt: (0, 0))],
        out_specs=new_q,
        scratch_shapes=[pltpu.VMEM((nbq, rows, LANE), F32), pltpu.VMEM((nbq, rows, 1), F32),
                        pltpu.VMEM((nbq, rows, 1), F32), pltpu.VMEM((nbq, rows, LANE), F32)])
    return pl.pallas_call(
        functools.partial(_decode_kernel, nq=nq, pps=pps, nbq=nbq),
        grid_spec=grid_spec,
        out_shape=jax.ShapeDtypeStruct((nb, nq, ATTN_HEADS * ATTN_V_DIM), F32),
        compiler_params=_params("arbitrary", "arbitrary"),
        name="attn_decode",
    )(page_table, q, k_new, v_new, *([cache_k] * len(pages)), *([cache_v] * len(pages)), bias_pages, bias_new, lamp,
      subln_w.reshape(1, ATTN_V_DIM))


HALF = D_MODEL // 2
PACKED = jnp.int32


def _pack_rows(x):
    word = pltpu.pack_elementwise([x[:, :HALF], x[:, HALF:]], packed_dtype=BF16)
    return lax.bitcast_convert_type(word, PACKED)


def _unpack_rows(p):
    halves = [pltpu.unpack_elementwise(p, index=i, packed_dtype=BF16, unpacked_dtype=F32) for i in range(2)]
    return jnp.concatenate(halves, axis=1)


def _layer_norm(r, g, b):
    mu = jnp.mean(r, axis=-1, keepdims=True)
    d = r - mu
    var = jnp.mean(d * d, axis=-1, keepdims=True)
    return d * lax.rsqrt(var + LN_EPS) * g + b


def _merge_kernel(yn_ref, on_ref, ga_ref, gb_ref, x_ref, g1_ref, sc2_ref, sh2_ref, wa_ref, wb_ref, wo_ref,
                  lg_ref, lb_ref, wr_ref, x1_ref, u2p_ref, logit_ref):
    ba = _dot(yn_ref[...].astype(BF16), wa_ref[...])
    bb = _dot(on_ref[...].astype(BF16), wb_ref[...])
    merged = jax.nn.sigmoid(ga_ref[...].astype(F32)) * ba + jax.nn.sigmoid(gb_ref[...].astype(F32)) * bb
    t = _dot(merged.astype(BF16), wo_ref[...])
    x1 = _layer_norm(ALPHA * x_ref[...] + g1_ref[...] * t, lg_ref[...], lb_ref[...])
    x1_ref[...] = x1
    u2 = x1 * (1.0 + sc2_ref[...]) + sh2_ref[...]
    u2p_ref[...] = _pack_rows(u2)
    u_hi = u2.astype(BF16)
    u_lo = (u2 - u_hi.astype(F32)).astype(BF16)
    by_hi = _dot_nt(wr_ref[...], u_hi)
    logit_ref[...] = by_hi[:N_EXPERTS] + by_hi[N_EXPERTS:] + _dot_nt(wr_ref[:N_EXPERTS, :], u_lo)


def _mod_spec(mod, t, tm, rows_per_mod):
    if mod.shape[1] != 1:
        return mod.reshape(t, D_MODEL), pl.BlockSpec((tm, D_MODEL), lambda i: (i, 0))
    assert rows_per_mod % tm == 0
    return mod, pl.BlockSpec((None, 1, D_MODEL), lambda i: (i // (rows_per_mod // tm), 0, 0))


def _merge(yn, on, ga, gb, x, g1, sc2, sh2, wa, wb, wo, ln_g, ln_b, w_router, rows_per_mod):
    t = x.shape[0]
    tm = min(256, t)
    rows = pl.BlockSpec((tm, D_MODEL), lambda i: (i, 0))
    wspec = pl.BlockSpec((D_MODEL, D_MODEL), lambda i: (0, 0))
    vec = pl.BlockSpec((1, D_MODEL), lambda i: (0, 0))
    g1a, mspec = _mod_spec(g1, t, tm, rows_per_mod)
    sc2a, _ = _mod_spec(sc2, t, tm, rows_per_mod)
    sh2a, _ = _mod_spec(sh2, t, tm, rows_per_mod)
    wr_t = w_router.T
    wr_hi = wr_t.astype(BF16)
    return pl.pallas_call(
        _merge_kernel,
        grid=(t // tm,),
        in_specs=[rows, rows, rows, rows, rows, mspec, mspec, mspec, wspec, wspec, wspec, vec, vec,
                  pl.BlockSpec((2 * N_EXPERTS, D_MODEL), lambda i: (0, 0))],
        out_specs=[rows, pl.BlockSpec((tm, HALF), lambda i: (i, 0)), pl.BlockSpec((N_EXPERTS, tm), lambda i: (0, i))],
        out_shape=[jax.ShapeDtypeStruct((t, D_MODEL), F32), jax.ShapeDtypeStruct((t, HALF), PACKED),
                   jax.ShapeDtypeStruct((N_EXPERTS, t), F32)],
        compiler_params=_params("arbitrary"),
        name="merge",
    )(yn, on, ga, gb, x, g1a, sc2a, sh2a, wa, wb, wo, ln_g.reshape(1, D_MODEL), ln_b.reshape(1, D_MODEL),
      jnp.concatenate([wr_hi, (wr_t - wr_hi.astype(F32)).astype(BF16)], axis=0))


def _router_kernel(logit_ref, br_ref, e_ref, w_ref, r_ref, cnt_ref, carry_scr, tri_scr, *, tr):
    i = pl.program_id(0)

    @pl.when(i == 0)
    def _():
        carry_scr[...] = jnp.zeros_like(carry_scr)
        a = lax.broadcasted_iota(jnp.int32, (tr, tr), 0)
        b = lax.broadcasted_iota(jnp.int32, (tr, tr), 1)
        tri_scr[...] = (a < b).astype(BF16)

    scores = jax.nn.sigmoid(logit_ref[...])
    biased = scores + br_ref[...]
    ninf = -jnp.inf

    b3 = biased.reshape(N_EXPERT_GROUPS, GROUP_SIZE, tr)
    j3 = lax.broadcasted_iota(jnp.int32, b3.shape, 1).astype(F32)
    top1 = jnp.max(b3, axis=1, keepdims=True)
    first = jnp.min(jnp.where(b3 == top1, j3, float(GROUP_SIZE)), axis=1, keepdims=True)
    top2 = jnp.max(jnp.where(j3 == first, ninf, b3), axis=1, keepdims=True)
    gscore = (top1 + top2).reshape(N_EXPERT_GROUPS, tr)
    gi = lax.broadcasted_iota(jnp.int32, gscore.shape, 0).astype(F32)
    gsel = jnp.zeros(gscore.shape, F32)
    for _ in range(TOPK_GROUPS):
        mx = jnp.max(gscore, axis=0, keepdims=True)
        pick = gi == jnp.min(jnp.where(gscore == mx, gi, float(N_EXPERT_GROUPS)), axis=0, keepdims=True)
        gsel = jnp.where(pick, 1.0, gsel)
        gscore = jnp.where(pick, ninf, gscore)
    emask = jnp.broadcast_to(gsel.reshape(N_EXPERT_GROUPS, 1, tr), b3.shape).reshape(N_EXPERTS, tr)
    masked = jnp.where(emask > 0.5, biased, ninf)

    ei = lax.broadcasted_iota(jnp.int32, masked.shape, 0).astype(F32)
    picked = jnp.zeros(masked.shape, F32)
    idxs, wsel = [], []
    for _ in range(TOP_K):
        mx = jnp.max(masked, axis=0, keepdims=True)
        idx = jnp.min(jnp.where(masked == mx, ei, float(N_EXPERTS)), axis=0, keepdims=True)
        pick = ei == idx
        idxs.append(idx)
        wsel.append(jnp.sum(jnp.where(pick, scores, 0.0), axis=0, keepdims=True))
        picked = jnp.where(pick, 1.0, picked)
        masked = jnp.where(pick, ninf, masked)
    wall = jnp.concatenate(wsel, axis=0)
    w_ref[...] = wall / jnp.sum(wall, axis=0, keepdims=True) * ROUTED_SCALE
    e_ref[...] = jnp.concatenate(idxs, axis=0).astype(jnp.int32)

    rank = carry_scr[:, 0:1] + _dot(picked.astype(BF16), tri_scr[...])
    r_ref[...] = jnp.concatenate(
        [jnp.sum(jnp.where(ei == idx, rank, 0.0), axis=0, keepdims=True) for idx in idxs], axis=0).astype(jnp.int32)
    total = carry_scr[...] + jnp.sum(picked, axis=1, keepdims=True)
    carry_scr[...] = total
    cnt_ref[...] = total.astype(jnp.int32)


def _router(logits, b_router):
    t = logits.shape[1]
    tr = min(512, t)
    tok = pl.BlockSpec((TOP_K, tr), lambda i: (0, i))
    e_t, w_t, r_t, cnt = pl.pallas_call(
        functools.partial(_router_kernel, tr=tr),
        grid=(t // tr,),
        in_specs=[pl.BlockSpec((N_EXPERTS, tr), lambda i: (0, i)),
                  pl.BlockSpec((N_EXPERTS, 1), lambda i: (0, 0))],
        out_specs=[tok, tok, tok, pl.BlockSpec((N_EXPERTS, LANE), lambda i: (0, 0))],
        out_shape=[jax.ShapeDtypeStruct((TOP_K, t), jnp.int32), jax.ShapeDtypeStruct((TOP_K, t), F32),
                   jax.ShapeDtypeStruct((TOP_K, t), jnp.int32), jax.ShapeDtypeStruct((N_EXPERTS, LANE), jnp.int32)],
        scratch_shapes=[pltpu.VMEM((N_EXPERTS, LANE), F32), pltpu.VMEM((tr, tr), BF16)],
        compiler_params=_params("arbitrary"),
        name="router",
    )(logits, b_router.reshape(N_EXPERTS, 1))
    return e_t, w_t, r_t, cnt


def _plan_kernel(e_ref, r_ref, cnt_ref, dest_ref, be_ref, bv_ref, nu_ref, *, br, nblk_pad):
    cnt = cnt_ref[...]
    shift = br.bit_length() - 1
    padded = lax.shift_left(lax.shift_right_logical(cnt + (br - 1), shift), shift)
    ea = lax.broadcasted_iota(jnp.int32, (N_EXPERTS, N_EXPERTS), 0)
    eb = lax.broadcasted_iota(jnp.int32, (N_EXPERTS, N_EXPERTS), 1)
    pends = jnp.dot((eb <= ea).astype(F32), padded.astype(F32), preferred_element_type=F32,
                    precision=HIGHEST).astype(jnp.int32)
    pstart = pends - padded

    e = e_ref[...]
    first_row = jnp.zeros(e.shape, jnp.int32)
    for x in range(N_EXPERTS):
        first_row = jnp.where(e == x, pstart[x:x + 1, 0:1], first_row)
    dest_ref[...] = first_row + r_ref[...]

    @pl.when(pl.program_id(0) == 0)
    def _():
        esub = lax.broadcasted_iota(jnp.int32, (N_EXPERTS, LANE), 0)
        real_end = (pstart + cnt).astype(F32)
        for c in range(nblk_pad // LANE):
            bstart = (lax.broadcasted_iota(jnp.int32, (1, LANE), 1) + c * LANE) * br
            be = jnp.minimum(jnp.sum((pends <= bstart).astype(F32), axis=0, keepdims=True),
                             float(N_EXPERTS - 1)).astype(jnp.int32)
            end = jnp.sum(jnp.where(esub == be, real_end, 0.0), axis=0, keepdims=True).astype(jnp.int32)
            be_ref[:, c * LANE:(c + 1) * LANE] = be
            bv_ref[:, c * LANE:(c + 1) * LANE] = jnp.clip(end - bstart, 0, br)
        nu_ref[...] = lax.shift_right_logical(pends[N_EXPERTS - 1:N_EXPERTS, :], shift)


def _plan(e_t, r_t, cnt, br, nblk):
    t = e_t.shape[1]
    tc = min(2048, t)
    nblk_pad = -(-nblk // LANE) * LANE
    tok = pl.BlockSpec((TOP_K, tc), lambda i: (0, i))
    blk = pl.BlockSpec((1, nblk_pad), lambda i: (0, 0))
    dest, be, bv, nu = pl.pallas_call(
        functools.partial(_plan_kernel, br=br, nblk_pad=nblk_pad),
        grid=(t // tc,),
        in_specs=[tok, tok, pl.BlockSpec((N_EXPERTS, LANE), lambda i: (0, 0))],
        out_specs=[tok, blk, blk, pl.BlockSpec((1, LANE), lambda i: (0, 0))],
        out_shape=[jax.ShapeDtypeStruct((TOP_K, t), jnp.int32), jax.ShapeDtypeStruct((1, nblk_pad), jnp.int32),
                   jax.ShapeDtypeStruct((1, nblk_pad), jnp.int32), jax.ShapeDtypeStruct((1, LANE), jnp.int32)],
        compiler_params=_params("arbitrary"),
        name="moe_plan",
    )(e_t, r_t, cnt)
    return dest, be[0, :nblk], bv[0, :nblk], nu[0, :1]


def _row_copy(src, s, dst, d, sem):
    return pltpu.make_async_copy(src.at[pl.ds(s, 1), :], dst.at[pl.ds(d, 1), :], sem)


def _dispatch_kernel(dest_ref, u_ref, xs_ref, sem, *, tm):
    def start(r, carry):
        for k in range(TOP_K):
            _row_copy(u_ref, r, xs_ref, dest_ref[k, r], sem).start()
        return carry

    def wait(r, carry):
        for k in range(TOP_K):
            _row_copy(u_ref, r, xs_ref, dest_ref[k, r], sem).wait()
        return carry

    lax.fori_loop(0, tm, start, 0)
    lax.fori_loop(0, tm, wait, 0)


def _dispatch(u2p, dest_t, n_rows):
    t = u2p.shape[0]
    tm = min(256, t)
    nt = t // tm
    dest_blocks = dest_t.reshape(TOP_K, nt, tm).transpose(1, 0, 2)
    return pl.pallas_call(
        functools.partial(_dispatch_kernel, tm=tm),
        grid=(nt,),
        in_specs=[pl.BlockSpec((None, TOP_K, tm), lambda i: (i, 0, 0), memory_space=pltpu.SMEM),
                  pl.BlockSpec((tm, HALF), lambda i: (i, 0))],
        out_specs=pl.BlockSpec(memory_space=pl.ANY),
        out_shape=jax.ShapeDtypeStruct((n_rows, HALF), PACKED),
        scratch_shapes=[pltpu.SemaphoreType.DMA(())],
        compiler_params=_params("arbitrary"),
        name="moe_dispatch",
    )(dest_blocks, u2p)


SC_CORES = 2
SC_SUBCORES = 16
SC_WINDOW = 128
SC_WORKERS = SC_CORES * SC_SUBCORES


def _sc_mesh():
    return plsc.VectorSubcoreMesh(core_axis_name="c", subcore_axis_name="s")


def _sc_worker():
    return lax.axis_index("s") * SC_CORES + lax.axis_index("c")


def _sc_scatter_rows(rows, dest_blocks, n_rows):
    t, width = rows.shape
    nchunks = t // SC_WINDOW // SC_WORKERS

    def body(rows_hbm, dest_hbm, out_hbm, idx_v, rows_v, sem):
        wid = _sc_worker()

        @pl.loop(0, nchunks)
        def _(c):
            chunk = wid * nchunks + c
            pltpu.sync_copy(dest_hbm.at[chunk], idx_v)
            pltpu.sync_copy(rows_hbm.at[pl.ds(pl.multiple_of(chunk * SC_WINDOW, SC_WINDOW), SC_WINDOW)], rows_v)
            copies = [pltpu.async_copy(rows_v, out_hbm.at[idx_v.at[k]], sem) for k in range(TOP_K)]
            for cp in copies:
                cp.wait()

    return pl.kernel(
        body, out_type=jax.ShapeDtypeStruct((n_rows, width), rows.dtype), mesh=_sc_mesh(),
        scratch_types=[pltpu.VMEM((TOP_K, SC_WINDOW), jnp.int32), pltpu.VMEM((SC_WINDOW, width), rows.dtype),
                       pltpu.SemaphoreType.DMA],
        name="moe_sc_dispatch")(rows, dest_blocks)


def _sc_gather_rows(table, idx):
    n = idx.shape[0]
    width = table.shape[1]
    nchunks = n // SC_WINDOW // SC_WORKERS

    def body(table_hbm, idx_hbm, out_hbm, idx_v, rows_v, sem):
        wid = _sc_worker()

        @pl.loop(0, nchunks)
        def _(c):
            base = pl.multiple_of((wid * nchunks + c) * SC_WINDOW, SC_WINDOW)
            pltpu.sync_copy(idx_hbm.at[pl.ds(base, SC_WINDOW)], idx_v)
            pltpu.async_copy(table_hbm.at[idx_v], rows_v, sem).wait()
            pltpu.sync_copy(rows_v, out_hbm.at[pl.ds(base, SC_WINDOW)])

    return pl.kernel(
        body, out_type=jax.ShapeDtypeStruct((n, width), table.dtype), mesh=_sc_mesh(),
        scratch_types=[pltpu.VMEM((SC_WINDOW,), jnp.int32), pltpu.VMEM((SC_WINDOW, width), table.dtype),
                       pltpu.SemaphoreType.DMA],
        name="moe_sc_gather")(table, idx)


def _expert_kernel(be_ref, bv_ref, nu_ref, x_ref, wg_ref, wu_ref, wd_ref, after_ref, y_ref, wg_scr, wu_scr, wd_scr,
                   *, br):
    del after_ref
    i = pl.program_id(0)
    prev = be_ref[jnp.maximum(i - 1, 0)]

    @pl.when(jnp.logical_and(i < nu_ref[0], jnp.logical_or(i == 0, be_ref[i] != prev)))
    def _():
        wg_scr[...] = wg_ref[...].astype(BF16)
        wu_scr[...] = wu_ref[...].astype(BF16)
        wd_scr[...] = wd_ref[...].astype(BF16)

    @pl.when(i < nu_ref[0])
    def _():
        live = lax.broadcasted_iota(jnp.int32, (br, HALF), 0) < bv_ref[i]
        x = _unpack_rows(jnp.where(live, x_ref[...], 0)).astype(BF16)
        hcat = _silu(_dot(x, wg_scr[...])) * _dot(x, wu_scr[...])
        y_ref[...] = _pack_rows(_dot(hcat.astype(BF16), wd_scr[...]))


def _experts(x_sorted, blk_e, blk_valid, n_used, w_gate, w_up, w_down, br, after):
    n_rows = x_sorted.shape[0]
    nblk = n_rows // br

    def blk(i, be, bv, nu):
        return (jnp.minimum(i, nu[0] - 1), 0)

    grid_spec = pltpu.PrefetchScalarGridSpec(
        num_scalar_prefetch=3,
        grid=(nblk,),
        in_specs=[pl.BlockSpec((br, HALF), blk),
                  pl.BlockSpec((None, D_MODEL, D_EXPERT), lambda i, be, bv, nu: (be[i], 0, 0)),
                  pl.BlockSpec((None, D_MODEL, D_EXPERT), lambda i, be, bv, nu: (be[i], 0, 0)),
                  pl.BlockSpec((None, D_EXPERT, D_MODEL), lambda i, be, bv, nu: (be[i], 0, 0)),
                  pl.BlockSpec(after.shape, lambda i, be, bv, nu: (0,) * after.ndim)],
        out_specs=pl.BlockSpec((br, HALF), blk),
        scratch_shapes=[pltpu.VMEM((D_MODEL, D_EXPERT), BF16), pltpu.VMEM((D_MODEL, D_EXPERT), BF16),
                        pltpu.VMEM((D_EXPERT, D_MODEL), BF16)])
    return pl.pallas_call(
        functools.partial(_expert_kernel, br=br),
        grid_spec=grid_spec,
        out_shape=jax.ShapeDtypeStruct((n_rows, HALF), PACKED),
        compiler_params=_params("arbitrary"),
        name="moe_experts",
    )(blk_e, blk_valid, n_used, x_sorted, w_gate, w_up, w_down, after)


def _combine_kernel(dest_ref, ys_ref, w_ref, u_ref, x1_ref, g2_ref, sg_ref, su_ref, sd_ref, lg_ref, lb_ref,
                    o_ref, g_scr, sem, *, tm):
    def start(r, carry):
        for k in range(TOP_K):
            _row_copy(ys_ref, dest_ref[k, r], g_scr.at[k], r, sem).start()
        return carry

    def wait(r, carry):
        for k in range(TOP_K):
            _row_copy(ys_ref, dest_ref[k, r], g_scr.at[k], r, sem).wait()
        return carry

    lax.fori_loop(0, tm, start, 0)
    f = _shared_ffn(u_ref, sg_ref, su_ref, sd_ref)
    lax.fori_loop(0, tm, wait, 0)
    f = _add_routed(f, g_scr, w_ref[...])
    o_ref[...] = _layer_norm(ALPHA * x1_ref[...] + g2_ref[...] * f, lg_ref[...], lb_ref[...])


def _shared_ffn(u_ref, sg_ref, su_ref, sd_ref):
    ub = _unpack_rows(u_ref[...]).astype(BF16)
    hs = _silu(_dot(ub, sg_ref[...])) * _dot(ub, su_ref[...])
    return _dot(hs.astype(BF16), sd_ref[...])


def _add_routed(f, slots_ref, w):
    for k in range(TOP_K):
        f = f + _unpack_rows(slots_ref[k]) * w[:, k:k + 1]
    return f


def _combine_gathered_kernel(g_ref, w_ref, u_ref, x1_ref, g2_ref, sg_ref, su_ref, sd_ref, lg_ref, lb_ref, o_ref):
    f = _add_routed(_shared_ffn(u_ref, sg_ref, su_ref, sd_ref), g_ref, w_ref[...])
    o_ref[...] = _layer_norm(ALPHA * x1_ref[...] + g2_ref[...] * f, lg_ref[...], lb_ref[...])


def _combine(y_rows, dest_t, w_t, u2p, x1, g2, ws_gate, ws_up, ws_down, ln_g, ln_b, rows_per_mod, gathered):
    t = u2p.shape[0]
    tm = min(256 if gathered else 128, t)
    nt = t // tm
    rows = pl.BlockSpec((tm, D_MODEL), lambda i: (i, 0))
    vec = pl.BlockSpec((1, D_MODEL), lambda i: (0, 0))
    g2a, mspec = _mod_spec(g2, t, tm, rows_per_mod)
    common_specs = [pl.BlockSpec((tm, TOP_K), lambda i: (i, 0)),
                    pl.BlockSpec((tm, HALF), lambda i: (i, 0)), rows, mspec,
                    pl.BlockSpec((D_MODEL, D_SHARED), lambda i: (0, 0)),
                    pl.BlockSpec((D_MODEL, D_SHARED), lambda i: (0, 0)),
                    pl.BlockSpec((D_SHARED, D_MODEL), lambda i: (0, 0)),
                    vec, vec]
    common_args = (w_t.T, u2p, x1, g2a, ws_gate.astype(BF16), ws_up.astype(BF16), ws_down.astype(BF16),
                   ln_g.reshape(1, D_MODEL), ln_b.reshape(1, D_MODEL))
    if gathered:
        return pl.pallas_call(
            _combine_gathered_kernel,
            grid=(nt,),
            in_specs=[pl.BlockSpec((TOP_K, tm, HALF), lambda i: (0, i, 0))] + common_specs,
            out_specs=rows,
            out_shape=jax.ShapeDtypeStruct((t, D_MODEL), F32),
            compiler_params=_params("arbitrary"),
            name="moe_combine_gathered",
        )(y_rows.reshape(TOP_K, t, HALF), *common_args)
    dest_blocks = dest_t.reshape(TOP_K, nt, tm).transpose(1, 0, 2)
    return pl.pallas_call(
        functools.partial(_combine_kernel, tm=tm),
        grid=(nt,),
        in_specs=[pl.BlockSpec((None, TOP_K, tm), lambda i: (i, 0, 0), memory_space=pltpu.SMEM),
                  pl.BlockSpec(memory_space=pl.ANY)] + common_specs,
        out_specs=rows,
        out_shape=jax.ShapeDtypeStruct((t, D_MODEL), F32),
        scratch_shapes=[pltpu.VMEM((TOP_K, tm, HALF), PACKED), pltpu.SemaphoreType.DMA(())],
        compiler_params=_params("arbitrary"),
        name="moe_combine",
    )(dest_blocks, y_rows, *common_args)


def _moe(u2p, logits, x1, g2, p, rows_per_mod, br, after):
    t = u2p.shape[0]
    e_t, w_t, r_t, counts = _router(logits, p["b_router"])
    n_rows = (t * TOP_K + N_EXPERTS * (br - 1) + br - 1) // br * br
    dest_t, blk_e, blk_valid, n_used = _plan(e_t, r_t, counts, br, n_rows // br)
    on_sc = t % (SC_WORKERS * SC_WINDOW) == 0
    if on_sc:
        dest_blocks = dest_t.reshape(TOP_K, t // SC_WINDOW, SC_WINDOW).transpose(1, 0, 2)
        x_sorted = _sc_scatter_rows(u2p, dest_blocks, n_rows)
    else:
        x_sorted = _dispatch(u2p, dest_t, n_rows)
    y_rows = _experts(x_sorted, blk_e, blk_valid, n_used, p["w_gate"], p["w_up"], p["w_down"], br, after)
    if on_sc:
        y_rows = _sc_gather_rows(y_rows, dest_t.reshape(TOP_K * t))
    return _combine(y_rows, dest_t, w_t, u2p, x1, g2, p["ws_gate"], p["ws_up"], p["ws_down"],
                    p["ln2_g"], p["ln2_b"], rows_per_mod, on_sc)


def _cat_w_in(w_in):
    sizes = (SSD_INNER, CONV_DIM, SSD_HEADS, 1024, 1024, 1024, D_MODEL, D_MODEL)
    offs = [0]
    for s in sizes:
        offs.append(offs[-1] + s)
    z, xbc, dt, q, k, v, ga, gb = [w_in[:, offs[i]:offs[i + 1]] for i in range(8)]
    dt = jnp.pad(dt, ((0, 0), (0, SEG_END - SEG_DT - SSD_HEADS)))
    return jnp.concatenate([z, xbc, q, k, v, ga, gb, dt], axis=1).astype(BF16)


def kernel(x_prompt, x_sample, c_prompt, c_sample, cache_k, cache_v, page_table, state_conv, state_ssm, rel_bias, w_ada, b_ada, w_in, conv_w, conv_b, dt_bias, a_log, d_skip, ssd_norm_w, lam_q1, lam_k1, lam_q2, lam_k2, subln_w, w_br_ssd, w_br_attn, w_out, ln1_g, ln1_b, w_router, b_router, w_gate, w_up, w_down, ws_gate, ws_up, ws_down, ln2_g, ln2_b):
    assert w_in.shape[0] == DEPTH
    nbp, seq, _ = x_prompt.shape
    nbs, dseq, _ = x_sample.shape
    tp, ts = nbp * seq, nbs * dseq
    past = page_table.shape[1] * PAGE_SIZE
    cl = SSD_CHUNK

    pad_p = -nbp % 8
    c_all = jnp.concatenate([c_prompt, jnp.zeros((pad_p, D_MODEL), F32), jnp.repeat(c_sample, dseq, axis=0)], axis=0)
    mod = _adaln(c_all, w_ada[0], b_ada[0])
    mod_p = mod[:nbp].reshape(nbp, 1, 6, D_MODEL)
    mod_s = mod[nbp + pad_p:].reshape(nbs, dseq, 6, D_MODEL)
    mp = [mod_p[:, :, i] for i in range(6)]
    ms = [mod_s[:, :, i] for i in range(6)]

    w_cat = _cat_w_in(w_in[0])
    wa, wb, wo = w_br_ssd[0].astype(BF16), w_br_attn[0].astype(BF16), w_out[0].astype(BF16)
    lamp = jnp.stack([lam_q1[0], lam_k1[0], lam_q2[0], lam_k2[0]])
    moe_p = dict(b_router=b_router[0], w_gate=w_gate[0], w_up=w_up[0], w_down=w_down[0],
                 ws_gate=ws_gate[0], ws_up=ws_up[0], ws_down=ws_down[0], ln2_g=ln2_g[0], ln2_b=ln2_b[0])
    ssd_w = (conv_w[0], conv_b[0], dt_bias[0], a_log[0], d_skip[0], ssd_norm_w[0])

    xp = x_prompt.reshape(tp, D_MODEL)
    z, xbc, dt, q, k, v, kb, vb, ga, gb = _inproj(xp, mp[1], mp[0], w_cat, seq)
    yn, conv_p, h_p = _ssd(xbc, dt, z, *ssd_w, jnp.zeros((nbp, CONV_W - 1, CONV_DIM), F32),
                           jnp.zeros((nbp, SSD_HEADS, SSD_HEADDIM, D_STATE), F32), nbp, seq // cl, cl)
    yn = yn.reshape(tp, SSD_INNER)
    on = _attn_prompt(q, kb, vb, rel_bias, lamp, subln_w[0], nbp, seq)
    x1_p, u2p_p, logits_p = _merge(yn, on, ga, gb, xp, mp[2], mp[4], mp[3], wa, wb, wo, ln1_g[0], ln1_b[0],
                                   w_router[0], seq)
    k_prompt = k.reshape(1, nbp, seq, ATTN_HEADS, 2 * ATTN_HEAD_DIM)
    v_prompt = v.reshape(1, nbp, seq, ATTN_HEADS, ATTN_V_DIM)

    xs_ = x_sample.reshape(ts, D_MODEL)
    z, xbc, dt, q, k, v, kb, vb, ga, gb = _inproj(xs_, ms[1], ms[0], w_cat, dseq)
    yn, conv_s, h_s = _ssd(xbc, dt, z, *ssd_w, state_conv[0], state_ssm[0], nbs, 1, dseq)
    yn = yn.reshape(ts, SSD_INNER)
    on = _attn_decode(q.reshape(nbs, dseq, -1), k.reshape(nbs, dseq * ATTN_HEADS, LANE),
                      v.reshape(nbs, dseq * ATTN_HEADS, LANE), cache_k[0], cache_v[0], page_table,
                      rel_bias, lamp, subln_w[0])
    y_prompt = _moe(u2p_p, logits_p, x1_p, mp[5], moe_p, seq, 512, on[0, :, :LANE]).reshape(nbp, seq, D_MODEL)
    x1, u2p, logits = _merge(yn, on.reshape(ts, -1), ga, gb, xs_, ms[2], ms[4], ms[3], wa, wb, wo, ln1_g[0], ln1_b[0],
                             w_router[0], dseq)
    y_sample = _moe(u2p, logits, x1, ms[5], moe_p, dseq, 64, x1[:8, :LANE]).reshape(nbs, dseq, D_MODEL)
    k_sample = k.reshape(1, nbs, dseq, ATTN_HEADS, 2 * ATTN_HEAD_DIM)
    v_sample = v.reshape(1, nbs, dseq, ATTN_HEADS, ATTN_V_DIM)

    return (y_prompt, y_sample, k_prompt, v_prompt, conv_p[None], h_p.reshape(1, nbp, SSD_HEADS, SSD_HEADDIM, D_STATE),
            k_sample, v_sample, conv_s[None], h_s.reshape(1, nbs, SSD_HEADS, SSD_HEADDIM, D_STATE))
```

```python
import functools
import math

import jax
import jax.numpy as jnp
from jax import lax
from jax.experimental import pallas as pl
from jax.experimental.pallas import tpu as pltpu
from jax.experimental.pallas import tpu_sc as plsc

F32 = jnp.float32
BF16 = jnp.bfloat16
HIGHEST = lax.Precision.HIGHEST

D_MODEL = 1024
SSD_INNER = 1024
SSD_HEADDIM = 64
SSD_HEADS = 16
SSD_GROUPS = 2
D_STATE = 128
CONV_W = 4
CONV_DIM = SSD_INNER + 2 * SSD_GROUPS * D_STATE
SSD_CHUNK = 128
ATTN_HEADS = 8
ATTN_HEAD_DIM = 64
ATTN_V_DIM = 128
N_BUCKETS = 32
MAX_DISTANCE = 128
N_EXPERTS = 64
N_EXPERT_GROUPS = 8
GROUP_SIZE = N_EXPERTS // N_EXPERT_GROUPS
TOPK_GROUPS = 4
TOP_K = 8
D_EXPERT = 256
D_SHARED = 256
ROUTED_SCALE = 2.5
PAGE_SIZE = 128
DEPTH = 1
ALPHA = (2 * DEPTH) ** 0.25
LN_EPS = 1e-5
RMS_EPS = 1e-5
LAM_INIT = 0.8 - 0.6 * math.exp(-0.3 * 0)
LOG2E = math.log2(math.e)
LANE = 128
VMEM_LIMIT = 56 * 1024 * 1024

SEG_Z, SEG_XBC, SEG_Q, SEG_K, SEG_V, SEG_GA, SEG_GB, SEG_DT, SEG_END = (
    0, 1024, 2560, 3584, 4608, 5632, 6656, 7680, 7808)


def _silu(x):
    return x * jax.nn.sigmoid(x)


def _dot(a, b):
    return jnp.dot(a, b, preferred_element_type=F32)


def _dot_nt(a, b):
    return lax.dot_general(a, b, (((1,), (1,)), ((), ())), preferred_element_type=F32)


def _dot_tn(a, b):
    return lax.dot_general(a, b, (((0,), (0,)), ((), ())), preferred_element_type=F32)


def _params(*sem):
    return pltpu.CompilerParams(dimension_semantics=sem, vmem_limit_bytes=VMEM_LIMIT)


def _adaln_kernel(c_ref, w_ref, b_ref, o_ref):
    s = _silu(c_ref[...]).astype(BF16)
    o_ref[...] = _dot(s, w_ref[...].astype(BF16)) + b_ref[...]


def _adaln(c, w_ada, b_ada):
    r = c.shape[0]
    n = w_ada.shape[1]
    tn = 1024
    return pl.pallas_call(
        _adaln_kernel,
        grid=(n // tn,),
        in_specs=[pl.BlockSpec((r, D_MODEL), lambda j: (0, 0)),
                  pl.BlockSpec((D_MODEL, tn), lambda j: (0, j)),
                  pl.BlockSpec((1, tn), lambda j: (0, j))],
        out_specs=pl.BlockSpec((r, tn), lambda j: (0, j)),
        out_shape=jax.ShapeDtypeStruct((r, n), F32),
        compiler_params=_params("arbitrary"),
        name="adaln",
    )(c, w_ada, b_ada.reshape(1, n))


def _inproj_kernel(x_ref, sc_ref, sh_ref, w_ref, z_ref, xbc_ref, dt_ref, q_ref, k_ref, v_ref,
                   kb_ref, vb_ref, ga_ref, gb_ref):
    u = (x_ref[...] * (1.0 + sc_ref[...]) + sh_ref[...]).astype(BF16)

    def seg(a, b):
        return _dot(u, w_ref[:, a:b])

    z_ref[...] = seg(SEG_Z, SEG_XBC).astype(BF16)
    xbc_ref[...] = seg(SEG_XBC, SEG_Q)
    q_ref[...] = (seg(SEG_Q, SEG_K) * (ATTN_HEAD_DIM ** -0.5 * LOG2E)).astype(BF16)
    kk = seg(SEG_K, SEG_V)
    k_ref[...] = kk
    kb_ref[...] = kk.astype(BF16)
    vv = seg(SEG_V, SEG_GA)
    v_ref[...] = vv
    vb_ref[...] = vv.astype(BF16)
    ga_ref[...] = seg(SEG_GA, SEG_GB).astype(BF16)
    gb_ref[...] = seg(SEG_GB, SEG_DT).astype(BF16)
    dt_ref[...] = seg(SEG_DT, SEG_END)


def _inproj(x, sc, sh, w_cat, rows_per_mod):
    t = x.shape[0]
    tm = min(256, t)
    per_row = sc.shape[1] != 1
    if per_row:
        sc2 = sc.reshape(t, D_MODEL)
        sh2 = sh.reshape(t, D_MODEL)
        mod_spec = pl.BlockSpec((tm, D_MODEL), lambda i: (i, 0))
    else:
        assert rows_per_mod % tm == 0
        sc2, sh2 = sc, sh
        mod_spec = pl.BlockSpec((None, 1, D_MODEL), lambda i: (i // (rows_per_mod // tm), 0, 0))

    def rows(width):
        return pl.BlockSpec((tm, width), lambda i: (i, 0))

    widths = (1024, CONV_DIM, LANE, 1024, 1024, 1024, 1024, 1024, 1024, 1024)
    dtypes = (BF16, F32, F32, BF16, F32, F32, BF16, BF16, BF16, BF16)
    return pl.pallas_call(
        _inproj_kernel,
        grid=(t // tm,),
        in_specs=[rows(D_MODEL), mod_spec, mod_spec,
                  pl.BlockSpec((D_MODEL, SEG_END), lambda i: (0, 0), pipeline_mode=pl.Buffered(1))],
        out_specs=[rows(w) for w in widths],
        out_shape=[jax.ShapeDtypeStruct((t, w), d) for w, d in zip(widths, dtypes)],
        compiler_params=_params("arbitrary"),
        name="inproj",
    )(x, sc2, sh2, w_cat)


def _ssd_kernel(xbc_ref, dt_ref, z_ref, cw_ref, cb_ref, dtb_ref, alog_ref, dsk_ref, nw_ref,
                cbuf_ref, h0_ref, yn_ref, cnew_ref, hnew_ref, xp_scr, h_scr, y_scr, *, valid_len):
    c = pl.program_id(1)
    nc = pl.num_programs(1)
    cl = SSD_CHUNK
    head = 8

    @pl.when(c == 0)
    def _():
        xp_scr[head - (CONV_W - 1):head, :] = cbuf_ref[...]
        h_scr[...] = h0_ref[...]

    @pl.when(c > 0)
    def _():
        xp_scr[head - (CONV_W - 1):head, :] = xp_scr[head + cl - (CONV_W - 1):head + cl, :]

    xp_scr[head:head + valid_len, :] = xbc_ref[...]
    if valid_len < cl:
        xp_scr[head + valid_len:head + cl, :] = jnp.zeros((cl - valid_len, CONV_DIM), F32)

    acc = xp_scr[head:head + cl, :] * cw_ref[CONV_W - 1:CONV_W, :]
    for j in range(CONV_W - 1):
        lo = head - (CONV_W - 1) + j
        acc = acc + xp_scr[lo:lo + cl, :] * cw_ref[j:j + 1, :]
    xc = _silu(acc + cb_ref[...])
    xs = xc[:, :SSD_INNER]
    bmat = [xc[:, SSD_INNER + g * D_STATE:SSD_INNER + (g + 1) * D_STATE].astype(BF16)
            for g in range(SSD_GROUPS)]
    coff = SSD_INNER + SSD_GROUPS * D_STATE
    cmat = [xc[:, coff + g * D_STATE:coff + (g + 1) * D_STATE].astype(BF16) for g in range(SSD_GROUPS)]

    li = lax.broadcasted_iota(jnp.int32, (cl, cl), 0)
    si = lax.broadcasted_iota(jnp.int32, (cl, cl), 1)
    causal = li >= si
    lane_lo = si < SSD_HEADDIM
    sub_lo = li < SSD_HEADDIM

    def rows_padded(v):
        return v if valid_len == cl else jnp.concatenate([v, jnp.zeros((cl - valid_len, v.shape[1]), F32)], axis=0)

    dpre = rows_padded(dt_ref[...]) + dtb_ref[...]
    dtv = jnp.maximum(dpre, 0.0) + jnp.log(1.0 + jnp.exp(-jnp.abs(dpre)))
    if valid_len < cl:
        dtv = jnp.where(li < valid_len, dtv, 0.0)
    da = dtv * (-jnp.exp(alog_ref[...]))
    tril = causal.astype(F32)
    acum = jnp.dot(tril, da, preferred_element_type=F32, precision=HIGHEST)
    acum_t = acum.T
    dt_t = dtv.T
    last = acum[cl - 1:cl, :]
    e_all = jnp.exp(acum)
    w_all = jnp.exp(last - acum) * dtv
    e_last = jnp.exp(last)

    cb = [_dot_nt(cmat[g], bmat[g]) for g in range(SSD_GROUPS)]

    def colb(a, h):
        return jnp.broadcast_to(a[:, h:h + 1], (cl, cl))

    pairs = SSD_HEADS // 2
    for p in range(pairs):
        h0, h1 = 2 * p, 2 * p + 1
        g = h0 // (SSD_HEADS // SSD_GROUPS)
        xpair = xs[:, p * LANE:(p + 1) * LANE]
        x_lo = jnp.where(lane_lo, xpair, 0.0).astype(BF16)
        x_hi = jnp.where(lane_lo, 0.0, xpair).astype(BF16)
        ydiag = None
        for hh, xm in ((h0, x_lo), (h1, x_hi)):
            seg = colb(acum, hh) - acum_t[hh:hh + 1, :]
            dec = jnp.exp(jnp.where(causal, seg, -jnp.inf))
            m = (cb[g] * dec * dt_t[hh:hh + 1, :]).astype(BF16)
            part = _dot(m, xm)
            ydiag = part if ydiag is None else ydiag + part
        ecol = jnp.where(lane_lo, colb(e_all, h0), colb(e_all, h1))
        hp = h_scr[p]
        yoff = _dot_nt(cmat[g], hp.astype(BF16)) * ecol
        wcol = jnp.where(lane_lo, colb(w_all, h0), colb(w_all, h1))
        dx = (xpair * wcol).astype(BF16)
        st = _dot_tn(dx, bmat[g])
        hdec = jnp.where(sub_lo, e_last[:, h0:h0 + 1], e_last[:, h1:h1 + 1])
        h_scr[p] = hp * hdec + st
        y_scr[:, p * LANE:(p + 1) * LANE] = ydiag + yoff + dsk_ref[:, p * LANE:(p + 1) * LANE] * xpair

    zf = z_ref[...].astype(F32)
    gated = y_scr[0:valid_len, :] * _silu(zf)
    gw = SSD_INNER // SSD_GROUPS
    for g in range(SSD_GROUPS):
        sg = gated[:, g * gw:(g + 1) * gw]
        ms = jnp.mean(sg * sg, axis=-1, keepdims=True)
        yn_ref[:, g * gw:(g + 1) * gw] = (sg * lax.rsqrt(ms + RMS_EPS) * nw_ref[:, g * gw:(g + 1) * gw]).astype(yn_ref.dtype)

    @pl.when(c == nc - 1)
    def _():
        cnew_ref[...] = xp_scr[head + valid_len - (CONV_W - 1):head + valid_len, :]
        hnew_ref[...] = h_scr[...]


def _ssd(xbc, dt, z, conv_w, conv_b, dt_bias, a_log, d_skip, norm_w, cbuf, h0, nb, nc, valid_len):
    cl = SSD_CHUNK
    assert valid_len == cl or nc == 1
    pairs = SSD_HEADS // 2
    pad = LANE - SSD_HEADS
    row = lambda b, c: (b * nc + c, 0, 0)
    const = lambda b, c: (0, 0)
    chunks = lambda a: a.reshape(nb * nc, valid_len, a.shape[-1])
    z = z if valid_len % 16 == 0 else z.astype(F32)
    return pl.pallas_call(
        functools.partial(_ssd_kernel, valid_len=valid_len),
        grid=(nb, nc),
        in_specs=[pl.BlockSpec((None, valid_len, CONV_DIM), row), pl.BlockSpec((None, valid_len, LANE), row),
                  pl.BlockSpec((None, valid_len, SSD_INNER), row),
                  pl.BlockSpec((CONV_W, CONV_DIM), const), pl.BlockSpec((1, CONV_DIM), const),
                  pl.BlockSpec((1, LANE), const), pl.BlockSpec((1, LANE), const),
                  pl.BlockSpec((1, SSD_INNER), const), pl.BlockSpec((1, SSD_INNER), const),
                  pl.BlockSpec((None, CONV_W - 1, CONV_DIM), lambda b, c: (b, 0, 0)),
                  pl.BlockSpec((None, pairs, LANE, D_STATE), lambda b, c: (b, 0, 0, 0))],
        out_specs=[pl.BlockSpec((None, valid_len, SSD_INNER), row),
                   pl.BlockSpec((None, CONV_W - 1, CONV_DIM), lambda b, c: (b, 0, 0)),
                   pl.BlockSpec((None, pairs, LANE, D_STATE), lambda b, c: (b, 0, 0, 0))],
        out_shape=[jax.ShapeDtypeStruct((nb * nc, valid_len, SSD_INNER), z.dtype),
                   jax.ShapeDtypeStruct((nb, CONV_W - 1, CONV_DIM), F32),
                   jax.ShapeDtypeStruct((nb, pairs, LANE, D_STATE), F32)],
        scratch_shapes=[pltpu.VMEM((8 + cl, CONV_DIM), F32), pltpu.VMEM((pairs, LANE, D_STATE), F32),
                        pltpu.VMEM((cl, SSD_INNER), F32)],
        compiler_params=_params("arbitrary", "arbitrary"),
        name="ssd",
    )(chunks(xbc), chunks(dt), chunks(z), conv_w, conv_b.reshape(1, CONV_DIM),
      jnp.pad(dt_bias, (0, pad)).reshape(1, LANE), jnp.pad(a_log, (0, pad)).reshape(1, LANE),
      jnp.repeat(d_skip, SSD_HEADDIM).reshape(1, SSD_INNER), norm_w.reshape(1, SSD_INNER),
      cbuf, h0.reshape(nb, pairs, LANE, D_STATE))


def _bias_kernel(rel_ref, tab_ref, o_ref, *, shift_far):
    rel = rel_ref[...]
    n = jnp.maximum(rel, 0)
    max_exact = N_BUCKETS // 2
    large = max_exact + (jnp.log(jnp.maximum(n, 1).astype(F32) / max_exact)
                         / math.log(MAX_DISTANCE / max_exact) * (N_BUCKETS - max_exact)).astype(jnp.int32)
    bucket = jnp.where(n < max_exact, n, jnp.minimum(large, N_BUCKETS - 1))
    outs = [jnp.zeros(rel.shape, F32) for _ in range(ATTN_HEADS)]
    for kb in range(N_BUCKETS):
        hit = bucket == kb
        for h in range(ATTN_HEADS):
            far = tab_ref[N_BUCKETS - 1, h] if shift_far else 0.0
            outs[h] = jnp.where(hit, (tab_ref[kb, h] - far) * LOG2E, outs[h])
    for h in range(ATTN_HEADS):
        o_ref[h] = jnp.where(rel >= 0, outs[h], -jnp.inf)


def _bias_tiles(rel, rel_bias, shift_far):
    r, c = rel.shape
    tr = min(r, 64)
    return pl.pallas_call(
        functools.partial(_bias_kernel, shift_far=shift_far),
        grid=(r // tr,),
        in_specs=[pl.BlockSpec((tr, c), lambda i: (i, 0)),
                  pl.BlockSpec(memory_space=pltpu.SMEM)],
        out_specs=pl.BlockSpec((ATTN_HEADS, tr, c), lambda i: (0, i, 0)),
        out_shape=jax.ShapeDtypeStruct((ATTN_HEADS, r, c), F32),
        compiler_params=_params("arbitrary"),
        name="t5bias",
    )(rel, rel_bias)


def _lam(lamp_ref):
    lp = lamp_ref[...]
    e1 = jnp.exp(jnp.sum(lp[0:1, :] * lp[1:2, :], axis=-1, keepdims=True))
    e2 = jnp.exp(jnp.sum(lp[2:3, :] * lp[3:4, :], axis=-1, keepdims=True))
    return e1 - e2 + LAM_INIT


def _subln(o, sw):
    ms = jnp.mean(o * o, axis=-1, keepdims=True)
    return o * lax.rsqrt(ms + RMS_EPS) * sw * (1.0 - LAM_INIT)


ATTN_TQ = 512
ATTN_STRIP = 512
ATTN_TK = 512
ATTN_D_MIN = 1 - ATTN_STRIP // ATTN_TK
ATTN_D_FAR = -(-(MAX_DISTANCE + ATTN_TK - 1) // ATTN_TK)


def _attn_kernel(q_ref, k_ref, v_ref, bias_ref, lamp_ref, sw_ref, o_ref, m_scr, l_scr, a_scr, *, tq, strip, tk):
    qi = pl.program_id(2)
    lane_lo = lax.broadcasted_iota(jnp.int32, (strip, LANE), 1) < ATTN_HEAD_DIM
    reps = tk // LANE
    lam = _lam(lamp_ref)
    for st in range(tq // strip):
        q = q_ref[st * strip:(st + 1) * strip, :]
        zero = jnp.zeros_like(q)
        qm = (jnp.where(lane_lo, q, zero), jnp.where(lane_lo, zero, q))
        m_scr[...] = jnp.full(m_scr.shape, -jnp.inf, F32)
        l_scr[...] = jnp.zeros(l_scr.shape, F32)
        a_scr[...] = jnp.zeros(a_scr.shape, F32)
        row0 = qi * (tq // strip) + st
        ntiles = (row0 * strip + strip - 1) // tk + 1

        def tile(j, near):
            off = pl.multiple_of(j * tk, tk)
            kt = k_ref[pl.ds(off, tk), :]
            vt = v_ref[pl.ds(off, tk), :]
            scs = [_dot_nt(qm[mi], kt) for mi in range(2)]
            if near:
                bias = bias_ref[row0 * (strip // tk) - j - ATTN_D_MIN]
                scs = [sc + bias for sc in scs]
            ms = [jnp.maximum(m_scr[mi], jnp.max(scs[mi], axis=-1, keepdims=True)) for mi in range(2)]
            ps = [jnp.exp2(scs[mi] - jnp.concatenate([ms[mi]] * reps, axis=1)) for mi in range(2)]
            for mi in range(2):
                alpha = jnp.exp2(m_scr[mi] - ms[mi])
                l_scr[mi] = alpha * l_scr[mi] + jnp.sum(ps[mi], axis=-1, keepdims=True)
                a_scr[mi] = alpha * a_scr[mi] + _dot(ps[mi].astype(BF16), vt)
                m_scr[mi] = ms[mi]

        nfar = jnp.maximum(row0 * (strip // tk) - ATTN_D_FAR + 1, 0)
        lax.fori_loop(0, nfar, lambda j, c: (tile(j, False), c)[1], 0)
        lax.fori_loop(nfar, ntiles, lambda j, c: (tile(j, True), c)[1], 0)
        o = a_scr[0] / l_scr[0] - lam * (a_scr[1] / l_scr[1])
        o_ref[st * strip:(st + 1) * strip, :] = _subln(o, sw_ref[...]).astype(BF16)


def _attn_prompt(q, kb, vb, rel_bias, lamp, subln_w, nb, seq):
    tq, strip, tk = ATTN_TQ, ATTN_STRIP, ATTN_TK
    assert seq % tq == 0 and tq % strip == 0 and strip % tk == 0
    nq = seq // tq
    nd = ATTN_D_FAR - ATTN_D_MIN
    rel = (jnp.arange(ATTN_D_MIN, ATTN_D_FAR, dtype=jnp.int32)[:, None, None] * tk
           + jnp.arange(strip, dtype=jnp.int32)[None, :, None] - jnp.arange(tk, dtype=jnp.int32)[None, None, :])
    bias = _bias_tiles(rel.reshape(-1, tk), rel_bias, True).reshape(ATTN_HEADS, nd, strip, tk)
    return pl.pallas_call(
        functools.partial(_attn_kernel, tq=tq, strip=strip, tk=tk),
        grid=(nb, ATTN_HEADS, nq),
        in_specs=[pl.BlockSpec((tq, LANE), lambda b, h, i: (b * nq + i, h)),
                  pl.BlockSpec((seq, LANE), lambda b, h, i: (b, h)),
                  pl.BlockSpec((seq, LANE), lambda b, h, i: (b, h)),
                  pl.BlockSpec((None, nd, strip, tk), lambda b, h, i: (h, 0, 0, 0)),
                  pl.BlockSpec((4, ATTN_HEAD_DIM), lambda b, h, i: (0, 0)),
                  pl.BlockSpec((1, ATTN_V_DIM), lambda b, h, i: (0, 0))],
        out_specs=pl.BlockSpec((tq, LANE), lambda b, h, i: (b * nq + i, h)),
        out_shape=jax.ShapeDtypeStruct((nb * seq, ATTN_HEADS * ATTN_V_DIM), BF16),
        scratch_shapes=[pltpu.VMEM((2, strip, LANE), F32)] * 3,
        compiler_params=_params("arbitrary", "arbitrary", "arbitrary"),
        name="attn_prompt",
    )(q, kb, vb, bias, lamp, subln_w.reshape(1, ATTN_V_DIM))


DECODE_PAGES = 8
DECODE_SEQS = 1


def _decode_kernel(pt_ref, q_ref, kn_ref, vn_ref, *rest, nq, pps, nbq):
    npg = nbq * pps
    ck_refs, cv_refs = rest[:npg], rest[npg:2 * npg]
    blast_ref, bnew_ref, tab_ref, lamp_ref, sw_ref, o_ref, qa_scr, m_scr, l_scr, a_scr = rest[2 * npg:]
    j = pl.program_id(1)
    nsteps = pl.num_programs(1)
    is_last = j == nsteps - 1

    def head_rows(ref, h, n):
        return ref[pl.ds(h, n, stride=ATTN_HEADS), :]

    @pl.when(j == 0)
    def _():
        lane_lo = lax.broadcasted_iota(jnp.int32, (nq, LANE), 1) < ATTN_HEAD_DIM
        zpad = jnp.zeros((PAGE_SIZE - nq, LANE), F32)
        for bi in range(nbq):
            for h in range(ATTN_HEADS):
                qh = q_ref[bi, :, h * LANE:(h + 1) * LANE].astype(F32)
                qa = jnp.concatenate([jnp.where(lane_lo, qh, 0.0), jnp.where(lane_lo, 0.0, qh)], axis=0)
                qa_scr[bi, h] = qa
                kn = jnp.concatenate([head_rows(kn_ref.at[bi], h, nq), zpad], axis=0).astype(BF16)
                vn = jnp.concatenate([head_rows(vn_ref.at[bi], h, nq), zpad], axis=0).astype(BF16)
                s = _dot_nt(qa.astype(BF16), kn) + bnew_ref[h]
                m = jnp.max(s, axis=-1, keepdims=True)
                p = jnp.exp2(s - m)
                m_scr[bi, h] = m
                l_scr[bi, h] = jnp.sum(p, axis=-1, keepdims=True)
                a_scr[bi, h] = _dot(p.astype(BF16), vn)

    heads = [(bi, h) for bi in range(nbq) for h in range(ATTN_HEADS)]
    scores = {}
    for bi, h in heads:
        qa = qa_scr[bi, h].astype(BF16)
        far = tab_ref[N_BUCKETS - 1, h] * LOG2E
        for i in range(pps):
            bias = jnp.where(is_last, blast_ref[h], far) if i == pps - 1 else far
            kh = head_rows(ck_refs[bi * pps + i], h, PAGE_SIZE).astype(BF16)
            scores[bi, h, i] = _dot_nt(qa, kh) + bias
    m_new, probs = {}, {}
    for bi, h in heads:
        m = m_scr[bi, h]
        for i in range(pps):
            m = jnp.maximum(m, jnp.max(scores[bi, h, i], axis=-1, keepdims=True))
        m_new[bi, h] = m
        for i in range(pps):
            probs[bi, h, i] = jnp.exp2(scores[bi, h, i] - m)
    for bi, h in heads:
        alpha = jnp.exp2(m_scr[bi, h] - m_new[bi, h])
        l_new = alpha * l_scr[bi, h]
        a_new = alpha * a_scr[bi, h]
        for i in range(pps):
            p = probs[bi, h, i]
            l_new = l_new + jnp.sum(p, axis=-1, keepdims=True)
            a_new = a_new + _dot(p.astype(BF16), head_rows(cv_refs[bi * pps + i], h, PAGE_SIZE).astype(BF16))
        l_scr[bi, h] = l_new
        a_scr[bi, h] = a_new
        m_scr[bi, h] = m_new[bi, h]

    @pl.when(is_last)
    def _():
        lam = _lam(lamp_ref)
        for bi in range(nbq):
            for h in range(ATTN_HEADS):
                on = a_scr[bi, h] / l_scr[bi, h]
                o = on[0:nq, :] - lam * on[nq:2 * nq, :]
                o_ref[bi, :, h * LANE:(h + 1) * LANE] = _subln(o, sw_ref[...])


def _attn_decode(q, k_new, v_new, cache_k, cache_v, page_table, rel_bias, lamp, subln_w):
    nb, nq, _ = q.shape
    npages = page_table.shape[1]
    past = npages * PAGE_SIZE
    rows = 2 * nq
    prow = PAGE_SIZE * ATTN_HEADS
    nnew = nq * ATTN_HEADS
    assert npages >= 1 and nq <= PAGE_SIZE
    qpos = past + jnp.arange(rows, dtype=jnp.int32) % nq
    kpos = jnp.arange(past - PAGE_SIZE, past + PAGE_SIZE, dtype=jnp.int32).reshape(2, 1, PAGE_SIZE)
    rel = jnp.where(kpos < past + nq, qpos[None, :, None] - kpos, -1)
    tiles = _bias_tiles(rel.reshape(2 * rows, PAGE_SIZE), rel_bias, False).reshape(ATTN_HEADS, 2, rows, PAGE_SIZE)
    pps = math.gcd(DECODE_PAGES, npages)
    nbq = math.gcd(DECODE_SEQS, nb)
    new_q = pl.BlockSpec((nbq, nq, ATTN_HEADS * LANE), lambda b, j, pt: (b, 0, 0))
    new_kv = pl.BlockSpec((nbq, nnew, LANE), lambda b, j, pt: (b, 0, 0))
    head_tile = pl.BlockSpec((ATTN_HEADS, rows, PAGE_SIZE), lambda b, j, pt: (0, 0, 0))

    def page(bi, i):
        return pl.BlockSpec((None, prow, LANE), lambda b, j, pt: (pt[b * nbq + bi, j * pps + i], 0, 0))

    pages = [page(bi, i) for bi in range(nbq) for i in range(pps)]
    grid_spec = pltpu.PrefetchScalarGridSpec(
        num_scalar_prefetch=1,
        grid=(nb // nbq, npages // pps),
        in_specs=[new_q, new_kv, new_kv, *pages, *pages, head_tile, head_tile,
                  pl.BlockSpec(memory_space=pltpu.SMEM),
                  pl.BlockSpec((4, ATTN_HEAD_DIM), lambda b, j, pt: (0, 0)),
                  pl.BlockSpec((1, ATTN_V_DIM), lambda b, j, pt: (0, 0))],
        out_specs=new_q,
        scratch_shapes=[pltpu.VMEM((nbq, ATTN_HEADS, rows, LANE), F32), pltpu.VMEM((nbq, ATTN_HEADS, rows, 1), F32),
                        pltpu.VMEM((nbq, ATTN_HEADS, rows, 1), F32), pltpu.VMEM((nbq, ATTN_HEADS, rows, LANE), F32)])
    ck = cache_k.reshape(cache_k.shape[0], prow, LANE)
    cv = cache_v.reshape(cache_v.shape[0], prow, LANE)
    return pl.pallas_call(
        functools.partial(_decode_kernel, nq=nq, pps=pps, nbq=nbq),
        grid_spec=grid_spec,
        out_shape=jax.ShapeDtypeStruct((nb, nq, ATTN_HEADS * ATTN_V_DIM), F32),
        compiler_params=_params("arbitrary", "arbitrary"),
        name="attn_decode",
    )(page_table, q, k_new, v_new, *([ck] * len(pages)), *([cv] * len(pages)), tiles[:, 0], tiles[:, 1], rel_bias,
      lamp, subln_w.reshape(1, ATTN_V_DIM))


HALF = D_MODEL // 2
PACKED = jnp.int32


def _pack_rows(x):
    word = pltpu.pack_elementwise([x[:, :HALF], x[:, HALF:]], packed_dtype=BF16)
    return lax.bitcast_convert_type(word, PACKED)


def _unpack_rows(p):
    halves = [pltpu.unpack_elementwise(p, index=i, packed_dtype=BF16, unpacked_dtype=F32) for i in range(2)]
    return jnp.concatenate(halves, axis=1)


def _layer_norm(r, g, b):
    mu = jnp.mean(r, axis=-1, keepdims=True)
    d = r - mu
    var = jnp.mean(d * d, axis=-1, keepdims=True)
    return d * lax.rsqrt(var + LN_EPS) * g + b


def _merge_kernel(yn_ref, on_ref, ga_ref, gb_ref, x_ref, g1_ref, sc2_ref, sh2_ref, wa_ref, wb_ref, wo_ref,
                  lg_ref, lb_ref, wr_ref, x1_ref, u2p_ref, logit_ref):
    ba = _dot(yn_ref[...].astype(BF16), wa_ref[...])
    bb = _dot(on_ref[...].astype(BF16), wb_ref[...])
    merged = jax.nn.sigmoid(ga_ref[...].astype(F32)) * ba + jax.nn.sigmoid(gb_ref[...].astype(F32)) * bb
    t = _dot(merged.astype(BF16), wo_ref[...])
    x1 = _layer_norm(ALPHA * x_ref[...] + g1_ref[...] * t, lg_ref[...], lb_ref[...])
    x1_ref[...] = x1
    u2 = x1 * (1.0 + sc2_ref[...]) + sh2_ref[...]
    u2p_ref[...] = _pack_rows(u2)
    u_hi = u2.astype(BF16)
    u_lo = (u2 - u_hi.astype(F32)).astype(BF16)
    by_hi = _dot_nt(wr_ref[...], u_hi)
    logit_ref[...] = by_hi[:N_EXPERTS] + by_hi[N_EXPERTS:] + _dot_nt(wr_ref[:N_EXPERTS, :], u_lo)


def _mod_spec(mod, t, tm, rows_per_mod):
    if mod.shape[1] != 1:
        return mod.reshape(t, D_MODEL), pl.BlockSpec((tm, D_MODEL), lambda i: (i, 0))
    assert rows_per_mod % tm == 0
    return mod, pl.BlockSpec((None, 1, D_MODEL), lambda i: (i // (rows_per_mod // tm), 0, 0))


def _merge(yn, on, ga, gb, x, g1, sc2, sh2, wa, wb, wo, ln_g, ln_b, w_router, rows_per_mod):
    t = x.shape[0]
    tm = min(512, t)
    rows = pl.BlockSpec((tm, D_MODEL), lambda i: (i, 0))
    wspec = pl.BlockSpec((D_MODEL, D_MODEL), lambda i: (0, 0))
    vec = pl.BlockSpec((1, D_MODEL), lambda i: (0, 0))
    g1a, mspec = _mod_spec(g1, t, tm, rows_per_mod)
    sc2a, _ = _mod_spec(sc2, t, tm, rows_per_mod)
    sh2a, _ = _mod_spec(sh2, t, tm, rows_per_mod)
    wr_t = w_router.T
    wr_hi = wr_t.astype(BF16)
    return pl.pallas_call(
        _merge_kernel,
        grid=(t // tm,),
        in_specs=[rows, rows, rows, rows, rows, mspec, mspec, mspec, wspec, wspec, wspec, vec, vec,
                  pl.BlockSpec((2 * N_EXPERTS, D_MODEL), lambda i: (0, 0))],
        out_specs=[rows, pl.BlockSpec((tm, HALF), lambda i: (i, 0)), pl.BlockSpec((N_EXPERTS, tm), lambda i: (0, i))],
        out_shape=[jax.ShapeDtypeStruct((t, D_MODEL), F32), jax.ShapeDtypeStruct((t, HALF), PACKED),
                   jax.ShapeDtypeStruct((N_EXPERTS, t), F32)],
        compiler_params=_params("arbitrary"),
        name="merge",
    )(yn, on, ga, gb, x, g1a, sc2a, sh2a, wa, wb, wo, ln_g.reshape(1, D_MODEL), ln_b.reshape(1, D_MODEL),
      jnp.concatenate([wr_hi, (wr_t - wr_hi.astype(F32)).astype(BF16)], axis=0))


def _router_kernel(logit_ref, br_ref, e_ref, w_ref, r_ref, cnt_ref, carry_scr, tri_scr, *, tr):
    i = pl.program_id(0)

    @pl.when(i == 0)
    def _():
        carry_scr[...] = jnp.zeros_like(carry_scr)
        a = lax.broadcasted_iota(jnp.int32, (tr, tr), 0)
        b = lax.broadcasted_iota(jnp.int32, (tr, tr), 1)
        tri_scr[...] = (a < b).astype(BF16)

    scores = jax.nn.sigmoid(logit_ref[...])
    biased = scores + br_ref[...]
    ninf = -jnp.inf

    b3 = biased.reshape(N_EXPERT_GROUPS, GROUP_SIZE, tr)
    j3 = lax.broadcasted_iota(jnp.int32, b3.shape, 1).astype(F32)
    top1 = jnp.max(b3, axis=1, keepdims=True)
    first = jnp.min(jnp.where(b3 == top1, j3, float(GROUP_SIZE)), axis=1, keepdims=True)
    top2 = jnp.max(jnp.where(j3 == first, ninf, b3), axis=1, keepdims=True)
    gscore = (top1 + top2).reshape(N_EXPERT_GROUPS, tr)
    gi = lax.broadcasted_iota(jnp.int32, gscore.shape, 0).astype(F32)
    gsel = jnp.zeros(gscore.shape, F32)
    for _ in range(TOPK_GROUPS):
        mx = jnp.max(gscore, axis=0, keepdims=True)
        pick = gi == jnp.min(jnp.where(gscore == mx, gi, float(N_EXPERT_GROUPS)), axis=0, keepdims=True)
        gsel = jnp.where(pick, 1.0, gsel)
        gscore = jnp.where(pick, ninf, gscore)
    emask = jnp.broadcast_to(gsel.reshape(N_EXPERT_GROUPS, 1, tr), b3.shape).reshape(N_EXPERTS, tr)
    masked = jnp.where(emask > 0.5, biased, ninf)

    ei = lax.broadcasted_iota(jnp.int32, masked.shape, 0).astype(F32)
    picked = jnp.zeros(masked.shape, F32)
    idxs, wsel = [], []
    for _ in range(TOP_K):
        mx = jnp.max(masked, axis=0, keepdims=True)
        idx = jnp.min(jnp.where(masked == mx, ei, float(N_EXPERTS)), axis=0, keepdims=True)
        pick = ei == idx
        idxs.append(idx)
        wsel.append(jnp.sum(jnp.where(pick, scores, 0.0), axis=0, keepdims=True))
        picked = jnp.where(pick, 1.0, picked)
        masked = jnp.where(pick, ninf, masked)
    wall = jnp.concatenate(wsel, axis=0)
    w_ref[...] = wall / jnp.sum(wall, axis=0, keepdims=True) * ROUTED_SCALE
    e_ref[...] = jnp.concatenate(idxs, axis=0).astype(jnp.int32)

    rank = carry_scr[:, 0:1] + _dot(picked.astype(BF16), tri_scr[...])
    r_ref[...] = jnp.concatenate(
        [jnp.sum(jnp.where(ei == idx, rank, 0.0), axis=0, keepdims=True) for idx in idxs], axis=0).astype(jnp.int32)
    total = carry_scr[...] + jnp.sum(picked, axis=1, keepdims=True)
    carry_scr[...] = total
    cnt_ref[...] = total.astype(jnp.int32)


def _router(logits, b_router):
    t = logits.shape[1]
    tr = min(512, t)
    tok = pl.BlockSpec((TOP_K, tr), lambda i: (0, i))
    e_t, w_t, r_t, cnt = pl.pallas_call(
        functools.partial(_router_kernel, tr=tr),
        grid=(t // tr,),
        in_specs=[pl.BlockSpec((N_EXPERTS, tr), lambda i: (0, i)),
                  pl.BlockSpec((N_EXPERTS, 1), lambda i: (0, 0))],
        out_specs=[tok, tok, tok, pl.BlockSpec((N_EXPERTS, LANE), lambda i: (0, 0))],
        out_shape=[jax.ShapeDtypeStruct((TOP_K, t), jnp.int32), jax.ShapeDtypeStruct((TOP_K, t), F32),
                   jax.ShapeDtypeStruct((TOP_K, t), jnp.int32), jax.ShapeDtypeStruct((N_EXPERTS, LANE), jnp.int32)],
        scratch_shapes=[pltpu.VMEM((N_EXPERTS, LANE), F32), pltpu.VMEM((tr, tr), BF16)],
        compiler_params=_params("arbitrary"),
        name="router",
    )(logits, b_router.reshape(N_EXPERTS, 1))
    return e_t, w_t, r_t, cnt


def _plan_kernel(e_ref, r_ref, cnt_ref, dest_ref, be_ref, bv_ref, nu_ref, *, br, nblk_pad):
    cnt = cnt_ref[...]
    shift = br.bit_length() - 1
    padded = lax.shift_left(lax.shift_right_logical(cnt + (br - 1), shift), shift)
    ea = lax.broadcasted_iota(jnp.int32, (N_EXPERTS, N_EXPERTS), 0)
    eb = lax.broadcasted_iota(jnp.int32, (N_EXPERTS, N_EXPERTS), 1)
    pends = jnp.dot((eb <= ea).astype(F32), padded.astype(F32), preferred_element_type=F32,
                    precision=HIGHEST).astype(jnp.int32)
    pstart = pends - padded

    e = e_ref[...]
    first_row = jnp.zeros(e.shape, jnp.int32)
    for x in range(N_EXPERTS):
        first_row = jnp.where(e == x, pstart[x:x + 1, 0:1], first_row)
    dest_ref[...] = first_row + r_ref[...]

    @pl.when(pl.program_id(0) == 0)
    def _():
        esub = lax.broadcasted_iota(jnp.int32, (N_EXPERTS, LANE), 0)
        real_end = (pstart + cnt).astype(F32)
        for c in range(nblk_pad // LANE):
            bstart = (lax.broadcasted_iota(jnp.int32, (1, LANE), 1) + c * LANE) * br
            be = jnp.minimum(jnp.sum((pends <= bstart).astype(F32), axis=0, keepdims=True),
                             float(N_EXPERTS - 1)).astype(jnp.int32)
            end = jnp.sum(jnp.where(esub == be, real_end, 0.0), axis=0, keepdims=True).astype(jnp.int32)
            be_ref[:, c * LANE:(c + 1) * LANE] = be
            bv_ref[:, c * LANE:(c + 1) * LANE] = jnp.clip(end - bstart, 0, br)
        nu_ref[...] = lax.shift_right_logical(pends[N_EXPERTS - 1:N_EXPERTS, :], shift)


def _plan(e_t, r_t, cnt, br, nblk):
    t = e_t.shape[1]
    tc = min(2048, t)
    nblk_pad = -(-nblk // LANE) * LANE
    tok = pl.BlockSpec((TOP_K, tc), lambda i: (0, i))
    blk = pl.BlockSpec((1, nblk_pad), lambda i: (0, 0))
    dest, be, bv, nu = pl.pallas_call(
        functools.partial(_plan_kernel, br=br, nblk_pad=nblk_pad),
        grid=(t // tc,),
        in_specs=[tok, tok, pl.BlockSpec((N_EXPERTS, LANE), lambda i: (0, 0))],
        out_specs=[tok, blk, blk, pl.BlockSpec((1, LANE), lambda i: (0, 0))],
        out_shape=[jax.ShapeDtypeStruct((TOP_K, t), jnp.int32), jax.ShapeDtypeStruct((1, nblk_pad), jnp.int32),
                   jax.ShapeDtypeStruct((1, nblk_pad), jnp.int32), jax.ShapeDtypeStruct((1, LANE), jnp.int32)],
        compiler_params=_params("arbitrary"),
        name="moe_plan",
    )(e_t, r_t, cnt)
    return dest, be[0, :nblk], bv[0, :nblk], nu[0, :1]


def _row_copy(src, s, dst, d, sem):
    return pltpu.make_async_copy(src.at[pl.ds(s, 1), :], dst.at[pl.ds(d, 1), :], sem)


def _dispatch_kernel(dest_ref, u_ref, xs_ref, sem, *, tm):
    def start(r, carry):
        for k in range(TOP_K):
            _row_copy(u_ref, r, xs_ref, dest_ref[k, r], sem).start()
        return carry

    def wait(r, carry):
        for k in range(TOP_K):
            _row_copy(u_ref, r, xs_ref, dest_ref[k, r], sem).wait()
        return carry

    lax.fori_loop(0, tm, start, 0)
    lax.fori_loop(0, tm, wait, 0)


def _dispatch(u2p, dest_t, n_rows):
    t = u2p.shape[0]
    tm = min(256, t)
    nt = t // tm
    dest_blocks = dest_t.reshape(TOP_K, nt, tm).transpose(1, 0, 2)
    return pl.pallas_call(
        functools.partial(_dispatch_kernel, tm=tm),
        grid=(nt,),
        in_specs=[pl.BlockSpec((None, TOP_K, tm), lambda i: (i, 0, 0), memory_space=pltpu.SMEM),
                  pl.BlockSpec((tm, HALF), lambda i: (i, 0))],
        out_specs=pl.BlockSpec(memory_space=pl.ANY),
        out_shape=jax.ShapeDtypeStruct((n_rows, HALF), PACKED),
        scratch_shapes=[pltpu.SemaphoreType.DMA(())],
        compiler_params=_params("arbitrary"),
        name="moe_dispatch",
    )(dest_blocks, u2p)


SC_CORES = 2
SC_SUBCORES = 16
SC_WINDOW = 128
SC_WORKERS = SC_CORES * SC_SUBCORES


def _sc_mesh():
    return plsc.VectorSubcoreMesh(core_axis_name="c", subcore_axis_name="s")


def _sc_worker():
    return lax.axis_index("s") * SC_CORES + lax.axis_index("c")


def _sc_scatter_rows(rows, dest_blocks, n_rows):
    t, width = rows.shape
    nchunks = t // SC_WINDOW // SC_WORKERS

    def body(rows_hbm, dest_hbm, out_hbm, idx_v, rows_v, sem):
        wid = _sc_worker()

        @pl.loop(0, nchunks)
        def _(c):
            chunk = wid * nchunks + c
            pltpu.sync_copy(dest_hbm.at[chunk], idx_v)
            pltpu.sync_copy(rows_hbm.at[pl.ds(pl.multiple_of(chunk * SC_WINDOW, SC_WINDOW), SC_WINDOW)], rows_v)
            copies = [pltpu.async_copy(rows_v, out_hbm.at[idx_v.at[k]], sem) for k in range(TOP_K)]
            for cp in copies:
                cp.wait()

    return pl.kernel(
        body, out_type=jax.ShapeDtypeStruct((n_rows, width), rows.dtype), mesh=_sc_mesh(),
        scratch_types=[pltpu.VMEM((TOP_K, SC_WINDOW), jnp.int32), pltpu.VMEM((SC_WINDOW, width), rows.dtype),
                       pltpu.SemaphoreType.DMA],
        name="moe_sc_dispatch")(rows, dest_blocks)


def _sc_gather_rows(table, idx):
    n = idx.shape[0]
    width = table.shape[1]
    nchunks = n // SC_WINDOW // SC_WORKERS

    def body(table_hbm, idx_hbm, out_hbm, idx_v, rows_v, sem):
        wid = _sc_worker()

        @pl.loop(0, nchunks)
        def _(c):
            base = pl.multiple_of((wid * nchunks + c) * SC_WINDOW, SC_WINDOW)
            pltpu.sync_copy(idx_hbm.at[pl.ds(base, SC_WINDOW)], idx_v)
            pltpu.async_copy(table_hbm.at[idx_v], rows_v, sem).wait()
            pltpu.sync_copy(rows_v, out_hbm.at[pl.ds(base, SC_WINDOW)])

    return pl.kernel(
        body, out_type=jax.ShapeDtypeStruct((n, width), table.dtype), mesh=_sc_mesh(),
        scratch_types=[pltpu.VMEM((SC_WINDOW,), jnp.int32), pltpu.VMEM((SC_WINDOW, width), table.dtype),
                       pltpu.SemaphoreType.DMA],
        name="moe_sc_gather")(table, idx)


def _expert_kernel(be_ref, bv_ref, nu_ref, x_ref, wg_ref, wu_ref, wd_ref, after_ref, y_ref, wg_scr, wu_scr, wd_scr,
                   *, br):
    del after_ref
    i = pl.program_id(0)
    prev = be_ref[jnp.maximum(i - 1, 0)]

    @pl.when(jnp.logical_and(i < nu_ref[0], jnp.logical_or(i == 0, be_ref[i] != prev)))
    def _():
        wg_scr[...] = wg_ref[...].astype(BF16)
        wu_scr[...] = wu_ref[...].astype(BF16)
        wd_scr[...] = wd_ref[...].astype(BF16)

    @pl.when(i < nu_ref[0])
    def _():
        live = lax.broadcasted_iota(jnp.int32, (br, HALF), 0) < bv_ref[i]
        x = _unpack_rows(jnp.where(live, x_ref[...], 0)).astype(BF16)
        hcat = _silu(_dot(x, wg_scr[...])) * _dot(x, wu_scr[...])
        y_ref[...] = _pack_rows(_dot(hcat.astype(BF16), wd_scr[...]))


def _experts(x_sorted, blk_e, blk_valid, n_used, w_gate, w_up, w_down, br, after):
    n_rows = x_sorted.shape[0]
    nblk = n_rows // br

    def blk(i, be, bv, nu):
        return (jnp.minimum(i, nu[0] - 1), 0)

    grid_spec = pltpu.PrefetchScalarGridSpec(
        num_scalar_prefetch=3,
        grid=(nblk,),
        in_specs=[pl.BlockSpec((br, HALF), blk),
                  pl.BlockSpec((None, D_MODEL, D_EXPERT), lambda i, be, bv, nu: (be[i], 0, 0)),
                  pl.BlockSpec((None, D_MODEL, D_EXPERT), lambda i, be, bv, nu: (be[i], 0, 0)),
                  pl.BlockSpec((None, D_EXPERT, D_MODEL), lambda i, be, bv, nu: (be[i], 0, 0)),
                  pl.BlockSpec(after.shape, lambda i, be, bv, nu: (0,) * after.ndim)],
        out_specs=pl.BlockSpec((br, HALF), blk),
        scratch_shapes=[pltpu.VMEM((D_MODEL, D_EXPERT), BF16), pltpu.VMEM((D_MODEL, D_EXPERT), BF16),
                        pltpu.VMEM((D_EXPERT, D_MODEL), BF16)])
    return pl.pallas_call(
        functools.partial(_expert_kernel, br=br),
        grid_spec=grid_spec,
        out_shape=jax.ShapeDtypeStruct((n_rows, HALF), PACKED),
        compiler_params=_params("arbitrary"),
        name="moe_experts",
    )(blk_e, blk_valid, n_used, x_sorted, w_gate, w_up, w_down, after)


def _combine_kernel(dest_ref, ys_ref, w_ref, u_ref, x1_ref, g2_ref, sg_ref, su_ref, sd_ref, lg_ref, lb_ref,
                    o_ref, g_scr, sem, *, tm):
    def start(r, carry):
        for k in range(TOP_K):
            _row_copy(ys_ref, dest_ref[k, r], g_scr.at[k], r, sem).start()
        return carry

    def wait(r, carry):
        for k in range(TOP_K):
            _row_copy(ys_ref, dest_ref[k, r], g_scr.at[k], r, sem).wait()
        return carry

    lax.fori_loop(0, tm, start, 0)
    f = _shared_ffn(u_ref, sg_ref, su_ref, sd_ref)
    lax.fori_loop(0, tm, wait, 0)
    f = _add_routed(f, g_scr, w_ref[...])
    o_ref[...] = _layer_norm(ALPHA * x1_ref[...] + g2_ref[...] * f, lg_ref[...], lb_ref[...])


def _shared_ffn(u_ref, sg_ref, su_ref, sd_ref):
    ub = _unpack_rows(u_ref[...]).astype(BF16)
    hs = _silu(_dot(ub, sg_ref[...])) * _dot(ub, su_ref[...])
    return _dot(hs.astype(BF16), sd_ref[...])


def _add_routed(f, slots_ref, w):
    for k in range(TOP_K):
        f = f + _unpack_rows(slots_ref[k]) * w[:, k:k + 1]
    return f


def _combine_gathered_kernel(g_ref, w_ref, u_ref, x1_ref, g2_ref, sg_ref, su_ref, sd_ref, lg_ref, lb_ref, o_ref):
    f = _add_routed(_shared_ffn(u_ref, sg_ref, su_ref, sd_ref), g_ref, w_ref[...])
    o_ref[...] = _layer_norm(ALPHA * x1_ref[...] + g2_ref[...] * f, lg_ref[...], lb_ref[...])


def _combine(y_rows, dest_t, w_t, u2p, x1, g2, ws_gate, ws_up, ws_down, ln_g, ln_b, rows_per_mod, gathered):
    t = u2p.shape[0]
    tm = min(256 if gathered else 128, t)
    nt = t // tm
    rows = pl.BlockSpec((tm, D_MODEL), lambda i: (i, 0))
    vec = pl.BlockSpec((1, D_MODEL), lambda i: (0, 0))
    g2a, mspec = _mod_spec(g2, t, tm, rows_per_mod)
    common_specs = [pl.BlockSpec((tm, TOP_K), lambda i: (i, 0)),
                    pl.BlockSpec((tm, HALF), lambda i: (i, 0)), rows, mspec,
                    pl.BlockSpec((D_MODEL, D_SHARED), lambda i: (0, 0)),
                    pl.BlockSpec((D_MODEL, D_SHARED), lambda i: (0, 0)),
                    pl.BlockSpec((D_SHARED, D_MODEL), lambda i: (0, 0)),
                    vec, vec]
    common_args = (w_t.T, u2p, x1, g2a, ws_gate.astype(BF16), ws_up.astype(BF16), ws_down.astype(BF16),
                   ln_g.reshape(1, D_MODEL), ln_b.reshape(1, D_MODEL))
    if gathered:
        return pl.pallas_call(
            _combine_gathered_kernel,
            grid=(nt,),
            in_specs=[pl.BlockSpec((TOP_K, tm, HALF), lambda i: (0, i, 0))] + common_specs,
            out_specs=rows,
            out_shape=jax.ShapeDtypeStruct((t, D_MODEL), F32),
            compiler_params=_params("arbitrary"),
            name="moe_combine_gathered",
        )(y_rows.reshape(TOP_K, t, HALF), *common_args)
    dest_blocks = dest_t.reshape(TOP_K, nt, tm).transpose(1, 0, 2)
    return pl.pallas_call(
        functools.partial(_combine_kernel, tm=tm),
        grid=(nt,),
        in_specs=[pl.BlockSpec((None, TOP_K, tm), lambda i: (i, 0, 0), memory_space=pltpu.SMEM),
                  pl.BlockSpec(memory_space=pl.ANY)] + common_specs,
        out_specs=rows,
        out_shape=jax.ShapeDtypeStruct((t, D_MODEL), F32),
        scratch_shapes=[pltpu.VMEM((TOP_K, tm, HALF), PACKED), pltpu.SemaphoreType.DMA(())],
        compiler_params=_params("arbitrary"),
        name="moe_combine",
    )(dest_blocks, y_rows, *common_args)


def _moe(u2p, logits, x1, g2, p, rows_per_mod, br, after):
    t = u2p.shape[0]
    e_t, w_t, r_t, counts = _router(logits, p["b_router"])
    n_rows = (t * TOP_K + N_EXPERTS * (br - 1) + br - 1) // br * br
    dest_t, blk_e, blk_valid, n_used = _plan(e_t, r_t, counts, br, n_rows // br)
    on_sc = t % (SC_WORKERS * SC_WINDOW) == 0
    if on_sc:
        dest_blocks = dest_t.reshape(TOP_K, t // SC_WINDOW, SC_WINDOW).transpose(1, 0, 2)
        x_sorted = _sc_scatter_rows(u2p, dest_blocks, n_rows)
    else:
        x_sorted = _dispatch(u2p, dest_t, n_rows)
    y_rows = _experts(x_sorted, blk_e, blk_valid, n_used, p["w_gate"], p["w_up"], p["w_down"], br, after)
    if on_sc:
        y_rows = _sc_gather_rows(y_rows, dest_t.reshape(TOP_K * t))
    return _combine(y_rows, dest_t, w_t, u2p, x1, g2, p["ws_gate"], p["ws_up"], p["ws_down"],
                    p["ln2_g"], p["ln2_b"], rows_per_mod, on_sc)


def _cat_w_in(w_in):
    sizes = (SSD_INNER, CONV_DIM, SSD_HEADS, 1024, 1024, 1024, D_MODEL, D_MODEL)
    offs = [0]
    for s in sizes:
        offs.append(offs[-1] + s)
    z, xbc, dt, q, k, v, ga, gb = [w_in[:, offs[i]:offs[i + 1]] for i in range(8)]
    dt = jnp.pad(dt, ((0, 0), (0, SEG_END - SEG_DT - SSD_HEADS)))
    return jnp.concatenate([z, xbc, q, k, v, ga, gb, dt], axis=1).astype(BF16)


def kernel(x_prompt, x_sample, c_prompt, c_sample, cache_k, cache_v, page_table, state_conv, state_ssm, rel_bias, w_ada, b_ada, w_in, conv_w, conv_b, dt_bias, a_log, d_skip, ssd_norm_w, lam_q1, lam_k1, lam_q2, lam_k2, subln_w, w_br_ssd, w_br_attn, w_out, ln1_g, ln1_b, w_router, b_router, w_gate, w_up, w_down, ws_gate, ws_up, ws_down, ln2_g, ln2_b):
    assert w_in.shape[0] == DEPTH
    nbp, seq, _ = x_prompt.shape
    nbs, dseq, _ = x_sample.shape
    tp, ts = nbp * seq, nbs * dseq
    past = page_table.shape[1] * PAGE_SIZE
    cl = SSD_CHUNK

    pad_p = -nbp % 8
    c_all = jnp.concatenate([c_prompt, jnp.zeros((pad_p, D_MODEL), F32), jnp.repeat(c_sample, dseq, axis=0)], axis=0)
    mod = _adaln(c_all, w_ada[0], b_ada[0])
    mod_p = mod[:nbp].reshape(nbp, 1, 6, D_MODEL)
    mod_s = mod[nbp + pad_p:].reshape(nbs, dseq, 6, D_MODEL)
    mp = [mod_p[:, :, i] for i in range(6)]
    ms = [mod_s[:, :, i] for i in range(6)]

    w_cat = _cat_w_in(w_in[0])
    wa, wb, wo = w_br_ssd[0].astype(BF16), w_br_attn[0].astype(BF16), w_out[0].astype(BF16)
    lamp = jnp.stack([lam_q1[0], lam_k1[0], lam_q2[0], lam_k2[0]])
    moe_p = dict(b_router=b_router[0], w_gate=w_gate[0], w_up=w_up[0], w_down=w_down[0],
                 ws_gate=ws_gate[0], ws_up=ws_up[0], ws_down=ws_down[0], ln2_g=ln2_g[0], ln2_b=ln2_b[0])
    ssd_w = (conv_w[0], conv_b[0], dt_bias[0], a_log[0], d_skip[0], ssd_norm_w[0])

    xp = x_prompt.reshape(tp, D_MODEL)
    z, xbc, dt, q, k, v, kb, vb, ga, gb = _inproj(xp, mp[1], mp[0], w_cat, seq)
    yn, conv_p, h_p = _ssd(xbc, dt, z, *ssd_w, jnp.zeros((nbp, CONV_W - 1, CONV_DIM), F32),
                           jnp.zeros((nbp, SSD_HEADS, SSD_HEADDIM, D_STATE), F32), nbp, seq // cl, cl)
    yn = yn.reshape(tp, SSD_INNER)
    on = _attn_prompt(q, kb, vb, rel_bias, lamp, subln_w[0], nbp, seq)
    x1_p, u2p_p, logits_p = _merge(yn, on, ga, gb, xp, mp[2], mp[4], mp[3], wa, wb, wo, ln1_g[0], ln1_b[0],
                                   w_router[0], seq)
    k_prompt = k.reshape(1, nbp, seq, ATTN_HEADS, 2 * ATTN_HEAD_DIM)
    v_prompt = v.reshape(1, nbp, seq, ATTN_HEADS, ATTN_V_DIM)

    xs_ = x_sample.reshape(ts, D_MODEL)
    z, xbc, dt, q, k, v, kb, vb, ga, gb = _inproj(xs_, ms[1], ms[0], w_cat, dseq)
    yn, conv_s, h_s = _ssd(xbc, dt, z, *ssd_w, state_conv[0], state_ssm[0], nbs, 1, dseq)
    yn = yn.reshape(ts, SSD_INNER)
    on = _attn_decode(q.reshape(nbs, dseq, -1), k.reshape(nbs, dseq * ATTN_HEADS, LANE),
                      v.reshape(nbs, dseq * ATTN_HEADS, LANE), cache_k[0], cache_v[0], page_table,
                      rel_bias, lamp, subln_w[0])
    y_prompt = _moe(u2p_p, logits_p, x1_p, mp[5], moe_p, seq, 512, on[0, :, :LANE]).reshape(nbp, seq, D_MODEL)
    x1, u2p, logits = _merge(yn, on.reshape(ts, -1), ga, gb, xs_, ms[2], ms[4], ms[3], wa, wb, wo, ln1_g[0], ln1_b[0],
                             w_router[0], dseq)
    y_sample = _moe(u2p, logits, x1, ms[5], moe_p, dseq, 64, x1[:8, :LANE]).reshape(nbs, dseq, D_MODEL)
    k_sample = k.reshape(1, nbs, dseq, ATTN_HEADS, 2 * ATTN_HEAD_DIM)
    v_sample = v.reshape(1, nbs, dseq, ATTN_HEADS, ATTN_V_DIM)

    return (y_prompt, y_sample, k_prompt, v_prompt, conv_p[None], h_p.reshape(1, nbp, SSD_HEADS, SSD_HEADDIM, D_STATE),
            k_sample, v_sample, conv_s[None], h_s.reshape(1, nbs, SSD_HEADS, SSD_HEADDIM, D_STATE))
```

```python
import functools
import math

import jax
import jax.numpy as jnp
from jax import lax
from jax.experimental import pallas as pl
from jax.experimental.pallas import tpu as pltpu
from jax.experimental.pallas import tpu_sc as plsc

F32 = jnp.float32
BF16 = jnp.bfloat16
HIGHEST = lax.Precision.HIGHEST

D_MODEL = 1024
SSD_INNER = 1024
SSD_HEADDIM = 64
SSD_HEADS = 16
SSD_GROUPS = 2
D_STATE = 128
CONV_W = 4
CONV_DIM = SSD_INNER + 2 * SSD_GROUPS * D_STATE
SSD_CHUNK = 128
ATTN_HEADS = 8
ATTN_HEAD_DIM = 64
ATTN_V_DIM = 128
N_BUCKETS = 32
MAX_DISTANCE = 128
N_EXPERTS = 64
N_EXPERT_GROUPS = 8
GROUP_SIZE = N_EXPERTS // N_EXPERT_GROUPS
TOPK_GROUPS = 4
TOP_K = 8
D_EXPERT = 256
D_SHARED = 256
ROUTED_SCALE = 2.5
PAGE_SIZE = 128
DEPTH = 1
ALPHA = (2 * DEPTH) ** 0.25
LN_EPS = 1e-5
RMS_EPS = 1e-5
LAM_INIT = 0.8 - 0.6 * math.exp(-0.3 * 0)
LOG2E = math.log2(math.e)
LANE = 128
VMEM_LIMIT = 56 * 1024 * 1024

SEG_Z, SEG_XBC, SEG_Q, SEG_K, SEG_V, SEG_GA, SEG_GB, SEG_DT, SEG_END = (
    0, 1024, 2560, 3584, 4608, 5632, 6656, 7680, 7808)


def _silu(x):
    return x * jax.nn.sigmoid(x)


def _dot(a, b):
    return jnp.dot(a, b, preferred_element_type=F32)


def _dot_nt(a, b):
    return lax.dot_general(a, b, (((1,), (1,)), ((), ())), preferred_element_type=F32)


def _dot_tn(a, b):
    return lax.dot_general(a, b, (((0,), (0,)), ((), ())), preferred_element_type=F32)


def _params(*sem):
    return pltpu.CompilerParams(dimension_semantics=sem, vmem_limit_bytes=VMEM_LIMIT)


def _adaln_kernel(c_ref, w_ref, b_ref, o_ref):
    s = _silu(c_ref[...]).astype(BF16)
    o_ref[...] = _dot(s, w_ref[...].astype(BF16)) + b_ref[...]


def _adaln(c, w_ada, b_ada):
    r = c.shape[0]
    n = w_ada.shape[1]
    tn = 1024
    return pl.pallas_call(
        _adaln_kernel,
        grid=(n // tn,),
        in_specs=[pl.BlockSpec((r, D_MODEL), lambda j: (0, 0)),
                  pl.BlockSpec((D_MODEL, tn), lambda j: (0, j)),
                  pl.BlockSpec((1, tn), lambda j: (0, j))],
        out_specs=pl.BlockSpec((r, tn), lambda j: (0, j)),
        out_shape=jax.ShapeDtypeStruct((r, n), F32),
        compiler_params=_params("arbitrary"),
        name="adaln",
    )(c, w_ada, b_ada.reshape(1, n))


def _inproj_kernel(x_ref, sc_ref, sh_ref, w_ref, z_ref, xbc_ref, dt_ref, q_ref, k_ref, v_ref,
                   kb_ref, vb_ref, ga_ref, gb_ref):
    u = (x_ref[...] * (1.0 + sc_ref[...]) + sh_ref[...]).astype(BF16)

    def seg(a, b):
        return _dot(u, w_ref[:, a:b])

    z_ref[...] = seg(SEG_Z, SEG_XBC).astype(BF16)
    xbc_ref[...] = seg(SEG_XBC, SEG_Q)
    q_ref[...] = (seg(SEG_Q, SEG_K) * (ATTN_HEAD_DIM ** -0.5 * LOG2E)).astype(BF16)
    kk = seg(SEG_K, SEG_V)
    k_ref[...] = kk
    kb_ref[...] = kk.astype(BF16)
    vv = seg(SEG_V, SEG_GA)
    v_ref[...] = vv
    vb_ref[...] = vv.astype(BF16)
    ga_ref[...] = seg(SEG_GA, SEG_GB).astype(BF16)
    gb_ref[...] = seg(SEG_GB, SEG_DT).astype(BF16)
    dt_ref[...] = seg(SEG_DT, SEG_END)


def _inproj(x, sc, sh, w_cat, rows_per_mod):
    t = x.shape[0]
    tm = min(256, t)
    per_row = sc.shape[1] != 1
    if per_row:
        sc2 = sc.reshape(t, D_MODEL)
        sh2 = sh.reshape(t, D_MODEL)
        mod_spec = pl.BlockSpec((tm, D_MODEL), lambda i: (i, 0))
    else:
        assert rows_per_mod % tm == 0
        sc2, sh2 = sc, sh
        mod_spec = pl.BlockSpec((None, 1, D_MODEL), lambda i: (i // (rows_per_mod // tm), 0, 0))

    def rows(width):
        return pl.BlockSpec((tm, width), lambda i: (i, 0))

    widths = (1024, CONV_DIM, LANE, 1024, 1024, 1024, 1024, 1024, 1024, 1024)
    dtypes = (BF16, F32, F32, BF16, F32, F32, BF16, BF16, BF16, BF16)
    return pl.pallas_call(
        _inproj_kernel,
        grid=(t // tm,),
        in_specs=[rows(D_MODEL), mod_spec, mod_spec,
                  pl.BlockSpec((D_MODEL, SEG_END), lambda i: (0, 0), pipeline_mode=pl.Buffered(1))],
        out_specs=[rows(w) for w in widths],
        out_shape=[jax.ShapeDtypeStruct((t, w), d) for w, d in zip(widths, dtypes)],
        compiler_params=_params("arbitrary"),
        name="inproj",
    )(x, sc2, sh2, w_cat)


def _ssd_kernel(xbc_ref, dt_ref, z_ref, cw_ref, cb_ref, dtb_ref, alog_ref, dsk_ref, nw_ref,
                cbuf_ref, h0_ref, yn_ref, cnew_ref, hnew_ref, xp_scr, h_scr, y_scr, *, valid_len):
    c = pl.program_id(1)
    nc = pl.num_programs(1)
    cl = SSD_CHUNK
    head = 8

    @pl.when(c == 0)
    def _():
        xp_scr[head - (CONV_W - 1):head, :] = cbuf_ref[...]
        h_scr[...] = h0_ref[...]

    @pl.when(c > 0)
    def _():
        xp_scr[head - (CONV_W - 1):head, :] = xp_scr[head + cl - (CONV_W - 1):head + cl, :]

    xp_scr[head:head + valid_len, :] = xbc_ref[...]
    if valid_len < cl:
        xp_scr[head + valid_len:head + cl, :] = jnp.zeros((cl - valid_len, CONV_DIM), F32)

    acc = xp_scr[head:head + cl, :] * cw_ref[CONV_W - 1:CONV_W, :]
    for j in range(CONV_W - 1):
        lo = head - (CONV_W - 1) + j
        acc = acc + xp_scr[lo:lo + cl, :] * cw_ref[j:j + 1, :]
    xc = _silu(acc + cb_ref[...])
    xs = xc[:, :SSD_INNER]
    bmat = [xc[:, SSD_INNER + g * D_STATE:SSD_INNER + (g + 1) * D_STATE].astype(BF16)
            for g in range(SSD_GROUPS)]
    coff = SSD_INNER + SSD_GROUPS * D_STATE
    cmat = [xc[:, coff + g * D_STATE:coff + (g + 1) * D_STATE].astype(BF16) for g in range(SSD_GROUPS)]

    li = lax.broadcasted_iota(jnp.int32, (cl, cl), 0)
    si = lax.broadcasted_iota(jnp.int32, (cl, cl), 1)
    causal = li >= si
    lane_lo = si < SSD_HEADDIM
    sub_lo = li < SSD_HEADDIM

    def rows_padded(v):
        return v if valid_len == cl else jnp.concatenate([v, jnp.zeros((cl - valid_len, v.shape[1]), F32)], axis=0)

    dpre = rows_padded(dt_ref[...]) + dtb_ref[...]
    dtv = jnp.maximum(dpre, 0.0) + jnp.log(1.0 + jnp.exp(-jnp.abs(dpre)))
    if valid_len < cl:
        dtv = jnp.where(li < valid_len, dtv, 0.0)
    da = dtv * (-jnp.exp(alog_ref[...]))
    tril = causal.astype(F32)
    acum = jnp.dot(tril, da, preferred_element_type=F32, precision=HIGHEST)
    acum_t = acum.T
    dt_t = dtv.T
    last = acum[cl - 1:cl, :]
    e_all = jnp.exp(acum)
    w_all = jnp.exp(last - acum) * dtv
    e_last = jnp.exp(last)

    cb = [_dot_nt(cmat[g], bmat[g]) for g in range(SSD_GROUPS)]

    def colb(a, h):
        return jnp.broadcast_to(a[:, h:h + 1], (cl, cl))

    pairs = SSD_HEADS // 2
    for p in range(pairs):
        h0, h1 = 2 * p, 2 * p + 1
        g = h0 // (SSD_HEADS // SSD_GROUPS)
        xpair = xs[:, p * LANE:(p + 1) * LANE]
        x_lo = jnp.where(lane_lo, xpair, 0.0).astype(BF16)
        x_hi = jnp.where(lane_lo, 0.0, xpair).astype(BF16)
        ydiag = None
        for hh, xm in ((h0, x_lo), (h1, x_hi)):
            seg = colb(acum, hh) - acum_t[hh:hh + 1, :]
            dec = jnp.exp(jnp.where(causal, seg, -jnp.inf))
            m = (cb[g] * dec * dt_t[hh:hh + 1, :]).astype(BF16)
            part = _dot(m, xm)
            ydiag = part if ydiag is None else ydiag + part
        ecol = jnp.where(lane_lo, colb(e_all, h0), colb(e_all, h1))
        hp = h_scr[p]
        yoff = _dot_nt(cmat[g], hp.astype(BF16)) * ecol
        wcol = jnp.where(lane_lo, colb(w_all, h0), colb(w_all, h1))
        dx = (xpair * wcol).astype(BF16)
        st = _dot_tn(dx, bmat[g])
        hdec = jnp.where(sub_lo, e_last[:, h0:h0 + 1], e_last[:, h1:h1 + 1])
        h_scr[p] = hp * hdec + st
        y_scr[:, p * LANE:(p + 1) * LANE] = ydiag + yoff + dsk_ref[:, p * LANE:(p + 1) * LANE] * xpair

    zf = z_ref[...].astype(F32)
    gated = y_scr[0:valid_len, :] * _silu(zf)
    gw = SSD_INNER // SSD_GROUPS
    for g in range(SSD_GROUPS):
        sg = gated[:, g * gw:(g + 1) * gw]
        ms = jnp.mean(sg * sg, axis=-1, keepdims=True)
        yn_ref[:, g * gw:(g + 1) * gw] = (sg * lax.rsqrt(ms + RMS_EPS) * nw_ref[:, g * gw:(g + 1) * gw]).astype(yn_ref.dtype)

    @pl.when(c == nc - 1)
    def _():
        cnew_ref[...] = xp_scr[head + valid_len - (CONV_W - 1):head + valid_len, :]
        hnew_ref[...] = h_scr[...]


def _ssd(xbc, dt, z, conv_w, conv_b, dt_bias, a_log, d_skip, norm_w, cbuf, h0, nb, nc, valid_len):
    cl = SSD_CHUNK
    assert valid_len == cl or nc == 1
    pairs = SSD_HEADS // 2
    pad = LANE - SSD_HEADS
    row = lambda b, c: (b * nc + c, 0, 0)
    const = lambda b, c: (0, 0)
    chunks = lambda a: a.reshape(nb * nc, valid_len, a.shape[-1])
    z = z if valid_len % 16 == 0 else z.astype(F32)
    return pl.pallas_call(
        functools.partial(_ssd_kernel, valid_len=valid_len),
        grid=(nb, nc),
        in_specs=[pl.BlockSpec((None, valid_len, CONV_DIM), row), pl.BlockSpec((None, valid_len, LANE), row),
                  pl.BlockSpec((None, valid_len, SSD_INNER), row),
                  pl.BlockSpec((CONV_W, CONV_DIM), const), pl.BlockSpec((1, CONV_DIM), const),
                  pl.BlockSpec((1, LANE), const), pl.BlockSpec((1, LANE), const),
                  pl.BlockSpec((1, SSD_INNER), const), pl.BlockSpec((1, SSD_INNER), const),
                  pl.BlockSpec((None, CONV_W - 1, CONV_DIM), lambda b, c: (b, 0, 0)),
                  pl.BlockSpec((None, pairs, LANE, D_STATE), lambda b, c: (b, 0, 0, 0))],
        out_specs=[pl.BlockSpec((None, valid_len, SSD_INNER), row),
                   pl.BlockSpec((None, CONV_W - 1, CONV_DIM), lambda b, c: (b, 0, 0)),
                   pl.BlockSpec((None, pairs, LANE, D_STATE), lambda b, c: (b, 0, 0, 0))],
        out_shape=[jax.ShapeDtypeStruct((nb * nc, valid_len, SSD_INNER), z.dtype),
                   jax.ShapeDtypeStruct((nb, CONV_W - 1, CONV_DIM), F32),
                   jax.ShapeDtypeStruct((nb, pairs, LANE, D_STATE), F32)],
        scratch_shapes=[pltpu.VMEM((8 + cl, CONV_DIM), F32), pltpu.VMEM((pairs, LANE, D_STATE), F32),
                        pltpu.VMEM((cl, SSD_INNER), F32)],
        compiler_params=_params("arbitrary", "arbitrary"),
        name="ssd",
    )(chunks(xbc), chunks(dt), chunks(z), conv_w, conv_b.reshape(1, CONV_DIM),
      jnp.pad(dt_bias, (0, pad)).reshape(1, LANE), jnp.pad(a_log, (0, pad)).reshape(1, LANE),
      jnp.repeat(d_skip, SSD_HEADDIM).reshape(1, SSD_INNER), norm_w.reshape(1, SSD_INNER),
      cbuf, h0.reshape(nb, pairs, LANE, D_STATE))


def _bias_kernel(rel_ref, tab_ref, o_ref, *, shift_far):
    rel = rel_ref[...]
    n = jnp.maximum(rel, 0)
    max_exact = N_BUCKETS // 2
    large = max_exact + (jnp.log(jnp.maximum(n, 1).astype(F32) / max_exact)
                         / math.log(MAX_DISTANCE / max_exact) * (N_BUCKETS - max_exact)).astype(jnp.int32)
    bucket = jnp.where(n < max_exact, n, jnp.minimum(large, N_BUCKETS - 1))
    outs = [jnp.zeros(rel.shape, F32) for _ in range(ATTN_HEADS)]
    for kb in range(N_BUCKETS):
        hit = bucket == kb
        for h in range(ATTN_HEADS):
            far = tab_ref[N_BUCKETS - 1, h] if shift_far else 0.0
            outs[h] = jnp.where(hit, (tab_ref[kb, h] - far) * LOG2E, outs[h])
    for h in range(ATTN_HEADS):
        o_ref[h] = jnp.where(rel >= 0, outs[h], -jnp.inf)


def _bias_tiles(rel, rel_bias, shift_far):
    r, c = rel.shape
    tr = min(r, 64)
    return pl.pallas_call(
        functools.partial(_bias_kernel, shift_far=shift_far),
        grid=(r // tr,),
        in_specs=[pl.BlockSpec((tr, c), lambda i: (i, 0)),
                  pl.BlockSpec(memory_space=pltpu.SMEM)],
        out_specs=pl.BlockSpec((ATTN_HEADS, tr, c), lambda i: (0, i, 0)),
        out_shape=jax.ShapeDtypeStruct((ATTN_HEADS, r, c), F32),
        compiler_params=_params("arbitrary"),
        name="t5bias",
    )(rel, rel_bias)


def _lam(lamp_ref):
    lp = lamp_ref[...]
    e1 = jnp.exp(jnp.sum(lp[0:1, :] * lp[1:2, :], axis=-1, keepdims=True))
    e2 = jnp.exp(jnp.sum(lp[2:3, :] * lp[3:4, :], axis=-1, keepdims=True))
    return e1 - e2 + LAM_INIT


def _subln(o, sw):
    ms = jnp.mean(o * o, axis=-1, keepdims=True)
    return o * lax.rsqrt(ms + RMS_EPS) * sw * (1.0 - LAM_INIT)


ATTN_TQ = 512
ATTN_STRIP = 512
ATTN_TK = 512
ATTN_D_MIN = 1 - ATTN_STRIP // ATTN_TK
ATTN_D_FAR = -(-(MAX_DISTANCE + ATTN_TK - 1) // ATTN_TK)


def _attn_kernel(q_ref, k_ref, v_ref, bias_ref, lamp_ref, sw_ref, o_ref, m_scr, l_scr, a_scr, *, tq, strip, tk):
    qi = pl.program_id(2)
    lane_lo = lax.broadcasted_iota(jnp.int32, (strip, LANE), 1) < ATTN_HEAD_DIM
    reps = tk // LANE
    lam = _lam(lamp_ref)
    for st in range(tq // strip):
        q = q_ref[st * strip:(st + 1) * strip, :]
        zero = jnp.zeros_like(q)
        qm = (jnp.where(lane_lo, q, zero), jnp.where(lane_lo, zero, q))
        m_scr[...] = jnp.full(m_scr.shape, -jnp.inf, F32)
        l_scr[...] = jnp.zeros(l_scr.shape, F32)
        a_scr[...] = jnp.zeros(a_scr.shape, F32)
        row0 = qi * (tq // strip) + st
        ntiles = (row0 * strip + strip - 1) // tk + 1

        def tile(j, near):
            off = pl.multiple_of(j * tk, tk)
            kt = k_ref[pl.ds(off, tk), :]
            vt = v_ref[pl.ds(off, tk), :]
            scs = [_dot_nt(qm[mi], kt) for mi in range(2)]
            if near:
                bias = bias_ref[row0 * (strip // tk) - j - ATTN_D_MIN]
                scs = [sc + bias for sc in scs]
            ms = [jnp.maximum(m_scr[mi], jnp.max(scs[mi], axis=-1, keepdims=True)) for mi in range(2)]
            ps = [jnp.exp2(scs[mi] - jnp.concatenate([ms[mi]] * reps, axis=1)) for mi in range(2)]
            for mi in range(2):
                alpha = jnp.exp2(m_scr[mi] - ms[mi])
                l_scr[mi] = alpha * l_scr[mi] + jnp.sum(ps[mi], axis=-1, keepdims=True)
                a_scr[mi] = alpha * a_scr[mi] + _dot(ps[mi].astype(BF16), vt)
                m_scr[mi] = ms[mi]

        nfar = jnp.maximum(row0 * (strip // tk) - ATTN_D_FAR + 1, 0)
        lax.fori_loop(0, nfar, lambda j, c: (tile(j, False), c)[1], 0)
        lax.fori_loop(nfar, ntiles, lambda j, c: (tile(j, True), c)[1], 0)
        o = a_scr[0] / l_scr[0] - lam * (a_scr[1] / l_scr[1])
        o_ref[st * strip:(st + 1) * strip, :] = _subln(o, sw_ref[...]).astype(BF16)


def _attn_prompt(q, kb, vb, rel_bias, lamp, subln_w, nb, seq):
    tq, strip, tk = ATTN_TQ, ATTN_STRIP, ATTN_TK
    assert seq % tq == 0 and tq % strip == 0 and strip % tk == 0
    nq = seq // tq
    nd = ATTN_D_FAR - ATTN_D_MIN
    rel = (jnp.arange(ATTN_D_MIN, ATTN_D_FAR, dtype=jnp.int32)[:, None, None] * tk
           + jnp.arange(strip, dtype=jnp.int32)[None, :, None] - jnp.arange(tk, dtype=jnp.int32)[None, None, :])
    bias = _bias_tiles(rel.reshape(-1, tk), rel_bias, True).reshape(ATTN_HEADS, nd, strip, tk)
    return pl.pallas_call(
        functools.partial(_attn_kernel, tq=tq, strip=strip, tk=tk),
        grid=(nb, ATTN_HEADS, nq),
        in_specs=[pl.BlockSpec((tq, LANE), lambda b, h, i: (b * nq + i, h)),
                  pl.BlockSpec((seq, LANE), lambda b, h, i: (b, h)),
                  pl.BlockSpec((seq, LANE), lambda b, h, i: (b, h)),
                  pl.BlockSpec((None, nd, strip, tk), lambda b, h, i: (h, 0, 0, 0)),
                  pl.BlockSpec((4, ATTN_HEAD_DIM), lambda b, h, i: (0, 0)),
                  pl.BlockSpec((1, ATTN_V_DIM), lambda b, h, i: (0, 0))],
        out_specs=pl.BlockSpec((tq, LANE), lambda b, h, i: (b * nq + i, h)),
        out_shape=jax.ShapeDtypeStruct((nb * seq, ATTN_HEADS * ATTN_V_DIM), BF16),
        scratch_shapes=[pltpu.VMEM((2, strip, LANE), F32)] * 3,
        compiler_params=_params("arbitrary", "arbitrary", "arbitrary"),
        name="attn_prompt",
    )(q, kb, vb, bias, lamp, subln_w.reshape(1, ATTN_V_DIM))


DECODE_PAGES = 8
DECODE_SEQS = 1


def _decode_kernel(pt_ref, q_ref, kn_ref, vn_ref, *rest, nq, pps, nbq):
    npg = nbq * pps
    ck_refs, cv_refs = rest[:npg], rest[npg:2 * npg]
    blast_ref, bnew_ref, tab_ref, lamp_ref, sw_ref, o_ref, qa_scr, m_scr, l_scr, a_scr = rest[2 * npg:]
    j = pl.program_id(1)
    nsteps = pl.num_programs(1)
    is_last = j == nsteps - 1

    def head_rows(ref, h, n):
        return ref[pl.ds(h, n, stride=ATTN_HEADS), :]

    @pl.when(j == 0)
    def _():
        lane_lo = lax.broadcasted_iota(jnp.int32, (nq, LANE), 1) < ATTN_HEAD_DIM
        zpad = jnp.zeros((PAGE_SIZE - nq, LANE), F32)
        for bi in range(nbq):
            for h in range(ATTN_HEADS):
                qh = q_ref[bi, :, h * LANE:(h + 1) * LANE].astype(F32)
                qa = jnp.concatenate([jnp.where(lane_lo, qh, 0.0), jnp.where(lane_lo, 0.0, qh)], axis=0)
                qa_scr[bi, h] = qa
                kn = jnp.concatenate([head_rows(kn_ref.at[bi], h, nq), zpad], axis=0).astype(BF16)
                vn = jnp.concatenate([head_rows(vn_ref.at[bi], h, nq), zpad], axis=0).astype(BF16)
                s = _dot_nt(qa.astype(BF16), kn) + bnew_ref[h]
                m = jnp.max(s, axis=-1, keepdims=True)
                p = jnp.exp2(s - m)
                m_scr[bi, h] = m
                l_scr[bi, h] = jnp.sum(p, axis=-1, keepdims=True)
                a_scr[bi, h] = _dot(p.astype(BF16), vn)

    heads = [(bi, h) for bi in range(nbq) for h in range(ATTN_HEADS)]
    scores = {}
    for bi, h in heads:
        qa = qa_scr[bi, h].astype(BF16)
        far = tab_ref[N_BUCKETS - 1, h] * LOG2E
        for i in range(pps):
            bias = jnp.where(is_last, blast_ref[h], far) if i == pps - 1 else far
            kh = head_rows(ck_refs[bi * pps + i], h, PAGE_SIZE).astype(BF16)
            scores[bi, h, i] = _dot_nt(qa, kh) + bias
    m_new, probs = {}, {}
    for bi, h in heads:
        m = m_scr[bi, h]
        for i in range(pps):
            m = jnp.maximum(m, jnp.max(scores[bi, h, i], axis=-1, keepdims=True))
        m_new[bi, h] = m
        for i in range(pps):
            probs[bi, h, i] = jnp.exp2(scores[bi, h, i] - m)
    for bi, h in heads:
        alpha = jnp.exp2(m_scr[bi, h] - m_new[bi, h])
        l_new = alpha * l_scr[bi, h]
        a_new = alpha * a_scr[bi, h]
        for i in range(pps):
            p = probs[bi, h, i]
            l_new = l_new + jnp.sum(p, axis=-1, keepdims=True)
            a_new = a_new + _dot(p.astype(BF16), head_rows(cv_refs[bi * pps + i], h, PAGE_SIZE).astype(BF16))
        l_scr[bi, h] = l_new
        a_scr[bi, h] = a_new
        m_scr[bi, h] = m_new[bi, h]

    @pl.when(is_last)
    def _():
        lam = _lam(lamp_ref)
        for bi in range(nbq):
            for h in range(ATTN_HEADS):
                on = a_scr[bi, h] / l_scr[bi, h]
                o = on[0:nq, :] - lam * on[nq:2 * nq, :]
                o_ref[bi, :, h * LANE:(h + 1) * LANE] = _subln(o, sw_ref[...])


def _attn_decode(q, k_new, v_new, cache_k, cache_v, page_table, rel_bias, lamp, subln_w):
    nb, nq, _ = q.shape
    npages = page_table.shape[1]
    past = npages * PAGE_SIZE
    rows = 2 * nq
    prow = PAGE_SIZE * ATTN_HEADS
    nnew = nq * ATTN_HEADS
    assert npages >= 1 and nq <= PAGE_SIZE
    qpos = past + jnp.arange(rows, dtype=jnp.int32) % nq
    kpos = jnp.arange(past - PAGE_SIZE, past + PAGE_SIZE, dtype=jnp.int32).reshape(2, 1, PAGE_SIZE)
    rel = jnp.where(kpos < past + nq, qpos[None, :, None] - kpos, -1)
    tiles = _bias_tiles(rel.reshape(2 * rows, PAGE_SIZE), rel_bias, False).reshape(ATTN_HEADS, 2, rows, PAGE_SIZE)
    pps = math.gcd(DECODE_PAGES, npages)
    nbq = math.gcd(DECODE_SEQS, nb)
    new_q = pl.BlockSpec((nbq, nq, ATTN_HEADS * LANE), lambda b, j, pt: (b, 0, 0))
    new_kv = pl.BlockSpec((nbq, nnew, LANE), lambda b, j, pt: (b, 0, 0))
    head_tile = pl.BlockSpec((ATTN_HEADS, rows, PAGE_SIZE), lambda b, j, pt: (0, 0, 0))

    def page(bi, i):
        return pl.BlockSpec((None, prow, LANE), lambda b, j, pt: (pt[b * nbq + bi, j * pps + i], 0, 0))

    pages = [page(bi, i) for bi in range(nbq) for i in range(pps)]
    grid_spec = pltpu.PrefetchScalarGridSpec(
        num_scalar_prefetch=1,
        grid=(nb // nbq, npages // pps),
        in_specs=[new_q, new_kv, new_kv, *pages, *pages, head_tile, head_tile,
                  pl.BlockSpec(memory_space=pltpu.SMEM),
                  pl.BlockSpec((4, ATTN_HEAD_DIM), lambda b, j, pt: (0, 0)),
                  pl.BlockSpec((1, ATTN_V_DIM), lambda b, j, pt: (0, 0))],
        out_specs=new_q,
        scratch_shapes=[pltpu.VMEM((nbq, ATTN_HEADS, rows, LANE), F32), pltpu.VMEM((nbq, ATTN_HEADS, rows, 1), F32),
                        pltpu.VMEM((nbq, ATTN_HEADS, rows, 1), F32), pltpu.VMEM((nbq, ATTN_HEADS, rows, LANE), F32)])
    ck = cache_k.reshape(cache_k.shape[0], prow, LANE)
    cv = cache_v.reshape(cache_v.shape[0], prow, LANE)
    return pl.pallas_call(
        functools.partial(_decode_kernel, nq=nq, pps=pps, nbq=nbq),
        grid_spec=grid_spec,
        out_shape=jax.ShapeDtypeStruct((nb, nq, ATTN_HEADS * ATTN_V_DIM), F32),
        compiler_params=_params("arbitrary", "arbitrary"),
        name="attn_decode",
    )(page_table, q, k_new, v_new, *([ck] * len(pages)), *([cv] * len(pages)), tiles[:, 0], tiles[:, 1], rel_bias,
      lamp, subln_w.reshape(1, ATTN_V_DIM))


HALF = D_MODEL // 2
PACKED = jnp.int32


def _pack_rows(x):
    word = pltpu.pack_elementwise([x[:, :HALF], x[:, HALF:]], packed_dtype=BF16)
    return lax.bitcast_convert_type(word, PACKED)


def _unpack_rows(p):
    halves = [pltpu.unpack_elementwise(p, index=i, packed_dtype=BF16, unpacked_dtype=F32) for i in range(2)]
    return jnp.concatenate(halves, axis=1)


def _layer_norm(r, g, b):
    mu = jnp.mean(r, axis=-1, keepdims=True)
    d = r - mu
    var = jnp.mean(d * d, axis=-1, keepdims=True)
    return d * lax.rsqrt(var + LN_EPS) * g + b


def _merge_kernel(yn_ref, on_ref, ga_ref, gb_ref, x_ref, g1_ref, sc2_ref, sh2_ref, wa_ref, wb_ref, wo_ref,
                  lg_ref, lb_ref, wr_ref, x1_ref, u2p_ref, logit_ref):
    ba = _dot(yn_ref[...].astype(BF16), wa_ref[...])
    bb = _dot(on_ref[...].astype(BF16), wb_ref[...])
    merged = jax.nn.sigmoid(ga_ref[...].astype(F32)) * ba + jax.nn.sigmoid(gb_ref[...].astype(F32)) * bb
    t = _dot(merged.astype(BF16), wo_ref[...])
    x1 = _layer_norm(ALPHA * x_ref[...] + g1_ref[...] * t, lg_ref[...], lb_ref[...])
    x1_ref[...] = x1
    u2 = x1 * (1.0 + sc2_ref[...]) + sh2_ref[...]
    u2p_ref[...] = _pack_rows(u2)
    u_hi = u2.astype(BF16)
    u_lo = (u2 - u_hi.astype(F32)).astype(BF16)
    by_hi = _dot_nt(wr_ref[...], u_hi)
    logit_ref[...] = by_hi[:N_EXPERTS] + by_hi[N_EXPERTS:] + _dot_nt(wr_ref[:N_EXPERTS, :], u_lo)


def _mod_spec(mod, t, tm, rows_per_mod):
    if mod.shape[1] != 1:
        return mod.reshape(t, D_MODEL), pl.BlockSpec((tm, D_MODEL), lambda i: (i, 0))
    assert rows_per_mod % tm == 0
    return mod, pl.BlockSpec((None, 1, D_MODEL), lambda i: (i // (rows_per_mod // tm), 0, 0))


def _merge(yn, on, ga, gb, x, g1, sc2, sh2, wa, wb, wo, ln_g, ln_b, w_router, rows_per_mod):
    t = x.shape[0]
    tm = min(512, t)
    rows = pl.BlockSpec((tm, D_MODEL), lambda i: (i, 0))
    wspec = pl.BlockSpec((D_MODEL, D_MODEL), lambda i: (0, 0))
    vec = pl.BlockSpec((1, D_MODEL), lambda i: (0, 0))
    g1a, mspec = _mod_spec(g1, t, tm, rows_per_mod)
    sc2a, _ = _mod_spec(sc2, t, tm, rows_per_mod)
    sh2a, _ = _mod_spec(sh2, t, tm, rows_per_mod)
    wr_t = w_router.T
    wr_hi = wr_t.astype(BF16)
    return pl.pallas_call(
        _merge_kernel,
        grid=(t // tm,),
        in_specs=[rows, rows, rows, rows, rows, mspec, mspec, mspec, wspec, wspec, wspec, vec, vec,
                  pl.BlockSpec((2 * N_EXPERTS, D_MODEL), lambda i: (0, 0))],
        out_specs=[rows, pl.BlockSpec((tm, HALF), lambda i: (i, 0)), pl.BlockSpec((N_EXPERTS, tm), lambda i: (0, i))],
        out_shape=[jax.ShapeDtypeStruct((t, D_MODEL), F32), jax.ShapeDtypeStruct((t, HALF), PACKED),
                   jax.ShapeDtypeStruct((N_EXPERTS, t), F32)],
        compiler_params=_params("arbitrary"),
        name="merge",
    )(yn, on, ga, gb, x, g1a, sc2a, sh2a, wa, wb, wo, ln_g.reshape(1, D_MODEL), ln_b.reshape(1, D_MODEL),
      jnp.concatenate([wr_hi, (wr_t - wr_hi.astype(F32)).astype(BF16)], axis=0))


def _router_kernel(logit_ref, br_ref, e_ref, w_ref, r_ref, cnt_ref, carry_scr, tri_scr, *, tr):
    i = pl.program_id(0)

    @pl.when(i == 0)
    def _():
        carry_scr[...] = jnp.zeros_like(carry_scr)
        a = lax.broadcasted_iota(jnp.int32, (tr, tr), 0)
        b = lax.broadcasted_iota(jnp.int32, (tr, tr), 1)
        tri_scr[...] = (a < b).astype(BF16)

    scores = jax.nn.sigmoid(logit_ref[...])
    biased = scores + br_ref[...]
    ninf = -jnp.inf

    b3 = biased.reshape(N_EXPERT_GROUPS, GROUP_SIZE, tr)
    j3 = lax.broadcasted_iota(jnp.int32, b3.shape, 1).astype(F32)
    top1 = jnp.max(b3, axis=1, keepdims=True)
    first = jnp.min(jnp.where(b3 == top1, j3, float(GROUP_SIZE)), axis=1, keepdims=True)
    top2 = jnp.max(jnp.where(j3 == first, ninf, b3), axis=1, keepdims=True)
    gscore = (top1 + top2).reshape(N_EXPERT_GROUPS, tr)
    gi = lax.broadcasted_iota(jnp.int32, gscore.shape, 0).astype(F32)
    gsel = jnp.zeros(gscore.shape, F32)
    for _ in range(TOPK_GROUPS):
        mx = jnp.max(gscore, axis=0, keepdims=True)
        pick = gi == jnp.min(jnp.where(gscore == mx, gi, float(N_EXPERT_GROUPS)), axis=0, keepdims=True)
        gsel = jnp.where(pick, 1.0, gsel)
        gscore = jnp.where(pick, ninf, gscore)
    emask = jnp.broadcast_to(gsel.reshape(N_EXPERT_GROUPS, 1, tr), b3.shape).reshape(N_EXPERTS, tr)
    masked = jnp.where(emask > 0.5, biased, ninf)

    ei = lax.broadcasted_iota(jnp.int32, masked.shape, 0).astype(F32)
    picked = jnp.zeros(masked.shape, F32)
    idxs, wsel = [], []
    for _ in range(TOP_K):
        mx = jnp.max(masked, axis=0, keepdims=True)
        idx = jnp.min(jnp.where(masked == mx, ei, float(N_EXPERTS)), axis=0, keepdims=True)
        pick = ei == idx
        idxs.append(idx)
        wsel.append(jnp.sum(jnp.where(pick, scores, 0.0), axis=0, keepdims=True))
        picked = jnp.where(pick, 1.0, picked)
        masked = jnp.where(pick, ninf, masked)
    wall = jnp.concatenate(wsel, axis=0)
    w_ref[...] = wall / jnp.sum(wall, axis=0, keepdims=True) * ROUTED_SCALE
    e_ref[...] = jnp.concatenate(idxs, axis=0).astype(jnp.int32)

    rank = carry_scr[:, 0:1] + _dot(picked.astype(BF16), tri_scr[...])
    r_ref[...] = jnp.concatenate(
        [jnp.sum(jnp.where(ei == idx, rank, 0.0), axis=0, keepdims=True) for idx in idxs], axis=0).astype(jnp.int32)
    total = carry_scr[...] + jnp.sum(picked, axis=1, keepdims=True)
    carry_scr[...] = total
    cnt_ref[...] = total.astype(jnp.int32)


def _router(logits, b_router):
    t = logits.shape[1]
    tr = min(512, t)
    tok = pl.BlockSpec((TOP_K, tr), lambda i: (0, i))
    e_t, w_t, r_t, cnt = pl.pallas_call(
        functools.partial(_router_kernel, tr=tr),
        grid=(t // tr,),
        in_specs=[pl.BlockSpec((N_EXPERTS, tr), lambda i: (0, i)),
                  pl.BlockSpec((N_EXPERTS, 1), lambda i: (0, 0))],
        out_specs=[tok, tok, tok, pl.BlockSpec((N_EXPERTS, LANE), lambda i: (0, 0))],
        out_shape=[jax.ShapeDtypeStruct((TOP_K, t), jnp.int32), jax.ShapeDtypeStruct((TOP_K, t), F32),
                   jax.ShapeDtypeStruct((TOP_K, t), jnp.int32), jax.ShapeDtypeStruct((N_EXPERTS, LANE), jnp.int32)],
        scratch_shapes=[pltpu.VMEM((N_EXPERTS, LANE), F32), pltpu.VMEM((tr, tr), BF16)],
        compiler_params=_params("arbitrary"),
        name="router",
    )(logits, b_router.reshape(N_EXPERTS, 1))
    return e_t, w_t, r_t, cnt


def _plan_kernel(e_ref, r_ref, cnt_ref, dest_ref, be_ref, bv_ref, nu_ref, *, br, nblk_pad):
    cnt = cnt_ref[...]
    shift = br.bit_length() - 1
    padded = lax.shift_left(lax.shift_right_logical(cnt + (br - 1), shift), shift)
    ea = lax.broadcasted_iota(jnp.int32, (N_EXPERTS, N_EXPERTS), 0)
    eb = lax.broadcasted_iota(jnp.int32, (N_EXPERTS, N_EXPERTS), 1)
    pends = jnp.dot((eb <= ea).astype(F32), padded.astype(F32), preferred_element_type=F32,
                    precision=HIGHEST).astype(jnp.int32)
    pstart = pends - padded

    e = e_ref[...]
    first_row = jnp.zeros(e.shape, jnp.int32)
    for x in range(N_EXPERTS):
        first_row = jnp.where(e == x, pstart[x:x + 1, 0:1], first_row)
    dest_ref[...] = first_row + r_ref[...]

    @pl.when(pl.program_id(0) == 0)
    def _():
        esub = lax.broadcasted_iota(jnp.int32, (N_EXPERTS, LANE), 0)
        real_end = (pstart + cnt).astype(F32)
        for c in range(nblk_pad // LANE):
            bstart = (lax.broadcasted_iota(jnp.int32, (1, LANE), 1) + c * LANE) * br
            be = jnp.minimum(jnp.sum((pends <= bstart).astype(F32), axis=0, keepdims=True),
                             float(N_EXPERTS - 1)).astype(jnp.int32)
            end = jnp.sum(jnp.where(esub == be, real_end, 0.0), axis=0, keepdims=True).astype(jnp.int32)
            be_ref[:, c * LANE:(c + 1) * LANE] = be
            bv_ref[:, c * LANE:(c + 1) * LANE] = jnp.clip(end - bstart, 0, br)
        nu_ref[...] = lax.shift_right_logical(pends[N_EXPERTS - 1:N_EXPERTS, :], shift)


def _plan(e_t, r_t, cnt, br, nblk):
    t = e_t.shape[1]
    tc = min(2048, t)
    nblk_pad = -(-nblk // LANE) * LANE
    tok = pl.BlockSpec((TOP_K, tc), lambda i: (0, i))
    blk = pl.BlockSpec((1, nblk_pad), lambda i: (0, 0))
    dest, be, bv, nu = pl.pallas_call(
        functools.partial(_plan_kernel, br=br, nblk_pad=nblk_pad),
        grid=(t // tc,),
        in_specs=[tok, tok, pl.BlockSpec((N_EXPERTS, LANE), lambda i: (0, 0))],
        out_specs=[tok, blk, blk, pl.BlockSpec((1, LANE), lambda i: (0, 0))],
        out_shape=[jax.ShapeDtypeStruct((TOP_K, t), jnp.int32), jax.ShapeDtypeStruct((1, nblk_pad), jnp.int32),
                   jax.ShapeDtypeStruct((1, nblk_pad), jnp.int32), jax.ShapeDtypeStruct((1, LANE), jnp.int32)],
        compiler_params=_params("arbitrary"),
        name="moe_plan",
    )(e_t, r_t, cnt)
    return dest, be[0, :nblk], bv[0, :nblk], nu[0, :1]


def _row_copy(src, s, dst, d, sem):
    return pltpu.make_async_copy(src.at[pl.ds(s, 1), :], dst.at[pl.ds(d, 1), :], sem)


def _dispatch_kernel(dest_ref, u_ref, xs_ref, sem, *, tm):
    def start(r, carry):
        for k in range(TOP_K):
            _row_copy(u_ref, r, xs_ref, dest_ref[k, r], sem).start()
        return carry

    def wait(r, carry):
        for k in range(TOP_K):
            _row_copy(u_ref, r, xs_ref, dest_ref[k, r], sem).wait()
        return carry

    lax.fori_loop(0, tm, start, 0)
    lax.fori_loop(0, tm, wait, 0)


def _dispatch(u2p, dest_t, n_rows):
    t = u2p.shape[0]
    tm = min(256, t)
    nt = t // tm
    dest_blocks = dest_t.reshape(TOP_K, nt, tm).transpose(1, 0, 2)
    return pl.pallas_call(
        functools.partial(_dispatch_kernel, tm=tm),
        grid=(nt,),
        in_specs=[pl.BlockSpec((None, TOP_K, tm), lambda i: (i, 0, 0), memory_space=pltpu.SMEM),
                  pl.BlockSpec((tm, HALF), lambda i: (i, 0))],
        out_specs=pl.BlockSpec(memory_space=pl.ANY),
        out_shape=jax.ShapeDtypeStruct((n_rows, HALF), PACKED),
        scratch_shapes=[pltpu.SemaphoreType.DMA(())],
        compiler_params=_params("arbitrary"),
        name="moe_dispatch",
    )(dest_blocks, u2p)


SC_CORES = 2
SC_SUBCORES = 16
SC_WINDOW = 128
SC_WORKERS = SC_CORES * SC_SUBCORES


def _sc_mesh():
    return plsc.VectorSubcoreMesh(core_axis_name="c", subcore_axis_name="s")


def _sc_worker():
    return lax.axis_index("s") * SC_CORES + lax.axis_index("c")


def _sc_scatter_rows(rows, dest_blocks, n_rows):
    t, width = rows.shape
    nchunks = t // SC_WINDOW // SC_WORKERS

    def body(rows_hbm, dest_hbm, out_hbm, idx_v, rows_v, sem):
        wid = _sc_worker()

        @pl.loop(0, nchunks)
        def _(c):
            chunk = wid * nchunks + c
            pltpu.sync_copy(dest_hbm.at[chunk], idx_v)
            pltpu.sync_copy(rows_hbm.at[pl.ds(pl.multiple_of(chunk * SC_WINDOW, SC_WINDOW), SC_WINDOW)], rows_v)
            copies = [pltpu.async_copy(rows_v, out_hbm.at[idx_v.at[k]], sem) for k in range(TOP_K)]
            for cp in copies:
                cp.wait()

    return pl.kernel(
        body, out_type=jax.ShapeDtypeStruct((n_rows, width), rows.dtype), mesh=_sc_mesh(),
        scratch_types=[pltpu.VMEM((TOP_K, SC_WINDOW), jnp.int32), pltpu.VMEM((SC_WINDOW, width), rows.dtype),
                       pltpu.SemaphoreType.DMA],
        name="moe_sc_dispatch")(rows, dest_blocks)


def _sc_gather_rows(table, idx):
    n = idx.shape[0]
    width = table.shape[1]
    nchunks = n // SC_WINDOW // SC_WORKERS

    def body(table_hbm, idx_hbm, out_hbm, idx_v, rows_v, sem):
        wid = _sc_worker()

        @pl.loop(0, nchunks)
        def _(c):
            base = pl.multiple_of((wid * nchunks + c) * SC_WINDOW, SC_WINDOW)
            pltpu.sync_copy(idx_hbm.at[pl.ds(base, SC_WINDOW)], idx_v)
            pltpu.async_copy(table_hbm.at[idx_v], rows_v, sem).wait()
            pltpu.sync_copy(rows_v, out_hbm.at[pl.ds(base, SC_WINDOW)])

    return pl.kernel(
        body, out_type=jax.ShapeDtypeStruct((n, width), table.dtype), mesh=_sc_mesh(),
        scratch_types=[pltpu.VMEM((SC_WINDOW,), jnp.int32), pltpu.VMEM((SC_WINDOW, width), table.dtype),
                       pltpu.SemaphoreType.DMA],
        name="moe_sc_gather")(table, idx)


def _expert_kernel(be_ref, bv_ref, nu_ref, x_ref, wg_ref, wu_ref, wd_ref, after_ref, y_ref, wg_scr, wu_scr, wd_scr,
                   *, br):
    del after_ref
    i = pl.program_id(0)
    prev = be_ref[jnp.maximum(i - 1, 0)]

    @pl.when(jnp.logical_and(i < nu_ref[0], jnp.logical_or(i == 0, be_ref[i] != prev)))
    def _():
        wg_scr[...] = wg_ref[...].astype(BF16)
        wu_scr[...] = wu_ref[...].astype(BF16)
        wd_scr[...] = wd_ref[...].astype(BF16)

    @pl.when(i < nu_ref[0])
    def _():
        live = lax.broadcasted_iota(jnp.int32, (br, HALF), 0) < bv_ref[i]
        x = _unpack_rows(jnp.where(live, x_ref[...], 0)).astype(BF16)
        hcat = _silu(_dot(x, wg_scr[...])) * _dot(x, wu_scr[...])
        y_ref[...] = _pack_rows(_dot(hcat.astype(BF16), wd_scr[...]))


def _experts(x_sorted, blk_e, blk_valid, n_used, w_gate, w_up, w_down, br, after):
    n_rows = x_sorted.shape[0]
    nblk = n_rows // br

    def blk(i, be, bv, nu):
        return (jnp.minimum(i, nu[0] - 1), 0)

    grid_spec = pltpu.PrefetchScalarGridSpec(
        num_scalar_prefetch=3,
        grid=(nblk,),
        in_specs=[pl.BlockSpec((br, HALF), blk),
                  pl.BlockSpec((None, D_MODEL, D_EXPERT), lambda i, be, bv, nu: (be[i], 0, 0)),
                  pl.BlockSpec((None, D_MODEL, D_EXPERT), lambda i, be, bv, nu: (be[i], 0, 0)),
                  pl.BlockSpec((None, D_EXPERT, D_MODEL), lambda i, be, bv, nu: (be[i], 0, 0)),
                  pl.BlockSpec(after.shape, lambda i, be, bv, nu: (0,) * after.ndim)],
        out_specs=pl.BlockSpec((br, HALF), blk),
        scratch_shapes=[pltpu.VMEM((D_MODEL, D_EXPERT), BF16), pltpu.VMEM((D_MODEL, D_EXPERT), BF16),
                        pltpu.VMEM((D_EXPERT, D_MODEL), BF16)])
    return pl.pallas_call(
        functools.partial(_expert_kernel, br=br),
        grid_spec=grid_spec,
        out_shape=jax.ShapeDtypeStruct((n_rows, HALF), PACKED),
        compiler_params=_params("arbitrary"),
        name="moe_experts",
    )(blk_e, blk_valid, n_used, x_sorted, w_gate, w_up, w_down, after)


def _combine_kernel(dest_ref, ys_ref, w_ref, u_ref, x1_ref, g2_ref, sg_ref, su_ref, sd_ref, lg_ref, lb_ref,
                    o_ref, g_scr, sem, *, tm):
    def start(r, carry):
        for k in range(TOP_K):
            _row_copy(ys_ref, dest_ref[k, r], g_scr.at[k], r, sem).start()
        return carry

    def wait(r, carry):
        for k in range(TOP_K):
            _row_copy(ys_ref, dest_ref[k, r], g_scr.at[k], r, sem).wait()
        return carry

    lax.fori_loop(0, tm, start, 0)
    f = _shared_ffn(u_ref, sg_ref, su_ref, sd_ref)
    lax.fori_loop(0, tm, wait, 0)
    f = _add_routed(f, g_scr, w_ref[...])
    o_ref[...] = _layer_norm(ALPHA * x1_ref[...] + g2_ref[...] * f, lg_ref[...], lb_ref[...])


def _shared_ffn(u_ref, sg_ref, su_ref, sd_ref):
    ub = _unpack_rows(u_ref[...]).astype(BF16)
    hs = _silu(_dot(ub, sg_ref[...])) * _dot(ub, su_ref[...])
    return _dot(hs.astype(BF16), sd_ref[...])


def _add_routed(f, slots_ref, w):
    for k in range(TOP_K):
        f = f + _unpack_rows(slots_ref[k]) * w[:, k:k + 1]
    return f


def _combine_gathered_kernel(g_ref, w_ref, u_ref, x1_ref, g2_ref, sg_ref, su_ref, sd_ref, lg_ref, lb_ref, o_ref):
    f = _add_routed(_shared_ffn(u_ref, sg_ref, su_ref, sd_ref), g_ref, w_ref[...])
    o_ref[...] = _layer_norm(ALPHA * x1_ref[...] + g2_ref[...] * f, lg_ref[...], lb_ref[...])


def _combine(y_rows, dest_t, w_t, u2p, x1, g2, ws_gate, ws_up, ws_down, ln_g, ln_b, rows_per_mod, gathered):
    t = u2p.shape[0]
    tm = min(512 if gathered else 128, t)
    nt = t // tm
    rows = pl.BlockSpec((tm, D_MODEL), lambda i: (i, 0))
    vec = pl.BlockSpec((1, D_MODEL), lambda i: (0, 0))
    g2a, mspec = _mod_spec(g2, t, tm, rows_per_mod)
    common_specs = [pl.BlockSpec((tm, TOP_K), lambda i: (i, 0)),
                    pl.BlockSpec((tm, HALF), lambda i: (i, 0)), rows, mspec,
                    pl.BlockSpec((D_MODEL, D_SHARED), lambda i: (0, 0)),
                    pl.BlockSpec((D_MODEL, D_SHARED), lambda i: (0, 0)),
                    pl.BlockSpec((D_SHARED, D_MODEL), lambda i: (0, 0)),
                    vec, vec]
    common_args = (w_t.T, u2p, x1, g2a, ws_gate.astype(BF16), ws_up.astype(BF16), ws_down.astype(BF16),
                   ln_g.reshape(1, D_MODEL), ln_b.reshape(1, D_MODEL))
    if gathered:
        return pl.pallas_call(
            _combine_gathered_kernel,
            grid=(nt,),
            in_specs=[pl.BlockSpec((TOP_K, tm, HALF), lambda i: (0, i, 0))] + common_specs,
            out_specs=rows,
            out_shape=jax.ShapeDtypeStruct((t, D_MODEL), F32),
            compiler_params=_params("arbitrary"),
            name="moe_combine_gathered",
        )(y_rows.reshape(TOP_K, t, HALF), *common_args)
    dest_blocks = dest_t.reshape(TOP_K, nt, tm).transpose(1, 0, 2)
    return pl.pallas_call(
        functools.partial(_combine_kernel, tm=tm),
        grid=(nt,),
        in_specs=[pl.BlockSpec((None, TOP_K, tm), lambda i: (i, 0, 0), memory_space=pltpu.SMEM),
                  pl.BlockSpec(memory_space=pl.ANY)] + common_specs,
        out_specs=rows,
        out_shape=jax.ShapeDtypeStruct((t, D_MODEL), F32),
        scratch_shapes=[pltpu.VMEM((TOP_K, tm, HALF), PACKED), pltpu.SemaphoreType.DMA(())],
        compiler_params=_params("arbitrary"),
        name="moe_combine",
    )(dest_blocks, y_rows, *common_args)


def _moe(u2p, logits, x1, g2, p, rows_per_mod, br, after):
    t = u2p.shape[0]
    e_t, w_t, r_t, counts = _router(logits, p["b_router"])
    n_rows = (t * TOP_K + N_EXPERTS * (br - 1) + br - 1) // br * br
    dest_t, blk_e, blk_valid, n_used = _plan(e_t, r_t, counts, br, n_rows // br)
    on_sc = t % (SC_WORKERS * SC_WINDOW) == 0
    if on_sc:
        dest_blocks = dest_t.reshape(TOP_K, t // SC_WINDOW, SC_WINDOW).transpose(1, 0, 2)
        x_sorted = _sc_scatter_rows(u2p, dest_blocks, n_rows)
    else:
        x_sorted = _dispatch(u2p, dest_t, n_rows)
    y_rows = _experts(x_sorted, blk_e, blk_valid, n_used, p["w_gate"], p["w_up"], p["w_down"], br, after)
    if on_sc:
        y_rows = _sc_gather_rows(y_rows, dest_t.reshape(TOP_K * t))
    return _combine(y_rows, dest_t, w_t, u2p, x1, g2, p["ws_gate"], p["ws_up"], p["ws_down"],
                    p["ln2_g"], p["ln2_b"], rows_per_mod, on_sc)


def _cat_w_in(w_in):
    sizes = (SSD_INNER, CONV_DIM, SSD_HEADS, 1024, 1024, 1024, D_MODEL, D_MODEL)
    offs = [0]
    for s in sizes:
        offs.append(offs[-1] + s)
    z, xbc, dt, q, k, v, ga, gb = [w_in[:, offs[i]:offs[i + 1]] for i in range(8)]
    dt = jnp.pad(dt, ((0, 0), (0, SEG_END - SEG_DT - SSD_HEADS)))
    return jnp.concatenate([z, xbc, q, k, v, ga, gb, dt], axis=1).astype(BF16)


def kernel(x_prompt, x_sample, c_prompt, c_sample, cache_k, cache_v, page_table, state_conv, state_ssm, rel_bias, w_ada, b_ada, w_in, conv_w, conv_b, dt_bias, a_log, d_skip, ssd_norm_w, lam_q1, lam_k1, lam_q2, lam_k2, subln_w, w_br_ssd, w_br_attn, w_out, ln1_g, ln1_b, w_router, b_router, w_gate, w_up, w_down, ws_gate, ws_up, ws_down, ln2_g, ln2_b):
    assert w_in.shape[0] == DEPTH
    nbp, seq, _ = x_prompt.shape
    nbs, dseq, _ = x_sample.shape
    tp, ts = nbp * seq, nbs * dseq
    past = page_table.shape[1] * PAGE_SIZE
    cl = SSD_CHUNK

    pad_p = -nbp % 8
    c_all = jnp.concatenate([c_prompt, jnp.zeros((pad_p, D_MODEL), F32), jnp.repeat(c_sample, dseq, axis=0)], axis=0)
    mod = _adaln(c_all, w_ada[0], b_ada[0])
    mod_p = mod[:nbp].reshape(nbp, 1, 6, D_MODEL)
    mod_s = mod[nbp + pad_p:].reshape(nbs, dseq, 6, D_MODEL)
    mp = [mod_p[:, :, i] for i in range(6)]
    ms = [mod_s[:, :, i] for i in range(6)]

    w_cat = _cat_w_in(w_in[0])
    wa, wb, wo = w_br_ssd[0].astype(BF16), w_br_attn[0].astype(BF16), w_out[0].astype(BF16)
    lamp = jnp.stack([lam_q1[0], lam_k1[0], lam_q2[0], lam_k2[0]])
    moe_p = dict(b_router=b_router[0], w_gate=w_gate[0], w_up=w_up[0], w_down=w_down[0],
                 ws_gate=ws_gate[0], ws_up=ws_up[0], ws_down=ws_down[0], ln2_g=ln2_g[0], ln2_b=ln2_b[0])
    ssd_w = (conv_w[0], conv_b[0], dt_bias[0], a_log[0], d_skip[0], ssd_norm_w[0])

    xp = x_prompt.reshape(tp, D_MODEL)
    z, xbc, dt, q, k, v, kb, vb, ga, gb = _inproj(xp, mp[1], mp[0], w_cat, seq)
    yn, conv_p, h_p = _ssd(xbc, dt, z, *ssd_w, jnp.zeros((nbp, CONV_W - 1, CONV_DIM), F32),
                           jnp.zeros((nbp, SSD_HEADS, SSD_HEADDIM, D_STATE), F32), nbp, seq // cl, cl)
    yn = yn.reshape(tp, SSD_INNER)
    on = _attn_prompt(q, kb, vb, rel_bias, lamp, subln_w[0], nbp, seq)
    x1_p, u2p_p, logits_p = _merge(yn, on, ga, gb, xp, mp[2], mp[4], mp[3], wa, wb, wo, ln1_g[0], ln1_b[0],
                                   w_router[0], seq)
    k_prompt = k.reshape(1, nbp, seq, ATTN_HEADS, 2 * ATTN_HEAD_DIM)
    v_prompt = v.reshape(1, nbp, seq, ATTN_HEADS, ATTN_V_DIM)

    xs_ = x_sample.reshape(ts, D_MODEL)
    z, xbc, dt, q, k, v, kb, vb, ga, gb = _inproj(xs_, ms[1], ms[0], w_cat, dseq)
    yn, conv_s, h_s = _ssd(xbc, dt, z, *ssd_w, state_conv[0], state_ssm[0], nbs, 1, dseq)
    yn = yn.reshape(ts, SSD_INNER)
    on = _attn_decode(q.reshape(nbs, dseq, -1), k.reshape(nbs, dseq * ATTN_HEADS, LANE),
                      v.reshape(nbs, dseq * ATTN_HEADS, LANE), cache_k[0], cache_v[0], page_table,
                      rel_bias, lamp, subln_w[0])
    y_prompt = _moe(u2p_p, logits_p, x1_p, mp[5], moe_p, seq, 1024, on[0, :, :LANE]).reshape(nbp, seq, D_MODEL)
    x1, u2p, logits = _merge(yn, on.reshape(ts, -1), ga, gb, xs_, ms[2], ms[4], ms[3], wa, wb, wo, ln1_g[0], ln1_b[0],
                             w_router[0], dseq)
    y_sample = _moe(u2p, logits, x1, ms[5], moe_p, dseq, 64, x1[:8, :LANE]).reshape(nbs, dseq, D_MODEL)
    k_sample = k.reshape(1, nbs, dseq, ATTN_HEADS, 2 * ATTN_HEAD_DIM)
    v_sample = v.reshape(1, nbs, dseq, ATTN_HEADS, ATTN_V_DIM)

    return (y_prompt, y_sample, k_prompt, v_prompt, conv_p[None], h_p.reshape(1, nbp, SSD_HEADS, SSD_HEADDIM, D_STATE),
            k_sample, v_sample, conv_s[None], h_s.reshape(1, nbs, SSD_HEADS, SSD_HEADDIM, D_STATE))
```

```python
import functools
import math

import jax
import jax.numpy as jnp
from jax import lax
from jax.experimental import pallas as pl
from jax.experimental.pallas import tpu as pltpu
from jax.experimental.pallas import tpu_sc as plsc

F32 = jnp.float32
BF16 = jnp.bfloat16
HIGHEST = lax.Precision.HIGHEST

D_MODEL = 1024
SSD_INNER = 1024
SSD_HEADDIM = 64
SSD_HEADS = 16
SSD_GROUPS = 2
D_STATE = 128
CONV_W = 4
CONV_DIM = SSD_INNER + 2 * SSD_GROUPS * D_STATE
SSD_CHUNK = 128
ATTN_HEADS = 8
ATTN_HEAD_DIM = 64
ATTN_V_DIM = 128
N_BUCKETS = 32
MAX_DISTANCE = 128
N_EXPERTS = 64
N_EXPERT_GROUPS = 8
GROUP_SIZE = N_EXPERTS // N_EXPERT_GROUPS
TOPK_GROUPS = 4
TOP_K = 8
D_EXPERT = 256
D_SHARED = 256
ROUTED_SCALE = 2.5
PAGE_SIZE = 128
DEPTH = 1
ALPHA = (2 * DEPTH) ** 0.25
LN_EPS = 1e-5
RMS_EPS = 1e-5
LAM_INIT = 0.8 - 0.6 * math.exp(-0.3 * 0)
LOG2E = math.log2(math.e)
LANE = 128
VMEM_LIMIT = 56 * 1024 * 1024

SEG_Z, SEG_XBC, SEG_Q, SEG_K, SEG_V, SEG_GA, SEG_GB, SEG_DT, SEG_END = (
    0, 1024, 2560, 3584, 4608, 5632, 6656, 7680, 7808)


def _silu(x):
    return x * jax.nn.sigmoid(x)


def _dot(a, b):
    return jnp.dot(a, b, preferred_element_type=F32)


def _dot_nt(a, b):
    return lax.dot_general(a, b, (((1,), (1,)), ((), ())), preferred_element_type=F32)


def _dot_tn(a, b):
    return lax.dot_general(a, b, (((0,), (0,)), ((), ())), preferred_element_type=F32)


def _params(*sem):
    return pltpu.CompilerParams(dimension_semantics=sem, vmem_limit_bytes=VMEM_LIMIT)


def _adaln_kernel(c_ref, w_ref, b_ref, o_ref):
    s = _silu(c_ref[...]).astype(BF16)
    o_ref[...] = _dot(s, w_ref[...].astype(BF16)) + b_ref[...]


def _adaln(c, w_ada, b_ada):
    r = c.shape[0]
    n = w_ada.shape[1]
    tn = 1024
    return pl.pallas_call(
        _adaln_kernel,
        grid=(n // tn,),
        in_specs=[pl.BlockSpec((r, D_MODEL), lambda j: (0, 0)),
                  pl.BlockSpec((D_MODEL, tn), lambda j: (0, j)),
                  pl.BlockSpec((1, tn), lambda j: (0, j))],
        out_specs=pl.BlockSpec((r, tn), lambda j: (0, j)),
        out_shape=jax.ShapeDtypeStruct((r, n), F32),
        compiler_params=_params("arbitrary"),
        name="adaln",
    )(c, w_ada, b_ada.reshape(1, n))


def _inproj_kernel(x_ref, sc_ref, sh_ref, w_ref, z_ref, xbc_ref, dt_ref, q_ref, k_ref, v_ref,
                   kb_ref, vb_ref, ga_ref, gb_ref):
    u = (x_ref[...] * (1.0 + sc_ref[...]) + sh_ref[...]).astype(BF16)

    def seg(a, b):
        return _dot(u, w_ref[:, a:b])

    z_ref[...] = seg(SEG_Z, SEG_XBC).astype(BF16)
    xbc_ref[...] = seg(SEG_XBC, SEG_Q)
    q_ref[...] = (seg(SEG_Q, SEG_K) * (ATTN_HEAD_DIM ** -0.5 * LOG2E)).astype(BF16)
    kk = seg(SEG_K, SEG_V)
    k_ref[...] = kk
    kb_ref[...] = kk.astype(BF16)
    vv = seg(SEG_V, SEG_GA)
    v_ref[...] = vv
    vb_ref[...] = vv.astype(BF16)
    ga_ref[...] = seg(SEG_GA, SEG_GB).astype(BF16)
    gb_ref[...] = seg(SEG_GB, SEG_DT).astype(BF16)
    dt_ref[...] = seg(SEG_DT, SEG_END)


def _inproj(x, sc, sh, w_cat, rows_per_mod):
    t = x.shape[0]
    tm = min(256, t)
    per_row = sc.shape[1] != 1
    if per_row:
        sc2 = sc.reshape(t, D_MODEL)
        sh2 = sh.reshape(t, D_MODEL)
        mod_spec = pl.BlockSpec((tm, D_MODEL), lambda i: (i, 0))
    else:
        assert rows_per_mod % tm == 0
        sc2, sh2 = sc, sh
        mod_spec = pl.BlockSpec((None, 1, D_MODEL), lambda i: (i // (rows_per_mod // tm), 0, 0))

    def rows(width):
        return pl.BlockSpec((tm, width), lambda i: (i, 0))

    widths = (1024, CONV_DIM, LANE, 1024, 1024, 1024, 1024, 1024, 1024, 1024)
    dtypes = (BF16, F32, F32, BF16, F32, F32, BF16, BF16, BF16, BF16)
    return pl.pallas_call(
        _inproj_kernel,
        grid=(t // tm,),
        in_specs=[rows(D_MODEL), mod_spec, mod_spec,
                  pl.BlockSpec((D_MODEL, SEG_END), lambda i: (0, 0), pipeline_mode=pl.Buffered(1))],
        out_specs=[rows(w) for w in widths],
        out_shape=[jax.ShapeDtypeStruct((t, w), d) for w, d in zip(widths, dtypes)],
        compiler_params=_params("arbitrary"),
        name="inproj",
    )(x, sc2, sh2, w_cat)


def _ssd_kernel(xbc_ref, dt_ref, z_ref, cw_ref, cb_ref, dtb_ref, alog_ref, dsk_ref, nw_ref,
                cbuf_ref, h0_ref, yn_ref, cnew_ref, hnew_ref, xp_scr, h_scr, y_scr, *, valid_len):
    c = pl.program_id(1)
    nc = pl.num_programs(1)
    cl = SSD_CHUNK
    head = 8

    @pl.when(c == 0)
    def _():
        xp_scr[head - (CONV_W - 1):head, :] = cbuf_ref[...]
        h_scr[...] = h0_ref[...]

    @pl.when(c > 0)
    def _():
        xp_scr[head - (CONV_W - 1):head, :] = xp_scr[head + cl - (CONV_W - 1):head + cl, :]

    xp_scr[head:head + valid_len, :] = xbc_ref[...]
    if valid_len < cl:
        xp_scr[head + valid_len:head + cl, :] = jnp.zeros((cl - valid_len, CONV_DIM), F32)

    acc = xp_scr[head:head + cl, :] * cw_ref[CONV_W - 1:CONV_W, :]
    for j in range(CONV_W - 1):
        lo = head - (CONV_W - 1) + j
        acc = acc + xp_scr[lo:lo + cl, :] * cw_ref[j:j + 1, :]
    xc = _silu(acc + cb_ref[...])
    xs = xc[:, :SSD_INNER]
    bmat = [xc[:, SSD_INNER + g * D_STATE:SSD_INNER + (g + 1) * D_STATE].astype(BF16)
            for g in range(SSD_GROUPS)]
    coff = SSD_INNER + SSD_GROUPS * D_STATE
    cmat = [xc[:, coff + g * D_STATE:coff + (g + 1) * D_STATE].astype(BF16) for g in range(SSD_GROUPS)]

    li = lax.broadcasted_iota(jnp.int32, (cl, cl), 0)
    si = lax.broadcasted_iota(jnp.int32, (cl, cl), 1)
    causal = li >= si
    lane_lo = si < SSD_HEADDIM
    sub_lo = li < SSD_HEADDIM

    def rows_padded(v):
        return v if valid_len == cl else jnp.concatenate([v, jnp.zeros((cl - valid_len, v.shape[1]), F32)], axis=0)

    dpre = rows_padded(dt_ref[...]) + dtb_ref[...]
    dtv = jnp.maximum(dpre, 0.0) + jnp.log(1.0 + jnp.exp(-jnp.abs(dpre)))
    if valid_len < cl:
        dtv = jnp.where(li < valid_len, dtv, 0.0)
    da = dtv * (-jnp.exp(alog_ref[...]))
    tril = causal.astype(F32)
    acum = jnp.dot(tril, da, preferred_element_type=F32, precision=HIGHEST)
    acum_t = acum.T
    dt_t = dtv.T
    last = acum[cl - 1:cl, :]
    e_all = jnp.exp(acum)
    w_all = jnp.exp(last - acum) * dtv
    e_last = jnp.exp(last)

    cb = [_dot_nt(cmat[g], bmat[g]) for g in range(SSD_GROUPS)]

    def colb(a, h):
        return jnp.broadcast_to(a[:, h:h + 1], (cl, cl))

    pairs = SSD_HEADS // 2
    for p in range(pairs):
        h0, h1 = 2 * p, 2 * p + 1
        g = h0 // (SSD_HEADS // SSD_GROUPS)
        xpair = xs[:, p * LANE:(p + 1) * LANE]
        x_lo = jnp.where(lane_lo, xpair, 0.0).astype(BF16)
        x_hi = jnp.where(lane_lo, 0.0, xpair).astype(BF16)
        ydiag = None
        for hh, xm in ((h0, x_lo), (h1, x_hi)):
            seg = colb(acum, hh) - acum_t[hh:hh + 1, :]
            dec = jnp.exp(jnp.where(causal, seg, -jnp.inf))
            m = (cb[g] * dec * dt_t[hh:hh + 1, :]).astype(BF16)
            part = _dot(m, xm)
            ydiag = part if ydiag is None else ydiag + part
        ecol = jnp.where(lane_lo, colb(e_all, h0), colb(e_all, h1))
        hp = h_scr[p]
        yoff = _dot_nt(cmat[g], hp.astype(BF16)) * ecol
        wcol = jnp.where(lane_lo, colb(w_all, h0), colb(w_all, h1))
        dx = (xpair * wcol).astype(BF16)
        st = _dot_tn(dx, bmat[g])
        hdec = jnp.where(sub_lo, e_last[:, h0:h0 + 1], e_last[:, h1:h1 + 1])
        h_scr[p] = hp * hdec + st
        y_scr[:, p * LANE:(p + 1) * LANE] = ydiag + yoff + dsk_ref[:, p * LANE:(p + 1) * LANE] * xpair

    zf = z_ref[...].astype(F32)
    gated = y_scr[0:valid_len, :] * _silu(zf)
    gw = SSD_INNER // SSD_GROUPS
    for g in range(SSD_GROUPS):
        sg = gated[:, g * gw:(g + 1) * gw]
        ms = jnp.mean(sg * sg, axis=-1, keepdims=True)
        yn_ref[:, g * gw:(g + 1) * gw] = (sg * lax.rsqrt(ms + RMS_EPS) * nw_ref[:, g * gw:(g + 1) * gw]).astype(yn_ref.dtype)

    @pl.when(c == nc - 1)
    def _():
        cnew_ref[...] = xp_scr[head + valid_len - (CONV_W - 1):head + valid_len, :]
        hnew_ref[...] = h_scr[...]


def _ssd(xbc, dt, z, conv_w, conv_b, dt_bias, a_log, d_skip, norm_w, cbuf, h0, nb, nc, valid_len):
    cl = SSD_CHUNK
    assert valid_len == cl or nc == 1
    pairs = SSD_HEADS // 2
    pad = LANE - SSD_HEADS
    row = lambda b, c: (b * nc + c, 0, 0)
    const = lambda b, c: (0, 0)
    chunks = lambda a: a.reshape(nb * nc, valid_len, a.shape[-1])
    z = z if valid_len % 16 == 0 else z.astype(F32)
    return pl.pallas_call(
        functools.partial(_ssd_kernel, valid_len=valid_len),
        grid=(nb, nc),
        in_specs=[pl.BlockSpec((None, valid_len, CONV_DIM), row), pl.BlockSpec((None, valid_len, LANE), row),
                  pl.BlockSpec((None, valid_len, SSD_INNER), row),
                  pl.BlockSpec((CONV_W, CONV_DIM), const), pl.BlockSpec((1, CONV_DIM), const),
                  pl.BlockSpec((1, LANE), const), pl.BlockSpec((1, LANE), const),
                  pl.BlockSpec((1, SSD_INNER), const), pl.BlockSpec((1, SSD_INNER), const),
                  pl.BlockSpec((None, CONV_W - 1, CONV_DIM), lambda b, c: (b, 0, 0)),
                  pl.BlockSpec((None, pairs, LANE, D_STATE), lambda b, c: (b, 0, 0, 0))],
        out_specs=[pl.BlockSpec((None, valid_len, SSD_INNER), row),
                   pl.BlockSpec((None, CONV_W - 1, CONV_DIM), lambda b, c: (b, 0, 0)),
                   pl.BlockSpec((None, pairs, LANE, D_STATE), lambda b, c: (b, 0, 0, 0))],
        out_shape=[jax.ShapeDtypeStruct((nb * nc, valid_len, SSD_INNER), z.dtype),
                   jax.ShapeDtypeStruct((nb, CONV_W - 1, CONV_DIM), F32),
                   jax.ShapeDtypeStruct((nb, pairs, LANE, D_STATE), F32)],
        scratch_shapes=[pltpu.VMEM((8 + cl, CONV_DIM), F32), pltpu.VMEM((pairs, LANE, D_STATE), F32),
                        pltpu.VMEM((cl, SSD_INNER), F32)],
        compiler_params=_params("arbitrary", "arbitrary"),
        name="ssd",
    )(chunks(xbc), chunks(dt), chunks(z), conv_w, conv_b.reshape(1, CONV_DIM),
      jnp.pad(dt_bias, (0, pad)).reshape(1, LANE), jnp.pad(a_log, (0, pad)).reshape(1, LANE),
      jnp.repeat(d_skip, SSD_HEADDIM).reshape(1, SSD_INNER), norm_w.reshape(1, SSD_INNER),
      cbuf, h0.reshape(nb, pairs, LANE, D_STATE))


def _bias_kernel(rel_ref, tab_ref, o_ref, *, shift_far):
    rel = rel_ref[...]
    n = jnp.maximum(rel, 0)
    max_exact = N_BUCKETS // 2
    large = max_exact + (jnp.log(jnp.maximum(n, 1).astype(F32) / max_exact)
                         / math.log(MAX_DISTANCE / max_exact) * (N_BUCKETS - max_exact)).astype(jnp.int32)
    bucket = jnp.where(n < max_exact, n, jnp.minimum(large, N_BUCKETS - 1))
    outs = [jnp.zeros(rel.shape, F32) for _ in range(ATTN_HEADS)]
    for kb in range(N_BUCKETS):
        hit = bucket == kb
        for h in range(ATTN_HEADS):
            far = tab_ref[N_BUCKETS - 1, h] if shift_far else 0.0
            outs[h] = jnp.where(hit, (tab_ref[kb, h] - far) * LOG2E, outs[h])
    for h in range(ATTN_HEADS):
        o_ref[h] = jnp.where(rel >= 0, outs[h], -jnp.inf)


def _bias_tiles(rel, rel_bias, shift_far):
    r, c = rel.shape
    tr = min(r, 64)
    return pl.pallas_call(
        functools.partial(_bias_kernel, shift_far=shift_far),
        grid=(r // tr,),
        in_specs=[pl.BlockSpec((tr, c), lambda i: (i, 0)),
                  pl.BlockSpec(memory_space=pltpu.SMEM)],
        out_specs=pl.BlockSpec((ATTN_HEADS, tr, c), lambda i: (0, i, 0)),
        out_shape=jax.ShapeDtypeStruct((ATTN_HEADS, r, c), F32),
        compiler_params=_params("arbitrary"),
        name="t5bias",
    )(rel, rel_bias)


def _lam(lamp_ref):
    lp = lamp_ref[...]
    e1 = jnp.exp(jnp.sum(lp[0:1, :] * lp[1:2, :], axis=-1, keepdims=True))
    e2 = jnp.exp(jnp.sum(lp[2:3, :] * lp[3:4, :], axis=-1, keepdims=True))
    return e1 - e2 + LAM_INIT


def _subln(o, sw):
    ms = jnp.mean(o * o, axis=-1, keepdims=True)
    return o * lax.rsqrt(ms + RMS_EPS) * sw * (1.0 - LAM_INIT)


ATTN_TQ = 512
ATTN_STRIP = 512
ATTN_TK = 512
ATTN_D_MIN = 1 - ATTN_STRIP // ATTN_TK
ATTN_D_FAR = -(-(MAX_DISTANCE + ATTN_TK - 1) // ATTN_TK)


def _attn_kernel(q_ref, k_ref, v_ref, bias_ref, lamp_ref, sw_ref, o_ref, m_scr, l_scr, a_scr, *, tq, strip, tk):
    qi = pl.program_id(2)
    lane_lo = lax.broadcasted_iota(jnp.int32, (strip, LANE), 1) < ATTN_HEAD_DIM
    lam = _lam(lamp_ref)
    for st in range(tq // strip):
        q = q_ref[st * strip:(st + 1) * strip, :]
        zero = jnp.zeros_like(q)
        qm = (jnp.where(lane_lo, q, zero), jnp.where(lane_lo, zero, q))
        m_scr[...] = jnp.full(m_scr.shape, -jnp.inf, F32)
        l_scr[...] = jnp.zeros(l_scr.shape, F32)
        a_scr[...] = jnp.zeros(a_scr.shape, F32)
        row0 = qi * (tq // strip) + st
        ntiles = (row0 * strip + strip - 1) // tk + 1

        def tile(j, near, r0=0, nr=strip, nc=tk):
            off = pl.multiple_of(j * tk, tk)
            kt = k_ref[pl.ds(off, nc), :]
            vt = v_ref[pl.ds(off, nc), :]
            rows = slice(r0, r0 + nr)
            scs = [_dot_nt(qm[mi][rows], kt) for mi in range(2)]
            if near:
                bias = bias_ref[row0 * (strip // tk) - j - ATTN_D_MIN, rows, 0:nc]
                scs = [sc + bias for sc in scs]
            ms = [jnp.maximum(m_scr[mi, rows], jnp.max(scs[mi], axis=-1, keepdims=True)) for mi in range(2)]
            ps = [jnp.exp2(scs[mi] - jnp.concatenate([ms[mi]] * (nc // LANE), axis=1)) for mi in range(2)]
            for mi in range(2):
                alpha = jnp.exp2(m_scr[mi, rows] - ms[mi])
                l_scr[mi, rows] = alpha * l_scr[mi, rows] + jnp.sum(ps[mi], axis=-1, keepdims=True)
                a_scr[mi, rows] = alpha * a_scr[mi, rows] + _dot(ps[mi].astype(BF16), vt)
                m_scr[mi, rows] = ms[mi]

        nfar = jnp.maximum(row0 * (strip // tk) - ATTN_D_FAR + 1, 0)
        lax.fori_loop(0, nfar, lambda j, c: (tile(j, False), c)[1], 0)
        if strip == tk:
            lax.fori_loop(nfar, ntiles - 1, lambda j, c: (tile(j, True), c)[1], 0)
            tile(ntiles - 1, True, 0, strip // 2, tk // 2)
            tile(ntiles - 1, True, strip // 2, strip // 2, tk)
        else:
            lax.fori_loop(nfar, ntiles, lambda j, c: (tile(j, True), c)[1], 0)
        o = a_scr[0] / l_scr[0] - lam * (a_scr[1] / l_scr[1])
        o_ref[st * strip:(st + 1) * strip, :] = _subln(o, sw_ref[...]).astype(BF16)


def _attn_prompt(q, kb, vb, rel_bias, lamp, subln_w, nb, seq):
    tq, strip, tk = ATTN_TQ, ATTN_STRIP, ATTN_TK
    assert seq % tq == 0 and tq % strip == 0 and strip % tk == 0
    nq = seq // tq
    nd = ATTN_D_FAR - ATTN_D_MIN
    rel = (jnp.arange(ATTN_D_MIN, ATTN_D_FAR, dtype=jnp.int32)[:, None, None] * tk
           + jnp.arange(strip, dtype=jnp.int32)[None, :, None] - jnp.arange(tk, dtype=jnp.int32)[None, None, :])
    bias = _bias_tiles(rel.reshape(-1, tk), rel_bias, True).reshape(ATTN_HEADS, nd, strip, tk)
    return pl.pallas_call(
        functools.partial(_attn_kernel, tq=tq, strip=strip, tk=tk),
        grid=(nb, ATTN_HEADS, nq),
        in_specs=[pl.BlockSpec((tq, LANE), lambda b, h, i: (b * nq + i, h)),
                  pl.BlockSpec((seq, LANE), lambda b, h, i: (b, h)),
                  pl.BlockSpec((seq, LANE), lambda b, h, i: (b, h)),
                  pl.BlockSpec((None, nd, strip, tk), lambda b, h, i: (h, 0, 0, 0)),
                  pl.BlockSpec((4, ATTN_HEAD_DIM), lambda b, h, i: (0, 0)),
                  pl.BlockSpec((1, ATTN_V_DIM), lambda b, h, i: (0, 0))],
        out_specs=pl.BlockSpec((tq, LANE), lambda b, h, i: (b * nq + i, h)),
        out_shape=jax.ShapeDtypeStruct((nb * seq, ATTN_HEADS * ATTN_V_DIM), BF16),
        scratch_shapes=[pltpu.VMEM((2, strip, LANE), F32)] * 3,
        compiler_params=_params("arbitrary", "arbitrary", "arbitrary"),
        name="attn_prompt",
    )(q, kb, vb, bias, lamp, subln_w.reshape(1, ATTN_V_DIM))


DECODE_PAGES = 8
DECODE_SEQS = 1


def _decode_kernel(pt_ref, q_ref, kn_ref, vn_ref, *rest, nq, pps, nbq):
    npg = nbq * pps
    ck_refs, cv_refs = rest[:npg], rest[npg:2 * npg]
    blast_ref, bnew_ref, tab_ref, lamp_ref, sw_ref, o_ref, qa_scr, m_scr, l_scr, a_scr = rest[2 * npg:]
    j = pl.program_id(1)
    nsteps = pl.num_programs(1)
    is_last = j == nsteps - 1

    def head_rows(ref, h, n):
        return ref[pl.ds(h, n, stride=ATTN_HEADS), :]

    @pl.when(j == 0)
    def _():
        lane_lo = lax.broadcasted_iota(jnp.int32, (nq, LANE), 1) < ATTN_HEAD_DIM
        zpad = jnp.zeros((PAGE_SIZE - nq, LANE), F32)
        for bi in range(nbq):
            for h in range(ATTN_HEADS):
                qh = q_ref[bi, :, h * LANE:(h + 1) * LANE].astype(F32)
                qa = jnp.concatenate([jnp.where(lane_lo, qh, 0.0), jnp.where(lane_lo, 0.0, qh)], axis=0)
                qa_scr[bi, h] = qa
                kn = jnp.concatenate([head_rows(kn_ref.at[bi], h, nq), zpad], axis=0).astype(BF16)
                vn = jnp.concatenate([head_rows(vn_ref.at[bi], h, nq), zpad], axis=0).astype(BF16)
                s = _dot_nt(qa.astype(BF16), kn) + bnew_ref[h]
                m = jnp.max(s, axis=-1, keepdims=True)
                p = jnp.exp2(s - m)
                m_scr[bi, h] = m
                l_scr[bi, h] = jnp.sum(p, axis=-1, keepdims=True)
                a_scr[bi, h] = _dot(p.astype(BF16), vn)

    heads = [(bi, h) for bi in range(nbq) for h in range(ATTN_HEADS)]
    scores = {}
    for bi, h in heads:
        qa = qa_scr[bi, h].astype(BF16)
        far = tab_ref[N_BUCKETS - 1, h] * LOG2E
        for i in range(pps):
            bias = jnp.where(is_last, blast_ref[h], far) if i == pps - 1 else far
            kh = head_rows(ck_refs[bi * pps + i], h, PAGE_SIZE).astype(BF16)
            scores[bi, h, i] = _dot_nt(qa, kh) + bias
    m_new, probs = {}, {}
    for bi, h in heads:
        m = m_scr[bi, h]
        for i in range(pps):
            m = jnp.maximum(m, jnp.max(scores[bi, h, i], axis=-1, keepdims=True))
        m_new[bi, h] = m
        for i in range(pps):
            probs[bi, h, i] = jnp.exp2(scores[bi, h, i] - m)
    for bi, h in heads:
        alpha = jnp.exp2(m_scr[bi, h] - m_new[bi, h])
        l_new = alpha * l_scr[bi, h]
        a_new = alpha * a_scr[bi, h]
        for i in range(pps):
            p = probs[bi, h, i]
            l_new = l_new + jnp.sum(p, axis=-1, keepdims=True)
            a_new = a_new + _dot(p.astype(BF16), head_rows(cv_refs[bi * pps + i], h, PAGE_SIZE).astype(BF16))
        l_scr[bi, h] = l_new
        a_scr[bi, h] = a_new
        m_scr[bi, h] = m_new[bi, h]

    @pl.when(is_last)
    def _():
        lam = _lam(lamp_ref)
        for bi in range(nbq):
            for h in range(ATTN_HEADS):
                on = a_scr[bi, h] / l_scr[bi, h]
                o = on[0:nq, :] - lam * on[nq:2 * nq, :]
                o_ref[bi, :, h * LANE:(h + 1) * LANE] = _subln(o, sw_ref[...])


def _attn_decode(q, k_new, v_new, cache_k, cache_v, page_table, rel_bias, lamp, subln_w):
    nb, nq, _ = q.shape
    npages = page_table.shape[1]
    past = npages * PAGE_SIZE
    rows = 2 * nq
    prow = PAGE_SIZE * ATTN_HEADS
    nnew = nq * ATTN_HEADS
    assert npages >= 1 and nq <= PAGE_SIZE
    qpos = past + jnp.arange(rows, dtype=jnp.int32) % nq
    kpos = jnp.arange(past - PAGE_SIZE, past + PAGE_SIZE, dtype=jnp.int32).reshape(2, 1, PAGE_SIZE)
    rel = jnp.where(kpos < past + nq, qpos[None, :, None] - kpos, -1)
    tiles = _bias_tiles(rel.reshape(2 * rows, PAGE_SIZE), rel_bias, False).reshape(ATTN_HEADS, 2, rows, PAGE_SIZE)
    pps = math.gcd(DECODE_PAGES, npages)
    nbq = math.gcd(DECODE_SEQS, nb)
    new_q = pl.BlockSpec((nbq, nq, ATTN_HEADS * LANE), lambda b, j, pt: (b, 0, 0))
    new_kv = pl.BlockSpec((nbq, nnew, LANE), lambda b, j, pt: (b, 0, 0))
    head_tile = pl.BlockSpec((ATTN_HEADS, rows, PAGE_SIZE), lambda b, j, pt: (0, 0, 0))

    def page(bi, i):
        return pl.BlockSpec((None, prow, LANE), lambda b, j, pt: (pt[b * nbq + bi, j * pps + i], 0, 0))

    pages = [page(bi, i) for bi in range(nbq) for i in range(pps)]
    grid_spec = pltpu.PrefetchScalarGridSpec(
        num_scalar_prefetch=1,
        grid=(nb // nbq, npages // pps),
        in_specs=[new_q, new_kv, new_kv, *pages, *pages, head_tile, head_tile,
                  pl.BlockSpec(memory_space=pltpu.SMEM),
                  pl.BlockSpec((4, ATTN_HEAD_DIM), lambda b, j, pt: (0, 0)),
                  pl.BlockSpec((1, ATTN_V_DIM), lambda b, j, pt: (0, 0))],
        out_specs=new_q,
        scratch_shapes=[pltpu.VMEM((nbq, ATTN_HEADS, rows, LANE), F32), pltpu.VMEM((nbq, ATTN_HEADS, rows, 1), F32),
                        pltpu.VMEM((nbq, ATTN_HEADS, rows, 1), F32), pltpu.VMEM((nbq, ATTN_HEADS, rows, LANE), F32)])
    ck = cache_k.reshape(cache_k.shape[0], prow, LANE)
    cv = cache_v.reshape(cache_v.shape[0], prow, LANE)
    return pl.pallas_call(
        functools.partial(_decode_kernel, nq=nq, pps=pps, nbq=nbq),
        grid_spec=grid_spec,
        out_shape=jax.ShapeDtypeStruct((nb, nq, ATTN_HEADS * ATTN_V_DIM), F32),
        compiler_params=_params("arbitrary", "arbitrary"),
        name="attn_decode",
    )(page_table, q, k_new, v_new, *([ck] * len(pages)), *([cv] * len(pages)), tiles[:, 0], tiles[:, 1], rel_bias,
      lamp, subln_w.reshape(1, ATTN_V_DIM))


HALF = D_MODEL // 2
PACKED = jnp.int32


def _pack_rows(x):
    word = pltpu.pack_elementwise([x[:, :HALF], x[:, HALF:]], packed_dtype=BF16)
    return lax.bitcast_convert_type(word, PACKED)


def _unpack_rows(p):
    halves = [pltpu.unpack_elementwise(p, index=i, packed_dtype=BF16, unpacked_dtype=F32) for i in range(2)]
    return jnp.concatenate(halves, axis=1)


def _layer_norm(r, g, b):
    mu = jnp.mean(r, axis=-1, keepdims=True)
    d = r - mu
    var = jnp.mean(d * d, axis=-1, keepdims=True)
    return d * lax.rsqrt(var + LN_EPS) * g + b


def _merge_kernel(yn_ref, on_ref, ga_ref, gb_ref, x_ref, g1_ref, sc2_ref, sh2_ref, wa_ref, wb_ref, wo_ref,
                  lg_ref, lb_ref, wr_ref, x1_ref, u2p_ref, logit_ref):
    ba = _dot(yn_ref[...].astype(BF16), wa_ref[...])
    bb = _dot(on_ref[...].astype(BF16), wb_ref[...])
    merged = jax.nn.sigmoid(ga_ref[...].astype(F32)) * ba + jax.nn.sigmoid(gb_ref[...].astype(F32)) * bb
    t = _dot(merged.astype(BF16), wo_ref[...])
    x1 = _layer_norm(ALPHA * x_ref[...] + g1_ref[...] * t, lg_ref[...], lb_ref[...])
    x1_ref[...] = x1
    u2 = x1 * (1.0 + sc2_ref[...]) + sh2_ref[...]
    u2p_ref[...] = _pack_rows(u2)
    u_hi = u2.astype(BF16)
    u_lo = (u2 - u_hi.astype(F32)).astype(BF16)
    by_hi = _dot_nt(wr_ref[...], u_hi)
    logit_ref[...] = by_hi[:N_EXPERTS] + by_hi[N_EXPERTS:] + _dot_nt(wr_ref[:N_EXPERTS, :], u_lo)


def _mod_spec(mod, t, tm, rows_per_mod):
    if mod.shape[1] != 1:
        return mod.reshape(t, D_MODEL), pl.BlockSpec((tm, D_MODEL), lambda i: (i, 0))
    assert rows_per_mod % tm == 0
    return mod, pl.BlockSpec((None, 1, D_MODEL), lambda i: (i // (rows_per_mod // tm), 0, 0))


def _merge(yn, on, ga, gb, x, g1, sc2, sh2, wa, wb, wo, ln_g, ln_b, w_router, rows_per_mod):
    t = x.shape[0]
    tm = min(512, t)
    rows = pl.BlockSpec((tm, D_MODEL), lambda i: (i, 0))
    wspec = pl.BlockSpec((D_MODEL, D_MODEL), lambda i: (0, 0))
    vec = pl.BlockSpec((1, D_MODEL), lambda i: (0, 0))
    g1a, mspec = _mod_spec(g1, t, tm, rows_per_mod)
    sc2a, _ = _mod_spec(sc2, t, tm, rows_per_mod)
    sh2a, _ = _mod_spec(sh2, t, tm, rows_per_mod)
    wr_t = w_router.T
    wr_hi = wr_t.astype(BF16)
    return pl.pallas_call(
        _merge_kernel,
        grid=(t // tm,),
        in_specs=[rows, rows, rows, rows, rows, mspec, mspec, mspec, wspec, wspec, wspec, vec, vec,
                  pl.BlockSpec((2 * N_EXPERTS, D_MODEL), lambda i: (0, 0))],
        out_specs=[rows, pl.BlockSpec((tm, HALF), lambda i: (i, 0)), pl.BlockSpec((N_EXPERTS, tm), lambda i: (0, i))],
        out_shape=[jax.ShapeDtypeStruct((t, D_MODEL), F32), jax.ShapeDtypeStruct((t, HALF), PACKED),
                   jax.ShapeDtypeStruct((N_EXPERTS, t), F32)],
        compiler_params=_params("arbitrary"),
        name="merge",
    )(yn, on, ga, gb, x, g1a, sc2a, sh2a, wa, wb, wo, ln_g.reshape(1, D_MODEL), ln_b.reshape(1, D_MODEL),
      jnp.concatenate([wr_hi, (wr_t - wr_hi.astype(F32)).astype(BF16)], axis=0))


def _router_kernel(logit_ref, br_ref, e_ref, w_ref, r_ref, cnt_ref, carry_scr, tri_scr, *, tr):
    i = pl.program_id(0)

    @pl.when(i == 0)
    def _():
        carry_scr[...] = jnp.zeros_like(carry_scr)
        a = lax.broadcasted_iota(jnp.int32, (tr, tr), 0)
        b = lax.broadcasted_iota(jnp.int32, (tr, tr), 1)
        tri_scr[...] = (a < b).astype(BF16)

    scores = jax.nn.sigmoid(logit_ref[...])
    biased = scores + br_ref[...]
    ninf = -jnp.inf

    b3 = biased.reshape(N_EXPERT_GROUPS, GROUP_SIZE, tr)
    j3 = lax.broadcasted_iota(jnp.int32, b3.shape, 1).astype(F32)
    top1 = jnp.max(b3, axis=1, keepdims=True)
    first = jnp.min(jnp.where(b3 == top1, j3, float(GROUP_SIZE)), axis=1, keepdims=True)
    top2 = jnp.max(jnp.where(j3 == first, ninf, b3), axis=1, keepdims=True)
    gscore = (top1 + top2).reshape(N_EXPERT_GROUPS, tr)
    gi = lax.broadcasted_iota(jnp.int32, gscore.shape, 0).astype(F32)
    gsel = jnp.zeros(gscore.shape, F32)
    for _ in range(TOPK_GROUPS):
        mx = jnp.max(gscore, axis=0, keepdims=True)
        pick = gi == jnp.min(jnp.where(gscore == mx, gi, float(N_EXPERT_GROUPS)), axis=0, keepdims=True)
        gsel = jnp.where(pick, 1.0, gsel)
        gscore = jnp.where(pick, ninf, gscore)
    emask = jnp.broadcast_to(gsel.reshape(N_EXPERT_GROUPS, 1, tr), b3.shape).reshape(N_EXPERTS, tr)
    masked = jnp.where(emask > 0.5, biased, ninf)

    ei = lax.broadcasted_iota(jnp.int32, masked.shape, 0).astype(F32)
    picked = jnp.zeros(masked.shape, F32)
    idxs, wsel = [], []
    for _ in range(TOP_K):
        mx = jnp.max(masked, axis=0, keepdims=True)
        idx = jnp.min(jnp.where(masked == mx, ei, float(N_EXPERTS)), axis=0, keepdims=True)
        pick = ei == idx
        idxs.append(idx)
        wsel.append(jnp.sum(jnp.where(pick, scores, 0.0), axis=0, keepdims=True))
        picked = jnp.where(pick, 1.0, picked)
        masked = jnp.where(pick, ninf, masked)
    wall = jnp.concatenate(wsel, axis=0)
    w_ref[...] = wall / jnp.sum(wall, axis=0, keepdims=True) * ROUTED_SCALE
    e_ref[...] = jnp.concatenate(idxs, axis=0).astype(jnp.int32)

    rank = carry_scr[:, 0:1] + _dot(picked.astype(BF16), tri_scr[...])
    r_ref[...] = jnp.concatenate(
        [jnp.sum(jnp.where(ei == idx, rank, 0.0), axis=0, keepdims=True) for idx in idxs], axis=0).astype(jnp.int32)
    total = carry_scr[...] + jnp.sum(picked, axis=1, keepdims=True)
    carry_scr[...] = total
    cnt_ref[...] = total.astype(jnp.int32)


def _router(logits, b_router):
    t = logits.shape[1]
    tr = min(512, t)
    tok = pl.BlockSpec((TOP_K, tr), lambda i: (0, i))
    e_t, w_t, r_t, cnt = pl.pallas_call(
        functools.partial(_router_kernel, tr=tr),
        grid=(t // tr,),
        in_specs=[pl.BlockSpec((N_EXPERTS, tr), lambda i: (0, i)),
                  pl.BlockSpec((N_EXPERTS, 1), lambda i: (0, 0))],
        out_specs=[tok, tok, tok, pl.BlockSpec((N_EXPERTS, LANE), lambda i: (0, 0))],
        out_shape=[jax.ShapeDtypeStruct((TOP_K, t), jnp.int32), jax.ShapeDtypeStruct((TOP_K, t), F32),
                   jax.ShapeDtypeStruct((TOP_K, t), jnp.int32), jax.ShapeDtypeStruct((N_EXPERTS, LANE), jnp.int32)],
        scratch_shapes=[pltpu.VMEM((N_EXPERTS, LANE), F32), pltpu.VMEM((tr, tr), BF16)],
        compiler_params=_params("arbitrary"),
        name="router",
    )(logits, b_router.reshape(N_EXPERTS, 1))
    return e_t, w_t, r_t, cnt


def _plan_kernel(e_ref, r_ref, cnt_ref, dest_ref, be_ref, bv_ref, nu_ref, *, br, nblk_pad):
    cnt = cnt_ref[...]
    shift = br.bit_length() - 1
    padded = lax.shift_left(lax.shift_right_logical(cnt + (br - 1), shift), shift)
    ea = lax.broadcasted_iota(jnp.int32, (N_EXPERTS, N_EXPERTS), 0)
    eb = lax.broadcasted_iota(jnp.int32, (N_EXPERTS, N_EXPERTS), 1)
    pends = jnp.dot((eb <= ea).astype(F32), padded.astype(F32), preferred_element_type=F32,
                    precision=HIGHEST).astype(jnp.int32)
    pstart = pends - padded

    e = e_ref[...]
    first_row = jnp.zeros(e.shape, jnp.int32)
    for x in range(N_EXPERTS):
        first_row = jnp.where(e == x, pstart[x:x + 1, 0:1], first_row)
    dest_ref[...] = first_row + r_ref[...]

    @pl.when(pl.program_id(0) == 0)
    def _():
        esub = lax.broadcasted_iota(jnp.int32, (N_EXPERTS, LANE), 0)
        real_end = (pstart + cnt).astype(F32)
        for c in range(nblk_pad // LANE):
            bstart = (lax.broadcasted_iota(jnp.int32, (1, LANE), 1) + c * LANE) * br
            be = jnp.minimum(jnp.sum((pends <= bstart).astype(F32), axis=0, keepdims=True),
                             float(N_EXPERTS - 1)).astype(jnp.int32)
            end = jnp.sum(jnp.where(esub == be, real_end, 0.0), axis=0, keepdims=True).astype(jnp.int32)
            be_ref[:, c * LANE:(c + 1) * LANE] = be
            bv_ref[:, c * LANE:(c + 1) * LANE] = jnp.clip(end - bstart, 0, br)
        nu_ref[...] = lax.shift_right_logical(pends[N_EXPERTS - 1:N_EXPERTS, :], shift)


def _plan(e_t, r_t, cnt, br, nblk):
    t = e_t.shape[1]
    tc = min(2048, t)
    nblk_pad = -(-nblk // LANE) * LANE
    tok = pl.BlockSpec((TOP_K, tc), lambda i: (0, i))
    blk = pl.BlockSpec((1, nblk_pad), lambda i: (0, 0))
    dest, be, bv, nu = pl.pallas_call(
        functools.partial(_plan_kernel, br=br, nblk_pad=nblk_pad),
        grid=(t // tc,),
        in_specs=[tok, tok, pl.BlockSpec((N_EXPERTS, LANE), lambda i: (0, 0))],
        out_specs=[tok, blk, blk, pl.BlockSpec((1, LANE), lambda i: (0, 0))],
        out_shape=[jax.ShapeDtypeStruct((TOP_K, t), jnp.int32), jax.ShapeDtypeStruct((1, nblk_pad), jnp.int32),
                   jax.ShapeDtypeStruct((1, nblk_pad), jnp.int32), jax.ShapeDtypeStruct((1, LANE), jnp.int32)],
        compiler_params=_params("arbitrary"),
        name="moe_plan",
    )(e_t, r_t, cnt)
    return dest, be[0, :nblk], bv[0, :nblk], nu[0, :1]


def _row_copy(src, s, dst, d, sem):
    return pltpu.make_async_copy(src.at[pl.ds(s, 1), :], dst.at[pl.ds(d, 1), :], sem)


def _dispatch_kernel(dest_ref, u_ref, xs_ref, sem, *, tm):
    def start(r, carry):
        for k in range(TOP_K):
            _row_copy(u_ref, r, xs_ref, dest_ref[k, r], sem).start()
        return carry

    def wait(r, carry):
        for k in range(TOP_K):
            _row_copy(u_ref, r, xs_ref, dest_ref[k, r], sem).wait()
        return carry

    lax.fori_loop(0, tm, start, 0)
    lax.fori_loop(0, tm, wait, 0)


def _dispatch(u2p, dest_t, n_rows):
    t = u2p.shape[0]
    tm = min(256, t)
    nt = t // tm
    dest_blocks = dest_t.reshape(TOP_K, nt, tm).transpose(1, 0, 2)
    return pl.pallas_call(
        functools.partial(_dispatch_kernel, tm=tm),
        grid=(nt,),
        in_specs=[pl.BlockSpec((None, TOP_K, tm), lambda i: (i, 0, 0), memory_space=pltpu.SMEM),
                  pl.BlockSpec((tm, HALF), lambda i: (i, 0))],
        out_specs=pl.BlockSpec(memory_space=pl.ANY),
        out_shape=jax.ShapeDtypeStruct((n_rows, HALF), PACKED),
        scratch_shapes=[pltpu.SemaphoreType.DMA(())],
        compiler_params=_params("arbitrary"),
        name="moe_dispatch",
    )(dest_blocks, u2p)


SC_CORES = 2
SC_SUBCORES = 16
SC_WINDOW = 128
SC_WORKERS = SC_CORES * SC_SUBCORES


def _sc_mesh():
    return plsc.VectorSubcoreMesh(core_axis_name="c", subcore_axis_name="s")


def _sc_worker():
    return lax.axis_index("s") * SC_CORES + lax.axis_index("c")


def _sc_scatter_rows(rows, dest_blocks, n_rows):
    t, width = rows.shape
    nchunks = t // SC_WINDOW // SC_WORKERS

    def body(rows_hbm, dest_hbm, out_hbm, idx_v, rows_v, sem):
        wid = _sc_worker()

        @pl.loop(0, nchunks)
        def _(c):
            chunk = wid * nchunks + c
            pltpu.sync_copy(dest_hbm.at[chunk], idx_v)
            pltpu.sync_copy(rows_hbm.at[pl.ds(pl.multiple_of(chunk * SC_WINDOW, SC_WINDOW), SC_WINDOW)], rows_v)
            copies = [pltpu.async_copy(rows_v, out_hbm.at[idx_v.at[k]], sem) for k in range(TOP_K)]
            for cp in copies:
                cp.wait()

    return pl.kernel(
        body, out_type=jax.ShapeDtypeStruct((n_rows, width), rows.dtype), mesh=_sc_mesh(),
        scratch_types=[pltpu.VMEM((TOP_K, SC_WINDOW), jnp.int32), pltpu.VMEM((SC_WINDOW, width), rows.dtype),
                       pltpu.SemaphoreType.DMA],
        name="moe_sc_dispatch")(rows, dest_blocks)


def _sc_gather_rows(table, idx):
    n = idx.shape[0]
    width = table.shape[1]
    nchunks = n // SC_WINDOW // SC_WORKERS

    def body(table_hbm, idx_hbm, out_hbm, idx_v, rows_v, sem):
        wid = _sc_worker()

        @pl.loop(0, nchunks)
        def _(c):
            base = pl.multiple_of((wid * nchunks + c) * SC_WINDOW, SC_WINDOW)
            pltpu.sync_copy(idx_hbm.at[pl.ds(base, SC_WINDOW)], idx_v)
            pltpu.async_copy(table_hbm.at[idx_v], rows_v, sem).wait()
            pltpu.sync_copy(rows_v, out_hbm.at[pl.ds(base, SC_WINDOW)])

    return pl.kernel(
        body, out_type=jax.ShapeDtypeStruct((n, width), table.dtype), mesh=_sc_mesh(),
        scratch_types=[pltpu.VMEM((SC_WINDOW,), jnp.int32), pltpu.VMEM((SC_WINDOW, width), table.dtype),
                       pltpu.SemaphoreType.DMA],
        name="moe_sc_gather")(table, idx)


def _expert_kernel(be_ref, bv_ref, nu_ref, x_ref, wg_ref, wu_ref, wd_ref, after_ref, y_ref, wg_scr, wu_scr, wd_scr,
                   *, br):
    del after_ref
    i = pl.program_id(0)
    prev = be_ref[jnp.maximum(i - 1, 0)]

    @pl.when(jnp.logical_and(i < nu_ref[0], jnp.logical_or(i == 0, be_ref[i] != prev)))
    def _():
        wg_scr[...] = wg_ref[...].astype(BF16)
        wu_scr[...] = wu_ref[...].astype(BF16)
        wd_scr[...] = wd_ref[...].astype(BF16)

    @pl.when(i < nu_ref[0])
    def _():
        live = lax.broadcasted_iota(jnp.int32, (br, HALF), 0) < bv_ref[i]
        x = _unpack_rows(jnp.where(live, x_ref[...], 0)).astype(BF16)
        hcat = _silu(_dot(x, wg_scr[...])) * _dot(x, wu_scr[...])
        y_ref[...] = _pack_rows(_dot(hcat.astype(BF16), wd_scr[...]))


def _experts(x_sorted, blk_e, blk_valid, n_used, w_gate, w_up, w_down, br, after):
    n_rows = x_sorted.shape[0]
    nblk = n_rows // br

    def blk(i, be, bv, nu):
        return (jnp.minimum(i, nu[0] - 1), 0)

    grid_spec = pltpu.PrefetchScalarGridSpec(
        num_scalar_prefetch=3,
        grid=(nblk,),
        in_specs=[pl.BlockSpec((br, HALF), blk),
                  pl.BlockSpec((None, D_MODEL, D_EXPERT), lambda i, be, bv, nu: (be[i], 0, 0)),
                  pl.BlockSpec((None, D_MODEL, D_EXPERT), lambda i, be, bv, nu: (be[i], 0, 0)),
                  pl.BlockSpec((None, D_EXPERT, D_MODEL), lambda i, be, bv, nu: (be[i], 0, 0)),
                  pl.BlockSpec(after.shape, lambda i, be, bv, nu: (0,) * after.ndim)],
        out_specs=pl.BlockSpec((br, HALF), blk),
        scratch_shapes=[pltpu.VMEM((D_MODEL, D_EXPERT), BF16), pltpu.VMEM((D_MODEL, D_EXPERT), BF16),
                        pltpu.VMEM((D_EXPERT, D_MODEL), BF16)])
    return pl.pallas_call(
        functools.partial(_expert_kernel, br=br),
        grid_spec=grid_spec,
        out_shape=jax.ShapeDtypeStruct((n_rows, HALF), PACKED),
        compiler_params=_params("arbitrary"),
        name="moe_experts",
    )(blk_e, blk_valid, n_used, x_sorted, w_gate, w_up, w_down, after)


def _combine_kernel(dest_ref, ys_ref, w_ref, u_ref, x1_ref, g2_ref, sg_ref, su_ref, sd_ref, lg_ref, lb_ref,
                    o_ref, g_scr, sem, *, tm):
    def start(r, carry):
        for k in range(TOP_K):
            _row_copy(ys_ref, dest_ref[k, r], g_scr.at[k], r, sem).start()
        return carry

    def wait(r, carry):
        for k in range(TOP_K):
            _row_copy(ys_ref, dest_ref[k, r], g_scr.at[k], r, sem).wait()
        return carry

    lax.fori_loop(0, tm, start, 0)
    f = _shared_ffn(u_ref, sg_ref, su_ref, sd_ref)
    lax.fori_loop(0, tm, wait, 0)
    f = _add_routed(f, g_scr, w_ref[...])
    o_ref[...] = _layer_norm(ALPHA * x1_ref[...] + g2_ref[...] * f, lg_ref[...], lb_ref[...])


def _shared_ffn(u_ref, sg_ref, su_ref, sd_ref):
    ub = _unpack_rows(u_ref[...]).astype(BF16)
    hs = _silu(_dot(ub, sg_ref[...])) * _dot(ub, su_ref[...])
    return _dot(hs.astype(BF16), sd_ref[...])


def _add_routed(f, slots_ref, w):
    for k in range(TOP_K):
        f = f + _unpack_rows(slots_ref[k]) * w[:, k:k + 1]
    return f


def _combine_gathered_kernel(g_ref, w_ref, u_ref, x1_ref, g2_ref, sg_ref, su_ref, sd_ref, lg_ref, lb_ref, o_ref):
    f = _add_routed(_shared_ffn(u_ref, sg_ref, su_ref, sd_ref), g_ref, w_ref[...])
    o_ref[...] = _layer_norm(ALPHA * x1_ref[...] + g2_ref[...] * f, lg_ref[...], lb_ref[...])


def _combine(y_rows, dest_t, w_t, u2p, x1, g2, ws_gate, ws_up, ws_down, ln_g, ln_b, rows_per_mod, gathered):
    t = u2p.shape[0]
    tm = min(512 if gathered else 128, t)
    nt = t // tm
    rows = pl.BlockSpec((tm, D_MODEL), lambda i: (i, 0))
    vec = pl.BlockSpec((1, D_MODEL), lambda i: (0, 0))
    g2a, mspec = _mod_spec(g2, t, tm, rows_per_mod)
    common_specs = [pl.BlockSpec((tm, TOP_K), lambda i: (i, 0)),
                    pl.BlockSpec((tm, HALF), lambda i: (i, 0)), rows, mspec,
                    pl.BlockSpec((D_MODEL, D_SHARED), lambda i: (0, 0)),
                    pl.BlockSpec((D_MODEL, D_SHARED), lambda i: (0, 0)),
                    pl.BlockSpec((D_SHARED, D_MODEL), lambda i: (0, 0)),
                    vec, vec]
    common_args = (w_t.T, u2p, x1, g2a, ws_gate.astype(BF16), ws_up.astype(BF16), ws_down.astype(BF16),
                   ln_g.reshape(1, D_MODEL), ln_b.reshape(1, D_MODEL))
    if gathered:
        return pl.pallas_call(
            _combine_gathered_kernel,
            grid=(nt,),
            in_specs=[pl.BlockSpec((TOP_K, tm, HALF), lambda i: (0, i, 0))] + common_specs,
            out_specs=rows,
            out_shape=jax.ShapeDtypeStruct((t, D_MODEL), F32),
            compiler_params=_params("arbitrary"),
            name="moe_combine_gathered",
        )(y_rows.reshape(TOP_K, t, HALF), *common_args)
    dest_blocks = dest_t.reshape(TOP_K, nt, tm).transpose(1, 0, 2)
    return pl.pallas_call(
        functools.partial(_combine_kernel, tm=tm),
        grid=(nt,),
        in_specs=[pl.BlockSpec((None, TOP_K, tm), lambda i: (i, 0, 0), memory_space=pltpu.SMEM),
                  pl.BlockSpec(memory_space=pl.ANY)] + common_specs,
        out_specs=rows,
        out_shape=jax.ShapeDtypeStruct((t, D_MODEL), F32),
        scratch_shapes=[pltpu.VMEM((TOP_K, tm, HALF), PACKED), pltpu.SemaphoreType.DMA(())],
        compiler_params=_params("arbitrary"),
        name="moe_combine",
    )(dest_blocks, y_rows, *common_args)


def _moe(u2p, logits, x1, g2, p, rows_per_mod, br, after):
    t = u2p.shape[0]
    e_t, w_t, r_t, counts = _router(logits, p["b_router"])
    n_rows = (t * TOP_K + N_EXPERTS * (br - 1) + br - 1) // br * br
    dest_t, blk_e, blk_valid, n_used = _plan(e_t, r_t, counts, br, n_rows // br)
    on_sc = t % (SC_WORKERS * SC_WINDOW) == 0
    if on_sc:
        dest_blocks = dest_t.reshape(TOP_K, t // SC_WINDOW, SC_WINDOW).transpose(1, 0, 2)
        x_sorted = _sc_scatter_rows(u2p, dest_blocks, n_rows)
    else:
        x_sorted = _dispatch(u2p, dest_t, n_rows)
    y_rows = _experts(x_sorted, blk_e, blk_valid, n_used, p["w_gate"], p["w_up"], p["w_down"], br, after)
    if on_sc:
        y_rows = _sc_gather_rows(y_rows, dest_t.reshape(TOP_K * t))
    return _combine(y_rows, dest_t, w_t, u2p, x1, g2, p["ws_gate"], p["ws_up"], p["ws_down"],
                    p["ln2_g"], p["ln2_b"], rows_per_mod, on_sc)


def _cat_w_in(w_in):
    sizes = (SSD_INNER, CONV_DIM, SSD_HEADS, 1024, 1024, 1024, D_MODEL, D_MODEL)
    offs = [0]
    for s in sizes:
        offs.append(offs[-1] + s)
    w_in = w_in.astype(BF16)
    z, xbc, dt, q, k, v, ga, gb = [w_in[:, offs[i]:offs[i + 1]] for i in range(8)]
    dt = jnp.pad(dt, ((0, 0), (0, SEG_END - SEG_DT - SSD_HEADS)))
    return jnp.concatenate([z, xbc, q, k, v, ga, gb, dt], axis=1)


def kernel(x_prompt, x_sample, c_prompt, c_sample, cache_k, cache_v, page_table, state_conv, state_ssm, rel_bias, w_ada, b_ada, w_in, conv_w, conv_b, dt_bias, a_log, d_skip, ssd_norm_w, lam_q1, lam_k1, lam_q2, lam_k2, subln_w, w_br_ssd, w_br_attn, w_out, ln1_g, ln1_b, w_router, b_router, w_gate, w_up, w_down, ws_gate, ws_up, ws_down, ln2_g, ln2_b):
    assert w_in.shape[0] == DEPTH
    nbp, seq, _ = x_prompt.shape
    nbs, dseq, _ = x_sample.shape
    tp, ts = nbp * seq, nbs * dseq
    past = page_table.shape[1] * PAGE_SIZE
    cl = SSD_CHUNK

    pad_p = -nbp % 8
    c_all = jnp.concatenate([c_prompt, jnp.zeros((pad_p, D_MODEL), F32), jnp.repeat(c_sample, dseq, axis=0)], axis=0)
    mod = _adaln(c_all, w_ada[0], b_ada[0])
    mod_p = mod[:nbp].reshape(nbp, 1, 6, D_MODEL)
    mod_s = mod[nbp + pad_p:].reshape(nbs, dseq, 6, D_MODEL)
    mp = [mod_p[:, :, i] for i in range(6)]
    ms = [mod_s[:, :, i] for i in range(6)]

    w_cat = _cat_w_in(w_in[0])
    wa, wb, wo = w_br_ssd[0].astype(BF16), w_br_attn[0].astype(BF16), w_out[0].astype(BF16)
    lamp = jnp.stack([lam_q1[0], lam_k1[0], lam_q2[0], lam_k2[0]])
    moe_p = dict(b_router=b_router[0], w_gate=w_gate[0], w_up=w_up[0], w_down=w_down[0],
                 ws_gate=ws_gate[0], ws_up=ws_up[0], ws_down=ws_down[0], ln2_g=ln2_g[0], ln2_b=ln2_b[0])
    ssd_w = (conv_w[0], conv_b[0], dt_bias[0], a_log[0], d_skip[0], ssd_norm_w[0])

    xp = x_prompt.reshape(tp, D_MODEL)
    z, xbc, dt, q, k, v, kb, vb, ga, gb = _inproj(xp, mp[1], mp[0], w_cat, seq)
    yn, conv_p, h_p = _ssd(xbc, dt, z, *ssd_w, jnp.zeros((nbp, CONV_W - 1, CONV_DIM), F32),
                           jnp.zeros((nbp, SSD_HEADS, SSD_HEADDIM, D_STATE), F32), nbp, seq // cl, cl)
    yn = yn.reshape(tp, SSD_INNER)
    on = _attn_prompt(q, kb, vb, rel_bias, lamp, subln_w[0], nbp, seq)
    x1_p, u2p_p, logits_p = _merge(yn, on, ga, gb, xp, mp[2], mp[4], mp[3], wa, wb, wo, ln1_g[0], ln1_b[0],
                                   w_router[0], seq)
    k_prompt = k.reshape(1, nbp, seq, ATTN_HEADS, 2 * ATTN_HEAD_DIM)
    v_prompt = v.reshape(1, nbp, seq, ATTN_HEADS, ATTN_V_DIM)

    xs_ = x_sample.reshape(ts, D_MODEL)
    z, xbc, dt, q, k, v, kb, vb, ga, gb = _inproj(xs_, ms[1], ms[0], w_cat, dseq)
    yn, conv_s, h_s = _ssd(xbc, dt, z, *ssd_w, state_conv[0], state_ssm[0], nbs, 1, dseq)
    yn = yn.reshape(ts, SSD_INNER)
    on = _attn_decode(q.reshape(nbs, dseq, -1), k.reshape(nbs, dseq * ATTN_HEADS, LANE),
                      v.reshape(nbs, dseq * ATTN_HEADS, LANE), cache_k[0], cache_v[0], page_table,
                      rel_bias, lamp, subln_w[0])
    y_prompt = _moe(u2p_p, logits_p, x1_p, mp[5], moe_p, seq, 1024, on[0, :, :LANE]).reshape(nbp, seq, D_MODEL)
    x1, u2p, logits = _merge(yn, on.reshape(ts, -1), ga, gb, xs_, ms[2], ms[4], ms[3], wa, wb, wo, ln1_g[0], ln1_b[0],
                             w_router[0], dseq)
    y_sample = _moe(u2p, logits, x1, ms[5], moe_p, dseq, 64, x1[:8, :LANE]).reshape(nbs, dseq, D_MODEL)
    k_sample = k.reshape(1, nbs, dseq, ATTN_HEADS, 2 * ATTN_HEAD_DIM)
    v_sample = v.reshape(1, nbs, dseq, ATTN_HEADS, ATTN_V_DIM)

    return (y_prompt, y_sample, k_prompt, v_prompt, conv_p[None], h_p.reshape(1, nbp, SSD_HEADS, SSD_HEADDIM, D_STATE),
            k_sample, v_sample, conv_s[None], h_s.reshape(1, nbs, SSD_HEADS, SSD_HEADDIM, D_STATE))
```

```python
import functools
import math

import jax
import jax.numpy as jnp
from jax import lax
from jax.experimental import pallas as pl
from jax.experimental.pallas import tpu as pltpu
from jax.experimental.pallas import tpu_sc as plsc

F32 = jnp.float32
BF16 = jnp.bfloat16
HIGHEST = lax.Precision.HIGHEST

D_MODEL = 1024
SSD_INNER = 1024
SSD_HEADDIM = 64
SSD_HEADS = 16
SSD_GROUPS = 2
D_STATE = 128
CONV_W = 4
CONV_DIM = SSD_INNER + 2 * SSD_GROUPS * D_STATE
SSD_CHUNK = 128
ATTN_HEADS = 8
ATTN_HEAD_DIM = 64
ATTN_V_DIM = 128
N_BUCKETS = 32
MAX_DISTANCE = 128
N_EXPERTS = 64
N_EXPERT_GROUPS = 8
GROUP_SIZE = N_EXPERTS // N_EXPERT_GROUPS
TOPK_GROUPS = 4
TOP_K = 8
D_EXPERT = 256
D_SHARED = 256
ROUTED_SCALE = 2.5
PAGE_SIZE = 128
DEPTH = 1
ALPHA = (2 * DEPTH) ** 0.25
LN_EPS = 1e-5
RMS_EPS = 1e-5
LAM_INIT = 0.8 - 0.6 * math.exp(-0.3 * 0)
LOG2E = math.log2(math.e)
LANE = 128
VMEM_LIMIT = 56 * 1024 * 1024

SEG_Z, SEG_XBC, SEG_Q, SEG_K, SEG_V, SEG_GA, SEG_GB, SEG_DT, SEG_END = (
    0, 1024, 2560, 3584, 4608, 5632, 6656, 7680, 7808)


def _silu(x):
    return x * jax.nn.sigmoid(x)


def _dot(a, b):
    return jnp.dot(a, b, preferred_element_type=F32)


def _dot_nt(a, b):
    return lax.dot_general(a, b, (((1,), (1,)), ((), ())), preferred_element_type=F32)


def _dot_tn(a, b):
    return lax.dot_general(a, b, (((0,), (0,)), ((), ())), preferred_element_type=F32)


def _params(*sem):
    return pltpu.CompilerParams(dimension_semantics=sem, vmem_limit_bytes=VMEM_LIMIT)


def _adaln_kernel(c_ref, w_ref, b_ref, o_ref):
    s = _silu(c_ref[...]).astype(BF16)
    o_ref[...] = _dot(s, w_ref[...].astype(BF16)) + b_ref[...]


def _adaln(c, w_ada, b_ada):
    r = c.shape[0]
    n = w_ada.shape[1]
    tn = 1024
    return pl.pallas_call(
        _adaln_kernel,
        grid=(n // tn,),
        in_specs=[pl.BlockSpec((r, D_MODEL), lambda j: (0, 0)),
                  pl.BlockSpec((D_MODEL, tn), lambda j: (0, j)),
                  pl.BlockSpec((1, tn), lambda j: (0, j))],
        out_specs=pl.BlockSpec((r, tn), lambda j: (0, j)),
        out_shape=jax.ShapeDtypeStruct((r, n), F32),
        compiler_params=_params("arbitrary"),
        name="adaln",
    )(c, w_ada, b_ada.reshape(1, n))


def _inproj_kernel(x_ref, sc_ref, sh_ref, w_ref, z_ref, xbc_ref, dt_ref, q_ref, k_ref, v_ref,
                   kb_ref, vb_ref, ga_ref, gb_ref):
    u = (x_ref[...] * (1.0 + sc_ref[...]) + sh_ref[...]).astype(BF16)

    def seg(a, b):
        return _dot(u, w_ref[:, a:b])

    z_ref[...] = seg(SEG_Z, SEG_XBC).astype(BF16)
    xbc_ref[...] = seg(SEG_XBC, SEG_Q)
    q_ref[...] = (seg(SEG_Q, SEG_K) * (ATTN_HEAD_DIM ** -0.5 * LOG2E)).astype(BF16)
    kk = seg(SEG_K, SEG_V)
    k_ref[...] = kk
    kb_ref[...] = kk.astype(BF16)
    vv = seg(SEG_V, SEG_GA)
    v_ref[...] = vv
    vb_ref[...] = vv.astype(BF16)
    ga_ref[...] = seg(SEG_GA, SEG_GB).astype(BF16)
    gb_ref[...] = seg(SEG_GB, SEG_DT).astype(BF16)
    dt_ref[...] = seg(SEG_DT, SEG_END)


def _inproj(x, sc, sh, w_cat, rows_per_mod):
    t = x.shape[0]
    tm = min(256, t)
    per_row = sc.shape[1] != 1
    if per_row:
        sc2 = sc.reshape(t, D_MODEL)
        sh2 = sh.reshape(t, D_MODEL)
        mod_spec = pl.BlockSpec((tm, D_MODEL), lambda i: (i, 0))
    else:
        assert rows_per_mod % tm == 0
        sc2, sh2 = sc, sh
        mod_spec = pl.BlockSpec((None, 1, D_MODEL), lambda i: (i // (rows_per_mod // tm), 0, 0))

    def rows(width):
        return pl.BlockSpec((tm, width), lambda i: (i, 0))

    widths = (1024, CONV_DIM, LANE, 1024, 1024, 1024, 1024, 1024, 1024, 1024)
    dtypes = (BF16, F32, F32, BF16, F32, F32, BF16, BF16, BF16, BF16)
    return pl.pallas_call(
        _inproj_kernel,
        grid=(t // tm,),
        in_specs=[rows(D_MODEL), mod_spec, mod_spec,
                  pl.BlockSpec((D_MODEL, SEG_END), lambda i: (0, 0), pipeline_mode=pl.Buffered(1))],
        out_specs=[rows(w) for w in widths],
        out_shape=[jax.ShapeDtypeStruct((t, w), d) for w, d in zip(widths, dtypes)],
        compiler_params=_params("arbitrary"),
        name="inproj",
    )(x, sc2, sh2, w_cat)


def _ssd_kernel(xbc_ref, dt_ref, z_ref, cw_ref, cb_ref, dtb_ref, alog_ref, dsk_ref, nw_ref,
                cbuf_ref, h0_ref, yn_ref, cnew_ref, hnew_ref, xp_scr, h_scr, y_scr, *, valid_len):
    c = pl.program_id(1)
    nc = pl.num_programs(1)
    cl = SSD_CHUNK
    head = 8

    @pl.when(c == 0)
    def _():
        xp_scr[head - (CONV_W - 1):head, :] = cbuf_ref[...]
        h_scr[...] = h0_ref[...]

    @pl.when(c > 0)
    def _():
        xp_scr[head - (CONV_W - 1):head, :] = xp_scr[head + cl - (CONV_W - 1):head + cl, :]

    xp_scr[head:head + valid_len, :] = xbc_ref[...]
    if valid_len < cl:
        xp_scr[head + valid_len:head + cl, :] = jnp.zeros((cl - valid_len, CONV_DIM), F32)

    acc = xp_scr[head:head + cl, :] * cw_ref[CONV_W - 1:CONV_W, :]
    for j in range(CONV_W - 1):
        lo = head - (CONV_W - 1) + j
        acc = acc + xp_scr[lo:lo + cl, :] * cw_ref[j:j + 1, :]
    xc = _silu(acc + cb_ref[...])
    xs = xc[:, :SSD_INNER]
    bmat = [xc[:, SSD_INNER + g * D_STATE:SSD_INNER + (g + 1) * D_STATE].astype(BF16)
            for g in range(SSD_GROUPS)]
    coff = SSD_INNER + SSD_GROUPS * D_STATE
    cmat = [xc[:, coff + g * D_STATE:coff + (g + 1) * D_STATE].astype(BF16) for g in range(SSD_GROUPS)]

    li = lax.broadcasted_iota(jnp.int32, (cl, cl), 0)
    si = lax.broadcasted_iota(jnp.int32, (cl, cl), 1)
    causal = li >= si
    lane_lo = si < SSD_HEADDIM
    sub_lo = li < SSD_HEADDIM

    def rows_padded(v):
        return v if valid_len == cl else jnp.concatenate([v, jnp.zeros((cl - valid_len, v.shape[1]), F32)], axis=0)

    dpre = rows_padded(dt_ref[...]) + dtb_ref[...]
    dtv = jnp.maximum(dpre, 0.0) + jnp.log(1.0 + jnp.exp(-jnp.abs(dpre)))
    if valid_len < cl:
        dtv = jnp.where(li < valid_len, dtv, 0.0)
    da = dtv * (-jnp.exp(alog_ref[...]))
    tril = causal.astype(F32)
    acum = jnp.dot(tril, da, preferred_element_type=F32, precision=HIGHEST)
    acum_t = acum.T
    dt_t = dtv.T
    last = acum[cl - 1:cl, :]
    e_all = jnp.exp(acum)
    w_all = jnp.exp(last - acum) * dtv
    e_last = jnp.exp(last)

    cb = [_dot_nt(cmat[g], bmat[g]) for g in range(SSD_GROUPS)]

    def colb(a, h):
        return jnp.broadcast_to(a[:, h:h + 1], (cl, cl))

    pairs = SSD_HEADS // 2
    for p in range(pairs):
        h0, h1 = 2 * p, 2 * p + 1
        g = h0 // (SSD_HEADS // SSD_GROUPS)
        xpair = xs[:, p * LANE:(p + 1) * LANE]
        x_lo = jnp.where(lane_lo, xpair, 0.0).astype(BF16)
        x_hi = jnp.where(lane_lo, 0.0, xpair).astype(BF16)
        ydiag = None
        for hh, xm in ((h0, x_lo), (h1, x_hi)):
            seg = colb(acum, hh) - acum_t[hh:hh + 1, :]
            dec = jnp.exp(jnp.where(causal, seg, -jnp.inf))
            m = (cb[g] * dec * dt_t[hh:hh + 1, :]).astype(BF16)
            part = _dot(m, xm)
            ydiag = part if ydiag is None else ydiag + part
        ecol = jnp.where(lane_lo, colb(e_all, h0), colb(e_all, h1))
        hp = h_scr[p]
        yoff = _dot_nt(cmat[g], hp.astype(BF16)) * ecol
        wcol = jnp.where(lane_lo, colb(w_all, h0), colb(w_all, h1))
        dx = (xpair * wcol).astype(BF16)
        st = _dot_tn(dx, bmat[g])
        hdec = jnp.where(sub_lo, e_last[:, h0:h0 + 1], e_last[:, h1:h1 + 1])
        h_scr[p] = hp * hdec + st
        y_scr[:, p * LANE:(p + 1) * LANE] = ydiag + yoff + dsk_ref[:, p * LANE:(p + 1) * LANE] * xpair

    zf = z_ref[...].astype(F32)
    gated = y_scr[0:valid_len, :] * _silu(zf)
    gw = SSD_INNER // SSD_GROUPS
    for g in range(SSD_GROUPS):
        sg = gated[:, g * gw:(g + 1) * gw]
        ms = jnp.mean(sg * sg, axis=-1, keepdims=True)
        yn_ref[:, g * gw:(g + 1) * gw] = (sg * lax.rsqrt(ms + RMS_EPS) * nw_ref[:, g * gw:(g + 1) * gw]).astype(yn_ref.dtype)

    @pl.when(c == nc - 1)
    def _():
        cnew_ref[...] = xp_scr[head + valid_len - (CONV_W - 1):head + valid_len, :]
        hnew_ref[...] = h_scr[...]


def _ssd(xbc, dt, z, conv_w, conv_b, dt_bias, a_log, d_skip, norm_w, cbuf, h0, nb, nc, valid_len):
    cl = SSD_CHUNK
    assert valid_len == cl or nc == 1
    pairs = SSD_HEADS // 2
    pad = LANE - SSD_HEADS
    row = lambda b, c: (b * nc + c, 0, 0)
    const = lambda b, c: (0, 0)
    chunks = lambda a: a.reshape(nb * nc, valid_len, a.shape[-1])
    z = z if valid_len % 16 == 0 else z.astype(F32)
    return pl.pallas_call(
        functools.partial(_ssd_kernel, valid_len=valid_len),
        grid=(nb, nc),
        in_specs=[pl.BlockSpec((None, valid_len, CONV_DIM), row), pl.BlockSpec((None, valid_len, LANE), row),
                  pl.BlockSpec((None, valid_len, SSD_INNER), row),
                  pl.BlockSpec((CONV_W, CONV_DIM), const), pl.BlockSpec((1, CONV_DIM), const),
                  pl.BlockSpec((1, LANE), const), pl.BlockSpec((1, LANE), const),
                  pl.BlockSpec((1, SSD_INNER), const), pl.BlockSpec((1, SSD_INNER), const),
                  pl.BlockSpec((None, CONV_W - 1, CONV_DIM), lambda b, c: (b, 0, 0)),
                  pl.BlockSpec((None, pairs, LANE, D_STATE), lambda b, c: (b, 0, 0, 0))],
        out_specs=[pl.BlockSpec((None, valid_len, SSD_INNER), row),
                   pl.BlockSpec((None, CONV_W - 1, CONV_DIM), lambda b, c: (b, 0, 0)),
                   pl.BlockSpec((None, pairs, LANE, D_STATE), lambda b, c: (b, 0, 0, 0))],
        out_shape=[jax.ShapeDtypeStruct((nb * nc, valid_len, SSD_INNER), z.dtype),
                   jax.ShapeDtypeStruct((nb, CONV_W - 1, CONV_DIM), F32),
                   jax.ShapeDtypeStruct((nb, pairs, LANE, D_STATE), F32)],
        scratch_shapes=[pltpu.VMEM((8 + cl, CONV_DIM), F32), pltpu.VMEM((pairs, LANE, D_STATE), F32),
                        pltpu.VMEM((cl, SSD_INNER), F32)],
        compiler_params=_params("arbitrary", "arbitrary"),
        name="ssd",
    )(chunks(xbc), chunks(dt), chunks(z), conv_w, conv_b.reshape(1, CONV_DIM),
      jnp.pad(dt_bias, (0, pad)).reshape(1, LANE), jnp.pad(a_log, (0, pad)).reshape(1, LANE),
      jnp.repeat(d_skip, SSD_HEADDIM).reshape(1, SSD_INNER), norm_w.reshape(1, SSD_INNER),
      cbuf, h0.reshape(nb, pairs, LANE, D_STATE))


def _bias_kernel(rel_ref, tab_ref, o_ref, *, shift_far):
    rel = rel_ref[...]
    n = jnp.maximum(rel, 0)
    max_exact = N_BUCKETS // 2
    large = max_exact + (jnp.log(jnp.maximum(n, 1).astype(F32) / max_exact)
                         / math.log(MAX_DISTANCE / max_exact) * (N_BUCKETS - max_exact)).astype(jnp.int32)
    bucket = jnp.where(n < max_exact, n, jnp.minimum(large, N_BUCKETS - 1))
    outs = [jnp.zeros(rel.shape, F32) for _ in range(ATTN_HEADS)]
    for kb in range(N_BUCKETS):
        hit = bucket == kb
        for h in range(ATTN_HEADS):
            far = tab_ref[N_BUCKETS - 1, h] if shift_far else 0.0
            outs[h] = jnp.where(hit, (tab_ref[kb, h] - far) * LOG2E, outs[h])
    for h in range(ATTN_HEADS):
        o_ref[h] = jnp.where(rel >= 0, outs[h], -jnp.inf)


def _bias_tiles(rel, rel_bias, shift_far):
    r, c = rel.shape
    tr = min(r, 64)
    return pl.pallas_call(
        functools.partial(_bias_kernel, shift_far=shift_far),
        grid=(r // tr,),
        in_specs=[pl.BlockSpec((tr, c), lambda i: (i, 0)),
                  pl.BlockSpec(memory_space=pltpu.SMEM)],
        out_specs=pl.BlockSpec((ATTN_HEADS, tr, c), lambda i: (0, i, 0)),
        out_shape=jax.ShapeDtypeStruct((ATTN_HEADS, r, c), F32),
        compiler_params=_params("arbitrary"),
        name="t5bias",
    )(rel, rel_bias)


def _lam(lamp_ref):
    lp = lamp_ref[...]
    e1 = jnp.exp(jnp.sum(lp[0:1, :] * lp[1:2, :], axis=-1, keepdims=True))
    e2 = jnp.exp(jnp.sum(lp[2:3, :] * lp[3:4, :], axis=-1, keepdims=True))
    return e1 - e2 + LAM_INIT


def _subln(o, sw):
    ms = jnp.mean(o * o, axis=-1, keepdims=True)
    return o * lax.rsqrt(ms + RMS_EPS) * sw * (1.0 - LAM_INIT)


ATTN_TQ = 512
ATTN_STRIP = 512
ATTN_TK = 512
ATTN_D_MIN = 1 - ATTN_STRIP // ATTN_TK
ATTN_D_FAR = -(-(MAX_DISTANCE + ATTN_TK - 1) // ATTN_TK)


def _attn_kernel(q_ref, k_ref, v_ref, bias_ref, lamp_ref, sw_ref, o_ref, m_scr, l_scr, a_scr, *, tq, strip, tk):
    qi = pl.program_id(2)
    lane_lo = lax.broadcasted_iota(jnp.int32, (strip, LANE), 1) < ATTN_HEAD_DIM
    reps = tk // LANE
    lam = _lam(lamp_ref)
    for st in range(tq // strip):
        q = q_ref[st * strip:(st + 1) * strip, :]
        zero = jnp.zeros_like(q)
        qm = (jnp.where(lane_lo, q, zero), jnp.where(lane_lo, zero, q))
        m_scr[...] = jnp.full(m_scr.shape, -jnp.inf, F32)
        l_scr[...] = jnp.zeros(l_scr.shape, F32)
        a_scr[...] = jnp.zeros(a_scr.shape, F32)
        row0 = qi * (tq // strip) + st
        ntiles = (row0 * strip + strip - 1) // tk + 1

        def tile(j, near):
            off = pl.multiple_of(j * tk, tk)
            kt = k_ref[pl.ds(off, tk), :]
            vt = v_ref[pl.ds(off, tk), :]
            scs = [_dot_nt(qm[mi], kt) for mi in range(2)]
            if near:
                bias = bias_ref[row0 * (strip // tk) - j - ATTN_D_MIN]
                scs = [sc + bias for sc in scs]
            ms = [jnp.maximum(m_scr[mi], jnp.max(scs[mi], axis=-1, keepdims=True)) for mi in range(2)]
            ps = [jnp.exp2(scs[mi] - jnp.concatenate([ms[mi]] * reps, axis=1)) for mi in range(2)]
            for mi in range(2):
                alpha = jnp.exp2(m_scr[mi] - ms[mi])
                l_scr[mi] = alpha * l_scr[mi] + jnp.sum(ps[mi], axis=-1, keepdims=True)
                a_scr[mi] = alpha * a_scr[mi] + _dot(ps[mi].astype(BF16), vt)
                m_scr[mi] = ms[mi]

        nfar = jnp.maximum(row0 * (strip // tk) - ATTN_D_FAR + 1, 0)
        lax.fori_loop(0, nfar, lambda j, c: (tile(j, False), c)[1], 0)
        lax.fori_loop(nfar, ntiles, lambda j, c: (tile(j, True), c)[1], 0)
        o = a_scr[0] / l_scr[0] - lam * (a_scr[1] / l_scr[1])
        o_ref[st * strip:(st + 1) * strip, :] = _subln(o, sw_ref[...]).astype(BF16)


def _attn_prompt(q, kb, vb, rel_bias, lamp, subln_w, nb, seq):
    tq, strip, tk = ATTN_TQ, ATTN_STRIP, ATTN_TK
    assert seq % tq == 0 and tq % strip == 0 and strip % tk == 0
    nq = seq // tq
    nd = ATTN_D_FAR - ATTN_D_MIN
    rel = (jnp.arange(ATTN_D_MIN, ATTN_D_FAR, dtype=jnp.int32)[:, None, None] * tk
           + jnp.arange(strip, dtype=jnp.int32)[None, :, None] - jnp.arange(tk, dtype=jnp.int32)[None, None, :])
    bias = _bias_tiles(rel.reshape(-1, tk), rel_bias, True).reshape(ATTN_HEADS, nd, strip, tk)
    return pl.pallas_call(
        functools.partial(_attn_kernel, tq=tq, strip=strip, tk=tk),
        grid=(nb, ATTN_HEADS, nq),
        in_specs=[pl.BlockSpec((tq, LANE), lambda b, h, i: (b * nq + i, h)),
                  pl.BlockSpec((seq, LANE), lambda b, h, i: (b, h)),
                  pl.BlockSpec((seq, LANE), lambda b, h, i: (b, h)),
                  pl.BlockSpec((None, nd, strip, tk), lambda b, h, i: (h, 0, 0, 0)),
                  pl.BlockSpec((4, ATTN_HEAD_DIM), lambda b, h, i: (0, 0)),
                  pl.BlockSpec((1, ATTN_V_DIM), lambda b, h, i: (0, 0))],
        out_specs=pl.BlockSpec((tq, LANE), lambda b, h, i: (b * nq + i, h)),
        out_shape=jax.ShapeDtypeStruct((nb * seq, ATTN_HEADS * ATTN_V_DIM), BF16),
        scratch_shapes=[pltpu.VMEM((2, strip, LANE), F32)] * 3,
        compiler_params=_params("arbitrary", "arbitrary", "arbitrary"),
        name="attn_prompt",
    )(q, kb, vb, bias, lamp, subln_w.reshape(1, ATTN_V_DIM))


DECODE_PAGES = 8
DECODE_SEQS = 1


def _decode_kernel(pt_ref, q_ref, kn_ref, vn_ref, *rest, nq, pps, nbq):
    npg = nbq * pps
    ck_refs, cv_refs = rest[:npg], rest[npg:2 * npg]
    blast_ref, bnew_ref, tab_ref, lamp_ref, sw_ref, o_ref, qa_scr, m_scr, l_scr, a_scr = rest[2 * npg:]
    j = pl.program_id(1)
    nsteps = pl.num_programs(1)
    is_last = j == nsteps - 1

    def head_rows(ref, h, n):
        return ref[pl.ds(h, n, stride=ATTN_HEADS), :]

    @pl.when(j == 0)
    def _():
        lane_lo = lax.broadcasted_iota(jnp.int32, (nq, LANE), 1) < ATTN_HEAD_DIM
        zpad = jnp.zeros((PAGE_SIZE - nq, LANE), F32)
        for bi in range(nbq):
            for h in range(ATTN_HEADS):
                qh = q_ref[bi, :, h * LANE:(h + 1) * LANE].astype(F32)
                qa = jnp.concatenate([jnp.where(lane_lo, qh, 0.0), jnp.where(lane_lo, 0.0, qh)], axis=0)
                qa_scr[bi, h] = qa
                kn = jnp.concatenate([head_rows(kn_ref.at[bi], h, nq), zpad], axis=0).astype(BF16)
                vn = jnp.concatenate([head_rows(vn_ref.at[bi], h, nq), zpad], axis=0).astype(BF16)
                s = _dot_nt(qa.astype(BF16), kn) + bnew_ref[h]
                m = jnp.max(s, axis=-1, keepdims=True)
                p = jnp.exp2(s - m)
                m_scr[bi, h] = m
                l_scr[bi, h] = jnp.sum(p, axis=-1, keepdims=True)
                a_scr[bi, h] = _dot(p.astype(BF16), vn)

    heads = [(bi, h) for bi in range(nbq) for h in range(ATTN_HEADS)]
    scores = {}
    for bi, h in heads:
        qa = qa_scr[bi, h].astype(BF16)
        far = tab_ref[N_BUCKETS - 1, h] * LOG2E
        for i in range(pps):
            bias = jnp.where(is_last, blast_ref[h], far) if i == pps - 1 else far
            kh = head_rows(ck_refs[bi * pps + i], h, PAGE_SIZE).astype(BF16)
            scores[bi, h, i] = _dot_nt(qa, kh) + bias
    m_new, probs = {}, {}
    for bi, h in heads:
        m = m_scr[bi, h]
        for i in range(pps):
            m = jnp.maximum(m, jnp.max(scores[bi, h, i], axis=-1, keepdims=True))
        m_new[bi, h] = m
        for i in range(pps):
            probs[bi, h, i] = jnp.exp2(scores[bi, h, i] - m)
    for bi, h in heads:
        alpha = jnp.exp2(m_scr[bi, h] - m_new[bi, h])
        l_new = alpha * l_scr[bi, h]
        a_new = alpha * a_scr[bi, h]
        for i in range(pps):
            p = probs[bi, h, i]
            l_new = l_new + jnp.sum(p, axis=-1, keepdims=True)
            a_new = a_new + _dot(p.astype(BF16), head_rows(cv_refs[bi * pps + i], h, PAGE_SIZE).astype(BF16))
        l_scr[bi, h] = l_new
        a_scr[bi, h] = a_new
        m_scr[bi, h] = m_new[bi, h]

    @pl.when(is_last)
    def _():
        lam = _lam(lamp_ref)
        for bi in range(nbq):
            for h in range(ATTN_HEADS):
                on = a_scr[bi, h] / l_scr[bi, h]
                o = on[0:nq, :] - lam * on[nq:2 * nq, :]
                o_ref[bi, :, h * LANE:(h + 1) * LANE] = _subln(o, sw_ref[...])


def _attn_decode(q, k_new, v_new, cache_k, cache_v, page_table, rel_bias, lamp, subln_w):
    nb, nq, _ = q.shape
    npages = page_table.shape[1]
    past = npages * PAGE_SIZE
    rows = 2 * nq
    prow = PAGE_SIZE * ATTN_HEADS
    nnew = nq * ATTN_HEADS
    assert npages >= 1 and nq <= PAGE_SIZE
    qpos = past + jnp.arange(rows, dtype=jnp.int32) % nq
    kpos = jnp.arange(past - PAGE_SIZE, past + PAGE_SIZE, dtype=jnp.int32).reshape(2, 1, PAGE_SIZE)
    rel = jnp.where(kpos < past + nq, qpos[None, :, None] - kpos, -1)
    tiles = _bias_tiles(rel.reshape(2 * rows, PAGE_SIZE), rel_bias, False).reshape(ATTN_HEADS, 2, rows, PAGE_SIZE)
    pps = math.gcd(DECODE_PAGES, npages)
    nbq = math.gcd(DECODE_SEQS, nb)
    new_q = pl.BlockSpec((nbq, nq, ATTN_HEADS * LANE), lambda b, j, pt: (b, 0, 0))
    new_kv = pl.BlockSpec((nbq, nnew, LANE), lambda b, j, pt: (b, 0, 0))
    head_tile = pl.BlockSpec((ATTN_HEADS, rows, PAGE_SIZE), lambda b, j, pt: (0, 0, 0))

    def page(bi, i):
        return pl.BlockSpec((None, prow, LANE), lambda b, j, pt: (pt[b * nbq + bi, j * pps + i], 0, 0))

    pages = [page(bi, i) for bi in range(nbq) for i in range(pps)]
    grid_spec = pltpu.PrefetchScalarGridSpec(
        num_scalar_prefetch=1,
        grid=(nb // nbq, npages // pps),
        in_specs=[new_q, new_kv, new_kv, *pages, *pages, head_tile, head_tile,
                  pl.BlockSpec(memory_space=pltpu.SMEM),
                  pl.BlockSpec((4, ATTN_HEAD_DIM), lambda b, j, pt: (0, 0)),
                  pl.BlockSpec((1, ATTN_V_DIM), lambda b, j, pt: (0, 0))],
        out_specs=new_q,
        scratch_shapes=[pltpu.VMEM((nbq, ATTN_HEADS, rows, LANE), F32), pltpu.VMEM((nbq, ATTN_HEADS, rows, 1), F32),
                        pltpu.VMEM((nbq, ATTN_HEADS, rows, 1), F32), pltpu.VMEM((nbq, ATTN_HEADS, rows, LANE), F32)])
    ck = cache_k.reshape(cache_k.shape[0], prow, LANE)
    cv = cache_v.reshape(cache_v.shape[0], prow, LANE)
    return pl.pallas_call(
        functools.partial(_decode_kernel, nq=nq, pps=pps, nbq=nbq),
        grid_spec=grid_spec,
        out_shape=jax.ShapeDtypeStruct((nb, nq, ATTN_HEADS * ATTN_V_DIM), F32),
        compiler_params=_params("arbitrary", "arbitrary"),
        name="attn_decode",
    )(page_table, q, k_new, v_new, *([ck] * len(pages)), *([cv] * len(pages)), tiles[:, 0], tiles[:, 1], rel_bias,
      lamp, subln_w.reshape(1, ATTN_V_DIM))


HALF = D_MODEL // 2
PACKED = jnp.int32


def _pack_rows(x):
    word = pltpu.pack_elementwise([x[:, :HALF], x[:, HALF:]], packed_dtype=BF16)
    return lax.bitcast_convert_type(word, PACKED)


def _unpack_rows(p):
    halves = [pltpu.unpack_elementwise(p, index=i, packed_dtype=BF16, unpacked_dtype=F32) for i in range(2)]
    return jnp.concatenate(halves, axis=1)


def _layer_norm(r, g, b):
    mu = jnp.mean(r, axis=-1, keepdims=True)
    d = r - mu
    var = jnp.mean(d * d, axis=-1, keepdims=True)
    return d * lax.rsqrt(var + LN_EPS) * g + b


def _merge_kernel(yn_ref, on_ref, ga_ref, gb_ref, x_ref, g1_ref, sc2_ref, sh2_ref, wa_ref, wb_ref, wo_ref,
                  lg_ref, lb_ref, wr_ref, x1_ref, u2p_ref, logit_ref):
    ba = _dot(yn_ref[...].astype(BF16), wa_ref[...])
    bb = _dot(on_ref[...].astype(BF16), wb_ref[...])
    merged = jax.nn.sigmoid(ga_ref[...].astype(F32)) * ba + jax.nn.sigmoid(gb_ref[...].astype(F32)) * bb
    t = _dot(merged.astype(BF16), wo_ref[...])
    x1 = _layer_norm(ALPHA * x_ref[...] + g1_ref[...] * t, lg_ref[...], lb_ref[...])
    x1_ref[...] = x1
    u2 = x1 * (1.0 + sc2_ref[...]) + sh2_ref[...]
    u2p_ref[...] = _pack_rows(u2)
    u_hi = u2.astype(BF16)
    u_lo = (u2 - u_hi.astype(F32)).astype(BF16)
    by_hi = _dot_nt(wr_ref[...], u_hi)
    logit_ref[...] = by_hi[:N_EXPERTS] + by_hi[N_EXPERTS:] + _dot_nt(wr_ref[:N_EXPERTS, :], u_lo)


def _mod_spec(mod, t, tm, rows_per_mod):
    if mod.shape[1] != 1:
        return mod.reshape(t, D_MODEL), pl.BlockSpec((tm, D_MODEL), lambda i: (i, 0))
    assert rows_per_mod % tm == 0
    return mod, pl.BlockSpec((None, 1, D_MODEL), lambda i: (i // (rows_per_mod // tm), 0, 0))


def _merge(yn, on, ga, gb, x, g1, sc2, sh2, wa, wb, wo, ln_g, ln_b, w_router, rows_per_mod):
    t = x.shape[0]
    tm = min(512, t)
    rows = pl.BlockSpec((tm, D_MODEL), lambda i: (i, 0))
    wspec = pl.BlockSpec((D_MODEL, D_MODEL), lambda i: (0, 0))
    vec = pl.BlockSpec((1, D_MODEL), lambda i: (0, 0))
    g1a, mspec = _mod_spec(g1, t, tm, rows_per_mod)
    sc2a, _ = _mod_spec(sc2, t, tm, rows_per_mod)
    sh2a, _ = _mod_spec(sh2, t, tm, rows_per_mod)
    wr_t = w_router.T
    wr_hi = wr_t.astype(BF16)
    return pl.pallas_call(
        _merge_kernel,
        grid=(t // tm,),
        in_specs=[rows, rows, rows, rows, rows, mspec, mspec, mspec, wspec, wspec, wspec, vec, vec,
                  pl.BlockSpec((2 * N_EXPERTS, D_MODEL), lambda i: (0, 0))],
        out_specs=[rows, pl.BlockSpec((tm, HALF), lambda i: (i, 0)), pl.BlockSpec((N_EXPERTS, tm), lambda i: (0, i))],
        out_shape=[jax.ShapeDtypeStruct((t, D_MODEL), F32), jax.ShapeDtypeStruct((t, HALF), PACKED),
                   jax.ShapeDtypeStruct((N_EXPERTS, t), F32)],
        compiler_params=_params("arbitrary"),
        name="merge",
    )(yn, on, ga, gb, x, g1a, sc2a, sh2a, wa, wb, wo, ln_g.reshape(1, D_MODEL), ln_b.reshape(1, D_MODEL),
      jnp.concatenate([wr_hi, (wr_t - wr_hi.astype(F32)).astype(BF16)], axis=0))


def _router_kernel(logit_ref, br_ref, e_ref, w_ref, r_ref, cnt_ref, carry_scr, tri_scr, *, tr):
    i = pl.program_id(0)

    @pl.when(i == 0)
    def _():
        carry_scr[...] = jnp.zeros_like(carry_scr)
        a = lax.broadcasted_iota(jnp.int32, (tr, tr), 0)
        b = lax.broadcasted_iota(jnp.int32, (tr, tr), 1)
        tri_scr[...] = (a < b).astype(BF16)

    scores = jax.nn.sigmoid(logit_ref[...])
    biased = scores + br_ref[...]
    ninf = -jnp.inf

    b3 = biased.reshape(N_EXPERT_GROUPS, GROUP_SIZE, tr)
    j3 = lax.broadcasted_iota(jnp.int32, b3.shape, 1).astype(F32)
    top1 = jnp.max(b3, axis=1, keepdims=True)
    first = jnp.min(jnp.where(b3 == top1, j3, float(GROUP_SIZE)), axis=1, keepdims=True)
    top2 = jnp.max(jnp.where(j3 == first, ninf, b3), axis=1, keepdims=True)
    gscore = (top1 + top2).reshape(N_EXPERT_GROUPS, tr)
    gi = lax.broadcasted_iota(jnp.int32, gscore.shape, 0).astype(F32)
    gsel = jnp.zeros(gscore.shape, F32)
    for _ in range(TOPK_GROUPS):
        mx = jnp.max(gscore, axis=0, keepdims=True)
        pick = gi == jnp.min(jnp.where(gscore == mx, gi, float(N_EXPERT_GROUPS)), axis=0, keepdims=True)
        gsel = jnp.where(pick, 1.0, gsel)
        gscore = jnp.where(pick, ninf, gscore)
    emask = jnp.broadcast_to(gsel.reshape(N_EXPERT_GROUPS, 1, tr), b3.shape).reshape(N_EXPERTS, tr)
    masked = jnp.where(emask > 0.5, biased, ninf)

    ei = lax.broadcasted_iota(jnp.int32, masked.shape, 0).astype(F32)
    picked = jnp.zeros(masked.shape, F32)
    idxs, wsel = [], []
    for _ in range(TOP_K):
        mx = jnp.max(masked, axis=0, keepdims=True)
        idx = jnp.min(jnp.where(masked == mx, ei, float(N_EXPERTS)), axis=0, keepdims=True)
        pick = ei == idx
        idxs.append(idx)
        wsel.append(jnp.sum(jnp.where(pick, scores, 0.0), axis=0, keepdims=True))
        picked = jnp.where(pick, 1.0, picked)
        masked = jnp.where(pick, ninf, masked)
    wall = jnp.concatenate(wsel, axis=0)
    w_ref[...] = wall / jnp.sum(wall, axis=0, keepdims=True) * ROUTED_SCALE
    e_ref[...] = jnp.concatenate(idxs, axis=0).astype(jnp.int32)

    rank = carry_scr[:, 0:1] + _dot(picked.astype(BF16), tri_scr[...])
    r_ref[...] = jnp.concatenate(
        [jnp.sum(jnp.where(ei == idx, rank, 0.0), axis=0, keepdims=True) for idx in idxs], axis=0).astype(jnp.int32)
    total = carry_scr[...] + jnp.sum(picked, axis=1, keepdims=True)
    carry_scr[...] = total
    cnt_ref[...] = total.astype(jnp.int32)


def _router(logits, b_router):
    t = logits.shape[1]
    tr = min(512, t)
    tok = pl.BlockSpec((TOP_K, tr), lambda i: (0, i))
    e_t, w_t, r_t, cnt = pl.pallas_call(
        functools.partial(_router_kernel, tr=tr),
        grid=(t // tr,),
        in_specs=[pl.BlockSpec((N_EXPERTS, tr), lambda i: (0, i)),
                  pl.BlockSpec((N_EXPERTS, 1), lambda i: (0, 0))],
        out_specs=[tok, tok, tok, pl.BlockSpec((N_EXPERTS, LANE), lambda i: (0, 0))],
        out_shape=[jax.ShapeDtypeStruct((TOP_K, t), jnp.int32), jax.ShapeDtypeStruct((TOP_K, t), F32),
                   jax.ShapeDtypeStruct((TOP_K, t), jnp.int32), jax.ShapeDtypeStruct((N_EXPERTS, LANE), jnp.int32)],
        scratch_shapes=[pltpu.VMEM((N_EXPERTS, LANE), F32), pltpu.VMEM((tr, tr), BF16)],
        compiler_params=_params("arbitrary"),
        name="router",
    )(logits, b_router.reshape(N_EXPERTS, 1))
    return e_t, w_t, r_t, cnt


def _plan_kernel(e_ref, r_ref, cnt_ref, dest_ref, be_ref, bv_ref, nu_ref, *, br, nblk_pad):
    cnt = cnt_ref[...]
    shift = br.bit_length() - 1
    padded = jnp.maximum(lax.shift_left(lax.shift_right_logical(cnt + (br - 1), shift), shift), br)
    ea = lax.broadcasted_iota(jnp.int32, (N_EXPERTS, N_EXPERTS), 0)
    eb = lax.broadcasted_iota(jnp.int32, (N_EXPERTS, N_EXPERTS), 1)
    pends = jnp.dot((eb <= ea).astype(F32), padded.astype(F32), preferred_element_type=F32,
                    precision=HIGHEST).astype(jnp.int32)
    pstart = pends - padded

    e = e_ref[...]
    first_row = jnp.zeros(e.shape, jnp.int32)
    for x in range(N_EXPERTS):
        first_row = jnp.where(e == x, pstart[x:x + 1, 0:1], first_row)
    dest_ref[...] = first_row + r_ref[...]

    @pl.when(pl.program_id(0) == 0)
    def _():
        esub = lax.broadcasted_iota(jnp.int32, (N_EXPERTS, LANE), 0)
        real_end = (pstart + cnt).astype(F32)
        for c in range(nblk_pad // LANE):
            bstart = (lax.broadcasted_iota(jnp.int32, (1, LANE), 1) + c * LANE) * br
            be = jnp.minimum(jnp.sum((pends <= bstart).astype(F32), axis=0, keepdims=True),
                             float(N_EXPERTS - 1)).astype(jnp.int32)
            end = jnp.sum(jnp.where(esub == be, real_end, 0.0), axis=0, keepdims=True).astype(jnp.int32)
            be_ref[:, c * LANE:(c + 1) * LANE] = be
            bv_ref[:, c * LANE:(c + 1) * LANE] = jnp.clip(end - bstart, 0, br)
        nu_ref[...] = lax.shift_right_logical(pends[N_EXPERTS - 1:N_EXPERTS, :], shift)


def _plan(e_t, r_t, cnt, br, nblk):
    t = e_t.shape[1]
    tc = min(2048, t)
    nblk_pad = -(-nblk // LANE) * LANE
    tok = pl.BlockSpec((TOP_K, tc), lambda i: (0, i))
    blk = pl.BlockSpec((1, nblk_pad), lambda i: (0, 0))
    dest, be, bv, nu = pl.pallas_call(
        functools.partial(_plan_kernel, br=br, nblk_pad=nblk_pad),
        grid=(t // tc,),
        in_specs=[tok, tok, pl.BlockSpec((N_EXPERTS, LANE), lambda i: (0, 0))],
        out_specs=[tok, blk, blk, pl.BlockSpec((1, LANE), lambda i: (0, 0))],
        out_shape=[jax.ShapeDtypeStruct((TOP_K, t), jnp.int32), jax.ShapeDtypeStruct((1, nblk_pad), jnp.int32),
                   jax.ShapeDtypeStruct((1, nblk_pad), jnp.int32), jax.ShapeDtypeStruct((1, LANE), jnp.int32)],
        compiler_params=_params("arbitrary"),
        name="moe_plan",
    )(e_t, r_t, cnt)
    return dest, be[0, :nblk], bv[0, :nblk], nu[0, :1]


def _row_copy(src, s, dst, d, sem):
    return pltpu.make_async_copy(src.at[pl.ds(s, 1), :], dst.at[pl.ds(d, 1), :], sem)


def _dispatch_kernel(dest_ref, u_ref, xs_ref, sem, *, tm):
    def start(r, carry):
        for k in range(TOP_K):
            _row_copy(u_ref, r, xs_ref, dest_ref[k, r], sem).start()
        return carry

    def wait(r, carry):
        for k in range(TOP_K):
            _row_copy(u_ref, r, xs_ref, dest_ref[k, r], sem).wait()
        return carry

    lax.fori_loop(0, tm, start, 0)
    lax.fori_loop(0, tm, wait, 0)


def _dispatch(u2p, dest_t, n_rows):
    t = u2p.shape[0]
    tm = min(256, t)
    nt = t // tm
    dest_blocks = dest_t.reshape(TOP_K, nt, tm).transpose(1, 0, 2)
    return pl.pallas_call(
        functools.partial(_dispatch_kernel, tm=tm),
        grid=(nt,),
        in_specs=[pl.BlockSpec((None, TOP_K, tm), lambda i: (i, 0, 0), memory_space=pltpu.SMEM),
                  pl.BlockSpec((tm, HALF), lambda i: (i, 0))],
        out_specs=pl.BlockSpec(memory_space=pl.ANY),
        out_shape=jax.ShapeDtypeStruct((n_rows, HALF), PACKED),
        scratch_shapes=[pltpu.SemaphoreType.DMA(())],
        compiler_params=_params("arbitrary"),
        name="moe_dispatch",
    )(dest_blocks, u2p)


SC_CORES = 2
SC_SUBCORES = 16
SC_WINDOW = 128
SC_WORKERS = SC_CORES * SC_SUBCORES


def _sc_mesh():
    return plsc.VectorSubcoreMesh(core_axis_name="c", subcore_axis_name="s")


def _sc_worker():
    return lax.axis_index("s") * SC_CORES + lax.axis_index("c")


def _sc_scatter_rows(rows, dest_blocks, n_rows):
    t, width = rows.shape
    nchunks = t // SC_WINDOW // SC_WORKERS

    def body(rows_hbm, dest_hbm, out_hbm, idx_v, rows_v, sem):
        wid = _sc_worker()

        @pl.loop(0, nchunks)
        def _(c):
            chunk = wid * nchunks + c
            pltpu.sync_copy(dest_hbm.at[chunk], idx_v)
            pltpu.sync_copy(rows_hbm.at[pl.ds(pl.multiple_of(chunk * SC_WINDOW, SC_WINDOW), SC_WINDOW)], rows_v)
            copies = [pltpu.async_copy(rows_v, out_hbm.at[idx_v.at[k]], sem) for k in range(TOP_K)]
            for cp in copies:
                cp.wait()

    return pl.kernel(
        body, out_type=jax.ShapeDtypeStruct((n_rows, width), rows.dtype), mesh=_sc_mesh(),
        scratch_types=[pltpu.VMEM((TOP_K, SC_WINDOW), jnp.int32), pltpu.VMEM((SC_WINDOW, width), rows.dtype),
                       pltpu.SemaphoreType.DMA],
        name="moe_sc_dispatch")(rows, dest_blocks)


def _sc_gather_rows(table, idx):
    n = idx.shape[0]
    width = table.shape[1]
    nchunks = n // SC_WINDOW // SC_WORKERS

    def body(table_hbm, idx_hbm, out_hbm, idx_v, rows_v, sem):
        wid = _sc_worker()

        @pl.loop(0, nchunks)
        def _(c):
            base = pl.multiple_of((wid * nchunks + c) * SC_WINDOW, SC_WINDOW)
            pltpu.sync_copy(idx_hbm.at[pl.ds(base, SC_WINDOW)], idx_v)
            pltpu.async_copy(table_hbm.at[idx_v], rows_v, sem).wait()
            pltpu.sync_copy(rows_v, out_hbm.at[pl.ds(base, SC_WINDOW)])

    return pl.kernel(
        body, out_type=jax.ShapeDtypeStruct((n, width), table.dtype), mesh=_sc_mesh(),
        scratch_types=[pltpu.VMEM((SC_WINDOW,), jnp.int32), pltpu.VMEM((SC_WINDOW, width), table.dtype),
                       pltpu.SemaphoreType.DMA],
        name="moe_sc_gather")(table, idx)


def _expert_rows(x_ref, valid, wg, wu, wd, y_ref):
    live = lax.broadcasted_iota(jnp.int32, x_ref.shape, 0) < valid
    x = _unpack_rows(jnp.where(live, x_ref[...], 0)).astype(BF16)
    hcat = _silu(_dot(x, wg)) * _dot(x, wu)
    y_ref[...] = _pack_rows(_dot(hcat.astype(BF16), wd))


def _expert_cast_kernel(be_ref, bv_ref, nu_ref, x_ref, wg_ref, wu_ref, wd_ref, after_ref, y_ref,
                        wgb_ref, wub_ref, wdb_ref):
    del after_ref
    i = pl.program_id(0)
    prev = be_ref[jnp.maximum(i - 1, 0)]

    @pl.when(jnp.logical_and(i < nu_ref[0], jnp.logical_or(i == 0, be_ref[i] != prev)))
    def _():
        wgb_ref[...] = wg_ref[...].astype(BF16)
        wub_ref[...] = wu_ref[...].astype(BF16)
        wdb_ref[...] = wd_ref[...].astype(BF16)

    @pl.when(i < nu_ref[0])
    def _():
        _expert_rows(x_ref, bv_ref[i], wgb_ref[...], wub_ref[...], wdb_ref[...], y_ref)


def _expert_ready_kernel(be_ref, bv_ref, nu_ref, x_ref, wg_ref, wu_ref, wd_ref, y_ref):
    i = pl.program_id(0)

    @pl.when(i < nu_ref[0])
    def _():
        _expert_rows(x_ref, bv_ref[i], wg_ref[...], wu_ref[...], wd_ref[...], y_ref)


def _experts(x_sorted, blk_e, blk_valid, n_used, weights, br, after):
    n_rows = x_sorted.shape[0]
    nblk = n_rows // br
    cast = weights[0].dtype != BF16

    def blk(i, be, bv, nu):
        return (jnp.minimum(i, nu[0] - 1), 0)

    def expert(i, be, bv, nu):
        return (be[i], 0, 0)

    wspecs = [pl.BlockSpec((None,) + w.shape[1:], expert) for w in weights]
    y_shape = jax.ShapeDtypeStruct((n_rows, HALF), PACKED)
    if cast:
        in_specs = [pl.BlockSpec((br, HALF), blk), *wspecs,
                    pl.BlockSpec(after.shape, lambda i, be, bv, nu: (0,) * after.ndim)]
        out_specs = [pl.BlockSpec((br, HALF), blk), *wspecs]
        out_shape = [y_shape] + [jax.ShapeDtypeStruct(w.shape, BF16) for w in weights]
        body, args = _expert_cast_kernel, (x_sorted, *weights, after)
    else:
        in_specs = [pl.BlockSpec((br, HALF), blk), *wspecs]
        out_specs, out_shape = pl.BlockSpec((br, HALF), blk), y_shape
        body, args = _expert_ready_kernel, (x_sorted, *weights)
    out = pl.pallas_call(
        body,
        grid_spec=pltpu.PrefetchScalarGridSpec(num_scalar_prefetch=3, grid=(nblk,), in_specs=in_specs,
                                               out_specs=out_specs),
        out_shape=out_shape,
        compiler_params=_params("arbitrary"),
        name="moe_experts",
    )(blk_e, blk_valid, n_used, *args)
    return (out[0], tuple(out[1:])) if cast else (out, weights)


def _combine_kernel(dest_ref, ys_ref, w_ref, u_ref, x1_ref, g2_ref, sg_ref, su_ref, sd_ref, lg_ref, lb_ref,
                    o_ref, g_scr, sem, *, tm):
    def start(r, carry):
        for k in range(TOP_K):
            _row_copy(ys_ref, dest_ref[k, r], g_scr.at[k], r, sem).start()
        return carry

    def wait(r, carry):
        for k in range(TOP_K):
            _row_copy(ys_ref, dest_ref[k, r], g_scr.at[k], r, sem).wait()
        return carry

    lax.fori_loop(0, tm, start, 0)
    f = _shared_ffn(u_ref, sg_ref, su_ref, sd_ref)
    lax.fori_loop(0, tm, wait, 0)
    f = _add_routed(f, g_scr, w_ref[...])
    o_ref[...] = _layer_norm(ALPHA * x1_ref[...] + g2_ref[...] * f, lg_ref[...], lb_ref[...])


def _shared_ffn(u_ref, sg_ref, su_ref, sd_ref):
    ub = _unpack_rows(u_ref[...]).astype(BF16)
    hs = _silu(_dot(ub, sg_ref[...])) * _dot(ub, su_ref[...])
    return _dot(hs.astype(BF16), sd_ref[...])


def _add_routed(f, slots_ref, w):
    for k in range(TOP_K):
        f = f + _unpack_rows(slots_ref[k]) * w[:, k:k + 1]
    return f


def _combine_gathered_kernel(g_ref, w_ref, u_ref, x1_ref, g2_ref, sg_ref, su_ref, sd_ref, lg_ref, lb_ref, o_ref):
    f = _add_routed(_shared_ffn(u_ref, sg_ref, su_ref, sd_ref), g_ref, w_ref[...])
    o_ref[...] = _layer_norm(ALPHA * x1_ref[...] + g2_ref[...] * f, lg_ref[...], lb_ref[...])


def _combine(y_rows, dest_t, w_t, u2p, x1, g2, ws_gate, ws_up, ws_down, ln_g, ln_b, rows_per_mod, gathered):
    t = u2p.shape[0]
    tm = min(512 if gathered else 128, t)
    nt = t // tm
    rows = pl.BlockSpec((tm, D_MODEL), lambda i: (i, 0))
    vec = pl.BlockSpec((1, D_MODEL), lambda i: (0, 0))
    g2a, mspec = _mod_spec(g2, t, tm, rows_per_mod)
    common_specs = [pl.BlockSpec((tm, TOP_K), lambda i: (i, 0)),
                    pl.BlockSpec((tm, HALF), lambda i: (i, 0)), rows, mspec,
                    pl.BlockSpec((D_MODEL, D_SHARED), lambda i: (0, 0)),
                    pl.BlockSpec((D_MODEL, D_SHARED), lambda i: (0, 0)),
                    pl.BlockSpec((D_SHARED, D_MODEL), lambda i: (0, 0)),
                    vec, vec]
    common_args = (w_t.T, u2p, x1, g2a, ws_gate.astype(BF16), ws_up.astype(BF16), ws_down.astype(BF16),
                   ln_g.reshape(1, D_MODEL), ln_b.reshape(1, D_MODEL))
    if gathered:
        return pl.pallas_call(
            _combine_gathered_kernel,
            grid=(nt,),
            in_specs=[pl.BlockSpec((TOP_K, tm, HALF), lambda i: (0, i, 0))] + common_specs,
            out_specs=rows,
            out_shape=jax.ShapeDtypeStruct((t, D_MODEL), F32),
            compiler_params=_params("arbitrary"),
            name="moe_combine_gathered",
        )(y_rows.reshape(TOP_K, t, HALF), *common_args)
    dest_blocks = dest_t.reshape(TOP_K, nt, tm).transpose(1, 0, 2)
    return pl.pallas_call(
        functools.partial(_combine_kernel, tm=tm),
        grid=(nt,),
        in_specs=[pl.BlockSpec((None, TOP_K, tm), lambda i: (i, 0, 0), memory_space=pltpu.SMEM),
                  pl.BlockSpec(memory_space=pl.ANY)] + common_specs,
        out_specs=rows,
        out_shape=jax.ShapeDtypeStruct((t, D_MODEL), F32),
        scratch_shapes=[pltpu.VMEM((TOP_K, tm, HALF), PACKED), pltpu.SemaphoreType.DMA(())],
        compiler_params=_params("arbitrary"),
        name="moe_combine",
    )(dest_blocks, y_rows, *common_args)


def _moe(u2p, logits, x1, g2, p, weights, rows_per_mod, br, after):
    t = u2p.shape[0]
    e_t, w_t, r_t, counts = _router(logits, p["b_router"])
    n_rows = (t * TOP_K + N_EXPERTS * br + br - 1) // br * br
    dest_t, blk_e, blk_valid, n_used = _plan(e_t, r_t, counts, br, n_rows // br)
    on_sc = t % (SC_WORKERS * SC_WINDOW) == 0
    if on_sc:
        dest_blocks = dest_t.reshape(TOP_K, t // SC_WINDOW, SC_WINDOW).transpose(1, 0, 2)
        x_sorted = _sc_scatter_rows(u2p, dest_blocks, n_rows)
    else:
        x_sorted = _dispatch(u2p, dest_t, n_rows)
    y_rows, weights = _experts(x_sorted, blk_e, blk_valid, n_used, weights, br, after)
    if on_sc:
        y_rows = _sc_gather_rows(y_rows, dest_t.reshape(TOP_K * t))
    return _combine(y_rows, dest_t, w_t, u2p, x1, g2, p["ws_gate"], p["ws_up"], p["ws_down"],
                    p["ln2_g"], p["ln2_b"], rows_per_mod, on_sc), weights


def _cat_w_in(w_in):
    sizes = (SSD_INNER, CONV_DIM, SSD_HEADS, 1024, 1024, 1024, D_MODEL, D_MODEL)
    offs = [0]
    for s in sizes:
        offs.append(offs[-1] + s)
    z, xbc, dt, q, k, v, ga, gb = [w_in[:, offs[i]:offs[i + 1]] for i in range(8)]
    dt = jnp.pad(dt, ((0, 0), (0, SEG_END - SEG_DT - SSD_HEADS)))
    return jnp.concatenate([z, xbc, q, k, v, ga, gb, dt], axis=1).astype(BF16)


def kernel(x_prompt, x_sample, c_prompt, c_sample, cache_k, cache_v, page_table, state_conv, state_ssm, rel_bias, w_ada, b_ada, w_in, conv_w, conv_b, dt_bias, a_log, d_skip, ssd_norm_w, lam_q1, lam_k1, lam_q2, lam_k2, subln_w, w_br_ssd, w_br_attn, w_out, ln1_g, ln1_b, w_router, b_router, w_gate, w_up, w_down, ws_gate, ws_up, ws_down, ln2_g, ln2_b):
    assert w_in.shape[0] == DEPTH
    nbp, seq, _ = x_prompt.shape
    nbs, dseq, _ = x_sample.shape
    tp, ts = nbp * seq, nbs * dseq
    past = page_table.shape[1] * PAGE_SIZE
    cl = SSD_CHUNK

    pad_p = -nbp % 8
    c_all = jnp.concatenate([c_prompt, jnp.zeros((pad_p, D_MODEL), F32), jnp.repeat(c_sample, dseq, axis=0)], axis=0)
    mod = _adaln(c_all, w_ada[0], b_ada[0])
    mod_p = mod[:nbp].reshape(nbp, 1, 6, D_MODEL)
    mod_s = mod[nbp + pad_p:].reshape(nbs, dseq, 6, D_MODEL)
    mp = [mod_p[:, :, i] for i in range(6)]
    ms = [mod_s[:, :, i] for i in range(6)]

    w_cat = _cat_w_in(w_in[0])
    wa, wb, wo = w_br_ssd[0].astype(BF16), w_br_attn[0].astype(BF16), w_out[0].astype(BF16)
    lamp = jnp.stack([lam_q1[0], lam_k1[0], lam_q2[0], lam_k2[0]])
    moe_p = dict(b_router=b_router[0], ws_gate=ws_gate[0], ws_up=ws_up[0], ws_down=ws_down[0], ln2_g=ln2_g[0], ln2_b=ln2_b[0])
    ssd_w = (conv_w[0], conv_b[0], dt_bias[0], a_log[0], d_skip[0], ssd_norm_w[0])

    xp = x_prompt.reshape(tp, D_MODEL)
    z, xbc, dt, q, k, v, kb, vb, ga, gb = _inproj(xp, mp[1], mp[0], w_cat, seq)
    yn, conv_p, h_p = _ssd(xbc, dt, z, *ssd_w, jnp.zeros((nbp, CONV_W - 1, CONV_DIM), F32),
                           jnp.zeros((nbp, SSD_HEADS, SSD_HEADDIM, D_STATE), F32), nbp, seq // cl, cl)
    yn = yn.reshape(tp, SSD_INNER)
    on = _attn_prompt(q, kb, vb, rel_bias, lamp, subln_w[0], nbp, seq)
    x1_p, u2p_p, logits_p = _merge(yn, on, ga, gb, xp, mp[2], mp[4], mp[3], wa, wb, wo, ln1_g[0], ln1_b[0],
                                   w_router[0], seq)
    k_prompt = k.reshape(1, nbp, seq, ATTN_HEADS, 2 * ATTN_HEAD_DIM)
    v_prompt = v.reshape(1, nbp, seq, ATTN_HEADS, ATTN_V_DIM)

    xs_ = x_sample.reshape(ts, D_MODEL)
    z, xbc, dt, q, k, v, kb, vb, ga, gb = _inproj(xs_, ms[1], ms[0], w_cat, dseq)
    yn, conv_s, h_s = _ssd(xbc, dt, z, *ssd_w, state_conv[0], state_ssm[0], nbs, 1, dseq)
    yn = yn.reshape(ts, SSD_INNER)
    on = _attn_decode(q.reshape(nbs, dseq, -1), k.reshape(nbs, dseq * ATTN_HEADS, LANE),
                      v.reshape(nbs, dseq * ATTN_HEADS, LANE), cache_k[0], cache_v[0], page_table,
                      rel_bias, lamp, subln_w[0])
    y_prompt, w_bf16 = _moe(u2p_p, logits_p, x1_p, mp[5], moe_p, (w_gate[0], w_up[0], w_down[0]), seq, 1024,
                            on[0, :, :LANE])
    y_prompt = y_prompt.reshape(nbp, seq, D_MODEL)
    x1, u2p, logits = _merge(yn, on.reshape(ts, -1), ga, gb, xs_, ms[2], ms[4], ms[3], wa, wb, wo, ln1_g[0], ln1_b[0],
                             w_router[0], dseq)
    y_sample = _moe(u2p, logits, x1, ms[5], moe_p, w_bf16, dseq, 64, None)[0].reshape(nbs, dseq, D_MODEL)
    k_sample = k.reshape(1, nbs, dseq, ATTN_HEADS, 2 * ATTN_HEAD_DIM)
    v_sample = v.reshape(1, nbs, dseq, ATTN_HEADS, ATTN_V_DIM)

    return (y_prompt, y_sample, k_prompt, v_prompt, conv_p[None], h_p.reshape(1, nbp, SSD_HEADS, SSD_HEADDIM, D_STATE),
            k_sample, v_sample, conv_s[None], h_s.reshape(1, nbs, SSD_HEADS, SSD_HEADDIM, D_STATE))
```

```python
import functools
import math

import jax
import jax.numpy as jnp
from jax import lax
from jax.experimental import pallas as pl
from jax.experimental.pallas import tpu as pltpu
from jax.experimental.pallas import tpu_sc as plsc

F32 = jnp.float32
BF16 = jnp.bfloat16
HIGHEST = lax.Precision.HIGHEST

D_MODEL = 1024
SSD_INNER = 1024
SSD_HEADDIM = 64
SSD_HEADS = 16
SSD_GROUPS = 2
D_STATE = 128
CONV_W = 4
CONV_DIM = SSD_INNER + 2 * SSD_GROUPS * D_STATE
SSD_CHUNK = 128
ATTN_HEADS = 8
ATTN_HEAD_DIM = 64
ATTN_V_DIM = 128
N_BUCKETS = 32
MAX_DISTANCE = 128
N_EXPERTS = 64
N_EXPERT_GROUPS = 8
GROUP_SIZE = N_EXPERTS // N_EXPERT_GROUPS
TOPK_GROUPS = 4
TOP_K = 8
D_EXPERT = 256
D_SHARED = 256
ROUTED_SCALE = 2.5
PAGE_SIZE = 128
DEPTH = 1
ALPHA = (2 * DEPTH) ** 0.25
LN_EPS = 1e-5
RMS_EPS = 1e-5
LAM_INIT = 0.8 - 0.6 * math.exp(-0.3 * 0)
LOG2E = math.log2(math.e)
LANE = 128
VMEM_LIMIT = 56 * 1024 * 1024

SEG_Z, SEG_XBC, SEG_Q, SEG_K, SEG_V, SEG_GA, SEG_GB, SEG_DT, SEG_END = (
    0, 1024, 2560, 3584, 4608, 5632, 6656, 7680, 7808)


def _silu(x):
    return x * jax.nn.sigmoid(x)


def _dot(a, b):
    return jnp.dot(a, b, preferred_element_type=F32)


def _dot_nt(a, b):
    return lax.dot_general(a, b, (((1,), (1,)), ((), ())), preferred_element_type=F32)


def _dot_tn(a, b):
    return lax.dot_general(a, b, (((0,), (0,)), ((), ())), preferred_element_type=F32)


def _params(*sem):
    return pltpu.CompilerParams(dimension_semantics=sem, vmem_limit_bytes=VMEM_LIMIT)


def _adaln_kernel(c_ref, w_ref, b_ref, o_ref):
    s = _silu(c_ref[...]).astype(BF16)
    o_ref[...] = _dot(s, w_ref[...].astype(BF16)) + b_ref[...]


def _adaln(c, w_ada, b_ada):
    r = c.shape[0]
    n = w_ada.shape[1]
    tn = 1024
    return pl.pallas_call(
        _adaln_kernel,
        grid=(n // tn,),
        in_specs=[pl.BlockSpec((r, D_MODEL), lambda j: (0, 0)),
                  pl.BlockSpec((D_MODEL, tn), lambda j: (0, j)),
                  pl.BlockSpec((1, tn), lambda j: (0, j))],
        out_specs=pl.BlockSpec((r, tn), lambda j: (0, j)),
        out_shape=jax.ShapeDtypeStruct((r, n), F32),
        compiler_params=_params("arbitrary"),
        name="adaln",
    )(c, w_ada, b_ada.reshape(1, n))


def _inproj_kernel(x_ref, sc_ref, sh_ref, w_ref, z_ref, xbc_ref, dt_ref, q_ref, k_ref, v_ref,
                   kb_ref, vb_ref, ga_ref, gb_ref):
    u = (x_ref[...] * (1.0 + sc_ref[...]) + sh_ref[...]).astype(BF16)

    def seg(a, b):
        return _dot(u, w_ref[:, a:b])

    z_ref[...] = seg(SEG_Z, SEG_XBC).astype(BF16)
    xbc_ref[...] = seg(SEG_XBC, SEG_Q)
    q_ref[...] = (seg(SEG_Q, SEG_K) * (ATTN_HEAD_DIM ** -0.5 * LOG2E)).astype(BF16)
    kk = seg(SEG_K, SEG_V)
    k_ref[...] = kk
    kb_ref[...] = kk.astype(BF16)
    vv = seg(SEG_V, SEG_GA)
    v_ref[...] = vv
    vb_ref[...] = vv.astype(BF16)
    ga_ref[...] = seg(SEG_GA, SEG_GB).astype(BF16)
    gb_ref[...] = seg(SEG_GB, SEG_DT).astype(BF16)
    dt_ref[...] = seg(SEG_DT, SEG_END)


def _inproj(x, sc, sh, w_cat, rows_per_mod):
    t = x.shape[0]
    tm = min(256, t)
    per_row = sc.shape[1] != 1
    if per_row:
        sc2 = sc.reshape(t, D_MODEL)
        sh2 = sh.reshape(t, D_MODEL)
        mod_spec = pl.BlockSpec((tm, D_MODEL), lambda i: (i, 0))
    else:
        assert rows_per_mod % tm == 0
        sc2, sh2 = sc, sh
        mod_spec = pl.BlockSpec((None, 1, D_MODEL), lambda i: (i // (rows_per_mod // tm), 0, 0))

    def rows(width):
        return pl.BlockSpec((tm, width), lambda i: (i, 0))

    widths = (1024, CONV_DIM, LANE, 1024, 1024, 1024, 1024, 1024, 1024, 1024)
    dtypes = (BF16, F32, F32, BF16, F32, F32, BF16, BF16, BF16, BF16)
    return pl.pallas_call(
        _inproj_kernel,
        grid=(t // tm,),
        in_specs=[rows(D_MODEL), mod_spec, mod_spec,
                  pl.BlockSpec((D_MODEL, SEG_END), lambda i: (0, 0), pipeline_mode=pl.Buffered(1))],
        out_specs=[rows(w) for w in widths],
        out_shape=[jax.ShapeDtypeStruct((t, w), d) for w, d in zip(widths, dtypes)],
        compiler_params=_params("arbitrary"),
        name="inproj",
    )(x, sc2, sh2, w_cat)


def _ssd_kernel(xbc_ref, dt_ref, z_ref, cw_ref, cb_ref, dtb_ref, alog_ref, dsk_ref, nw_ref,
                cbuf_ref, h0_ref, yn_ref, cnew_ref, hnew_ref, xp_scr, h_scr, y_scr, *, valid_len):
    c = pl.program_id(1)
    nc = pl.num_programs(1)
    cl = SSD_CHUNK
    head = 8

    @pl.when(c == 0)
    def _():
        xp_scr[head - (CONV_W - 1):head, :] = cbuf_ref[...]
        h_scr[...] = h0_ref[...]

    @pl.when(c > 0)
    def _():
        xp_scr[head - (CONV_W - 1):head, :] = xp_scr[head + cl - (CONV_W - 1):head + cl, :]

    xp_scr[head:head + valid_len, :] = xbc_ref[...]
    if valid_len < cl:
        xp_scr[head + valid_len:head + cl, :] = jnp.zeros((cl - valid_len, CONV_DIM), F32)

    acc = xp_scr[head:head + cl, :] * cw_ref[CONV_W - 1:CONV_W, :]
    for j in range(CONV_W - 1):
        lo = head - (CONV_W - 1) + j
        acc = acc + xp_scr[lo:lo + cl, :] * cw_ref[j:j + 1, :]
    xc = _silu(acc + cb_ref[...])
    xs = xc[:, :SSD_INNER]
    bmat = [xc[:, SSD_INNER + g * D_STATE:SSD_INNER + (g + 1) * D_STATE].astype(BF16)
            for g in range(SSD_GROUPS)]
    coff = SSD_INNER + SSD_GROUPS * D_STATE
    cmat = [xc[:, coff + g * D_STATE:coff + (g + 1) * D_STATE].astype(BF16) for g in range(SSD_GROUPS)]

    li = lax.broadcasted_iota(jnp.int32, (cl, cl), 0)
    si = lax.broadcasted_iota(jnp.int32, (cl, cl), 1)
    causal = li >= si
    lane_lo = si < SSD_HEADDIM
    sub_lo = li < SSD_HEADDIM

    def rows_padded(v):
        return v if valid_len == cl else jnp.concatenate([v, jnp.zeros((cl - valid_len, v.shape[1]), F32)], axis=0)

    dpre = rows_padded(dt_ref[...]) + dtb_ref[...]
    dtv = jnp.maximum(dpre, 0.0) + jnp.log(1.0 + jnp.exp(-jnp.abs(dpre)))
    if valid_len < cl:
        dtv = jnp.where(li < valid_len, dtv, 0.0)
    da = dtv * (-jnp.exp(alog_ref[...]))
    tril = causal.astype(F32)
    acum = jnp.dot(tril, da, preferred_element_type=F32, precision=HIGHEST)
    acum_t = acum.T
    dt_t = dtv.T
    last = acum[cl - 1:cl, :]
    sel = (lax.broadcasted_iota(jnp.int32, (LANE, SSD_INNER), 0)
           == lax.broadcasted_iota(jnp.int32, (LANE, SSD_INNER), 1) // SSD_HEADDIM).astype(BF16)
    e_cols = _dot(jnp.exp(acum).astype(BF16), sel)
    w_cols = _dot((jnp.exp(last - acum) * dtv).astype(BF16), sel)
    e_last = jnp.exp(last)

    cb = [_dot_nt(cmat[g], bmat[g]) for g in range(SSD_GROUPS)]

    def colb(a, h):
        return jnp.broadcast_to(a[:, h:h + 1], (cl, cl))

    pairs = SSD_HEADS // 2
    for p in range(pairs):
        h0, h1 = 2 * p, 2 * p + 1
        g = h0 // (SSD_HEADS // SSD_GROUPS)
        xpair = xs[:, p * LANE:(p + 1) * LANE]
        x_lo = jnp.where(lane_lo, xpair, 0.0).astype(BF16)
        x_hi = jnp.where(lane_lo, 0.0, xpair).astype(BF16)
        ydiag = None
        for hh, xm in ((h0, x_lo), (h1, x_hi)):
            seg = colb(acum, hh) - acum_t[hh:hh + 1, :]
            dec = jnp.exp(jnp.where(causal, seg, -jnp.inf))
            m = (cb[g] * dec * dt_t[hh:hh + 1, :]).astype(BF16)
            part = _dot(m, xm)
            ydiag = part if ydiag is None else ydiag + part
        ecol = e_cols[:, p * LANE:(p + 1) * LANE]
        hp = h_scr[p]
        yoff = _dot_nt(cmat[g], hp.astype(BF16)) * ecol
        wcol = w_cols[:, p * LANE:(p + 1) * LANE]
        dx = (xpair * wcol).astype(BF16)
        st = _dot_tn(dx, bmat[g])
        hdec = jnp.where(sub_lo, e_last[:, h0:h0 + 1], e_last[:, h1:h1 + 1])
        h_scr[p] = hp * hdec + st
        y_scr[:, p * LANE:(p + 1) * LANE] = ydiag + yoff + dsk_ref[:, p * LANE:(p + 1) * LANE] * xpair

    zf = z_ref[...].astype(F32)
    gated = y_scr[0:valid_len, :] * _silu(zf)
    gw = SSD_INNER // SSD_GROUPS
    for g in range(SSD_GROUPS):
        sg = gated[:, g * gw:(g + 1) * gw]
        ms = jnp.mean(sg * sg, axis=-1, keepdims=True)
        yn_ref[:, g * gw:(g + 1) * gw] = (sg * lax.rsqrt(ms + RMS_EPS) * nw_ref[:, g * gw:(g + 1) * gw]).astype(yn_ref.dtype)

    @pl.when(c == nc - 1)
    def _():
        cnew_ref[...] = xp_scr[head + valid_len - (CONV_W - 1):head + valid_len, :]
        hnew_ref[...] = h_scr[...]


def _ssd(xbc, dt, z, conv_w, conv_b, dt_bias, a_log, d_skip, norm_w, cbuf, h0, nb, nc, valid_len):
    cl = SSD_CHUNK
    assert valid_len == cl or nc == 1
    pairs = SSD_HEADS // 2
    pad = LANE - SSD_HEADS
    row = lambda b, c: (b * nc + c, 0, 0)
    const = lambda b, c: (0, 0)
    chunks = lambda a: a.reshape(nb * nc, valid_len, a.shape[-1])
    z = z if valid_len % 16 == 0 else z.astype(F32)
    return pl.pallas_call(
        functools.partial(_ssd_kernel, valid_len=valid_len),
        grid=(nb, nc),
        in_specs=[pl.BlockSpec((None, valid_len, CONV_DIM), row), pl.BlockSpec((None, valid_len, LANE), row),
                  pl.BlockSpec((None, valid_len, SSD_INNER), row),
                  pl.BlockSpec((CONV_W, CONV_DIM), const), pl.BlockSpec((1, CONV_DIM), const),
                  pl.BlockSpec((1, LANE), const), pl.BlockSpec((1, LANE), const),
                  pl.BlockSpec((1, SSD_INNER), const), pl.BlockSpec((1, SSD_INNER), const),
                  pl.BlockSpec((None, CONV_W - 1, CONV_DIM), lambda b, c: (b, 0, 0)),
                  pl.BlockSpec((None, pairs, LANE, D_STATE), lambda b, c: (b, 0, 0, 0))],
        out_specs=[pl.BlockSpec((None, valid_len, SSD_INNER), row),
                   pl.BlockSpec((None, CONV_W - 1, CONV_DIM), lambda b, c: (b, 0, 0)),
                   pl.BlockSpec((None, pairs, LANE, D_STATE), lambda b, c: (b, 0, 0, 0))],
        out_shape=[jax.ShapeDtypeStruct((nb * nc, valid_len, SSD_INNER), z.dtype),
                   jax.ShapeDtypeStruct((nb, CONV_W - 1, CONV_DIM), F32),
                   jax.ShapeDtypeStruct((nb, pairs, LANE, D_STATE), F32)],
        scratch_shapes=[pltpu.VMEM((8 + cl, CONV_DIM), F32), pltpu.VMEM((pairs, LANE, D_STATE), F32),
                        pltpu.VMEM((cl, SSD_INNER), F32)],
        compiler_params=_params("arbitrary", "arbitrary"),
        name="ssd",
    )(chunks(xbc), chunks(dt), chunks(z), conv_w, conv_b.reshape(1, CONV_DIM),
      jnp.pad(dt_bias, (0, pad)).reshape(1, LANE), jnp.pad(a_log, (0, pad)).reshape(1, LANE),
      jnp.repeat(d_skip, SSD_HEADDIM).reshape(1, SSD_INNER), norm_w.reshape(1, SSD_INNER),
      cbuf, h0.reshape(nb, pairs, LANE, D_STATE))


def _bias_kernel(rel_ref, tab_ref, o_ref, *, shift_far):
    rel = rel_ref[...]
    n = jnp.maximum(rel, 0)
    max_exact = N_BUCKETS // 2
    large = max_exact + (jnp.log(jnp.maximum(n, 1).astype(F32) / max_exact)
                         / math.log(MAX_DISTANCE / max_exact) * (N_BUCKETS - max_exact)).astype(jnp.int32)
    bucket = jnp.where(n < max_exact, n, jnp.minimum(large, N_BUCKETS - 1))
    outs = [jnp.zeros(rel.shape, F32) for _ in range(ATTN_HEADS)]
    for kb in range(N_BUCKETS):
        hit = bucket == kb
        for h in range(ATTN_HEADS):
            far = tab_ref[N_BUCKETS - 1, h] if shift_far else 0.0
            outs[h] = jnp.where(hit, (tab_ref[kb, h] - far) * LOG2E, outs[h])
    for h in range(ATTN_HEADS):
        o_ref[h] = jnp.where(rel >= 0, outs[h], -jnp.inf)


def _bias_tiles(rel, rel_bias, shift_far):
    r, c = rel.shape
    tr = min(r, 64)
    return pl.pallas_call(
        functools.partial(_bias_kernel, shift_far=shift_far),
        grid=(r // tr,),
        in_specs=[pl.BlockSpec((tr, c), lambda i: (i, 0)),
                  pl.BlockSpec(memory_space=pltpu.SMEM)],
        out_specs=pl.BlockSpec((ATTN_HEADS, tr, c), lambda i: (0, i, 0)),
        out_shape=jax.ShapeDtypeStruct((ATTN_HEADS, r, c), F32),
        compiler_params=_params("arbitrary"),
        name="t5bias",
    )(rel, rel_bias)


def _lam(lamp_ref):
    lp = lamp_ref[...]
    e1 = jnp.exp(jnp.sum(lp[0:1, :] * lp[1:2, :], axis=-1, keepdims=True))
    e2 = jnp.exp(jnp.sum(lp[2:3, :] * lp[3:4, :], axis=-1, keepdims=True))
    return e1 - e2 + LAM_INIT


def _subln(o, sw):
    ms = jnp.mean(o * o, axis=-1, keepdims=True)
    return o * lax.rsqrt(ms + RMS_EPS) * sw * (1.0 - LAM_INIT)


ATTN_TQ = 512
ATTN_STRIP = 512
ATTN_TK = 512
ATTN_D_MIN = 1 - ATTN_STRIP // ATTN_TK
ATTN_D_FAR = -(-(MAX_DISTANCE + ATTN_TK - 1) // ATTN_TK)


def _attn_kernel(q_ref, k_ref, v_ref, bias_ref, lamp_ref, sw_ref, o_ref, m_scr, l_scr, a_scr, *, tq, strip, tk):
    qi = pl.program_id(2)
    lane_lo = lax.broadcasted_iota(jnp.int32, (strip, LANE), 1) < ATTN_HEAD_DIM
    reps = tk // LANE
    lam = _lam(lamp_ref)
    for st in range(tq // strip):
        q = q_ref[st * strip:(st + 1) * strip, :]
        zero = jnp.zeros_like(q)
        qm = (jnp.where(lane_lo, q, zero), jnp.where(lane_lo, zero, q))
        m_scr[...] = jnp.full(m_scr.shape, -jnp.inf, F32)
        l_scr[...] = jnp.zeros(l_scr.shape, F32)
        a_scr[...] = jnp.zeros(a_scr.shape, F32)
        row0 = qi * (tq // strip) + st
        ntiles = (row0 * strip + strip - 1) // tk + 1

        def tile(j, near):
            off = pl.multiple_of(j * tk, tk)
            kt = k_ref[pl.ds(off, tk), :]
            vt = v_ref[pl.ds(off, tk), :]
            scs = [_dot_nt(qm[mi], kt) for mi in range(2)]
            if near:
                bias = bias_ref[row0 * (strip // tk) - j - ATTN_D_MIN]
                scs = [sc + bias for sc in scs]
            ms = [jnp.maximum(m_scr[mi], jnp.max(scs[mi], axis=-1, keepdims=True)) for mi in range(2)]
            ps = [jnp.exp2(scs[mi] - jnp.concatenate([ms[mi]] * reps, axis=1)) for mi in range(2)]
            for mi in range(2):
                alpha = jnp.exp2(m_scr[mi] - ms[mi])
                l_scr[mi] = alpha * l_scr[mi] + jnp.sum(ps[mi], axis=-1, keepdims=True)
                a_scr[mi] = alpha * a_scr[mi] + _dot(ps[mi].astype(BF16), vt)
                m_scr[mi] = ms[mi]

        nfar = jnp.maximum(row0 * (strip // tk) - ATTN_D_FAR + 1, 0)
        lax.fori_loop(0, nfar, lambda j, c: (tile(j, False), c)[1], 0)
        lax.fori_loop(nfar, ntiles, lambda j, c: (tile(j, True), c)[1], 0)
        o = a_scr[0] / l_scr[0] - lam * (a_scr[1] / l_scr[1])
        o_ref[st * strip:(st + 1) * strip, :] = _subln(o, sw_ref[...]).astype(BF16)


def _attn_prompt(q, kb, vb, rel_bias, lamp, subln_w, nb, seq):
    tq, strip, tk = ATTN_TQ, ATTN_STRIP, ATTN_TK
    assert seq % tq == 0 and tq % strip == 0 and strip % tk == 0
    nq = seq // tq
    nd = ATTN_D_FAR - ATTN_D_MIN
    rel = (jnp.arange(ATTN_D_MIN, ATTN_D_FAR, dtype=jnp.int32)[:, None, None] * tk
           + jnp.arange(strip, dtype=jnp.int32)[None, :, None] - jnp.arange(tk, dtype=jnp.int32)[None, None, :])
    bias = _bias_tiles(rel.reshape(-1, tk), rel_bias, True).reshape(ATTN_HEADS, nd, strip, tk)
    return pl.pallas_call(
        functools.partial(_attn_kernel, tq=tq, strip=strip, tk=tk),
        grid=(nb, ATTN_HEADS, nq),
        in_specs=[pl.BlockSpec((tq, LANE), lambda b, h, i: (b * nq + i, h)),
                  pl.BlockSpec((seq, LANE), lambda b, h, i: (b, h)),
                  pl.BlockSpec((seq, LANE), lambda b, h, i: (b, h)),
                  pl.BlockSpec((None, nd, strip, tk), lambda b, h, i: (h, 0, 0, 0)),
                  pl.BlockSpec((4, ATTN_HEAD_DIM), lambda b, h, i: (0, 0)),
                  pl.BlockSpec((1, ATTN_V_DIM), lambda b, h, i: (0, 0))],
        out_specs=pl.BlockSpec((tq, LANE), lambda b, h, i: (b * nq + i, h)),
        out_shape=jax.ShapeDtypeStruct((nb * seq, ATTN_HEADS * ATTN_V_DIM), BF16),
        scratch_shapes=[pltpu.VMEM((2, strip, LANE), F32)] * 3,
        compiler_params=_params("arbitrary", "arbitrary", "arbitrary"),
        name="attn_prompt",
    )(q, kb, vb, bias, lamp, subln_w.reshape(1, ATTN_V_DIM))


DECODE_PAGES = 8
DECODE_SEQS = 1


def _decode_kernel(pt_ref, q_ref, kn_ref, vn_ref, *rest, nq, pps, nbq):
    npg = nbq * pps
    ck_refs, cv_refs = rest[:npg], rest[npg:2 * npg]
    blast_ref, bnew_ref, tab_ref, lamp_ref, sw_ref, o_ref, qa_scr, m_scr, l_scr, a_scr = rest[2 * npg:]
    j = pl.program_id(1)
    nsteps = pl.num_programs(1)
    is_last = j == nsteps - 1

    def head_rows(ref, h, n):
        return ref[pl.ds(h, n, stride=ATTN_HEADS), :]

    @pl.when(j == 0)
    def _():
        lane_lo = lax.broadcasted_iota(jnp.int32, (nq, LANE), 1) < ATTN_HEAD_DIM
        zpad = jnp.zeros((PAGE_SIZE - nq, LANE), F32)
        for bi in range(nbq):
            for h in range(ATTN_HEADS):
                qh = q_ref[bi, :, h * LANE:(h + 1) * LANE].astype(F32)
                qa = jnp.concatenate([jnp.where(lane_lo, qh, 0.0), jnp.where(lane_lo, 0.0, qh)], axis=0)
                qa_scr[bi, h] = qa
                kn = jnp.concatenate([head_rows(kn_ref.at[bi], h, nq), zpad], axis=0).astype(BF16)
                vn = jnp.concatenate([head_rows(vn_ref.at[bi], h, nq), zpad], axis=0).astype(BF16)
                s = _dot_nt(qa.astype(BF16), kn) + bnew_ref[h]
                m = jnp.max(s, axis=-1, keepdims=True)
                p = jnp.exp2(s - m)
                m_scr[bi, h] = m
                l_scr[bi, h] = jnp.sum(p, axis=-1, keepdims=True)
                a_scr[bi, h] = _dot(p.astype(BF16), vn)

    heads = [(bi, h) for bi in range(nbq) for h in range(ATTN_HEADS)]
    scores = {}
    for bi, h in heads:
        qa = qa_scr[bi, h].astype(BF16)
        far = tab_ref[N_BUCKETS - 1, h] * LOG2E
        for i in range(pps):
            bias = jnp.where(is_last, blast_ref[h], far) if i == pps - 1 else far
            kh = head_rows(ck_refs[bi * pps + i], h, PAGE_SIZE).astype(BF16)
            scores[bi, h, i] = _dot_nt(qa, kh) + bias
    m_new, probs = {}, {}
    for bi, h in heads:
        m = m_scr[bi, h]
        for i in range(pps):
            m = jnp.maximum(m, jnp.max(scores[bi, h, i], axis=-1, keepdims=True))
        m_new[bi, h] = m
        for i in range(pps):
            probs[bi, h, i] = jnp.exp2(scores[bi, h, i] - m)
    for bi, h in heads:
        alpha = jnp.exp2(m_scr[bi, h] - m_new[bi, h])
        l_new = alpha * l_scr[bi, h]
        a_new = alpha * a_scr[bi, h]
        for i in range(pps):
            p = probs[bi, h, i]
            l_new = l_new + jnp.sum(p, axis=-1, keepdims=True)
            a_new = a_new + _dot(p.astype(BF16), head_rows(cv_refs[bi * pps + i], h, PAGE_SIZE).astype(BF16))
        l_scr[bi, h] = l_new
        a_scr[bi, h] = a_new
        m_scr[bi, h] = m_new[bi, h]

    @pl.when(is_last)
    def _():
        lam = _lam(lamp_ref)
        for bi in range(nbq):
            for h in range(ATTN_HEADS):
                on = a_scr[bi, h] / l_scr[bi, h]
                o = on[0:nq, :] - lam * on[nq:2 * nq, :]
                o_ref[bi, :, h * LANE:(h + 1) * LANE] = _subln(o, sw_ref[...])


def _attn_decode(q, k_new, v_new, cache_k, cache_v, page_table, rel_bias, lamp, subln_w):
    nb, nq, _ = q.shape
    npages = page_table.shape[1]
    past = npages * PAGE_SIZE
    rows = 2 * nq
    prow = PAGE_SIZE * ATTN_HEADS
    nnew = nq * ATTN_HEADS
    assert npages >= 1 and nq <= PAGE_SIZE
    qpos = past + jnp.arange(rows, dtype=jnp.int32) % nq
    kpos = jnp.arange(past - PAGE_SIZE, past + PAGE_SIZE, dtype=jnp.int32).reshape(2, 1, PAGE_SIZE)
    rel = jnp.where(kpos < past + nq, qpos[None, :, None] - kpos, -1)
    tiles = _bias_tiles(rel.reshape(2 * rows, PAGE_SIZE), rel_bias, False).reshape(ATTN_HEADS, 2, rows, PAGE_SIZE)
    pps = math.gcd(DECODE_PAGES, npages)
    nbq = math.gcd(DECODE_SEQS, nb)
    new_q = pl.BlockSpec((nbq, nq, ATTN_HEADS * LANE), lambda b, j, pt: (b, 0, 0))
    new_kv = pl.BlockSpec((nbq, nnew, LANE), lambda b, j, pt: (b, 0, 0))
    head_tile = pl.BlockSpec((ATTN_HEADS, rows, PAGE_SIZE), lambda b, j, pt: (0, 0, 0))

    def page(bi, i):
        return pl.BlockSpec((None, prow, LANE), lambda b, j, pt: (pt[b * nbq + bi, j * pps + i], 0, 0))

    pages = [page(bi, i) for bi in range(nbq) for i in range(pps)]
    grid_spec = pltpu.PrefetchScalarGridSpec(
        num_scalar_prefetch=1,
        grid=(nb // nbq, npages // pps),
        in_specs=[new_q, new_kv, new_kv, *pages, *pages, head_tile, head_tile,
                  pl.BlockSpec(memory_space=pltpu.SMEM),
                  pl.BlockSpec((4, ATTN_HEAD_DIM), lambda b, j, pt: (0, 0)),
                  pl.BlockSpec((1, ATTN_V_DIM), lambda b, j, pt: (0, 0))],
        out_specs=new_q,
        scratch_shapes=[pltpu.VMEM((nbq, ATTN_HEADS, rows, LANE), F32), pltpu.VMEM((nbq, ATTN_HEADS, rows, 1), F32),
                        pltpu.VMEM((nbq, ATTN_HEADS, rows, 1), F32), pltpu.VMEM((nbq, ATTN_HEADS, rows, LANE), F32)])
    ck = cache_k.reshape(cache_k.shape[0], prow, LANE)
    cv = cache_v.reshape(cache_v.shape[0], prow, LANE)
    return pl.pallas_call(
        functools.partial(_decode_kernel, nq=nq, pps=pps, nbq=nbq),
        grid_spec=grid_spec,
        out_shape=jax.ShapeDtypeStruct((nb, nq, ATTN_HEADS * ATTN_V_DIM), F32),
        compiler_params=_params("arbitrary", "arbitrary"),
        name="attn_decode",
    )(page_table, q, k_new, v_new, *([ck] * len(pages)), *([cv] * len(pages)), tiles[:, 0], tiles[:, 1], rel_bias,
      lamp, subln_w.reshape(1, ATTN_V_DIM))


HALF = D_MODEL // 2
PACKED = jnp.int32


def _pack_rows(x):
    word = pltpu.pack_elementwise([x[:, :HALF], x[:, HALF:]], packed_dtype=BF16)
    return lax.bitcast_convert_type(word, PACKED)


def _unpack_rows(p):
    halves = [pltpu.unpack_elementwise(p, index=i, packed_dtype=BF16, unpacked_dtype=F32) for i in range(2)]
    return jnp.concatenate(halves, axis=1)


def _layer_norm(r, g, b):
    mu = jnp.mean(r, axis=-1, keepdims=True)
    d = r - mu
    var = jnp.mean(d * d, axis=-1, keepdims=True)
    return d * lax.rsqrt(var + LN_EPS) * g + b


def _merge_kernel(yn_ref, on_ref, ga_ref, gb_ref, x_ref, g1_ref, sc2_ref, sh2_ref, wa_ref, wb_ref, wo_ref,
                  lg_ref, lb_ref, wr_ref, x1_ref, u2p_ref, logit_ref):
    ba = _dot(yn_ref[...].astype(BF16), wa_ref[...])
    bb = _dot(on_ref[...].astype(BF16), wb_ref[...])
    merged = jax.nn.sigmoid(ga_ref[...].astype(F32)) * ba + jax.nn.sigmoid(gb_ref[...].astype(F32)) * bb
    t = _dot(merged.astype(BF16), wo_ref[...])
    x1 = _layer_norm(ALPHA * x_ref[...] + g1_ref[...] * t, lg_ref[...], lb_ref[...])
    x1_ref[...] = x1
    u2 = x1 * (1.0 + sc2_ref[...]) + sh2_ref[...]
    u2p_ref[...] = _pack_rows(u2)
    u_hi = u2.astype(BF16)
    u_lo = (u2 - u_hi.astype(F32)).astype(BF16)
    by_hi = _dot_nt(wr_ref[...], u_hi)
    logit_ref[...] = by_hi[:N_EXPERTS] + by_hi[N_EXPERTS:] + _dot_nt(wr_ref[:N_EXPERTS, :], u_lo)


def _mod_spec(mod, t, tm, rows_per_mod):
    if mod.shape[1] != 1:
        return mod.reshape(t, D_MODEL), pl.BlockSpec((tm, D_MODEL), lambda i: (i, 0))
    assert rows_per_mod % tm == 0
    return mod, pl.BlockSpec((None, 1, D_MODEL), lambda i: (i // (rows_per_mod // tm), 0, 0))


def _merge(yn, on, ga, gb, x, g1, sc2, sh2, wa, wb, wo, ln_g, ln_b, w_router, rows_per_mod):
    t = x.shape[0]
    tm = min(512, t)
    rows = pl.BlockSpec((tm, D_MODEL), lambda i: (i, 0))
    wspec = pl.BlockSpec((D_MODEL, D_MODEL), lambda i: (0, 0))
    vec = pl.BlockSpec((1, D_MODEL), lambda i: (0, 0))
    g1a, mspec = _mod_spec(g1, t, tm, rows_per_mod)
    sc2a, _ = _mod_spec(sc2, t, tm, rows_per_mod)
    sh2a, _ = _mod_spec(sh2, t, tm, rows_per_mod)
    wr_t = w_router.T
    wr_hi = wr_t.astype(BF16)
    return pl.pallas_call(
        _merge_kernel,
        grid=(t // tm,),
        in_specs=[rows, rows, rows, rows, rows, mspec, mspec, mspec, wspec, wspec, wspec, vec, vec,
                  pl.BlockSpec((2 * N_EXPERTS, D_MODEL), lambda i: (0, 0))],
        out_specs=[rows, pl.BlockSpec((tm, HALF), lambda i: (i, 0)), pl.BlockSpec((N_EXPERTS, tm), lambda i: (0, i))],
        out_shape=[jax.ShapeDtypeStruct((t, D_MODEL), F32), jax.ShapeDtypeStruct((t, HALF), PACKED),
                   jax.ShapeDtypeStruct((N_EXPERTS, t), F32)],
        compiler_params=_params("arbitrary"),
        name="merge",
    )(yn, on, ga, gb, x, g1a, sc2a, sh2a, wa, wb, wo, ln_g.reshape(1, D_MODEL), ln_b.reshape(1, D_MODEL),
      jnp.concatenate([wr_hi, (wr_t - wr_hi.astype(F32)).astype(BF16)], axis=0))


def _router_kernel(logit_ref, br_ref, e_ref, w_ref, r_ref, cnt_ref, carry_scr, tri_scr, *, tr):
    i = pl.program_id(0)

    @pl.when(i == 0)
    def _():
        carry_scr[...] = jnp.zeros_like(carry_scr)
        a = lax.broadcasted_iota(jnp.int32, (tr, tr), 0)
        b = lax.broadcasted_iota(jnp.int32, (tr, tr), 1)
        tri_scr[...] = (a < b).astype(BF16)

    scores = jax.nn.sigmoid(logit_ref[...])
    biased = scores + br_ref[...]
    ninf = -jnp.inf

    b3 = biased.reshape(N_EXPERT_GROUPS, GROUP_SIZE, tr)
    j3 = lax.broadcasted_iota(jnp.int32, b3.shape, 1).astype(F32)
    top1 = jnp.max(b3, axis=1, keepdims=True)
    first = jnp.min(jnp.where(b3 == top1, j3, float(GROUP_SIZE)), axis=1, keepdims=True)
    top2 = jnp.max(jnp.where(j3 == first, ninf, b3), axis=1, keepdims=True)
    gscore = (top1 + top2).reshape(N_EXPERT_GROUPS, tr)
    gi = lax.broadcasted_iota(jnp.int32, gscore.shape, 0).astype(F32)
    gsel = jnp.zeros(gscore.shape, F32)
    for _ in range(TOPK_GROUPS):
        mx = jnp.max(gscore, axis=0, keepdims=True)
        pick = gi == jnp.min(jnp.where(gscore == mx, gi, float(N_EXPERT_GROUPS)), axis=0, keepdims=True)
        gsel = jnp.where(pick, 1.0, gsel)
        gscore = jnp.where(pick, ninf, gscore)
    emask = jnp.broadcast_to(gsel.reshape(N_EXPERT_GROUPS, 1, tr), b3.shape).reshape(N_EXPERTS, tr)
    masked = jnp.where(emask > 0.5, biased, ninf)

    ei = lax.broadcasted_iota(jnp.int32, masked.shape, 0).astype(F32)
    picked = jnp.zeros(masked.shape, F32)
    idxs, wsel = [], []
    for _ in range(TOP_K):
        mx = jnp.max(masked, axis=0, keepdims=True)
        idx = jnp.min(jnp.where(masked == mx, ei, float(N_EXPERTS)), axis=0, keepdims=True)
        pick = ei == idx
        idxs.append(idx)
        wsel.append(jnp.sum(jnp.where(pick, scores, 0.0), axis=0, keepdims=True))
        picked = jnp.where(pick, 1.0, picked)
        masked = jnp.where(pick, ninf, masked)
    wall = jnp.concatenate(wsel, axis=0)
    w_ref[...] = wall / jnp.sum(wall, axis=0, keepdims=True) * ROUTED_SCALE
    e_ref[...] = jnp.concatenate(idxs, axis=0).astype(jnp.int32)

    rank = carry_scr[:, 0:1] + _dot(picked.astype(BF16), tri_scr[...])
    r_ref[...] = jnp.concatenate(
        [jnp.sum(jnp.where(ei == idx, rank, 0.0), axis=0, keepdims=True) for idx in idxs], axis=0).astype(jnp.int32)
    total = carry_scr[...] + jnp.sum(picked, axis=1, keepdims=True)
    carry_scr[...] = total
    cnt_ref[...] = total.astype(jnp.int32)


def _router(logits, b_router):
    t = logits.shape[1]
    tr = min(512, t)
    tok = pl.BlockSpec((TOP_K, tr), lambda i: (0, i))
    e_t, w_t, r_t, cnt = pl.pallas_call(
        functools.partial(_router_kernel, tr=tr),
        grid=(t // tr,),
        in_specs=[pl.BlockSpec((N_EXPERTS, tr), lambda i: (0, i)),
                  pl.BlockSpec((N_EXPERTS, 1), lambda i: (0, 0))],
        out_specs=[tok, tok, tok, pl.BlockSpec((N_EXPERTS, LANE), lambda i: (0, 0))],
        out_shape=[jax.ShapeDtypeStruct((TOP_K, t), jnp.int32), jax.ShapeDtypeStruct((TOP_K, t), F32),
                   jax.ShapeDtypeStruct((TOP_K, t), jnp.int32), jax.ShapeDtypeStruct((N_EXPERTS, LANE), jnp.int32)],
        scratch_shapes=[pltpu.VMEM((N_EXPERTS, LANE), F32), pltpu.VMEM((tr, tr), BF16)],
        compiler_params=_params("arbitrary"),
        name="router",
    )(logits, b_router.reshape(N_EXPERTS, 1))
    return e_t, w_t, r_t, cnt


def _plan_kernel(e_ref, r_ref, cnt_ref, dest_ref, be_ref, bv_ref, nu_ref, *, br, nblk_pad):
    cnt = cnt_ref[...]
    shift = br.bit_length() - 1
    padded = jnp.maximum(lax.shift_left(lax.shift_right_logical(cnt + (br - 1), shift), shift), br)
    ea = lax.broadcasted_iota(jnp.int32, (N_EXPERTS, N_EXPERTS), 0)
    eb = lax.broadcasted_iota(jnp.int32, (N_EXPERTS, N_EXPERTS), 1)
    pends = jnp.dot((eb <= ea).astype(F32), padded.astype(F32), preferred_element_type=F32,
                    precision=HIGHEST).astype(jnp.int32)
    pstart = pends - padded

    e = e_ref[...]
    first_row = jnp.zeros(e.shape, jnp.int32)
    for x in range(N_EXPERTS):
        first_row = jnp.where(e == x, pstart[x:x + 1, 0:1], first_row)
    dest_ref[...] = first_row + r_ref[...]

    @pl.when(pl.program_id(0) == 0)
    def _():
        esub = lax.broadcasted_iota(jnp.int32, (N_EXPERTS, LANE), 0)
        real_end = (pstart + cnt).astype(F32)
        for c in range(nblk_pad // LANE):
            bstart = (lax.broadcasted_iota(jnp.int32, (1, LANE), 1) + c * LANE) * br
            be = jnp.minimum(jnp.sum((pends <= bstart).astype(F32), axis=0, keepdims=True),
                             float(N_EXPERTS - 1)).astype(jnp.int32)
            end = jnp.sum(jnp.where(esub == be, real_end, 0.0), axis=0, keepdims=True).astype(jnp.int32)
            be_ref[:, c * LANE:(c + 1) * LANE] = be
            bv_ref[:, c * LANE:(c + 1) * LANE] = jnp.clip(end - bstart, 0, br)
        nu_ref[...] = lax.shift_right_logical(pends[N_EXPERTS - 1:N_EXPERTS, :], shift)


def _plan(e_t, r_t, cnt, br, nblk):
    t = e_t.shape[1]
    tc = min(2048, t)
    nblk_pad = -(-nblk // LANE) * LANE
    tok = pl.BlockSpec((TOP_K, tc), lambda i: (0, i))
    blk = pl.BlockSpec((1, nblk_pad), lambda i: (0, 0))
    dest, be, bv, nu = pl.pallas_call(
        functools.partial(_plan_kernel, br=br, nblk_pad=nblk_pad),
        grid=(t // tc,),
        in_specs=[tok, tok, pl.BlockSpec((N_EXPERTS, LANE), lambda i: (0, 0))],
        out_specs=[tok, blk, blk, pl.BlockSpec((1, LANE), lambda i: (0, 0))],
        out_shape=[jax.ShapeDtypeStruct((TOP_K, t), jnp.int32), jax.ShapeDtypeStruct((1, nblk_pad), jnp.int32),
                   jax.ShapeDtypeStruct((1, nblk_pad), jnp.int32), jax.ShapeDtypeStruct((1, LANE), jnp.int32)],
        compiler_params=_params("arbitrary"),
        name="moe_plan",
    )(e_t, r_t, cnt)
    return dest, be[0, :nblk], bv[0, :nblk], nu[0, :1]


def _row_copy(src, s, dst, d, sem):
    return pltpu.make_async_copy(src.at[pl.ds(s, 1), :], dst.at[pl.ds(d, 1), :], sem)


def _dispatch_kernel(dest_ref, u_ref, xs_ref, sem, *, tm):
    def start(r, carry):
        for k in range(TOP_K):
            _row_copy(u_ref, r, xs_ref, dest_ref[k, r], sem).start()
        return carry

    def wait(r, carry):
        for k in range(TOP_K):
            _row_copy(u_ref, r, xs_ref, dest_ref[k, r], sem).wait()
        return carry

    lax.fori_loop(0, tm, start, 0)
    lax.fori_loop(0, tm, wait, 0)


def _dispatch(u2p, dest_t, n_rows):
    t = u2p.shape[0]
    tm = min(256, t)
    nt = t // tm
    dest_blocks = dest_t.reshape(TOP_K, nt, tm).transpose(1, 0, 2)
    return pl.pallas_call(
        functools.partial(_dispatch_kernel, tm=tm),
        grid=(nt,),
        in_specs=[pl.BlockSpec((None, TOP_K, tm), lambda i: (i, 0, 0), memory_space=pltpu.SMEM),
                  pl.BlockSpec((tm, HALF), lambda i: (i, 0))],
        out_specs=pl.BlockSpec(memory_space=pl.ANY),
        out_shape=jax.ShapeDtypeStruct((n_rows, HALF), PACKED),
        scratch_shapes=[pltpu.SemaphoreType.DMA(())],
        compiler_params=_params("arbitrary"),
        name="moe_dispatch",
    )(dest_blocks, u2p)


SC_CORES = 2
SC_SUBCORES = 16
SC_WINDOW = 128
SC_WORKERS = SC_CORES * SC_SUBCORES


def _sc_mesh():
    return plsc.VectorSubcoreMesh(core_axis_name="c", subcore_axis_name="s")


def _sc_worker():
    return lax.axis_index("s") * SC_CORES + lax.axis_index("c")


def _sc_scatter_rows(rows, dest_blocks, n_rows):
    t, width = rows.shape
    nchunks = t // SC_WINDOW // SC_WORKERS

    def body(rows_hbm, dest_hbm, out_hbm, idx_v, rows_v, sem):
        wid = _sc_worker()

        @pl.loop(0, nchunks)
        def _(c):
            chunk = wid * nchunks + c
            pltpu.sync_copy(dest_hbm.at[chunk], idx_v)
            pltpu.sync_copy(rows_hbm.at[pl.ds(pl.multiple_of(chunk * SC_WINDOW, SC_WINDOW), SC_WINDOW)], rows_v)
            copies = [pltpu.async_copy(rows_v, out_hbm.at[idx_v.at[k]], sem) for k in range(TOP_K)]
            for cp in copies:
                cp.wait()

    return pl.kernel(
        body, out_type=jax.ShapeDtypeStruct((n_rows, width), rows.dtype), mesh=_sc_mesh(),
        scratch_types=[pltpu.VMEM((TOP_K, SC_WINDOW), jnp.int32), pltpu.VMEM((SC_WINDOW, width), rows.dtype),
                       pltpu.SemaphoreType.DMA],
        name="moe_sc_dispatch")(rows, dest_blocks)


def _sc_gather_rows(table, idx):
    n = idx.shape[0]
    width = table.shape[1]
    nchunks = n // SC_WINDOW // SC_WORKERS

    def body(table_hbm, idx_hbm, out_hbm, idx_v, rows_v, sem):
        wid = _sc_worker()

        @pl.loop(0, nchunks)
        def _(c):
            base = pl.multiple_of((wid * nchunks + c) * SC_WINDOW, SC_WINDOW)
            pltpu.sync_copy(idx_hbm.at[pl.ds(base, SC_WINDOW)], idx_v)
            pltpu.async_copy(table_hbm.at[idx_v], rows_v, sem).wait()
            pltpu.sync_copy(rows_v, out_hbm.at[pl.ds(base, SC_WINDOW)])

    return pl.kernel(
        body, out_type=jax.ShapeDtypeStruct((n, width), table.dtype), mesh=_sc_mesh(),
        scratch_types=[pltpu.VMEM((SC_WINDOW,), jnp.int32), pltpu.VMEM((SC_WINDOW, width), table.dtype),
                       pltpu.SemaphoreType.DMA],
        name="moe_sc_gather")(table, idx)


def _expert_rows(x_ref, valid, wg, wu, wd, y_ref):
    live = lax.broadcasted_iota(jnp.int32, x_ref.shape, 0) < valid
    x = _unpack_rows(jnp.where(live, x_ref[...], 0)).astype(BF16)
    hcat = _silu(_dot(x, wg)) * _dot(x, wu)
    y_ref[...] = _pack_rows(_dot(hcat.astype(BF16), wd))


def _expert_cast_kernel(be_ref, bv_ref, nu_ref, x_ref, wg_ref, wu_ref, wd_ref, after_ref, y_ref,
                        wgb_ref, wub_ref, wdb_ref):
    del after_ref
    i = pl.program_id(0)
    prev = be_ref[jnp.maximum(i - 1, 0)]

    @pl.when(jnp.logical_and(i < nu_ref[0], jnp.logical_or(i == 0, be_ref[i] != prev)))
    def _():
        wgb_ref[...] = wg_ref[...].astype(BF16)
        wub_ref[...] = wu_ref[...].astype(BF16)
        wdb_ref[...] = wd_ref[...].astype(BF16)

    @pl.when(i < nu_ref[0])
    def _():
        _expert_rows(x_ref, bv_ref[i], wgb_ref[...], wub_ref[...], wdb_ref[...], y_ref)


def _expert_ready_kernel(be_ref, bv_ref, nu_ref, x_ref, wg_ref, wu_ref, wd_ref, y_ref):
    i = pl.program_id(0)

    @pl.when(i < nu_ref[0])
    def _():
        _expert_rows(x_ref, bv_ref[i], wg_ref[...], wu_ref[...], wd_ref[...], y_ref)


def _experts(x_sorted, blk_e, blk_valid, n_used, weights, br, after):
    n_rows = x_sorted.shape[0]
    nblk = n_rows // br
    cast = weights[0].dtype != BF16

    def blk(i, be, bv, nu):
        return (jnp.minimum(i, nu[0] - 1), 0)

    def expert(i, be, bv, nu):
        return (be[i], 0, 0)

    wspecs = [pl.BlockSpec((None,) + w.shape[1:], expert) for w in weights]
    y_shape = jax.ShapeDtypeStruct((n_rows, HALF), PACKED)
    if cast:
        in_specs = [pl.BlockSpec((br, HALF), blk), *wspecs,
                    pl.BlockSpec(after.shape, lambda i, be, bv, nu: (0,) * after.ndim)]
        out_specs = [pl.BlockSpec((br, HALF), blk), *wspecs]
        out_shape = [y_shape] + [jax.ShapeDtypeStruct(w.shape, BF16) for w in weights]
        body, args = _expert_cast_kernel, (x_sorted, *weights, after)
    else:
        in_specs = [pl.BlockSpec((br, HALF), blk), *wspecs]
        out_specs, out_shape = pl.BlockSpec((br, HALF), blk), y_shape
        body, args = _expert_ready_kernel, (x_sorted, *weights)
    out = pl.pallas_call(
        body,
        grid_spec=pltpu.PrefetchScalarGridSpec(num_scalar_prefetch=3, grid=(nblk,), in_specs=in_specs,
                                               out_specs=out_specs),
        out_shape=out_shape,
        compiler_params=_params("arbitrary"),
        name="moe_experts",
    )(blk_e, blk_valid, n_used, *args)
    return (out[0], tuple(out[1:])) if cast else (out, weights)


def _combine_kernel(dest_ref, ys_ref, w_ref, u_ref, x1_ref, g2_ref, sg_ref, su_ref, sd_ref, lg_ref, lb_ref,
                    o_ref, g_scr, sem, *, tm):
    def start(r, carry):
        for k in range(TOP_K):
            _row_copy(ys_ref, dest_ref[k, r], g_scr.at[k], r, sem).start()
        return carry

    def wait(r, carry):
        for k in range(TOP_K):
            _row_copy(ys_ref, dest_ref[k, r], g_scr.at[k], r, sem).wait()
        return carry

    lax.fori_loop(0, tm, start, 0)
    f = _shared_ffn(u_ref, sg_ref, su_ref, sd_ref)
    lax.fori_loop(0, tm, wait, 0)
    f = _add_routed(f, g_scr, w_ref[...])
    o_ref[...] = _layer_norm(ALPHA * x1_ref[...] + g2_ref[...] * f, lg_ref[...], lb_ref[...])


def _shared_ffn(u_ref, sg_ref, su_ref, sd_ref):
    ub = _unpack_rows(u_ref[...]).astype(BF16)
    hs = _silu(_dot(ub, sg_ref[...])) * _dot(ub, su_ref[...])
    return _dot(hs.astype(BF16), sd_ref[...])


def _add_routed(f, slots_ref, w):
    for k in range(TOP_K):
        f = f + _unpack_rows(slots_ref[k]) * w[:, k:k + 1]
    return f


def _combine_gathered_kernel(g_ref, w_ref, u_ref, x1_ref, g2_ref, sg_ref, su_ref, sd_ref, lg_ref, lb_ref, o_ref):
    f = _add_routed(_shared_ffn(u_ref, sg_ref, su_ref, sd_ref), g_ref, w_ref[...])
    o_ref[...] = _layer_norm(ALPHA * x1_ref[...] + g2_ref[...] * f, lg_ref[...], lb_ref[...])


def _combine(y_rows, dest_t, w_t, u2p, x1, g2, ws_gate, ws_up, ws_down, ln_g, ln_b, rows_per_mod, gathered):
    t = u2p.shape[0]
    tm = min(512 if gathered else 128, t)
    nt = t // tm
    rows = pl.BlockSpec((tm, D_MODEL), lambda i: (i, 0))
    vec = pl.BlockSpec((1, D_MODEL), lambda i: (0, 0))
    g2a, mspec = _mod_spec(g2, t, tm, rows_per_mod)
    common_specs = [pl.BlockSpec((tm, TOP_K), lambda i: (i, 0)),
                    pl.BlockSpec((tm, HALF), lambda i: (i, 0)), rows, mspec,
                    pl.BlockSpec((D_MODEL, D_SHARED), lambda i: (0, 0)),
                    pl.BlockSpec((D_MODEL, D_SHARED), lambda i: (0, 0)),
                    pl.BlockSpec((D_SHARED, D_MODEL), lambda i: (0, 0)),
                    vec, vec]
    common_args = (w_t.T, u2p, x1, g2a, ws_gate.astype(BF16), ws_up.astype(BF16), ws_down.astype(BF16),
                   ln_g.reshape(1, D_MODEL), ln_b.reshape(1, D_MODEL))
    if gathered:
        return pl.pallas_call(
            _combine_gathered_kernel,
            grid=(nt,),
            in_specs=[pl.BlockSpec((TOP_K, tm, HALF), lambda i: (0, i, 0))] + common_specs,
            out_specs=rows,
            out_shape=jax.ShapeDtypeStruct((t, D_MODEL), F32),
            compiler_params=_params("arbitrary"),
            name="moe_combine_gathered",
        )(y_rows.reshape(TOP_K, t, HALF), *common_args)
    dest_blocks = dest_t.reshape(TOP_K, nt, tm).transpose(1, 0, 2)
    return pl.pallas_call(
        functools.partial(_combine_kernel, tm=tm),
        grid=(nt,),
        in_specs=[pl.BlockSpec((None, TOP_K, tm), lambda i: (i, 0, 0), memory_space=pltpu.SMEM),
                  pl.BlockSpec(memory_space=pl.ANY)] + common_specs,
        out_specs=rows,
        out_shape=jax.ShapeDtypeStruct((t, D_MODEL), F32),
        scratch_shapes=[pltpu.VMEM((TOP_K, tm, HALF), PACKED), pltpu.SemaphoreType.DMA(())],
        compiler_params=_params("arbitrary"),
        name="moe_combine",
    )(dest_blocks, y_rows, *common_args)


def _moe(u2p, logits, x1, g2, p, weights, rows_per_mod, br, after):
    t = u2p.shape[0]
    e_t, w_t, r_t, counts = _router(logits, p["b_router"])
    n_rows = (t * TOP_K + N_EXPERTS * br + br - 1) // br * br
    dest_t, blk_e, blk_valid, n_used = _plan(e_t, r_t, counts, br, n_rows // br)
    on_sc = t % (SC_WORKERS * SC_WINDOW) == 0
    if on_sc:
        dest_blocks = dest_t.reshape(TOP_K, t // SC_WINDOW, SC_WINDOW).transpose(1, 0, 2)
        x_sorted = _sc_scatter_rows(u2p, dest_blocks, n_rows)
    else:
        x_sorted = _dispatch(u2p, dest_t, n_rows)
    y_rows, weights = _experts(x_sorted, blk_e, blk_valid, n_used, weights, br, after)
    if on_sc:
        y_rows = _sc_gather_rows(y_rows, dest_t.reshape(TOP_K * t))
    return _combine(y_rows, dest_t, w_t, u2p, x1, g2, p["ws_gate"], p["ws_up"], p["ws_down"],
                    p["ln2_g"], p["ln2_b"], rows_per_mod, on_sc), weights


def _cat_w_in(w_in):
    sizes = (SSD_INNER, CONV_DIM, SSD_HEADS, 1024, 1024, 1024, D_MODEL, D_MODEL)
    offs = [0]
    for s in sizes:
        offs.append(offs[-1] + s)
    z, xbc, dt, q, k, v, ga, gb = [w_in[:, offs[i]:offs[i + 1]] for i in range(8)]
    dt = jnp.pad(dt, ((0, 0), (0, SEG_END - SEG_DT - SSD_HEADS)))
    return jnp.concatenate([z, xbc, q, k, v, ga, gb, dt], axis=1).astype(BF16)


def kernel(x_prompt, x_sample, c_prompt, c_sample, cache_k, cache_v, page_table, state_conv, state_ssm, rel_bias, w_ada, b_ada, w_in, conv_w, conv_b, dt_bias, a_log, d_skip, ssd_norm_w, lam_q1, lam_k1, lam_q2, lam_k2, subln_w, w_br_ssd, w_br_attn, w_out, ln1_g, ln1_b, w_router, b_router, w_gate, w_up, w_down, ws_gate, ws_up, ws_down, ln2_g, ln2_b):
    assert w_in.shape[0] == DEPTH
    nbp, seq, _ = x_prompt.shape
    nbs, dseq, _ = x_sample.shape
    tp, ts = nbp * seq, nbs * dseq
    past = page_table.shape[1] * PAGE_SIZE
    cl = SSD_CHUNK

    pad_p = -nbp % 8
    c_all = jnp.concatenate([c_prompt, jnp.zeros((pad_p, D_MODEL), F32), jnp.repeat(c_sample, dseq, axis=0)], axis=0)
    mod = _adaln(c_all, w_ada[0], b_ada[0])
    mod_p = mod[:nbp].reshape(nbp, 1, 6, D_MODEL)
    mod_s = mod[nbp + pad_p:].reshape(nbs, dseq, 6, D_MODEL)
    mp = [mod_p[:, :, i] for i in range(6)]
    ms = [mod_s[:, :, i] for i in range(6)]

    w_cat = _cat_w_in(w_in[0])
    wa, wb, wo = w_br_ssd[0].astype(BF16), w_br_attn[0].astype(BF16), w_out[0].astype(BF16)
    lamp = jnp.stack([lam_q1[0], lam_k1[0], lam_q2[0], lam_k2[0]])
    moe_p = dict(b_router=b_router[0], ws_gate=ws_gate[0], ws_up=ws_up[0], ws_down=ws_down[0], ln2_g=ln2_g[0], ln2_b=ln2_b[0])
    ssd_w = (conv_w[0], conv_b[0], dt_bias[0], a_log[0], d_skip[0], ssd_norm_w[0])

    xp = x_prompt.reshape(tp, D_MODEL)
    z, xbc, dt, q, k, v, kb, vb, ga, gb = _inproj(xp, mp[1], mp[0], w_cat, seq)
    yn, conv_p, h_p = _ssd(xbc, dt, z, *ssd_w, jnp.zeros((nbp, CONV_W - 1, CONV_DIM), F32),
                           jnp.zeros((nbp, SSD_HEADS, SSD_HEADDIM, D_STATE), F32), nbp, seq // cl, cl)
    yn = yn.reshape(tp, SSD_INNER)
    on = _attn_prompt(q, kb, vb, rel_bias, lamp, subln_w[0], nbp, seq)
    x1_p, u2p_p, logits_p = _merge(yn, on, ga, gb, xp, mp[2], mp[4], mp[3], wa, wb, wo, ln1_g[0], ln1_b[0],
                                   w_router[0], seq)
    k_prompt = k.reshape(1, nbp, seq, ATTN_HEADS, 2 * ATTN_HEAD_DIM)
    v_prompt = v.reshape(1, nbp, seq, ATTN_HEADS, ATTN_V_DIM)

    xs_ = x_sample.reshape(ts, D_MODEL)
    z, xbc, dt, q, k, v, kb, vb, ga, gb = _inproj(xs_, ms[1], ms[0], w_cat, dseq)
    yn, conv_s, h_s = _ssd(xbc, dt, z, *ssd_w, state_conv[0], state_ssm[0], nbs, 1, dseq)
    yn = yn.reshape(ts, SSD_INNER)
    on = _attn_decode(q.reshape(nbs, dseq, -1), k.reshape(nbs, dseq * ATTN_HEADS, LANE),
                      v.reshape(nbs, dseq * ATTN_HEADS, LANE), cache_k[0], cache_v[0], page_table,
                      rel_bias, lamp, subln_w[0])
    y_prompt, w_bf16 = _moe(u2p_p, logits_p, x1_p, mp[5], moe_p, (w_gate[0], w_up[0], w_down[0]), seq, 1024,
                            on[0, :, :LANE])
    y_prompt = y_prompt.reshape(nbp, seq, D_MODEL)
    x1, u2p, logits = _merge(yn, on.reshape(ts, -1), ga, gb, xs_, ms[2], ms[4], ms[3], wa, wb, wo, ln1_g[0], ln1_b[0],
                             w_router[0], dseq)
    y_sample = _moe(u2p, logits, x1, ms[5], moe_p, w_bf16, dseq, 64, None)[0].reshape(nbs, dseq, D_MODEL)
    k_sample = k.reshape(1, nbs, dseq, ATTN_HEADS, 2 * ATTN_HEAD_DIM)
    v_sample = v.reshape(1, nbs, dseq, ATTN_HEADS, ATTN_V_DIM)

    return (y_prompt, y_sample, k_prompt, v_prompt, conv_p[None], h_p.reshape(1, nbp, SSD_HEADS, SSD_HEADDIM, D_STATE),
            k_sample, v_sample, conv_s[None], h_s.reshape(1, nbs, SSD_HEADS, SSD_HEADDIM, D_STATE))
```
